```python
import jax, jax.numpy as jnp
from jax import lax
import numpy as np

D_MODEL = 1024
BATCH = 8
SEQ = 2048
DEPTH = 2
DEC_BATCH = 128
DEC_SEQ = 8
PAST_LEN = 16384
PAGE_SIZE = 128

N_EVEN = (DEPTH + 1) // 2
N_ODD = DEPTH // 2
A_WIDTH = D_MODEL // 2
B_WIDTH = D_MODEL // 2
RWKV_HEAD = 64
RWKV_HEADS = B_WIDTH // RWKV_HEAD
LORA_W = 64
LORA_A = 64
LORA_G = 128
B_PROJ = 3 * B_WIDTH + LORA_W + LORA_A + LORA_G
IN_PROJ = 2 * A_WIDTH + B_PROJ
CONV_W = 31
CONV_BUF = CONV_W - 1
POOL_WINDOWS = (2, 4, 8, 16)
POOL_GROUPS = 4
POOL_GW = D_MODEL // POOL_GROUPS
POOL_BUF = max(POOL_WINDOWS) - 1
D_FF = ((8 * D_MODEL + 3 * 256 - 1) // (3 * 256)) * 256
ALPHA = (2 * DEPTH) ** 0.25
BETA = (8 * DEPTH) ** -0.25
LN_EPS = 1e-5
GN_EPS = 64e-5

kernel_name = 'hybrid_conv_rwkv7_pool_deepnorm_step'


def layer_norm(x, g, b, eps=LN_EPS):
    xf = x.astype(jnp.float32)
    mu = jnp.mean(xf, axis=-1, keepdims=True)
    var = jnp.mean(jnp.square(xf - mu), axis=-1, keepdims=True)
    y = (xf - mu) * lax.rsqrt(var + eps) * g.astype(jnp.float32) + b.astype(jnp.float32)
    return y.astype(x.dtype)


def conformer_conv(val, gate, conv_prev, w, b, g, bb):
    u = val * jax.nn.sigmoid(gate)
    ext = jnp.concatenate([conv_prev.astype(u.dtype), u], axis=1)
    y = lax.conv_general_dilated(ext, w[:, None, :].astype(u.dtype), window_strides=(1,), padding='VALID',
                                 dimension_numbers=('NWC', 'WIO', 'NWC'), feature_group_count=A_WIDTH) + b
    y = jax.nn.silu(layer_norm(y, g, bb))
    return y, ext[:, -CONV_BUF:]


def rwkv_scan(r, k, v, w, a, b, s0):
    def step(S, inp):
        r_t, k_t, v_t, w_t, a_t, b_t = inp
        sa = jnp.einsum('nhvk,nhk->nhv', S, a_t)
        S = S * w_t[:, :, None, :] + sa[..., None] * b_t[:, :, None, :] + v_t[..., None] * k_t[:, :, None, :]
        o = jnp.einsum('nhvk,nhk->nhv', S, r_t)
        return S, o
    xs = tuple(jnp.moveaxis(t.astype(jnp.float32), 1, 0) for t in (r, k, v, w, a, b))
    S, o = lax.scan(step, s0.astype(jnp.float32), xs)
    return jnp.moveaxis(o, 0, 1), S


def rwkv_mix(proj, shift_prev, wkv_prev, mu, w0, w2, a0, a2, g2, k_k, k_a, r_k, lnx_g, lnx_b):
    N, L, _ = proj.shape
    prev = jnp.concatenate([shift_prev[:, None, :].astype(proj.dtype), proj[:, :-1]], axis=1)
    xm = proj + mu * (prev - proj)
    s1, s2, s3 = B_WIDTH, 2 * B_WIDTH, 3 * B_WIDTH
    r, k, v = xm[..., :s1], xm[..., s1:s2], xm[..., s2:s3]
    dw = xm[..., s3:s3 + LORA_W]
    da = xm[..., s3 + LORA_W:s3 + LORA_W + LORA_A]
    dg = xm[..., s3 + LORA_W + LORA_A:]
    wlog = -jax.nn.softplus(-(w0 + jnp.tanh(dw) @ w2)) - 0.5
    decay = jnp.exp(-jnp.exp(wlog.astype(jnp.float32)))
    a = jax.nn.sigmoid((a0 + da @ a2).astype(jnp.float32))
    g = jax.nn.sigmoid(dg) @ g2
    hs = (N, L, RWKV_HEADS, RWKV_HEAD)
    rf = r.astype(jnp.float32).reshape(hs)
    kf = k.astype(jnp.float32)
    vf = v.astype(jnp.float32).reshape(hs)
    kk = (kf * k_k.astype(jnp.float32)).reshape(hs)
    kk = kk / jnp.maximum(jnp.sqrt(jnp.sum(kk * kk, axis=-1, keepdims=True)), 1e-12)
    kf = (kf * (1.0 + (a - 1.0) * k_a.astype(jnp.float32))).reshape(hs)
    ah = a.reshape(hs)
    o, S = rwkv_scan(rf, kf, vf, decay.reshape(hs), -kk, kk * ah, wkv_prev)
    om = jnp.mean(o, axis=-1, keepdims=True)
    ov = jnp.mean(jnp.square(o - om), axis=-1, keepdims=True)
    o = ((o - om) * lax.rsqrt(ov + GN_EPS)).reshape(N, L, B_WIDTH) * lnx_g.astype(jnp.float32) + lnx_b.astype(jnp.float32)
    bonus = jnp.sum(rf * kf * r_k.astype(jnp.float32), axis=-1, keepdims=True) * vf
    o = (o + bonus.reshape(N, L, B_WIDTH)) * g.astype(jnp.float32)
    return o.astype(proj.dtype), proj[:, -1], S.astype(wkv_prev.dtype)


def pool_mix(x, pool_prev, start, w, scale):
    N, L, D = x.shape
    ext = jnp.concatenate([pool_prev.astype(x.dtype), x], axis=1)
    cs = jnp.cumsum(ext.astype(jnp.float32), axis=1)
    cs0 = jnp.concatenate([jnp.zeros((N, 1, D), jnp.float32), cs], axis=1)
    pos = start + jnp.arange(L)
    outs = []
    for gi, wdw in enumerate(POOL_WINDOWS):
        lo, hi = gi * POOL_GW, (gi + 1) * POOL_GW
        ssum = cs0[:, POOL_BUF + 1:, lo:hi] - cs0[:, POOL_BUF + 1 - wdw:POOL_BUF + 1 - wdw + L, lo:hi]
        cnt = jnp.minimum(wdw, pos + 1).astype(jnp.float32)
        outs.append(ssum / cnt[None, :, None])
    pooled = jnp.concatenate(outs, axis=-1) - x.astype(jnp.float32)
    y = jnp.einsum('nlgc,gcd->nlgd', pooled.reshape(N, L, POOL_GROUPS, POOL_GW), w.astype(jnp.float32))
    y = y.reshape(N, L, D) * scale.astype(jnp.float32)
    return y.astype(x.dtype), ext[:, -POOL_BUF:]


def swiglu(x, wg, wu, wd):
    return (jax.nn.silu(x @ wg) * (x @ wu)) @ wd


def trunk(x, conv_prev, shift_prev, wkv_prev, pool_prev, start, p):
    new_conv, new_shift, new_wkv, new_pool = [], [], [], []
    for i in range(DEPTH):
        if i % 2 == 0:
            e = i // 2
            proj = x @ p['w_in'][e]
            a_out, c_new = conformer_conv(proj[..., :A_WIDTH], proj[..., A_WIDTH:2 * A_WIDTH], conv_prev[e],
                                          p['conv_w'][e], p['conv_b'][e], p['conv_ln_g'][e], p['conv_ln_b'][e])
            b_out, sh_new, s_new = rwkv_mix(proj[..., 2 * A_WIDTH:], shift_prev[e], wkv_prev[e],
                                            p['rwkv_mu'][e], p['rwkv_w0'][e], p['rwkv_w2'][e], p['rwkv_a0'][e],
                                            p['rwkv_a2'][e], p['rwkv_g2'][e], p['rwkv_kk'][e], p['rwkv_ka'][e],
                                            p['rwkv_rk'][e], p['rwkv_lnx_g'][e], p['rwkv_lnx_b'][e])
            mix = jnp.concatenate([a_out, b_out], axis=-1) @ p['w_out'][e]
            new_conv.append(c_new)
            new_shift.append(sh_new)
            new_wkv.append(s_new)
        else:
            o = i // 2
            mix, pl_new = pool_mix(x, pool_prev[o], start, p['pool_w'][o], p['pool_scale'][o])
            new_pool.append(pl_new)
        x = layer_norm(ALPHA * x + mix, p['ln_mix_g'][i], p['ln_mix_b'][i])
        x = layer_norm(ALPHA * x + swiglu(x, p['ffn_gate'][i], p['ffn_up'][i], p['ffn_down'][i]),
                       p['ln_ffn_g'][i], p['ln_ffn_b'][i])
    return x, jnp.stack(new_conv), jnp.stack(new_shift), jnp.stack(new_wkv), jnp.stack(new_pool)


def setup_inputs(seed: int = 0) -> dict:
    key = jax.random.key(seed)
    ks = jax.random.split(key, 40)
    nrm = jax.random.normal
    f = jnp.float32
    d = {}
    d['x_prompt'] = nrm(ks[0], (BATCH, SEQ, D_MODEL), f)
    d['x_sample'] = nrm(ks[1], (DEC_BATCH, DEC_SEQ, D_MODEL), f)
    d['state_conv'] = 0.5 * nrm(ks[2], (N_EVEN, DEC_BATCH, CONV_BUF, A_WIDTH), f)
    d['state_shift'] = nrm(ks[3], (N_EVEN, DEC_BATCH, B_PROJ), f)
    d['state_wkv'] = 0.5 * nrm(ks[4], (N_EVEN, DEC_BATCH, RWKV_HEADS, RWKV_HEAD, RWKV_HEAD), f)
    d['state_pool'] = nrm(ks[5], (N_ODD, DEC_BATCH, POOL_BUF, D_MODEL), f)
    d['w_in'] = nrm(ks[6], (N_EVEN, D_MODEL, IN_PROJ), f) * D_MODEL ** -0.5
    d['conv_w'] = nrm(ks[7], (N_EVEN, CONV_W, A_WIDTH), f) * CONV_W ** -0.5
    d['conv_b'] = 0.01 * nrm(ks[8], (N_EVEN, A_WIDTH), f)
    d['conv_ln_g'] = 1.0 + 0.02 * nrm(ks[9], (N_EVEN, A_WIDTH), f)
    d['conv_ln_b'] = 0.01 * nrm(ks[10], (N_EVEN, A_WIDTH), f)
    d['rwkv_mu'] = jax.random.uniform(ks[11], (N_EVEN, B_PROJ), f)
    d['rwkv_w0'] = jax.random.uniform(ks[12], (N_EVEN, B_WIDTH), f, -6.0, 1.0)
    d['rwkv_w2'] = 0.1 * nrm(ks[13], (N_EVEN, LORA_W, B_WIDTH), f) * LORA_W ** -0.5
    d['rwkv_a0'] = 0.1 * nrm(ks[14], (N_EVEN, B_WIDTH), f)
    d['rwkv_a2'] = 0.1 * nrm(ks[15], (N_EVEN, LORA_A, B_WIDTH), f) * LORA_A ** -0.5
    d['rwkv_g2'] = nrm(ks[16], (N_EVEN, LORA_G, B_WIDTH), f) * LORA_G ** -0.5
    d['rwkv_kk'] = 0.85 + 0.05 * nrm(ks[17], (N_EVEN, B_WIDTH), f)
    d['rwkv_ka'] = 1.0 + 0.05 * nrm(ks[18], (N_EVEN, B_WIDTH), f)
    d['rwkv_rk'] = 0.1 * nrm(ks[19], (N_EVEN, RWKV_HEADS, RWKV_HEAD), f)
    d['rwkv_lnx_g'] = 1.0 + 0.02 * nrm(ks[20], (N_EVEN, B_WIDTH), f)
    d['rwkv_lnx_b'] = 0.01 * nrm(ks[21], (N_EVEN, B_WIDTH), f)
    d['w_out'] = nrm(ks[22], (N_EVEN, A_WIDTH + B_WIDTH, D_MODEL), f) * (A_WIDTH + B_WIDTH) ** -0.5 * BETA
    d['pool_w'] = nrm(ks[23], (N_ODD, POOL_GROUPS, POOL_GW, POOL_GW), f) * POOL_GW ** -0.5 * BETA
    d['pool_scale'] = 1.0 + 0.02 * nrm(ks[24], (N_ODD, D_MODEL), f)
    d['ln_mix_g'] = 1.0 + 0.02 * nrm(ks[25], (DEPTH, D_MODEL), f)
    d['ln_mix_b'] = 0.01 * nrm(ks[26], (DEPTH, D_MODEL), f)
    d['ffn_gate'] = nrm(ks[27], (DEPTH, D_MODEL, D_FF), f) * D_MODEL ** -0.5
    d['ffn_up'] = nrm(ks[28], (DEPTH, D_MODEL, D_FF), f) * D_MODEL ** -0.5
    d['ffn_down'] = nrm(ks[29], (DEPTH, D_FF, D_MODEL), f) * D_FF ** -0.5 * BETA
    d['ln_ffn_g'] = 1.0 + 0.02 * nrm(ks[30], (DEPTH, D_MODEL), f)
    d['ln_ffn_b'] = 0.01 * nrm(ks[31], (DEPTH, D_MODEL), f)
    return d


def reference(x_prompt, x_sample, state_conv, state_shift, state_wkv, state_pool,
              w_in, conv_w, conv_b, conv_ln_g, conv_ln_b,
              rwkv_mu, rwkv_w0, rwkv_w2, rwkv_a0, rwkv_a2, rwkv_g2, rwkv_kk, rwkv_ka, rwkv_rk,
              rwkv_lnx_g, rwkv_lnx_b, w_out, pool_w, pool_scale,
              ln_mix_g, ln_mix_b, ffn_gate, ffn_up, ffn_down, ln_ffn_g, ln_ffn_b):
    p = dict(w_in=w_in, conv_w=conv_w, conv_b=conv_b, conv_ln_g=conv_ln_g, conv_ln_b=conv_ln_b,
             rwkv_mu=rwkv_mu, rwkv_w0=rwkv_w0, rwkv_w2=rwkv_w2, rwkv_a0=rwkv_a0, rwkv_a2=rwkv_a2,
             rwkv_g2=rwkv_g2, rwkv_kk=rwkv_kk, rwkv_ka=rwkv_ka, rwkv_rk=rwkv_rk,
             rwkv_lnx_g=rwkv_lnx_g, rwkv_lnx_b=rwkv_lnx_b, w_out=w_out, pool_w=pool_w, pool_scale=pool_scale,
             ln_mix_g=ln_mix_g, ln_mix_b=ln_mix_b, ffn_gate=ffn_gate, ffn_up=ffn_up, ffn_down=ffn_down,
             ln_ffn_g=ln_ffn_g, ln_ffn_b=ln_ffn_b)
    dt = x_prompt.dtype
    z_conv = jnp.zeros((N_EVEN, BATCH, CONV_BUF, A_WIDTH), dt)
    z_shift = jnp.zeros((N_EVEN, BATCH, B_PROJ), dt)
    z_wkv = jnp.zeros((N_EVEN, BATCH, RWKV_HEADS, RWKV_HEAD, RWKV_HEAD), state_wkv.dtype)
    z_pool = jnp.zeros((N_ODD, BATCH, POOL_BUF, D_MODEL), dt)
    y_prompt, p_conv, p_shift, p_wkv, p_pool = trunk(x_prompt, z_conv, z_shift, z_wkv, z_pool, 0, p)
    y_sample, s_conv, s_shift, s_wkv, s_pool = trunk(x_sample, state_conv, state_shift, state_wkv, state_pool,
                                                     PAST_LEN, p)
    return (y_prompt, y_sample, p_conv, p_shift, p_wkv, p_pool, s_conv, s_shift, s_wkv, s_pool)
```

```python
import functools

import jax
import jax.numpy as jnp
from jax import lax
from jax.experimental import pallas as pl
from jax.experimental.pallas import tpu as pltpu

F32 = jnp.float32
BF16 = jnp.bfloat16
HIGHEST = lax.Precision.HIGHEST

HEAD = 64
PAIR = 2 * HEAD
CHUNK = 64
SOLVE_BLOCK = 16
CONV_TAPS = 31
CONV_HIST = CONV_TAPS - 1
CONV_PAD = 32
POOL_WINDOWS = (2, 4, 8, 16)
POOL_HIST = max(POOL_WINDOWS) - 1
POOL_PAD = 16
PAST_LEN = 16384
LN_EPS = 1e-5
GN_EPS = 64e-5
VMEM_LIMIT = 56 * 1024 * 1024


def _ln(z, g, b, eps):
    mu = jnp.mean(z, axis=-1, keepdims=True)
    zc = z - mu
    var = jnp.mean(zc * zc, axis=-1, keepdims=True)
    return zc * lax.rsqrt(var + eps) * g + b


def _sigmoid(x):
    return 1.0 / (1.0 + jnp.exp(-x))


def _cparams(sem):
    return pltpu.CompilerParams(dimension_semantics=sem, vmem_limit_bytes=VMEM_LIMIT)


def _inproj_kernel(x_ref, wa_ref, wb_ref, oa_ref, ob_ref):
    xb = x_ref[...].astype(BF16)
    oa_ref[...] = jnp.dot(xb, wa_ref[...], preferred_element_type=F32)
    ob_ref[...] = jnp.dot(xb, wb_ref[...], preferred_element_type=F32)


def _inproj(x, wa, wb, tm):
    T, D = x.shape
    na, nb = wa.shape[1], wb.shape[1]
    return pl.pallas_call(
        _inproj_kernel,
        grid=(T // tm,),
        in_specs=[pl.BlockSpec((tm, D), lambda i: (i, 0)),
                  pl.BlockSpec((D, na), lambda i: (0, 0)),
                  pl.BlockSpec((D, nb), lambda i: (0, 0))],
        out_specs=[pl.BlockSpec((tm, na), lambda i: (i, 0)),
                   pl.BlockSpec((tm, nb), lambda i: (i, 0))],
        out_shape=[jax.ShapeDtypeStruct((T, na), F32), jax.ShapeDtypeStruct((T, nb), F32)],
        compiler_params=_cparams(("parallel",)),
        name="in_proj",
    )(x, wa, wb)


def _conv_kernel(p_ref, st_ref, w_ref, b_ref, g_ref, bb_ref, o_ref, nst_ref, ext_ref, *, tl, width):
    c = pl.program_id(1)

    @pl.when(c == 0)
    def _():
        ext_ref[CONV_PAD - CONV_HIST:CONV_PAD, :] = st_ref[0]

    p = p_ref[...]
    u = p[:, :width] * _sigmoid(p[:, width:])
    ext_ref[CONV_PAD:CONV_PAD + tl, :] = u
    acc = jnp.zeros((tl, width), F32) + b_ref[...]
    for j in range(CONV_TAPS):
        s = CONV_PAD - CONV_HIST + j
        acc = acc + ext_ref[s:s + tl, :] * w_ref[j:j + 1, :]
    y = _ln(acc, g_ref[...], bb_ref[...], LN_EPS)
    o_ref[...] = y * _sigmoid(y)
    tail = ext_ref[tl + CONV_PAD - CONV_HIST:tl + CONV_PAD, :]
    ext_ref[CONV_PAD - CONV_HIST:CONV_PAD, :] = tail
    nst_ref[0] = tail


def _conv_branch(proj_a, state, w, b, g, bb, n_seq, seq_len, tl):
    T, two_w = proj_a.shape
    width = two_w // 2
    nc = seq_len // tl
    vec = pl.BlockSpec((1, width), lambda n, c: (0, 0))
    return pl.pallas_call(
        functools.partial(_conv_kernel, tl=tl, width=width),
        grid=(n_seq, nc),
        in_specs=[pl.BlockSpec((tl, two_w), lambda n, c: (n * nc + c, 0)),
                  pl.BlockSpec((1, CONV_HIST, width), lambda n, c: (n, 0, 0)),
                  pl.BlockSpec((CONV_TAPS, width), lambda n, c: (0, 0)),
                  vec, vec, vec],
        out_specs=[pl.BlockSpec((tl, width), lambda n, c: (n * nc + c, 0)),
                   pl.BlockSpec((1, CONV_HIST, width), lambda n, c: (n, 0, 0))],
        out_shape=[jax.ShapeDtypeStruct((T, width), F32),
                   jax.ShapeDtypeStruct((n_seq, CONV_HIST, width), F32)],
        scratch_shapes=[pltpu.VMEM((CONV_PAD + tl, width), F32)],
        compiler_params=_cparams(("parallel", "arbitrary")),
        name="conv_branch",
    )(proj_a, state, w, b, g, bb)


def _mm(a, b):
    return jnp.dot(a, b, precision=HIGHEST, preferred_element_type=F32)


def _mm_nt(a, b):
    return lax.dot_general(a, b, (((1,), (1,)), ((), ())), precision=HIGHEST, preferred_element_type=F32)


def _rwkv_kernel(p_ref, sh_ref, wkv_ref, mu_ref, w0_ref, w2a2_ref, a0_ref, g2_ref, kk_ref, ka_ref, rk_ref,
                 lng_ref, lnb_ref, o_ref, nsh_ref, nwkv_ref, s_scr, prev_scr, *, ci, bw, lw):
    c = pl.program_id(1)
    n_pairs = bw // PAIR
    C = CHUNK

    @pl.when(c == 0)
    def _():
        prev_scr[0:1, :] = sh_ref[0]
        s_scr[...] = jnp.zeros_like(s_scr)
        for h in range(2 * n_pairs):
            lo = (h % 2) * HEAD
            s_scr[h // 2, lo:lo + HEAD, lo:lo + HEAD] = wkv_ref[0, h]

    p = p_ref[...]
    row = lax.broadcasted_iota(jnp.int32, (ci, 1), 0)
    prev = jnp.where(row == 0, prev_scr[0:1, :], pltpu.roll(p, 1, 0))
    prev_scr[0:1, :] = p[ci - 1:ci, :]
    nsh_ref[0] = p[ci - 1:ci, :]
    xm = p + mu_ref[...] * (prev - p)
    r = xm[:, 0:bw]
    k = xm[:, bw:2 * bw]
    v = xm[:, 2 * bw:3 * bw]
    dwa = xm[:, 3 * bw:3 * bw + 2 * lw]
    dg = xm[:, 3 * bw + 2 * lw:]
    lane_l = lax.broadcasted_iota(jnp.int32, (1, 2 * lw), 1)
    is_w = lane_l < lw
    wpre = w0_ref[...] + _mm(jnp.where(is_w, jnp.tanh(dwa), 0.0), w2a2_ref[...])
    apre = a0_ref[...] + _mm(jnp.where(is_w, 0.0, dwa), w2a2_ref[...])
    z = -wpre
    wlog = -(jnp.maximum(z, 0.0) + jnp.log1p(jnp.exp(-jnp.abs(z)))) - 0.5
    logw = -jnp.exp(wlog)
    a_lr = _sigmoid(apre)
    gate = _mm(_sigmoid(dg), g2_ref[...])
    kkv = k * kk_ref[...]
    k2 = k * (1.0 + (a_lr - 1.0) * ka_ref[...])
    rk2 = r * k2 * rk_ref[...]

    lane = lax.broadcasted_iota(jnp.int32, (1, PAIR), 1)
    m_l = lane < HEAD
    rowp = lax.broadcasted_iota(jnp.int32, (PAIR, PAIR), 0)
    colp = lax.broadcasted_iota(jnp.int32, (PAIR, PAIR), 1)
    bd = (rowp >= HEAD) == (colp >= HEAD)
    bd_f = bd.astype(F32)
    t_i = lax.broadcasted_iota(jnp.int32, (C, PAIR), 0)
    s_i = lax.broadcasted_iota(jnp.int32, (C, PAIR), 1) & (HEAD - 1)
    strict = s_i < t_i
    incl = s_i <= t_i
    eye = (s_i == t_i).astype(F32)
    blk_bits = SOLVE_BLOCK.bit_length() - 1
    dblk = (s_i >> blk_bits) == (t_i >> blk_bits)
    tri = (lax.broadcasted_iota(jnp.int32, (C, C), 1) <= lax.broadcasted_iota(jnp.int32, (C, C), 0)).astype(F32)

    def pad(x):
        if ci == C:
            return x
        return jnp.concatenate([x, jnp.zeros((C - ci, x.shape[1]), F32)], axis=0)

    def vs(y):
        return jnp.concatenate([jnp.where(m_l, y, 0.0), jnp.where(m_l, 0.0, y)], axis=0)

    def pm(a, y):
        return _mm(a, vs(y))

    logw_p = pad(logw)
    cs = _mm(tri, logw_p)
    g_in = jnp.exp(cs)
    g_ex = jnp.exp(cs - logw_p)
    g_inv = jnp.exp(-cs)
    g_end = g_in[C - 1:C, :]

    for pi in range(n_pairs):
        sl = slice(pi * PAIR, (pi + 1) * PAIR)
        kk_p = kkv[:, sl]
        ss = _mm(kk_p * kk_p, bd_f)
        kk_p = kk_p / jnp.maximum(jnp.sqrt(ss), 1e-12)
        a_p = pad(-kk_p)
        b_p = pad(kk_p * a_lr[:, sl])
        r_p = pad(r[:, sl])
        k_p = pad(k2[:, sl])
        v_p = pad(v[:, sl])
        gi, gx, gv, ge = g_in[:, sl], g_ex[:, sl], g_inv[:, sl], g_end[:, sl]
        at = a_p * gx
        rt = r_p * gi
        bt = b_p * gv
        kt = k_p * gv
        sm = _mm_nt(jnp.concatenate([at, rt], axis=0), jnp.concatenate([vs(bt), vs(kt)], axis=0))
        lab = jnp.where(strict, sm[0:C, 0:PAIR], 0.0)
        lak = jnp.where(strict, sm[0:C, PAIR:2 * PAIR], 0.0)
        mrb = jnp.where(incl, sm[C:2 * C, 0:PAIR], 0.0)
        mrk = jnp.where(incl, sm[C:2 * C, PAIR:2 * PAIR], 0.0)
        d1 = jnp.where(dblk, lab, 0.0)
        nn = lab - d1
        d2 = pm(d1, d1)
        d4 = pm(d2, d2)
        d8 = pm(d4, d4)
        td = pm(pm(eye + d1, eye + d2), pm(eye + d4, eye + d8))
        e1 = pm(td, nn)
        e2 = pm(e1, e1)
        tinv = pm(pm(eye + e1, eye + e2), td)
        x0 = pm(lak, v_p)
        pp = pm(tinv, at)
        u0 = pm(tinv, x0)
        s0 = s_scr[pi]
        u = _mm_nt(pp, s0) + u0
        o = _mm_nt(rt, s0) + pm(mrb, u) + pm(mrk, v_p)
        upd = _mm(jnp.concatenate([u, v_p], axis=0).T, jnp.concatenate([bt * ge, kt * ge], axis=0))
        s_scr[pi] = s0 * ge + jnp.where(bd, upd, 0.0)
        o = o[0:ci, :]
        om = _mm(o, bd_f) * (1.0 / HEAD)
        oc = o - om
        ov = _mm(oc * oc, bd_f) * (1.0 / HEAD)
        bonus = _mm(rk2[:, sl], bd_f) * v[:, sl]
        y = oc * lax.rsqrt(ov + GN_EPS) * lng_ref[:, sl] + lnb_ref[:, sl] + bonus
        o_ref[:, sl] = y * gate[:, sl]

    @pl.when(c == pl.num_programs(1) - 1)
    def _():
        for h in range(2 * n_pairs):
            lo = (h % 2) * HEAD
            nwkv_ref[0, h] = s_scr[h // 2, lo:lo + HEAD, lo:lo + HEAD]


def _rwkv_branch(proj_b, shift, wkv, prm, n_seq, seq_len, ci):
    T, pw = proj_b.shape
    n_heads = wkv.shape[1]
    bw = n_heads * HEAD
    lw = prm["w2a2"].shape[0] // 2
    nc = seq_len // ci

    def full(a):
        return pl.BlockSpec(a.shape, lambda n, c: (0,) * a.ndim)

    names = ("mu", "w0", "w2a2", "a0", "g2", "kk", "ka", "rk", "lng", "lnb")
    return pl.pallas_call(
        functools.partial(_rwkv_kernel, ci=ci, bw=bw, lw=lw),
        grid=(n_seq, nc),
        in_specs=[pl.BlockSpec((ci, pw), lambda n, c: (n * nc + c, 0)),
                  pl.BlockSpec((1, 1, pw), lambda n, c: (n, 0, 0)),
                  pl.BlockSpec((1, n_heads, HEAD, HEAD), lambda n, c: (n, 0, 0, 0))]
                 + [full(prm[k]) for k in names],
        out_specs=[pl.BlockSpec((ci, bw), lambda n, c: (n * nc + c, 0)),
                   pl.BlockSpec((1, 1, pw), lambda n, c: (n, 0, 0)),
                   pl.BlockSpec((1, n_heads, HEAD, HEAD), lambda n, c: (n, 0, 0, 0))],
        out_shape=[jax.ShapeDtypeStruct((T, bw), F32),
                   jax.ShapeDtypeStruct((n_seq, 1, pw), F32),
                   jax.ShapeDtypeStruct((n_seq, n_heads, HEAD, HEAD), F32)],
        scratch_shapes=[pltpu.VMEM((n_heads // 2, PAIR, PAIR), F32), pltpu.VMEM((8, pw), F32)],
        compiler_params=_cparams(("parallel", "arbitrary")),
        name="rwkv_branch",
    )(proj_b, shift.reshape(n_seq, 1, pw), wkv, *[prm[k] for k in names])


def _outproj_kernel(a_ref, b_ref, x_ref, wa_ref, wb_ref, g_ref, bb_ref, o_ref, *, alpha):
    mix = jnp.dot(a_ref[...].astype(BF16), wa_ref[...], preferred_element_type=F32)
    mix = mix + jnp.dot(b_ref[...].astype(BF16), wb_ref[...], preferred_element_type=F32)
    o_ref[...] = _ln(alpha * x_ref[...] + mix, g_ref[...], bb_ref[...], LN_EPS)


def _outproj(a, b, x, wa, wb, g, bb, alpha, tm):
    T, D = x.shape
    wdt = a.shape[1]
    vec = pl.BlockSpec((1, D), lambda i: (0, 0))
    return pl.pallas_call(
        functools.partial(_outproj_kernel, alpha=alpha),
        grid=(T // tm,),
        in_specs=[pl.BlockSpec((tm, wdt), lambda i: (i, 0)),
                  pl.BlockSpec((tm, wdt), lambda i: (i, 0)),
                  pl.BlockSpec((tm, D), lambda i: (i, 0)),
                  pl.BlockSpec((wdt, D), lambda i: (0, 0)),
                  pl.BlockSpec((wdt, D), lambda i: (0, 0)),
                  vec, vec],
        out_specs=pl.BlockSpec((tm, D), lambda i: (i, 0)),
        out_shape=jax.ShapeDtypeStruct((T, D), F32),
        compiler_params=_cparams(("parallel",)),
        name="out_proj_ln",
    )(a, b, x, wa, wb, g, bb)


def _ffn_kernel(x_ref, wg_ref, wu_ref, wd_ref, g_ref, bb_ref, o_ref, acc_ref, xb_ref, *, alpha):
    j = pl.program_id(1)

    @pl.when(j == 0)
    def _():
        xb_ref[...] = x_ref[...].astype(BF16)
        acc_ref[...] = jnp.zeros_like(acc_ref)

    xb = xb_ref[...]
    hg = jnp.dot(xb, wg_ref[...], preferred_element_type=F32)
    hu = jnp.dot(xb, wu_ref[...], preferred_element_type=F32)
    h = (hg * _sigmoid(hg) * hu).astype(BF16)
    acc_ref[...] += jnp.dot(h, wd_ref[...], preferred_element_type=F32)

    @pl.when(j == pl.num_programs(1) - 1)
    def _():
        o_ref[...] = _ln(alpha * x_ref[...] + acc_ref[...], g_ref[...], bb_ref[...], LN_EPS)


def _ffn(x, wg, wu, wd, g, bb, alpha, tm, tf):
    T, D = x.shape
    Fd = wg.shape[1]
    vec = pl.BlockSpec((1, D), lambda i, j: (0, 0))
    return pl.pallas_call(
        functools.partial(_ffn_kernel, alpha=alpha),
        grid=(T // tm, Fd // tf),
        in_specs=[pl.BlockSpec((tm, D), lambda i, j: (i, 0)),
                  pl.BlockSpec((D, tf), lambda i, j: (0, j)),
                  pl.BlockSpec((D, tf), lambda i, j: (0, j)),
                  pl.BlockSpec((tf, D), lambda i, j: (j, 0)),
                  vec, vec],
        out_specs=pl.BlockSpec((tm, D), lambda i, j: (i, 0)),
        out_shape=jax.ShapeDtypeStruct((T, D), F32),
        scratch_shapes=[pltpu.VMEM((tm, D), F32), pltpu.VMEM((tm, D), BF16)],
        compiler_params=_cparams(("parallel", "arbitrary")),
        name="ffn_ln",
    )(x, wg, wu, wd, g, bb)


def _pool_kernel(x_ref, st_ref, w_ref, sc_ref, g_ref, bb_ref, o_ref, nst_ref, ext_ref, *, tl, start, alpha):
    c = pl.program_id(1)

    @pl.when(c == 0)
    def _():
        ext_ref[POOL_PAD - POOL_HIST:POOL_PAD, :] = st_ref[0]

    x = x_ref[...]
    D = x.shape[1]
    gw = D // len(POOL_WINDOWS)
    ext_ref[POOL_PAD:POOL_PAD + tl, :] = x
    pos = start + c * tl + lax.broadcasted_iota(jnp.int32, (tl, 1), 0)
    outs = []
    for gi, wdw in enumerate(POOL_WINDOWS):
        lo = gi * gw
        ssum = x[:, lo:lo + gw]
        for j in range(1, wdw):
            ssum = ssum + ext_ref[POOL_PAD - j:POOL_PAD - j + tl, lo:lo + gw]
        cnt = jnp.minimum(wdw, pos + 1).astype(F32)
        pooled = ssum / cnt - x[:, lo:lo + gw]
        outs.append(jnp.dot(pooled.astype(BF16), w_ref[gi], preferred_element_type=F32))
    y = jnp.concatenate(outs, axis=-1) * sc_ref[...]
    o_ref[...] = _ln(alpha * x + y, g_ref[...], bb_ref[...], LN_EPS)
    tail = ext_ref[tl + POOL_PAD - POOL_HIST:tl + POOL_PAD, :]
    ext_ref[POOL_PAD - POOL_HIST:POOL_PAD, :] = tail
    nst_ref[0] = tail


def _pool_mix(x, state, w, scale, g, bb, n_seq, seq_len, tl, start, alpha):
    T, D = x.shape
    nc = seq_len // tl
    vec = pl.BlockSpec((1, D), lambda n, c: (0, 0))
    return pl.pallas_call(
        functools.partial(_pool_kernel, tl=tl, start=start, alpha=alpha),
        grid=(n_seq, nc),
        in_specs=[pl.BlockSpec((tl, D), lambda n, c: (n * nc + c, 0)),
                  pl.BlockSpec((1, POOL_HIST, D), lambda n, c: (n, 0, 0)),
                  pl.BlockSpec(w.shape, lambda n, c: (0, 0, 0)),
                  vec, vec, vec],
        out_specs=[pl.BlockSpec((tl, D), lambda n, c: (n * nc + c, 0)),
                   pl.BlockSpec((1, POOL_HIST, D), lambda n, c: (n, 0, 0))],
        out_shape=[jax.ShapeDtypeStruct((T, D), F32),
                   jax.ShapeDtypeStruct((n_seq, POOL_HIST, D), F32)],
        scratch_shapes=[pltpu.VMEM((POOL_PAD + tl, D), F32)],
        compiler_params=_cparams(("parallel", "arbitrary")),
        name="pool_mix_ln",
    )(x, state, w, scale, g, bb)


def _trunk(x3, conv_prev, shift_prev, wkv_prev, pool_prev, start, prm, tiles):
    n_seq, seq_len, D = x3.shape
    x = x3.reshape(n_seq * seq_len, D)
    alpha = prm["alpha"]
    tm = tiles["tm"]
    proj_a, proj_b = _inproj(x, prm["w_in_a"], prm["w_in_b"], tm)
    a_out, new_conv = _conv_branch(proj_a, conv_prev, prm["conv_w"], prm["conv_b"], prm["conv_ln_g"],
                                   prm["conv_ln_b"], n_seq, seq_len, tiles["conv_tl"])
    b_out, new_shift, new_wkv = _rwkv_branch(proj_b, shift_prev, wkv_prev, prm["rwkv"], n_seq, seq_len,
                                             tiles["rwkv_ci"])
    x = _outproj(a_out, b_out, x, prm["w_out_a"], prm["w_out_b"], prm["ln_mix_g"][0], prm["ln_mix_b"][0], alpha, tm)
    x = _ffn(x, prm["ffn_gate"][0], prm["ffn_up"][0], prm["ffn_down"][0], prm["ln_ffn_g"][0], prm["ln_ffn_b"][0],
             alpha, tm, tiles["tf"])
    x, new_pool = _pool_mix(x, pool_prev, prm["pool_w"], prm["pool_scale"], prm["ln_mix_g"][1], prm["ln_mix_b"][1],
                            n_seq, seq_len, tiles["pool_tl"], start, alpha)
    x = _ffn(x, prm["ffn_gate"][1], prm["ffn_up"][1], prm["ffn_down"][1], prm["ln_ffn_g"][1], prm["ln_ffn_b"][1],
             alpha, tm, tiles["tf"])
    return (x.reshape(n_seq, seq_len, D), new_conv[None], new_shift.reshape(1, n_seq, -1), new_wkv[None],
            new_pool[None])


def kernel(x_prompt, x_sample, state_conv, state_shift, state_wkv, state_pool, w_in, conv_w, conv_b, conv_ln_g, conv_ln_b, rwkv_mu, rwkv_w0, rwkv_w2, rwkv_a0, rwkv_a2, rwkv_g2, rwkv_kk, rwkv_ka, rwkv_rk, rwkv_lnx_g, rwkv_lnx_b, w_out, pool_w, pool_scale, ln_mix_g, ln_mix_b, ffn_gate, ffn_up, ffn_down, ln_ffn_g, ln_ffn_b):
    depth = ln_mix_g.shape[0]
    assert depth == 2 and w_in.shape[0] == 1 and pool_w.shape[0] == 1
    B, L, D = x_prompt.shape
    Bs, Ls, _ = x_sample.shape
    a_w = conv_w.shape[2]
    b_w = rwkv_w0.shape[1]
    n_heads = b_w // HEAD
    alpha = float((2 * depth) ** 0.25)

    row = lambda a: a.reshape(1, -1)
    prm = dict(
        alpha=alpha,
        w_in_a=w_in[0, :, :2 * a_w].astype(BF16), w_in_b=w_in[0, :, 2 * a_w:].astype(BF16),
        conv_w=conv_w[0], conv_b=row(conv_b[0]), conv_ln_g=row(conv_ln_g[0]), conv_ln_b=row(conv_ln_b[0]),
        rwkv=dict(mu=row(rwkv_mu[0]), w0=row(rwkv_w0[0]),
                  w2a2=jnp.concatenate([rwkv_w2[0], rwkv_a2[0]], axis=0), a0=row(rwkv_a0[0]), g2=rwkv_g2[0],
                  kk=row(rwkv_kk[0]), ka=row(rwkv_ka[0]), rk=row(rwkv_rk[0]),
                  lng=row(rwkv_lnx_g[0]), lnb=row(rwkv_lnx_b[0])),
        w_out_a=w_out[0, :a_w].astype(BF16), w_out_b=w_out[0, a_w:].astype(BF16),
        pool_w=pool_w[0].astype(BF16), pool_scale=row(pool_scale[0]),
        ln_mix_g=[row(ln_mix_g[i]) for i in range(depth)], ln_mix_b=[row(ln_mix_b[i]) for i in range(depth)],
        ffn_gate=[ffn_gate[i].astype(BF16) for i in range(depth)],
        ffn_up=[ffn_up[i].astype(BF16) for i in range(depth)],
        ffn_down=[ffn_down[i].astype(BF16) for i in range(depth)],
        ln_ffn_g=[row(ln_ffn_g[i]) for i in range(depth)], ln_ffn_b=[row(ln_ffn_b[i]) for i in range(depth)],
    )
    dt = x_prompt.dtype
    z_conv = jnp.zeros((B, CONV_HIST, a_w), dt)
    z_shift = jnp.zeros((B, state_shift.shape[-1]), dt)
    z_wkv = jnp.zeros((B, n_heads, HEAD, HEAD), state_wkv.dtype)
    z_pool = jnp.zeros((B, POOL_HIST, D), dt)
    tiles_p = dict(tm=1024, tf=256, conv_tl=256, rwkv_ci=CHUNK, pool_tl=512)
    tiles_s = dict(tm=1024, tf=256, conv_tl=Ls, rwkv_ci=Ls, pool_tl=Ls)
    yp = _trunk(x_prompt, z_conv, z_shift, z_wkv, z_pool, 0, prm, tiles_p)
    ys = _trunk(x_sample, state_conv[0], state_shift[0], state_wkv[0], state_pool[0], PAST_LEN, prm, tiles_s)
    return (yp[0], ys[0], yp[1], yp[2], yp[3], yp[4], ys[1], ys[2], ys[3], ys[4])
```

```python
import functools

import jax
import jax.numpy as jnp
from jax import lax
from jax.experimental import pallas as pl
from jax.experimental.pallas import tpu as pltpu

F32 = jnp.float32
BF16 = jnp.bfloat16
HIGHEST = lax.Precision.HIGHEST

HEAD = 64
PAIR = 2 * HEAD
CHUNK = 64
SOLVE_BLOCK = 16
CONV_TAPS = 31
CONV_HIST = CONV_TAPS - 1
CONV_PAD = 32
POOL_WINDOWS = (2, 4, 8, 16)
POOL_HIST = max(POOL_WINDOWS) - 1
POOL_PAD = 16
PAST_LEN = 16384
LN_EPS = 1e-5
GN_EPS = 64e-5
VMEM_LIMIT = 56 * 1024 * 1024


def _ln(z, g, b, eps):
    mu = jnp.mean(z, axis=-1, keepdims=True)
    zc = z - mu
    var = jnp.mean(zc * zc, axis=-1, keepdims=True)
    return zc * lax.rsqrt(var + eps) * g + b


def _sigmoid(x):
    return 1.0 / (1.0 + jnp.exp(-x))


def _cparams(sem):
    return pltpu.CompilerParams(dimension_semantics=sem, vmem_limit_bytes=VMEM_LIMIT)


def _inproj_kernel(x_ref, wa_ref, wb_ref, oa_ref, ob_ref):
    xb = x_ref[...].astype(BF16)
    oa_ref[...] = jnp.dot(xb, wa_ref[...], preferred_element_type=F32)
    ob_ref[...] = jnp.dot(xb, wb_ref[...], preferred_element_type=F32)


def _inproj(x, wa, wb, tm):
    T, D = x.shape
    na, nb = wa.shape[1], wb.shape[1]
    return pl.pallas_call(
        _inproj_kernel,
        grid=(T // tm,),
        in_specs=[pl.BlockSpec((tm, D), lambda i: (i, 0)),
                  pl.BlockSpec((D, na), lambda i: (0, 0)),
                  pl.BlockSpec((D, nb), lambda i: (0, 0))],
        out_specs=[pl.BlockSpec((tm, na), lambda i: (i, 0)),
                   pl.BlockSpec((tm, nb), lambda i: (i, 0))],
        out_shape=[jax.ShapeDtypeStruct((T, na), F32), jax.ShapeDtypeStruct((T, nb), F32)],
        compiler_params=_cparams(("parallel",)),
        name="in_proj",
    )(x, wa, wb)


def _conv_kernel(p_ref, st_ref, w_ref, b_ref, g_ref, bb_ref, o_ref, nst_ref, ext_ref, *, tl, width):
    c = pl.program_id(1)

    @pl.when(c == 0)
    def _():
        ext_ref[CONV_PAD - CONV_HIST:CONV_PAD, :] = st_ref[0]

    p = p_ref[...]
    u = p[:, :width] * _sigmoid(p[:, width:])
    ext_ref[CONV_PAD:CONV_PAD + tl, :] = u
    acc = jnp.zeros((tl, width), F32) + b_ref[...]
    for j in range(CONV_TAPS):
        s = CONV_PAD - CONV_HIST + j
        acc = acc + ext_ref[s:s + tl, :] * w_ref[j:j + 1, :]
    y = _ln(acc, g_ref[...], bb_ref[...], LN_EPS)
    o_ref[...] = y * _sigmoid(y)
    tail = ext_ref[tl + CONV_PAD - CONV_HIST:tl + CONV_PAD, :]
    ext_ref[CONV_PAD - CONV_HIST:CONV_PAD, :] = tail
    nst_ref[0] = tail


def _conv_branch(proj_a, state, w, b, g, bb, n_seq, seq_len, tl):
    T, two_w = proj_a.shape
    width = two_w // 2
    nc = seq_len // tl
    vec = pl.BlockSpec((1, width), lambda n, c: (0, 0))
    return pl.pallas_call(
        functools.partial(_conv_kernel, tl=tl, width=width),
        grid=(n_seq, nc),
        in_specs=[pl.BlockSpec((tl, two_w), lambda n, c: (n * nc + c, 0)),
                  pl.BlockSpec((1, CONV_HIST, width), lambda n, c: (n, 0, 0)),
                  pl.BlockSpec((CONV_TAPS, width), lambda n, c: (0, 0)),
                  vec, vec, vec],
        out_specs=[pl.BlockSpec((tl, width), lambda n, c: (n * nc + c, 0)),
                   pl.BlockSpec((1, CONV_HIST, width), lambda n, c: (n, 0, 0))],
        out_shape=[jax.ShapeDtypeStruct((T, width), F32),
                   jax.ShapeDtypeStruct((n_seq, CONV_HIST, width), F32)],
        scratch_shapes=[pltpu.VMEM((CONV_PAD + tl, width), F32)],
        compiler_params=_cparams(("parallel", "arbitrary")),
        name="conv_branch",
    )(proj_a, state, w, b, g, bb)


def _split(x):
    hi = x.astype(BF16)
    return hi, (x - hi.astype(F32)).astype(BF16)


def _dotb(a, b):
    return jnp.dot(a, b, preferred_element_type=F32)


def _dotb_nt(a, b):
    return lax.dot_general(a, b, (((1,), (1,)), ((), ())), preferred_element_type=F32)


def _lhs3(x):
    hi, lo = _split(x)
    return jnp.concatenate([hi, hi, lo], axis=1)


def _rhs3(x):
    hi, lo = _split(x)
    return jnp.concatenate([hi, lo, hi], axis=0)


def _rhs3_nt(x):
    hi, lo = _split(x)
    return jnp.concatenate([hi, lo, hi], axis=1)


def _mm3(a, b):
    return _dotb(_lhs3(a), _rhs3(b))


def _rwkv_kernel(p_ref, sh_ref, wkv_ref, mu_ref, w0_ref, w2a2_ref, a0_ref, g2_ref, kk_ref, ka_ref, rk_ref,
                 lng_ref, lnb_ref, o_ref, nsh_ref, nwkv_ref, s_scr, prev_scr, *, ci, bw, lw):
    c = pl.program_id(1)
    n_pairs = bw // PAIR
    C = CHUNK

    @pl.when(c == 0)
    def _():
        prev_scr[0:1, :] = sh_ref[0]
        s_scr[...] = jnp.zeros_like(s_scr)
        for h in range(2 * n_pairs):
            lo = (h % 2) * HEAD
            s_scr[h // 2, lo:lo + HEAD, lo:lo + HEAD] = wkv_ref[0, h]

    p = p_ref[...]
    row = lax.broadcasted_iota(jnp.int32, (ci, 1), 0)
    prev = jnp.where(row == 0, prev_scr[0:1, :], pltpu.roll(p, 1, 0))
    prev_scr[0:1, :] = p[ci - 1:ci, :]
    nsh_ref[0] = p[ci - 1:ci, :]
    xm = p + mu_ref[...] * (prev - p)
    r = xm[:, 0:bw]
    k = xm[:, bw:2 * bw]
    v = xm[:, 2 * bw:3 * bw]
    dwa = xm[:, 3 * bw:3 * bw + 2 * lw]
    dg = xm[:, 3 * bw + 2 * lw:]
    lane_l = lax.broadcasted_iota(jnp.int32, (1, 2 * lw), 1)
    is_w = lane_l < lw
    w2a2_3 = _rhs3(w2a2_ref[...])
    wpre = w0_ref[...] + _dotb(_lhs3(jnp.where(is_w, jnp.tanh(dwa), 0.0)), w2a2_3)
    apre = a0_ref[...] + _dotb(_lhs3(jnp.where(is_w, 0.0, dwa)), w2a2_3)
    z = -wpre
    wlog = -(jnp.maximum(z, 0.0) + jnp.log1p(jnp.exp(-jnp.abs(z)))) - 0.5
    logw = -jnp.exp(wlog)
    a_lr = _sigmoid(apre)
    gate = _mm3(_sigmoid(dg), g2_ref[...])
    kkv = k * kk_ref[...]
    k2 = k * (1.0 + (a_lr - 1.0) * ka_ref[...])
    rk2 = r * k2 * rk_ref[...]

    lane = lax.broadcasted_iota(jnp.int32, (1, PAIR), 1)
    m_l = lane < HEAD
    rowp = lax.broadcasted_iota(jnp.int32, (PAIR, PAIR), 0)
    colp = lax.broadcasted_iota(jnp.int32, (PAIR, PAIR), 1)
    bd = (rowp >= HEAD) == (colp >= HEAD)
    bd_f = bd.astype(F32)
    t_i = lax.broadcasted_iota(jnp.int32, (C, PAIR), 0)
    s_i = lax.broadcasted_iota(jnp.int32, (C, PAIR), 1) & (HEAD - 1)
    strict = s_i < t_i
    incl = s_i <= t_i
    eye = (s_i == t_i).astype(F32)
    blk_bits = SOLVE_BLOCK.bit_length() - 1
    dblk = (s_i >> blk_bits) == (t_i >> blk_bits)
    tri = (lax.broadcasted_iota(jnp.int32, (C, C), 1) <= lax.broadcasted_iota(jnp.int32, (C, C), 0)).astype(F32)

    def pad(x):
        if ci == C:
            return x
        return jnp.concatenate([x, jnp.zeros((C - ci, x.shape[1]), F32)], axis=0)

    def vs(y):
        return jnp.concatenate([jnp.where(m_l, y, 0.0), jnp.where(m_l, 0.0, y)], axis=0)

    def vs3(y):
        hi, lo = _split(y)
        vh = vs(hi)
        return jnp.concatenate([vh, vs(lo), vh], axis=0)

    def vs3_nt(y):
        hi, lo = _split(y)
        vh = vs(hi)
        return jnp.concatenate([vh, vs(lo), vh], axis=1)

    def pm(a, y):
        return _dotb(_lhs3(a), vs3(y))

    bd2 = jnp.concatenate([bd_f, bd_f], axis=0).astype(BF16)

    def bdsum(x):
        hi, lo = _split(x)
        return _dotb(jnp.concatenate([hi, lo], axis=1), bd2)

    P = range(n_pairs)
    sl = [slice(i * PAIR, (i + 1) * PAIR) for i in P]
    logw_p = pad(logw)
    lw_hi, lw_lo = _split(logw_p)
    tri_b = tri.astype(BF16)
    cs = _dotb(jnp.concatenate([tri_b, tri_b], axis=1), jnp.concatenate([lw_hi, lw_lo], axis=0))
    g_in = jnp.exp(cs)
    g_ex = jnp.exp(cs - logw_p)
    g_inv = jnp.exp(-cs)
    g_end = g_in[C - 1:C, :]

    kk_n = []
    for i in P:
        kk_p = kkv[:, sl[i]]
        kk_n.append(kk_p / jnp.maximum(jnp.sqrt(bdsum(kk_p * kk_p)), 1e-12))
    v_p = [pad(v[:, sl[i]]) for i in P]
    at = [pad(-kk_n[i]) * g_ex[:, sl[i]] for i in P]
    rt = [pad(r[:, sl[i]]) * g_in[:, sl[i]] for i in P]
    bt = [pad(kk_n[i] * a_lr[:, sl[i]]) * g_inv[:, sl[i]] for i in P]
    kt = [pad(k2[:, sl[i]]) * g_inv[:, sl[i]] for i in P]
    sm = [_dotb_nt(_lhs3(jnp.concatenate([at[i], rt[i]], axis=0)),
                   jnp.concatenate([vs3_nt(bt[i]), vs3_nt(kt[i])], axis=0)) for i in P]
    lab = [jnp.where(strict, sm[i][0:C, 0:PAIR], 0.0) for i in P]
    lak = [jnp.where(strict, sm[i][0:C, PAIR:2 * PAIR], 0.0) for i in P]
    mrb = [jnp.where(incl, sm[i][C:2 * C, 0:PAIR], 0.0) for i in P]
    mrk = [jnp.where(incl, sm[i][C:2 * C, PAIR:2 * PAIR], 0.0) for i in P]
    d1 = [jnp.where(dblk, lab[i], 0.0) for i in P]
    nn = [lab[i] - d1[i] for i in P]
    d2 = [pm(d1[i], d1[i]) for i in P]
    d4 = [pm(d2[i], d2[i]) for i in P]
    t1 = [pm(eye + d1[i], eye + d2[i]) for i in P]
    d8 = [pm(d4[i], d4[i]) for i in P]
    t2 = [pm(eye + d4[i], eye + d8[i]) for i in P]
    td = [pm(t1[i], t2[i]) for i in P]
    e1 = [pm(td[i], nn[i]) for i in P]
    e2 = [pm(e1[i], e1[i]) for i in P]
    t3 = [pm(eye + e1[i], eye + e2[i]) for i in P]
    tinv3 = [_lhs3(pm(t3[i], td[i])) for i in P]
    v3 = [vs3(v_p[i]) for i in P]
    x0 = [_dotb(_lhs3(lak[i]), v3[i]) for i in P]
    pp = [_dotb(tinv3[i], vs3(at[i])) for i in P]
    u0 = [_dotb(tinv3[i], vs3(x0[i])) for i in P]
    s0 = [s_scr[i] for i in P]
    hs = [_dotb_nt(_lhs3(jnp.concatenate([pp[i], rt[i]], axis=0)), _rhs3_nt(s0[i])) for i in P]
    u = [hs[i][0:C] + u0[i] for i in P]
    o = [hs[i][C:2 * C] + _dotb(jnp.concatenate([_lhs3(mrb[i]), _lhs3(mrk[i])], axis=1),
                                jnp.concatenate([vs3(u[i]), v3[i]], axis=0)) for i in P]
    for i in P:
        ge = g_end[:, sl[i]]
        upd = _dotb(_lhs3(jnp.concatenate([u[i], v_p[i]], axis=0).T),
                    _rhs3(jnp.concatenate([bt[i] * ge, kt[i] * ge], axis=0)))
        s_scr[i] = s0[i] * ge + jnp.where(bd, upd, 0.0)
    for i in P:
        oi = o[i][0:ci, :]
        om = bdsum(oi) * (1.0 / HEAD)
        oc = oi - om
        ov = bdsum(oc * oc) * (1.0 / HEAD)
        bonus = bdsum(rk2[:, sl[i]]) * v[:, sl[i]]
        y = oc * lax.rsqrt(ov + GN_EPS) * lng_ref[:, sl[i]] + lnb_ref[:, sl[i]] + bonus
        o_ref[:, sl[i]] = y * gate[:, sl[i]]

    @pl.when(c == pl.num_programs(1) - 1)
    def _():
        for h in range(2 * n_pairs):
            lo = (h % 2) * HEAD
            nwkv_ref[0, h] = s_scr[h // 2, lo:lo + HEAD, lo:lo + HEAD]


def _rwkv_branch(proj_b, shift, wkv, prm, n_seq, seq_len, ci):
    T, pw = proj_b.shape
    n_heads = wkv.shape[1]
    bw = n_heads * HEAD
    lw = prm["w2a2"].shape[0] // 2
    nc = seq_len // ci

    def full(a):
        return pl.BlockSpec(a.shape, lambda n, c: (0,) * a.ndim)

    names = ("mu", "w0", "w2a2", "a0", "g2", "kk", "ka", "rk", "lng", "lnb")
    return pl.pallas_call(
        functools.partial(_rwkv_kernel, ci=ci, bw=bw, lw=lw),
        grid=(n_seq, nc),
        in_specs=[pl.BlockSpec((ci, pw), lambda n, c: (n * nc + c, 0)),
                  pl.BlockSpec((1, 1, pw), lambda n, c: (n, 0, 0)),
                  pl.BlockSpec((1, n_heads, HEAD, HEAD), lambda n, c: (n, 0, 0, 0))]
                 + [full(prm[k]) for k in names],
        out_specs=[pl.BlockSpec((ci, bw), lambda n, c: (n * nc + c, 0)),
                   pl.BlockSpec((1, 1, pw), lambda n, c: (n, 0, 0)),
                   pl.BlockSpec((1, n_heads, HEAD, HEAD), lambda n, c: (n, 0, 0, 0))],
        out_shape=[jax.ShapeDtypeStruct((T, bw), F32),
                   jax.ShapeDtypeStruct((n_seq, 1, pw), F32),
                   jax.ShapeDtypeStruct((n_seq, n_heads, HEAD, HEAD), F32)],
        scratch_shapes=[pltpu.VMEM((n_heads // 2, PAIR, PAIR), F32), pltpu.VMEM((8, pw), F32)],
        compiler_params=_cparams(("parallel", "arbitrary")),
        name="rwkv_branch",
    )(proj_b, shift.reshape(n_seq, 1, pw), wkv, *[prm[k] for k in names])


def _outproj_kernel(a_ref, b_ref, x_ref, wa_ref, wb_ref, g_ref, bb_ref, o_ref, *, alpha):
    mix = jnp.dot(a_ref[...].astype(BF16), wa_ref[...], preferred_element_type=F32)
    mix = mix + jnp.dot(b_ref[...].astype(BF16), wb_ref[...], preferred_element_type=F32)
    o_ref[...] = _ln(alpha * x_ref[...] + mix, g_ref[...], bb_ref[...], LN_EPS)


def _outproj(a, b, x, wa, wb, g, bb, alpha, tm):
    T, D = x.shape
    wdt = a.shape[1]
    vec = pl.BlockSpec((1, D), lambda i: (0, 0))
    return pl.pallas_call(
        functools.partial(_outproj_kernel, alpha=alpha),
        grid=(T // tm,),
        in_specs=[pl.BlockSpec((tm, wdt), lambda i: (i, 0)),
                  pl.BlockSpec((tm, wdt), lambda i: (i, 0)),
                  pl.BlockSpec((tm, D), lambda i: (i, 0)),
                  pl.BlockSpec((wdt, D), lambda i: (0, 0)),
                  pl.BlockSpec((wdt, D), lambda i: (0, 0)),
                  vec, vec],
        out_specs=pl.BlockSpec((tm, D), lambda i: (i, 0)),
        out_shape=jax.ShapeDtypeStruct((T, D), F32),
        compiler_params=_cparams(("parallel",)),
        name="out_proj_ln",
    )(a, b, x, wa, wb, g, bb)


def _ffn_kernel(x_ref, wg_ref, wu_ref, wd_ref, g_ref, bb_ref, o_ref, acc_ref, xb_ref, *, alpha):
    j = pl.program_id(1)

    @pl.when(j == 0)
    def _():
        xb_ref[...] = x_ref[...].astype(BF16)
        acc_ref[...] = jnp.zeros_like(acc_ref)

    xb = xb_ref[...]
    hg = jnp.dot(xb, wg_ref[...], preferred_element_type=F32)
    hu = jnp.dot(xb, wu_ref[...], preferred_element_type=F32)
    h = (hg * _sigmoid(hg) * hu).astype(BF16)
    acc_ref[...] += jnp.dot(h, wd_ref[...], preferred_element_type=F32)

    @pl.when(j == pl.num_programs(1) - 1)
    def _():
        o_ref[...] = _ln(alpha * x_ref[...] + acc_ref[...], g_ref[...], bb_ref[...], LN_EPS)


def _ffn(x, wg, wu, wd, g, bb, alpha, tm, tf):
    T, D = x.shape
    Fd = wg.shape[1]
    vec = pl.BlockSpec((1, D), lambda i, j: (0, 0))
    return pl.pallas_call(
        functools.partial(_ffn_kernel, alpha=alpha),
        grid=(T // tm, Fd // tf),
        in_specs=[pl.BlockSpec((tm, D), lambda i, j: (i, 0)),
                  pl.BlockSpec((D, tf), lambda i, j: (0, j)),
                  pl.BlockSpec((D, tf), lambda i, j: (0, j)),
                  pl.BlockSpec((tf, D), lambda i, j: (j, 0)),
                  vec, vec],
        out_specs=pl.BlockSpec((tm, D), lambda i, j: (i, 0)),
        out_shape=jax.ShapeDtypeStruct((T, D), F32),
        scratch_shapes=[pltpu.VMEM((tm, D), F32), pltpu.VMEM((tm, D), BF16)],
        compiler_params=_cparams(("parallel", "arbitrary")),
        name="ffn_ln",
    )(x, wg, wu, wd, g, bb)


def _pool_kernel(x_ref, st_ref, w_ref, sc_ref, g_ref, bb_ref, o_ref, nst_ref, ext_ref, *, tl, start, alpha):
    c = pl.program_id(1)

    @pl.when(c == 0)
    def _():
        ext_ref[POOL_PAD - POOL_HIST:POOL_PAD, :] = st_ref[0]

    x = x_ref[...]
    D = x.shape[1]
    gw = D // len(POOL_WINDOWS)
    ext_ref[POOL_PAD:POOL_PAD + tl, :] = x
    pos = start + c * tl + lax.broadcasted_iota(jnp.int32, (tl, 1), 0)
    outs = []
    for gi, wdw in enumerate(POOL_WINDOWS):
        lo = gi * gw
        ssum = x[:, lo:lo + gw]
        for j in range(1, wdw):
            ssum = ssum + ext_ref[POOL_PAD - j:POOL_PAD - j + tl, lo:lo + gw]
        cnt = jnp.minimum(wdw, pos + 1).astype(F32)
        pooled = ssum / cnt - x[:, lo:lo + gw]
        outs.append(jnp.dot(pooled.astype(BF16), w_ref[gi], preferred_element_type=F32))
    y = jnp.concatenate(outs, axis=-1) * sc_ref[...]
    o_ref[...] = _ln(alpha * x + y, g_ref[...], bb_ref[...], LN_EPS)
    tail = ext_ref[tl + POOL_PAD - POOL_HIST:tl + POOL_PAD, :]
    ext_ref[POOL_PAD - POOL_HIST:POOL_PAD, :] = tail
    nst_ref[0] = tail


def _pool_mix(x, state, w, scale, g, bb, n_seq, seq_len, tl, start, alpha):
    T, D = x.shape
    nc = seq_len // tl
    vec = pl.BlockSpec((1, D), lambda n, c: (0, 0))
    return pl.pallas_call(
        functools.partial(_pool_kernel, tl=tl, start=start, alpha=alpha),
        grid=(n_seq, nc),
        in_specs=[pl.BlockSpec((tl, D), lambda n, c: (n * nc + c, 0)),
                  pl.BlockSpec((1, POOL_HIST, D), lambda n, c: (n, 0, 0)),
                  pl.BlockSpec(w.shape, lambda n, c: (0, 0, 0)),
                  vec, vec, vec],
        out_specs=[pl.BlockSpec((tl, D), lambda n, c: (n * nc + c, 0)),
                   pl.BlockSpec((1, POOL_HIST, D), lambda n, c: (n, 0, 0))],
        out_shape=[jax.ShapeDtypeStruct((T, D), F32),
                   jax.ShapeDtypeStruct((n_seq, POOL_HIST, D), F32)],
        scratch_shapes=[pltpu.VMEM((POOL_PAD + tl, D), F32)],
        compiler_params=_cparams(("parallel", "arbitrary")),
        name="pool_mix_ln",
    )(x, state, w, scale, g, bb)


def _trunk(x3, conv_prev, shift_prev, wkv_prev, pool_prev, start, prm, tiles):
    n_seq, seq_len, D = x3.shape
    x = x3.reshape(n_seq * seq_len, D)
    alpha = prm["alpha"]
    tm = tiles["tm"]
    proj_a, proj_b = _inproj(x, prm["w_in_a"], prm["w_in_b"], tm)
    a_out, new_conv = _conv_branch(proj_a, conv_prev, prm["conv_w"], prm["conv_b"], prm["conv_ln_g"],
                                   prm["conv_ln_b"], n_seq, seq_len, tiles["conv_tl"])
    b_out, new_shift, new_wkv = _rwkv_branch(proj_b, shift_prev, wkv_prev, prm["rwkv"], n_seq, seq_len,
                                             tiles["rwkv_ci"])
    x = _outproj(a_out, b_out, x, prm["w_out_a"], prm["w_out_b"], prm["ln_mix_g"][0], prm["ln_mix_b"][0], alpha, tm)
    x = _ffn(x, prm["ffn_gate"][0], prm["ffn_up"][0], prm["ffn_down"][0], prm["ln_ffn_g"][0], prm["ln_ffn_b"][0],
             alpha, tm, tiles["tf"])
    x, new_pool = _pool_mix(x, pool_prev, prm["pool_w"], prm["pool_scale"], prm["ln_mix_g"][1], prm["ln_mix_b"][1],
                            n_seq, seq_len, tiles["pool_tl"], start, alpha)
    x = _ffn(x, prm["ffn_gate"][1], prm["ffn_up"][1], prm["ffn_down"][1], prm["ln_ffn_g"][1], prm["ln_ffn_b"][1],
             alpha, tm, tiles["tf"])
    return (x.reshape(n_seq, seq_len, D), new_conv[None], new_shift.reshape(1, n_seq, -1), new_wkv[None],
            new_pool[None])


def kernel(x_prompt, x_sample, state_conv, state_shift, state_wkv, state_pool, w_in, conv_w, conv_b, conv_ln_g, conv_ln_b, rwkv_mu, rwkv_w0, rwkv_w2, rwkv_a0, rwkv_a2, rwkv_g2, rwkv_kk, rwkv_ka, rwkv_rk, rwkv_lnx_g, rwkv_lnx_b, w_out, pool_w, pool_scale, ln_mix_g, ln_mix_b, ffn_gate, ffn_up, ffn_down, ln_ffn_g, ln_ffn_b):
    depth = ln_mix_g.shape[0]
    assert depth == 2 and w_in.shape[0] == 1 and pool_w.shape[0] == 1
    B, L, D = x_prompt.shape
    Bs, Ls, _ = x_sample.shape
    a_w = conv_w.shape[2]
    b_w = rwkv_w0.shape[1]
    n_heads = b_w // HEAD
    alpha = float((2 * depth) ** 0.25)

    row = lambda a: a.reshape(1, -1)
    prm = dict(
        alpha=alpha,
        w_in_a=w_in[0, :, :2 * a_w].astype(BF16), w_in_b=w_in[0, :, 2 * a_w:].astype(BF16),
        conv_w=conv_w[0], conv_b=row(conv_b[0]), conv_ln_g=row(conv_ln_g[0]), conv_ln_b=row(conv_ln_b[0]),
        rwkv=dict(mu=row(rwkv_mu[0]), w0=row(rwkv_w0[0]),
                  w2a2=jnp.concatenate([rwkv_w2[0], rwkv_a2[0]], axis=0), a0=row(rwkv_a0[0]), g2=rwkv_g2[0],
                  kk=row(rwkv_kk[0]), ka=row(rwkv_ka[0]), rk=row(rwkv_rk[0]),
                  lng=row(rwkv_lnx_g[0]), lnb=row(rwkv_lnx_b[0])),
        w_out_a=w_out[0, :a_w].astype(BF16), w_out_b=w_out[0, a_w:].astype(BF16),
        pool_w=pool_w[0].astype(BF16), pool_scale=row(pool_scale[0]),
        ln_mix_g=[row(ln_mix_g[i]) for i in range(depth)], ln_mix_b=[row(ln_mix_b[i]) for i in range(depth)],
        ffn_gate=[ffn_gate[i].astype(BF16) for i in range(depth)],
        ffn_up=[ffn_up[i].astype(BF16) for i in range(depth)],
        ffn_down=[ffn_down[i].astype(BF16) for i in range(depth)],
        ln_ffn_g=[row(ln_ffn_g[i]) for i in range(depth)], ln_ffn_b=[row(ln_ffn_b[i]) for i in range(depth)],
    )
    dt = x_prompt.dtype
    z_conv = jnp.zeros((B, CONV_HIST, a_w), dt)
    z_shift = jnp.zeros((B, state_shift.shape[-1]), dt)
    z_wkv = jnp.zeros((B, n_heads, HEAD, HEAD), state_wkv.dtype)
    z_pool = jnp.zeros((B, POOL_HIST, D), dt)
    tiles_p = dict(tm=1024, tf=256, conv_tl=256, rwkv_ci=CHUNK, pool_tl=512)
    tiles_s = dict(tm=1024, tf=256, conv_tl=Ls, rwkv_ci=Ls, pool_tl=Ls)
    yp = _trunk(x_prompt, z_conv, z_shift, z_wkv, z_pool, 0, prm, tiles_p)
    ys = _trunk(x_sample, state_conv[0], state_shift[0], state_wkv[0], state_pool[0], PAST_LEN, prm, tiles_s)
    return (yp[0], ys[0], yp[1], yp[2], yp[3], yp[4], ys[1], ys[2], ys[3], ys[4])
```

```python
import functools

import jax
import jax.numpy as jnp
from jax import lax
from jax.experimental import pallas as pl
from jax.experimental.pallas import tpu as pltpu

F32 = jnp.float32
BF16 = jnp.bfloat16

HEAD = 64
PAIR = 2 * HEAD
CHUNK = 64
SOLVE_BLOCK = 16
CONV_TAPS = 31
CONV_HIST = CONV_TAPS - 1
CONV_PAD = 32
POOL_WINDOWS = (2, 4, 8, 16)
POOL_HIST = max(POOL_WINDOWS) - 1
POOL_PAD = 16
PAST_LEN = 16384
LN_EPS = 1e-5
GN_EPS = 64e-5
VMEM_LIMIT = 56 * 1024 * 1024


def _ln(z, g, b, eps):
    mu = jnp.mean(z, axis=-1, keepdims=True)
    zc = z - mu
    var = jnp.mean(zc * zc, axis=-1, keepdims=True)
    return zc * lax.rsqrt(var + eps) * g + b


def _sigmoid(x):
    return 1.0 / (1.0 + jnp.exp(-x))


def _cparams(sem):
    return pltpu.CompilerParams(dimension_semantics=sem, vmem_limit_bytes=VMEM_LIMIT)


def _log2(n):
    assert n & (n - 1) == 0
    return n.bit_length() - 1


def _inproj_kernel(x_ref, wa_ref, wb_ref, oa_ref, ob_ref):
    xb = x_ref[...].astype(BF16)
    oa_ref[...] = jnp.dot(xb, wa_ref[...], preferred_element_type=F32)
    ob_ref[...] = jnp.dot(xb, wb_ref[...], preferred_element_type=F32)


def _inproj(x, wa, wb, tm):
    T, D = x.shape
    na, nb = wa.shape[1], wb.shape[1]
    return pl.pallas_call(
        _inproj_kernel,
        grid=(T // tm,),
        in_specs=[pl.BlockSpec((tm, D), lambda i: (i, 0)),
                  pl.BlockSpec((D, na), lambda i: (0, 0)),
                  pl.BlockSpec((D, nb), lambda i: (0, 0))],
        out_specs=[pl.BlockSpec((tm, na), lambda i: (i, 0)),
                   pl.BlockSpec((tm, nb), lambda i: (i, 0))],
        out_shape=[jax.ShapeDtypeStruct((T, na), F32), jax.ShapeDtypeStruct((T, nb), F32)],
        compiler_params=_cparams(("parallel",)),
        name="in_proj",
    )(x, wa, wb)


def _conv_kernel(p_ref, st_ref, w_ref, b_ref, g_ref, bb_ref, o_ref, nst_ref, ext_ref, *, tl, width):
    c = pl.program_id(1)

    @pl.when(c == 0)
    def _():
        ext_ref[:, CONV_PAD - CONV_HIST:CONV_PAD, :] = st_ref[...]

    p = p_ref[...]
    u = p[..., :width] * _sigmoid(p[..., width:])
    ext_ref[:, CONV_PAD:CONV_PAD + tl, :] = u
    acc = jnp.zeros(u.shape, F32) + b_ref[...]
    for j in range(CONV_TAPS):
        s = CONV_PAD - CONV_HIST + j
        acc = acc + ext_ref[:, s:s + tl, :] * w_ref[j:j + 1, :]
    y = _ln(acc, g_ref[...], bb_ref[...], LN_EPS)
    o_ref[...] = y * _sigmoid(y)
    tail = ext_ref[:, tl + CONV_PAD - CONV_HIST:tl + CONV_PAD, :]
    ext_ref[:, CONV_PAD - CONV_HIST:CONV_PAD, :] = tail
    nst_ref[...] = tail


def _conv_branch(proj_a, state, w, b, g, bb, nb, tl):
    n_seq, seq_len, two_w = proj_a.shape
    width = two_w // 2
    vec = pl.BlockSpec((1, width), lambda n, c: (0, 0))
    return pl.pallas_call(
        functools.partial(_conv_kernel, tl=tl, width=width),
        grid=(n_seq // nb, seq_len // tl),
        in_specs=[pl.BlockSpec((nb, tl, two_w), lambda n, c: (n, c, 0)),
                  pl.BlockSpec((nb, CONV_HIST, width), lambda n, c: (n, 0, 0)),
                  pl.BlockSpec((CONV_TAPS, width), lambda n, c: (0, 0)),
                  vec, vec, vec],
        out_specs=[pl.BlockSpec((nb, tl, width), lambda n, c: (n, c, 0)),
                   pl.BlockSpec((nb, CONV_HIST, width), lambda n, c: (n, 0, 0))],
        out_shape=[jax.ShapeDtypeStruct((n_seq, seq_len, width), F32),
                   jax.ShapeDtypeStruct((n_seq, CONV_HIST, width), F32)],
        scratch_shapes=[pltpu.VMEM((nb, CONV_PAD + tl, width), F32)],
        compiler_params=_cparams(("parallel", "arbitrary")),
        name="conv_branch",
    )(proj_a, state, w, b, g, bb)


def _split(x):
    hi = x.astype(BF16)
    return hi, (x - hi.astype(F32)).astype(BF16)


def _dotb(a, b):
    return jnp.dot(a, b, preferred_element_type=F32)


def _dotb_nt(a, b):
    return lax.dot_general(a, b, (((1,), (1,)), ((), ())), preferred_element_type=F32)


def _lhs3(x):
    hi, lo = _split(x)
    return jnp.concatenate([hi, hi, lo], axis=1)


def _rhs3(x):
    hi, lo = _split(x)
    return jnp.concatenate([hi, lo, hi], axis=0)


def _rhs3_nt(x):
    hi, lo = _split(x)
    return jnp.concatenate([hi, lo, hi], axis=1)


def _mm3(a, b):
    return _dotb(_lhs3(a), _rhs3(b))


def _rwkv_kernel(p_ref, sh_ref, wkv_ref, mu_ref, w0_ref, w2a2_ref, a0_ref, g2_ref, kk_ref, ka_ref, rk_ref,
                 lng_ref, lnb_ref, o_ref, nsh_ref, nwkv_ref, s_scr, prev_scr, *, ns, ts, bw, lw):
    c = pl.program_id(1)
    n_pairs = bw // PAIR
    C = CHUNK
    assert ns * ts == C and HEAD == C
    ts_bits = _log2(ts)
    sb = min(SOLVE_BLOCK, ts)
    sb_bits = _log2(sb)

    @pl.when(c == 0)
    def _():
        prev_scr[...] = sh_ref[...]
        s_scr[...] = jnp.zeros_like(s_scr)
        for j in range(ns):
            for h in range(2 * n_pairs):
                lo = (h % 2) * HEAD
                s_scr[j, h // 2, lo:lo + HEAD, lo:lo + HEAD] = wkv_ref[j, h]

    p = p_ref[...]
    pw = p.shape[1]
    row = lax.broadcasted_iota(jnp.int32, (C, 1), 0)
    first = jnp.concatenate([jnp.broadcast_to(prev_scr[j], (ts, pw)) for j in range(ns)], axis=0)
    prev = jnp.where((row & (ts - 1)) == 0, first, pltpu.roll(p, 1, 0))
    for j in range(ns):
        last = p[(j + 1) * ts - 1:(j + 1) * ts, :]
        prev_scr[j] = last
        nsh_ref[j] = last
    xm = p + mu_ref[...] * (prev - p)
    r = xm[:, 0:bw]
    k = xm[:, bw:2 * bw]
    v = xm[:, 2 * bw:3 * bw]
    dwa = xm[:, 3 * bw:3 * bw + 2 * lw]
    dg = xm[:, 3 * bw + 2 * lw:]
    lane_l = lax.broadcasted_iota(jnp.int32, (1, 2 * lw), 1)
    is_w = lane_l < lw
    w2a2_3 = _rhs3(w2a2_ref[...])
    wpre = w0_ref[...] + _dotb(_lhs3(jnp.where(is_w, jnp.tanh(dwa), 0.0)), w2a2_3)
    apre = a0_ref[...] + _dotb(_lhs3(jnp.where(is_w, 0.0, dwa)), w2a2_3)
    z = -wpre
    wlog = -(jnp.maximum(z, 0.0) + jnp.log1p(jnp.exp(-jnp.abs(z)))) - 0.5
    logw = -jnp.exp(wlog)
    a_lr = _sigmoid(apre)
    gate = _mm3(_sigmoid(dg), g2_ref[...])
    kkv = k * kk_ref[...]
    k2 = k * (1.0 + (a_lr - 1.0) * ka_ref[...])
    rk2 = r * k2 * rk_ref[...]

    lane = lax.broadcasted_iota(jnp.int32, (1, PAIR), 1)
    m_l = lane < HEAD
    rowp = lax.broadcasted_iota(jnp.int32, (PAIR, PAIR), 0)
    colp = lax.broadcasted_iota(jnp.int32, (PAIR, PAIR), 1)
    bd = (rowp >= HEAD) == (colp >= HEAD)
    bd2 = jnp.concatenate([bd, bd], axis=0).astype(BF16)
    t_i = lax.broadcasted_iota(jnp.int32, (C, PAIR), 0)
    s_i = lax.broadcasted_iota(jnp.int32, (C, PAIR), 1) & (HEAD - 1)
    same = (s_i >> ts_bits) == (t_i >> ts_bits)
    strict = same & (s_i < t_i)
    incl = same & (s_i <= t_i)
    eye = (s_i == t_i).astype(F32)
    dblk = (s_i >> sb_bits) == (t_i >> sb_bits)
    t_c = lax.broadcasted_iota(jnp.int32, (C, C), 0)
    s_c = lax.broadcasted_iota(jnp.int32, (C, C), 1)
    same_c = (s_c >> ts_bits) == (t_c >> ts_bits)
    tri_b = (same_c & (s_c <= t_c)).astype(BF16)
    ones_b = same_c.astype(BF16)

    def vs(y):
        return jnp.concatenate([jnp.where(m_l, y, 0.0), jnp.where(m_l, 0.0, y)], axis=0)

    def vs3(y):
        hi, lo = _split(y)
        vh = vs(hi)
        return jnp.concatenate([vh, vs(lo), vh], axis=0)

    def vs3_nt(y):
        hi, lo = _split(y)
        vh = vs(hi)
        return jnp.concatenate([vh, vs(lo), vh], axis=1)

    def pm(a, y):
        return _dotb(_lhs3(a), vs3(y))

    def bdsum(x):
        hi, lo = _split(x)
        return _dotb(jnp.concatenate([hi, lo], axis=1), bd2)

    P = range(n_pairs)
    sl = [slice(i * PAIR, (i + 1) * PAIR) for i in P]

    def neumann(m, n_terms):
        factors = [[eye + m[i] for i in P]]
        pw_ = m
        for _ in range(_log2(n_terms) - 1):
            pw_ = [pm(pw_[i], pw_[i]) for i in P]
            factors.append([eye + pw_[i] for i in P])
        while len(factors) > 1:
            nxt = [[pm(factors[q][i], factors[q + 1][i]) for i in P] for q in range(0, len(factors) - 1, 2)]
            if len(factors) % 2:
                nxt.append(factors[-1])
            factors = nxt
        return factors[0]

    lw_hi, lw_lo = _split(logw)
    lw2 = jnp.concatenate([lw_hi, lw_lo], axis=0)
    cs = _dotb(jnp.concatenate([tri_b, tri_b], axis=1), lw2)
    g_in = jnp.exp(cs)
    g_ex = jnp.exp(cs - logw)
    g_inv = jnp.exp(-cs)
    g_end = jnp.exp(_dotb(jnp.concatenate([ones_b, ones_b], axis=1), lw2))

    sums = [bdsum(jnp.concatenate([kkv[:, sl[i]] * kkv[:, sl[i]], rk2[:, sl[i]]], axis=0)) for i in P]
    kk_n = [kkv[:, sl[i]] / jnp.maximum(jnp.sqrt(sums[i][0:C]), 1e-12) for i in P]
    v_p = [v[:, sl[i]] for i in P]
    at = [-kk_n[i] * g_ex[:, sl[i]] for i in P]
    rt = [r[:, sl[i]] * g_in[:, sl[i]] for i in P]
    bt = [kk_n[i] * a_lr[:, sl[i]] * g_inv[:, sl[i]] for i in P]
    kt = [k2[:, sl[i]] * g_inv[:, sl[i]] for i in P]
    sm = [_dotb_nt(_lhs3(jnp.concatenate([at[i], rt[i]], axis=0)),
                   jnp.concatenate([vs3_nt(bt[i]), vs3_nt(kt[i])], axis=0)) for i in P]
    lab = [jnp.where(strict, sm[i][0:C, 0:PAIR], 0.0) for i in P]
    lak = [jnp.where(strict, sm[i][0:C, PAIR:2 * PAIR], 0.0) for i in P]
    mrb = [jnp.where(incl, sm[i][C:2 * C, 0:PAIR], 0.0) for i in P]
    mrk = [jnp.where(incl, sm[i][C:2 * C, PAIR:2 * PAIR], 0.0) for i in P]
    d1 = [jnp.where(dblk, lab[i], 0.0) for i in P]
    tinv = neumann(d1, sb)
    if sb < ts:
        e1 = [pm(tinv[i], lab[i] - d1[i]) for i in P]
        te = neumann(e1, ts // sb)
        tinv = [pm(te[i], tinv[i]) for i in P]
    tinv3 = [_lhs3(tinv[i]) for i in P]
    v3 = [vs3(v_p[i]) for i in P]
    x0 = [_dotb(_lhs3(lak[i]), v3[i]) for i in P]
    s0 = [[s_scr[j, i] for i in P] for j in range(ns)]
    a_s, r_s = [], []
    for i in P:
        parts = []
        for j in range(ns):
            rows = slice(j * ts, (j + 1) * ts)
            parts.append(_dotb_nt(_lhs3(jnp.concatenate([at[i][rows], rt[i][rows]], axis=0)), _rhs3_nt(s0[j][i])))
        a_s.append(jnp.concatenate([q[0:ts] for q in parts], axis=0))
        r_s.append(jnp.concatenate([q[ts:2 * ts] for q in parts], axis=0))
    u = [_dotb(tinv3[i], vs3(a_s[i] + x0[i])) for i in P]
    o = [r_s[i] + _dotb(jnp.concatenate([_lhs3(mrb[i]), _lhs3(mrk[i])], axis=1),
                        jnp.concatenate([vs3(u[i]), v3[i]], axis=0)) for i in P]
    col_seq = (lax.broadcasted_iota(jnp.int32, (1, PAIR), 1) & (HEAD - 1)) >> ts_bits
    for i in P:
        ge = g_end[:, sl[i]]
        uv_hi, uv_lo = _split(jnp.concatenate([u[i], v_p[i]], axis=0).T)
        bk3 = _rhs3(jnp.concatenate([bt[i] * ge, kt[i] * ge], axis=0))
        for j in range(ns):
            if ns == 1:
                hi_j, lo_j = uv_hi, uv_lo
            else:
                hi_j = jnp.where(col_seq == j, uv_hi, 0.0)
                lo_j = jnp.where(col_seq == j, uv_lo, 0.0)
            upd = _dotb(jnp.concatenate([hi_j, hi_j, lo_j], axis=1), bk3)
            s_scr[j, i] = s0[j][i] * ge[j * ts:j * ts + 1, :] + jnp.where(bd, upd, 0.0)
    for i in P:
        om = bdsum(o[i]) * (1.0 / HEAD)
        oc = o[i] - om
        ov = bdsum(oc * oc) * (1.0 / HEAD)
        bonus = sums[i][C:2 * C] * v_p[i]
        y = oc * lax.rsqrt(ov + GN_EPS) * lng_ref[:, sl[i]] + lnb_ref[:, sl[i]] + bonus
        o_ref[:, sl[i]] = y * gate[:, sl[i]]

    @pl.when(c == pl.num_programs(1) - 1)
    def _():
        for j in range(ns):
            for h in range(2 * n_pairs):
                lo = (h % 2) * HEAD
                nwkv_ref[j, h] = s_scr[j, h // 2, lo:lo + HEAD, lo:lo + HEAD]


def _rwkv_branch(proj_b, shift, wkv, prm, n_seq, seq_len, ns, ts):
    T, pw = proj_b.shape
    n_heads = wkv.shape[1]
    bw = n_heads * HEAD
    lw = prm["w2a2"].shape[0] // 2
    nc = seq_len // ts
    assert ns == 1 or nc == 1

    def full(a):
        return pl.BlockSpec(a.shape, lambda n, c: (0,) * a.ndim)

    names = ("mu", "w0", "w2a2", "a0", "g2", "kk", "ka", "rk", "lng", "lnb")
    return pl.pallas_call(
        functools.partial(_rwkv_kernel, ns=ns, ts=ts, bw=bw, lw=lw),
        grid=(n_seq // ns, nc),
        in_specs=[pl.BlockSpec((CHUNK, pw), lambda n, c: (n * nc + c, 0)),
                  pl.BlockSpec((ns, 1, pw), lambda n, c: (n, 0, 0)),
                  pl.BlockSpec((ns, n_heads, HEAD, HEAD), lambda n, c: (n, 0, 0, 0))]
                 + [full(prm[k]) for k in names],
        out_specs=[pl.BlockSpec((CHUNK, bw), lambda n, c: (n * nc + c, 0)),
                   pl.BlockSpec((ns, 1, pw), lambda n, c: (n, 0, 0)),
                   pl.BlockSpec((ns, n_heads, HEAD, HEAD), lambda n, c: (n, 0, 0, 0))],
        out_shape=[jax.ShapeDtypeStruct((T, bw), F32),
                   jax.ShapeDtypeStruct((n_seq, 1, pw), F32),
                   jax.ShapeDtypeStruct((n_seq, n_heads, HEAD, HEAD), F32)],
        scratch_shapes=[pltpu.VMEM((ns, n_heads // 2, PAIR, PAIR), F32), pltpu.VMEM((ns, 1, pw), F32)],
        compiler_params=_cparams(("parallel", "arbitrary")),
        name="rwkv_branch",
    )(proj_b, shift.reshape(n_seq, 1, pw), wkv, *[prm[k] for k in names])


def _outproj_kernel(a_ref, b_ref, x_ref, wa_ref, wb_ref, g_ref, bb_ref, o_ref, *, alpha):
    mix = jnp.dot(a_ref[...].astype(BF16), wa_ref[...], preferred_element_type=F32)
    mix = mix + jnp.dot(b_ref[...].astype(BF16), wb_ref[...], preferred_element_type=F32)
    o_ref[...] = _ln(alpha * x_ref[...] + mix, g_ref[...], bb_ref[...], LN_EPS)


def _outproj(a, b, x, wa, wb, g, bb, alpha, tm):
    T, D = x.shape
    wdt = a.shape[1]
    vec = pl.BlockSpec((1, D), lambda i: (0, 0))
    return pl.pallas_call(
        functools.partial(_outproj_kernel, alpha=alpha),
        grid=(T // tm,),
        in_specs=[pl.BlockSpec((tm, wdt), lambda i: (i, 0)),
                  pl.BlockSpec((tm, wdt), lambda i: (i, 0)),
                  pl.BlockSpec((tm, D), lambda i: (i, 0)),
                  pl.BlockSpec((wdt, D), lambda i: (0, 0)),
                  pl.BlockSpec((wdt, D), lambda i: (0, 0)),
                  vec, vec],
        out_specs=pl.BlockSpec((tm, D), lambda i: (i, 0)),
        out_shape=jax.ShapeDtypeStruct((T, D), F32),
        compiler_params=_cparams(("parallel",)),
        name="out_proj_ln",
    )(a, b, x, wa, wb, g, bb)


def _ffn_kernel(x_ref, wg_ref, wu_ref, wd_ref, g_ref, bb_ref, o_ref, acc_ref, xb_ref, *, alpha):
    j = pl.program_id(1)

    @pl.when(j == 0)
    def _():
        xb_ref[...] = x_ref[...].astype(BF16)
        acc_ref[...] = jnp.zeros_like(acc_ref)

    xb = xb_ref[...]
    hg = jnp.dot(xb, wg_ref[...], preferred_element_type=F32)
    hu = jnp.dot(xb, wu_ref[...], preferred_element_type=F32)
    h = (hg * _sigmoid(hg) * hu).astype(BF16)
    acc_ref[...] += jnp.dot(h, wd_ref[...], preferred_element_type=F32)

    @pl.when(j == pl.num_programs(1) - 1)
    def _():
        o_ref[...] = _ln(alpha * x_ref[...] + acc_ref[...], g_ref[...], bb_ref[...], LN_EPS)


def _ffn(x, wg, wu, wd, g, bb, alpha, tm, tf):
    T, D = x.shape
    Fd = wg.shape[1]
    vec = pl.BlockSpec((1, D), lambda i, j: (0, 0))
    return pl.pallas_call(
        functools.partial(_ffn_kernel, alpha=alpha),
        grid=(T // tm, Fd // tf),
        in_specs=[pl.BlockSpec((tm, D), lambda i, j: (i, 0)),
                  pl.BlockSpec((D, tf), lambda i, j: (0, j)),
                  pl.BlockSpec((D, tf), lambda i, j: (0, j)),
                  pl.BlockSpec((tf, D), lambda i, j: (j, 0)),
                  vec, vec],
        out_specs=pl.BlockSpec((tm, D), lambda i, j: (i, 0)),
        out_shape=jax.ShapeDtypeStruct((T, D), F32),
        scratch_shapes=[pltpu.VMEM((tm, D), F32), pltpu.VMEM((tm, D), BF16)],
        compiler_params=_cparams(("parallel", "arbitrary")),
        name="ffn_ln",
    )(x, wg, wu, wd, g, bb)


def _pool_kernel(x_ref, st_ref, w_ref, sc_ref, g_ref, bb_ref, o_ref, nst_ref, ext_ref, *, tl, start, alpha):
    c = pl.program_id(1)

    @pl.when(c == 0)
    def _():
        ext_ref[:, POOL_PAD - POOL_HIST:POOL_PAD, :] = st_ref[...]

    x = x_ref[...]
    nb, _, D = x.shape
    gw = D // len(POOL_WINDOWS)
    ext_ref[:, POOL_PAD:POOL_PAD + tl, :] = x
    pos = start + c * tl + lax.broadcasted_iota(jnp.int32, (1, tl, 1), 1)
    outs = []
    for gi, wdw in enumerate(POOL_WINDOWS):
        lo = gi * gw
        ssum = x[..., lo:lo + gw]
        for j in range(1, wdw):
            ssum = ssum + ext_ref[:, POOL_PAD - j:POOL_PAD - j + tl, lo:lo + gw]
        cnt = jnp.minimum(wdw, pos + 1).astype(F32)
        pooled = ssum / cnt - x[..., lo:lo + gw]
        y = jnp.dot(pooled.reshape(nb * tl, gw).astype(BF16), w_ref[gi], preferred_element_type=F32)
        outs.append(y.reshape(nb, tl, gw))
    y = jnp.concatenate(outs, axis=-1) * sc_ref[...]
    o_ref[...] = _ln(alpha * x + y, g_ref[...], bb_ref[...], LN_EPS)
    tail = ext_ref[:, tl + POOL_PAD - POOL_HIST:tl + POOL_PAD, :]
    ext_ref[:, POOL_PAD - POOL_HIST:POOL_PAD, :] = tail
    nst_ref[...] = tail


def _pool_mix(x, state, w, scale, g, bb, nb, tl, start, alpha):
    n_seq, seq_len, D = x.shape
    vec = pl.BlockSpec((1, D), lambda n, c: (0, 0))
    return pl.pallas_call(
        functools.partial(_pool_kernel, tl=tl, start=start, alpha=alpha),
        grid=(n_seq // nb, seq_len // tl),
        in_specs=[pl.BlockSpec((nb, tl, D), lambda n, c: (n, c, 0)),
                  pl.BlockSpec((nb, POOL_HIST, D), lambda n, c: (n, 0, 0)),
                  pl.BlockSpec(w.shape, lambda n, c: (0, 0, 0)),
                  vec, vec, vec],
        out_specs=[pl.BlockSpec((nb, tl, D), lambda n, c: (n, c, 0)),
                   pl.BlockSpec((nb, POOL_HIST, D), lambda n, c: (n, 0, 0))],
        out_shape=[jax.ShapeDtypeStruct((n_seq, seq_len, D), F32),
                   jax.ShapeDtypeStruct((n_seq, POOL_HIST, D), F32)],
        scratch_shapes=[pltpu.VMEM((nb, POOL_PAD + tl, D), F32)],
        compiler_params=_cparams(("parallel", "arbitrary")),
        name="pool_mix_ln",
    )(x, state, w, scale, g, bb)


def _trunk(x3, conv_prev, shift_prev, wkv_prev, pool_prev, start, prm, tiles):
    n_seq, seq_len, D = x3.shape
    T = n_seq * seq_len
    x = x3.reshape(T, D)
    alpha = prm["alpha"]
    tm = tiles["tm"]
    proj_a, proj_b = _inproj(x, prm["w_in_a"], prm["w_in_b"], tm)
    a_out, new_conv = _conv_branch(proj_a.reshape(n_seq, seq_len, -1), conv_prev, prm["conv_w"], prm["conv_b"],
                                   prm["conv_ln_g"], prm["conv_ln_b"], tiles["seq_nb"], tiles["conv_tl"])
    b_out, new_shift, new_wkv = _rwkv_branch(proj_b, shift_prev, wkv_prev, prm["rwkv"], n_seq, seq_len,
                                             tiles["rwkv_ns"], CHUNK // tiles["rwkv_ns"])
    x = _outproj(a_out.reshape(T, -1), b_out, x, prm["w_out_a"], prm["w_out_b"], prm["ln_mix_g"][0],
                 prm["ln_mix_b"][0], alpha, tm)
    x = _ffn(x, prm["ffn_gate"][0], prm["ffn_up"][0], prm["ffn_down"][0], prm["ln_ffn_g"][0], prm["ln_ffn_b"][0],
             alpha, tm, tiles["tf"])
    x, new_pool = _pool_mix(x.reshape(n_seq, seq_len, D), pool_prev, prm["pool_w"], prm["pool_scale"],
                            prm["ln_mix_g"][1], prm["ln_mix_b"][1], tiles["seq_nb"], tiles["pool_tl"], start, alpha)
    x = _ffn(x.reshape(T, D), prm["ffn_gate"][1], prm["ffn_up"][1], prm["ffn_down"][1], prm["ln_ffn_g"][1],
             prm["ln_ffn_b"][1], alpha, tm, tiles["tf"])
    return (x.reshape(n_seq, seq_len, D), new_conv[None], new_shift.reshape(1, n_seq, -1), new_wkv[None],
            new_pool[None])


def kernel(x_prompt, x_sample, state_conv, state_shift, state_wkv, state_pool, w_in, conv_w, conv_b, conv_ln_g, conv_ln_b, rwkv_mu, rwkv_w0, rwkv_w2, rwkv_a0, rwkv_a2, rwkv_g2, rwkv_kk, rwkv_ka, rwkv_rk, rwkv_lnx_g, rwkv_lnx_b, w_out, pool_w, pool_scale, ln_mix_g, ln_mix_b, ffn_gate, ffn_up, ffn_down, ln_ffn_g, ln_ffn_b):
    depth = ln_mix_g.shape[0]
    assert depth == 2 and w_in.shape[0] == 1 and pool_w.shape[0] == 1
    B, L, D = x_prompt.shape
    Bs, Ls, _ = x_sample.shape
    a_w = conv_w.shape[2]
    b_w = rwkv_w0.shape[1]
    n_heads = b_w // HEAD
    alpha = float((2 * depth) ** 0.25)

    row = lambda a: a.reshape(1, -1)
    prm = dict(
        alpha=alpha,
        w_in_a=w_in[0, :, :2 * a_w].astype(BF16), w_in_b=w_in[0, :, 2 * a_w:].astype(BF16),
        conv_w=conv_w[0], conv_b=row(conv_b[0]), conv_ln_g=row(conv_ln_g[0]), conv_ln_b=row(conv_ln_b[0]),
        rwkv=dict(mu=row(rwkv_mu[0]), w0=row(rwkv_w0[0]),
                  w2a2=jnp.concatenate([rwkv_w2[0], rwkv_a2[0]], axis=0), a0=row(rwkv_a0[0]), g2=rwkv_g2[0],
                  kk=row(rwkv_kk[0]), ka=row(rwkv_ka[0]), rk=row(rwkv_rk[0]),
                  lng=row(rwkv_lnx_g[0]), lnb=row(rwkv_lnx_b[0])),
        w_out_a=w_out[0, :a_w].astype(BF16), w_out_b=w_out[0, a_w:].astype(BF16),
        pool_w=pool_w[0].astype(BF16), pool_scale=row(pool_scale[0]),
        ln_mix_g=[row(ln_mix_g[i]) for i in range(depth)], ln_mix_b=[row(ln_mix_b[i]) for i in range(depth)],
        ffn_gate=[ffn_gate[i].astype(BF16) for i in range(depth)],
        ffn_up=[ffn_up[i].astype(BF16) for i in range(depth)],
        ffn_down=[ffn_down[i].astype(BF16) for i in range(depth)],
        ln_ffn_g=[row(ln_ffn_g[i]) for i in range(depth)], ln_ffn_b=[row(ln_ffn_b[i]) for i in range(depth)],
    )
    dt = x_prompt.dtype
    z_conv = jnp.zeros((B, CONV_HIST, a_w), dt)
    z_shift = jnp.zeros((B, state_shift.shape[-1]), dt)
    z_wkv = jnp.zeros((B, n_heads, HEAD, HEAD), state_wkv.dtype)
    z_pool = jnp.zeros((B, POOL_HIST, D), dt)
    tiles_p = dict(tm=1024, tf=256, seq_nb=1, conv_tl=256, pool_tl=512, rwkv_ns=1)
    tiles_s = dict(tm=1024, tf=256, seq_nb=32, conv_tl=Ls, pool_tl=Ls, rwkv_ns=CHUNK // Ls)
    yp = _trunk(x_prompt, z_conv, z_shift, z_wkv, z_pool, 0, prm, tiles_p)
    ys = _trunk(x_sample, state_conv[0], state_shift[0], state_wkv[0], state_pool[0], PAST_LEN, prm, tiles_s)
    return (yp[0], ys[0], yp[1], yp[2], yp[3], yp[4], ys[1], ys[2], ys[3], ys[4])
```

```python
import functools

import jax
import jax.numpy as jnp
from jax import lax
from jax.experimental import pallas as pl
from jax.experimental.pallas import tpu as pltpu

F32 = jnp.float32
BF16 = jnp.bfloat16

HEAD = 64
PAIR = 2 * HEAD
CHUNK = 64
SOLVE_BLOCK = 16
CONV_TAPS = 31
CONV_HIST = CONV_TAPS - 1
CONV_PAD = 32
POOL_WINDOWS = (2, 4, 8, 16)
POOL_HIST = max(POOL_WINDOWS) - 1
POOL_PAD = 16
PAST_LEN = 16384
LN_EPS = 1e-5
GN_EPS = 64e-5
VMEM_LIMIT = 56 * 1024 * 1024


def _ln(z, g, b, eps):
    mu = jnp.mean(z, axis=-1, keepdims=True)
    zc = z - mu
    var = jnp.mean(zc * zc, axis=-1, keepdims=True)
    return zc * lax.rsqrt(var + eps) * g + b


def _sigmoid(x):
    return 1.0 / (1.0 + jnp.exp(-x))


def _cparams(sem):
    return pltpu.CompilerParams(dimension_semantics=sem, vmem_limit_bytes=VMEM_LIMIT)


def _log2(n):
    assert n & (n - 1) == 0
    return n.bit_length() - 1


def _inproj_kernel(x_ref, wa_ref, wb_ref, oa_ref, ob_ref):
    xb = x_ref[...].astype(BF16)
    oa_ref[...] = jnp.dot(xb, wa_ref[...], preferred_element_type=F32)
    ob_ref[...] = jnp.dot(xb, wb_ref[...], preferred_element_type=F32)


def _inproj(x, wa, wb, tm):
    T, D = x.shape
    na, nb = wa.shape[1], wb.shape[1]
    return pl.pallas_call(
        _inproj_kernel,
        grid=(T // tm,),
        in_specs=[pl.BlockSpec((tm, D), lambda i: (i, 0)),
                  pl.BlockSpec((D, na), lambda i: (0, 0)),
                  pl.BlockSpec((D, nb), lambda i: (0, 0))],
        out_specs=[pl.BlockSpec((tm, na), lambda i: (i, 0)),
                   pl.BlockSpec((tm, nb), lambda i: (i, 0))],
        out_shape=[jax.ShapeDtypeStruct((T, na), F32), jax.ShapeDtypeStruct((T, nb), F32)],
        compiler_params=_cparams(("parallel",)),
        name="in_proj",
    )(x, wa, wb)


def _conv_kernel(p_ref, st_ref, w_ref, b_ref, g_ref, bb_ref, o_ref, nst_ref, ext_ref, *, tl, width):
    c = pl.program_id(1)

    @pl.when(c == 0)
    def _():
        ext_ref[:, CONV_PAD - CONV_HIST:CONV_PAD, :] = st_ref[...]

    p = p_ref[...]
    u = p[..., :width] * _sigmoid(p[..., width:])
    ext_ref[:, CONV_PAD:CONV_PAD + tl, :] = u
    acc = jnp.zeros(u.shape, F32) + b_ref[...]
    for j in range(CONV_TAPS):
        s = CONV_PAD - CONV_HIST + j
        acc = acc + ext_ref[:, s:s + tl, :] * w_ref[j:j + 1, :]
    y = _ln(acc, g_ref[...], bb_ref[...], LN_EPS)
    o_ref[...] = y * _sigmoid(y)
    tail = ext_ref[:, tl + CONV_PAD - CONV_HIST:tl + CONV_PAD, :]
    ext_ref[:, CONV_PAD - CONV_HIST:CONV_PAD, :] = tail
    nst_ref[...] = tail


def _conv_branch(proj_a, state, w, b, g, bb, nb, tl):
    n_seq, seq_len, two_w = proj_a.shape
    width = two_w // 2
    vec = pl.BlockSpec((1, width), lambda n, c: (0, 0))
    return pl.pallas_call(
        functools.partial(_conv_kernel, tl=tl, width=width),
        grid=(n_seq // nb, seq_len // tl),
        in_specs=[pl.BlockSpec((nb, tl, two_w), lambda n, c: (n, c, 0)),
                  pl.BlockSpec((nb, CONV_HIST, width), lambda n, c: (n, 0, 0)),
                  pl.BlockSpec((CONV_TAPS, width), lambda n, c: (0, 0)),
                  vec, vec, vec],
        out_specs=[pl.BlockSpec((nb, tl, width), lambda n, c: (n, c, 0)),
                   pl.BlockSpec((nb, CONV_HIST, width), lambda n, c: (n, 0, 0))],
        out_shape=[jax.ShapeDtypeStruct((n_seq, seq_len, width), F32),
                   jax.ShapeDtypeStruct((n_seq, CONV_HIST, width), F32)],
        scratch_shapes=[pltpu.VMEM((nb, CONV_PAD + tl, width), F32)],
        compiler_params=_cparams(("parallel", "arbitrary")),
        name="conv_branch",
    )(proj_a, state, w, b, g, bb)


def _split(x):
    hi = x.astype(BF16)
    return hi, (x - hi.astype(F32)).astype(BF16)


def _dotb(a, b):
    return jnp.dot(a, b, preferred_element_type=F32)


def _dotb_nt(a, b):
    return lax.dot_general(a, b, (((1,), (1,)), ((), ())), preferred_element_type=F32)


def _lhs3(x):
    hi, lo = _split(x)
    return jnp.concatenate([hi, hi, lo], axis=1)


def _rhs3(x):
    hi, lo = _split(x)
    return jnp.concatenate([hi, lo, hi], axis=0)


def _rhs3_nt(x):
    hi, lo = _split(x)
    return jnp.concatenate([hi, lo, hi], axis=1)


def _mm3(a, b):
    return _dotb(_lhs3(a), _rhs3(b))


def _rwkv_kernel(p_ref, sh_ref, wkv_ref, mu_ref, w0_ref, w2a2_ref, a0_ref, g2_ref, kk_ref, ka_ref, rk_ref,
                 lng_ref, lnb_ref, o_ref, nsh_ref, nwkv_ref, s_scr, prev_scr, *, nsb, tb, bw, lw):
    c = pl.program_id(1)
    n_pairs = bw // PAIR
    C = CHUNK
    R = nsb * tb
    nq = R // C
    ts = min(tb, C)
    ns = C // ts
    assert HEAD == C and nq * C == R and (nsb == 1 or tb == ts)
    ts_bits = _log2(ts)
    sb = min(SOLVE_BLOCK, ts)
    sb_bits = _log2(sb)

    @pl.when(c == 0)
    def _():
        prev_scr[...] = sh_ref[...]
        s_scr[...] = jnp.zeros_like(s_scr)
        for j in range(nsb):
            for h in range(2 * n_pairs):
                lo = (h % 2) * HEAD
                s_scr[j, h // 2, lo:lo + HEAD, lo:lo + HEAD] = wkv_ref[j, h]

    p = p_ref[...]
    pw = p.shape[1]
    row = lax.broadcasted_iota(jnp.int32, (R, 1), 0)
    if nsb == 1:
        first = prev_scr[0]
    else:
        first = jnp.concatenate([jnp.broadcast_to(prev_scr[j], (tb, pw)) for j in range(nsb)], axis=0)
    prev = jnp.where((row & (tb - 1)) == 0, first, pltpu.roll(p, 1, 0))
    for j in range(nsb):
        last = p[(j + 1) * tb - 1:(j + 1) * tb, :]
        prev_scr[j] = last
        nsh_ref[j] = last
    xm = p + mu_ref[...] * (prev - p)
    r = xm[:, 0:bw]
    k = xm[:, bw:2 * bw]
    v = xm[:, 2 * bw:3 * bw]
    dwa = xm[:, 3 * bw:3 * bw + 2 * lw]
    dg = xm[:, 3 * bw + 2 * lw:]
    lane_l = lax.broadcasted_iota(jnp.int32, (1, 2 * lw), 1)
    is_w = lane_l < lw
    w2a2_3 = _rhs3(w2a2_ref[...])
    wpre = w0_ref[...] + _dotb(_lhs3(jnp.where(is_w, jnp.tanh(dwa), 0.0)), w2a2_3)
    apre = a0_ref[...] + _dotb(_lhs3(jnp.where(is_w, 0.0, dwa)), w2a2_3)
    z = -wpre
    wlog = -(jnp.maximum(z, 0.0) + jnp.log1p(jnp.exp(-jnp.abs(z)))) - 0.5
    logw = -jnp.exp(wlog)
    a_lr = _sigmoid(apre)
    gate = _mm3(_sigmoid(dg), g2_ref[...])
    kkv = k * kk_ref[...]
    k2 = k * (1.0 + (a_lr - 1.0) * ka_ref[...])
    rk2 = r * k2 * rk_ref[...]

    lane = lax.broadcasted_iota(jnp.int32, (1, PAIR), 1)
    m_l = lane < HEAD
    rowp = lax.broadcasted_iota(jnp.int32, (PAIR, PAIR), 0)
    colp = lax.broadcasted_iota(jnp.int32, (PAIR, PAIR), 1)
    bd = (rowp >= HEAD) == (colp >= HEAD)
    bd2 = jnp.concatenate([bd, bd], axis=0).astype(BF16)
    t_i = lax.broadcasted_iota(jnp.int32, (C, PAIR), 0)
    s_i = lax.broadcasted_iota(jnp.int32, (C, PAIR), 1) & (HEAD - 1)
    same = (s_i >> ts_bits) == (t_i >> ts_bits)
    strict = same & (s_i < t_i)
    incl = same & (s_i <= t_i)
    eye = (s_i == t_i).astype(F32)
    dblk = (s_i >> sb_bits) == (t_i >> sb_bits)
    t_c = lax.broadcasted_iota(jnp.int32, (R, R), 0)
    s_c = lax.broadcasted_iota(jnp.int32, (R, R), 1)
    same_c = (s_c >> ts_bits) == (t_c >> ts_bits)
    tri_b = (same_c & (s_c <= t_c)).astype(BF16)
    ones_b = same_c.astype(BF16)

    def vs(y):
        return jnp.concatenate([jnp.where(m_l, y, 0.0), jnp.where(m_l, 0.0, y)], axis=0)

    def vs3(y):
        hi, lo = _split(y)
        vh = vs(hi)
        return jnp.concatenate([vh, vs(lo), vh], axis=0)

    def vs3_nt(y):
        hi, lo = _split(y)
        vh = vs(hi)
        return jnp.concatenate([vh, vs(lo), vh], axis=1)

    def pm(a, y):
        return _dotb(_lhs3(a), vs3(y))

    def pm1(a, y):
        return _dotb(a.astype(BF16), vs(y.astype(BF16)))

    def bdsum(x):
        hi, lo = _split(x)
        return _dotb(jnp.concatenate([hi, lo], axis=1), bd2)

    P = range(n_pairs)
    units = [(q, i) for q in range(nq) for i in P]
    NU = range(len(units))

    def blk(x, n):
        q, i = units[n]
        return x[q * C:(q + 1) * C, i * PAIR:(i + 1) * PAIR]

    def neumann(m, n_terms):
        factors = [[eye + m[n] for n in NU]]
        pw_ = m
        for _ in range(_log2(n_terms) - 1):
            pw_ = [pm1(pw_[n], pw_[n]) for n in NU]
            factors.append([eye + pw_[n] for n in NU])
        while len(factors) > 1:
            nxt = [[pm1(factors[f][n], factors[f + 1][n]) for n in NU] for f in range(0, len(factors) - 1, 2)]
            if len(factors) % 2:
                nxt.append(factors[-1])
            factors = nxt
        return factors[0]

    lw_hi, lw_lo = _split(logw)
    lw2 = jnp.concatenate([lw_hi, lw_lo], axis=0)
    cs = _dotb(jnp.concatenate([tri_b, tri_b], axis=1), lw2)
    g_in = jnp.exp(cs)
    g_ex = jnp.exp(cs - logw)
    g_inv = jnp.exp(-cs)
    g_end = jnp.exp(_dotb(jnp.concatenate([ones_b, ones_b], axis=1), lw2))

    sums = [bdsum(jnp.concatenate([blk(kkv, n) * blk(kkv, n), blk(rk2, n)], axis=0)) for n in NU]
    kk_n = [blk(kkv, n) / jnp.maximum(jnp.sqrt(sums[n][0:C]), 1e-12) for n in NU]
    v_p = [blk(v, n) for n in NU]
    at = [-kk_n[n] * blk(g_ex, n) for n in NU]
    rt = [blk(r, n) * blk(g_in, n) for n in NU]
    bt = [kk_n[n] * blk(a_lr, n) * blk(g_inv, n) for n in NU]
    kt = [blk(k2, n) * blk(g_inv, n) for n in NU]
    sm_a, sm_r = [], []
    for n in NU:
        b_hi, b_lo = _split(bt[n])
        k_hi, k_lo = _split(kt[n])
        vbh, vkh = vs(b_hi), vs(k_hi)
        rhs3 = jnp.concatenate([jnp.concatenate([vbh, vs(b_lo), vbh], axis=1),
                                jnp.concatenate([vkh, vs(k_lo), vkh], axis=1)], axis=0)
        sm_a.append(_dotb_nt(_lhs3(at[n]), rhs3))
        sm_r.append(_dotb_nt(rt[n].astype(BF16), jnp.concatenate([vbh, vkh], axis=0)))
    lab = [jnp.where(strict, sm_a[n][:, 0:PAIR], 0.0) for n in NU]
    lak = [jnp.where(strict, sm_a[n][:, PAIR:2 * PAIR], 0.0) for n in NU]
    mrb = [jnp.where(incl, sm_r[n][:, 0:PAIR], 0.0) for n in NU]
    mrk = [jnp.where(incl, sm_r[n][:, PAIR:2 * PAIR], 0.0) for n in NU]
    d1 = [jnp.where(dblk, lab[n], 0.0) for n in NU]
    tinv = neumann(d1, sb)
    if sb < ts:
        e1 = [pm1(tinv[n], lab[n] - d1[n]) for n in NU]
        te = neumann(e1, ts // sb)
        tinv = [pm1(te[n], tinv[n]) for n in NU]
    res = [eye - tinv[n] + pm(lab[n], tinv[n]) for n in NU]
    tinv = [tinv[n] + pm1(tinv[n], res[n]) for n in NU]
    tinv3 = [_lhs3(tinv[n]) for n in NU]
    x0 = [pm(lak[n], v_p[n]) for n in NU]
    s_cur = [[s_scr[sq, i] for i in P] for sq in range(nsb)]
    col_seq = (lax.broadcasted_iota(jnp.int32, (1, PAIR), 1) & (HEAD - 1)) >> ts_bits
    o = []
    for q in range(nq):
        un = [q * n_pairs + i for i in P]
        seq = [q * ns + j if nsb > 1 else 0 for j in range(ns)]
        a_s, r_s = [], []
        for i in P:
            parts = []
            for j in range(ns):
                rows = slice(j * ts, (j + 1) * ts)
                lhs = _lhs3(jnp.concatenate([at[un[i]][rows], rt[un[i]][rows]], axis=0))
                parts.append(_dotb_nt(lhs, _rhs3_nt(s_cur[seq[j]][i])))
            a_s.append(jnp.concatenate([x[0:ts] for x in parts], axis=0))
            r_s.append(jnp.concatenate([x[ts:2 * ts] for x in parts], axis=0))
        u = [_dotb(tinv3[un[i]], vs3(a_s[i] + x0[un[i]])) for i in P]
        for i in P:
            n = un[i]
            ge = blk(g_end, n)
            uv_hi, uv_lo = _split(jnp.concatenate([u[i], v_p[n]], axis=0).T)
            bk3 = _rhs3(jnp.concatenate([bt[n] * ge, kt[n] * ge], axis=0))
            for j in range(ns):
                if ns == 1:
                    hi_j, lo_j = uv_hi, uv_lo
                else:
                    hi_j = jnp.where(col_seq == j, uv_hi, 0.0)
                    lo_j = jnp.where(col_seq == j, uv_lo, 0.0)
                upd = _dotb(jnp.concatenate([hi_j, hi_j, lo_j], axis=1), bk3)
                s_cur[seq[j]][i] = s_cur[seq[j]][i] * ge[j * ts:j * ts + 1, :] + jnp.where(bd, upd, 0.0)
        o += [r_s[i] + _dotb(jnp.concatenate([mrb[un[i]].astype(BF16), mrk[un[i]].astype(BF16)], axis=1),
                             jnp.concatenate([vs(u[i].astype(BF16)), vs(v_p[un[i]].astype(BF16))], axis=0))
              for i in P]
    for sq in range(nsb):
        for i in P:
            s_scr[sq, i] = s_cur[sq][i]
    for n in NU:
        q, i = units[n]
        ln = slice(i * PAIR, (i + 1) * PAIR)
        om = bdsum(o[n]) * (1.0 / HEAD)
        oc = o[n] - om
        ov = bdsum(oc * oc) * (1.0 / HEAD)
        bonus = sums[n][C:2 * C] * v_p[n]
        y = oc * lax.rsqrt(ov + GN_EPS) * lng_ref[:, ln] + lnb_ref[:, ln] + bonus
        o_ref[q * C:(q + 1) * C, ln] = y * blk(gate, n)

    @pl.when(c == pl.num_programs(1) - 1)
    def _():
        for j in range(nsb):
            for h in range(2 * n_pairs):
                lo = (h % 2) * HEAD
                nwkv_ref[j, h] = s_scr[j, h // 2, lo:lo + HEAD, lo:lo + HEAD]


def _rwkv_branch(proj_b, shift, wkv, prm, n_seq, seq_len, nsb, tb):
    T, pw = proj_b.shape
    n_heads = wkv.shape[1]
    bw = n_heads * HEAD
    lw = prm["w2a2"].shape[0] // 2
    nc = seq_len // tb
    assert nsb == 1 or nc == 1
    rows = nsb * tb

    def full(a):
        return pl.BlockSpec(a.shape, lambda n, c: (0,) * a.ndim)

    names = ("mu", "w0", "w2a2", "a0", "g2", "kk", "ka", "rk", "lng", "lnb")
    return pl.pallas_call(
        functools.partial(_rwkv_kernel, nsb=nsb, tb=tb, bw=bw, lw=lw),
        grid=(n_seq // nsb, nc),
        in_specs=[pl.BlockSpec((rows, pw), lambda n, c: (n * nc + c, 0)),
                  pl.BlockSpec((nsb, 1, pw), lambda n, c: (n, 0, 0)),
                  pl.BlockSpec((nsb, n_heads, HEAD, HEAD), lambda n, c: (n, 0, 0, 0))]
                 + [full(prm[k]) for k in names],
        out_specs=[pl.BlockSpec((rows, bw), lambda n, c: (n * nc + c, 0)),
                   pl.BlockSpec((nsb, 1, pw), lambda n, c: (n, 0, 0)),
                   pl.BlockSpec((nsb, n_heads, HEAD, HEAD), lambda n, c: (n, 0, 0, 0))],
        out_shape=[jax.ShapeDtypeStruct((T, bw), F32),
                   jax.ShapeDtypeStruct((n_seq, 1, pw), F32),
                   jax.ShapeDtypeStruct((n_seq, n_heads, HEAD, HEAD), F32)],
        scratch_shapes=[pltpu.VMEM((nsb, n_heads // 2, PAIR, PAIR), F32), pltpu.VMEM((nsb, 1, pw), F32)],
        compiler_params=_cparams(("parallel", "arbitrary")),
        name="rwkv_branch",
    )(proj_b, shift.reshape(n_seq, 1, pw), wkv, *[prm[k] for k in names])


def _outproj_kernel(a_ref, b_ref, x_ref, wa_ref, wb_ref, g_ref, bb_ref, o_ref, *, alpha):
    mix = jnp.dot(a_ref[...].astype(BF16), wa_ref[...], preferred_element_type=F32)
    mix = mix + jnp.dot(b_ref[...].astype(BF16), wb_ref[...], preferred_element_type=F32)
    o_ref[...] = _ln(alpha * x_ref[...] + mix, g_ref[...], bb_ref[...], LN_EPS)


def _outproj(a, b, x, wa, wb, g, bb, alpha, tm):
    T, D = x.shape
    wdt = a.shape[1]
    vec = pl.BlockSpec((1, D), lambda i: (0, 0))
    return pl.pallas_call(
        functools.partial(_outproj_kernel, alpha=alpha),
        grid=(T // tm,),
        in_specs=[pl.BlockSpec((tm, wdt), lambda i: (i, 0)),
                  pl.BlockSpec((tm, wdt), lambda i: (i, 0)),
                  pl.BlockSpec((tm, D), lambda i: (i, 0)),
                  pl.BlockSpec((wdt, D), lambda i: (0, 0)),
                  pl.BlockSpec((wdt, D), lambda i: (0, 0)),
                  vec, vec],
        out_specs=pl.BlockSpec((tm, D), lambda i: (i, 0)),
        out_shape=jax.ShapeDtypeStruct((T, D), F32),
        compiler_params=_cparams(("parallel",)),
        name="out_proj_ln",
    )(a, b, x, wa, wb, g, bb)


def _ffn_kernel(x_ref, wg_ref, wu_ref, wd_ref, g_ref, bb_ref, o_ref, acc_ref, xb_ref, *, alpha):
    j = pl.program_id(1)

    @pl.when(j == 0)
    def _():
        xb_ref[...] = x_ref[...].astype(BF16)
        acc_ref[...] = jnp.zeros_like(acc_ref)

    xb = xb_ref[...]
    hg = jnp.dot(xb, wg_ref[...], preferred_element_type=F32)
    hu = jnp.dot(xb, wu_ref[...], preferred_element_type=F32)
    h = (hg * _sigmoid(hg) * hu).astype(BF16)
    acc_ref[...] += jnp.dot(h, wd_ref[...], preferred_element_type=F32)

    @pl.when(j == pl.num_programs(1) - 1)
    def _():
        o_ref[...] = _ln(alpha * x_ref[...] + acc_ref[...], g_ref[...], bb_ref[...], LN_EPS)


def _ffn(x, wg, wu, wd, g, bb, alpha, tm, tf):
    T, D = x.shape
    Fd = wg.shape[1]
    vec = pl.BlockSpec((1, D), lambda i, j: (0, 0))
    return pl.pallas_call(
        functools.partial(_ffn_kernel, alpha=alpha),
        grid=(T // tm, Fd // tf),
        in_specs=[pl.BlockSpec((tm, D), lambda i, j: (i, 0)),
                  pl.BlockSpec((D, tf), lambda i, j: (0, j)),
                  pl.BlockSpec((D, tf), lambda i, j: (0, j)),
                  pl.BlockSpec((tf, D), lambda i, j: (j, 0)),
                  vec, vec],
        out_specs=pl.BlockSpec((tm, D), lambda i, j: (i, 0)),
        out_shape=jax.ShapeDtypeStruct((T, D), F32),
        scratch_shapes=[pltpu.VMEM((tm, D), F32), pltpu.VMEM((tm, D), BF16)],
        compiler_params=_cparams(("parallel", "arbitrary")),
        name="ffn_ln",
    )(x, wg, wu, wd, g, bb)


def _pool_kernel(x_ref, st_ref, w_ref, sc_ref, g_ref, bb_ref, o_ref, nst_ref, ext_ref, *, tl, start, alpha):
    c = pl.program_id(1)

    @pl.when(c == 0)
    def _():
        ext_ref[:, POOL_PAD - POOL_HIST:POOL_PAD, :] = st_ref[...]

    x = x_ref[...]
    nb, _, D = x.shape
    gw = D // len(POOL_WINDOWS)
    ext_ref[:, POOL_PAD:POOL_PAD + tl, :] = x
    pos = start + c * tl + lax.broadcasted_iota(jnp.int32, (1, tl, 1), 1)
    outs = []
    for gi, wdw in enumerate(POOL_WINDOWS):
        lo = gi * gw
        ssum = x[..., lo:lo + gw]
        for j in range(1, wdw):
            ssum = ssum + ext_ref[:, POOL_PAD - j:POOL_PAD - j + tl, lo:lo + gw]
        cnt = jnp.minimum(wdw, pos + 1).astype(F32)
        pooled = ssum / cnt - x[..., lo:lo + gw]
        y = jnp.dot(pooled.reshape(nb * tl, gw).astype(BF16), w_ref[gi], preferred_element_type=F32)
        outs.append(y.reshape(nb, tl, gw))
    y = jnp.concatenate(outs, axis=-1) * sc_ref[...]
    o_ref[...] = _ln(alpha * x + y, g_ref[...], bb_ref[...], LN_EPS)
    tail = ext_ref[:, tl + POOL_PAD - POOL_HIST:tl + POOL_PAD, :]
    ext_ref[:, POOL_PAD - POOL_HIST:POOL_PAD, :] = tail
    nst_ref[...] = tail


def _pool_mix(x, state, w, scale, g, bb, nb, tl, start, alpha):
    n_seq, seq_len, D = x.shape
    vec = pl.BlockSpec((1, D), lambda n, c: (0, 0))
    return pl.pallas_call(
        functools.partial(_pool_kernel, tl=tl, start=start, alpha=alpha),
        grid=(n_seq // nb, seq_len // tl),
        in_specs=[pl.BlockSpec((nb, tl, D), lambda n, c: (n, c, 0)),
                  pl.BlockSpec((nb, POOL_HIST, D), lambda n, c: (n, 0, 0)),
                  pl.BlockSpec(w.shape, lambda n, c: (0, 0, 0)),
                  vec, vec, vec],
        out_specs=[pl.BlockSpec((nb, tl, D), lambda n, c: (n, c, 0)),
                   pl.BlockSpec((nb, POOL_HIST, D), lambda n, c: (n, 0, 0))],
        out_shape=[jax.ShapeDtypeStruct((n_seq, seq_len, D), F32),
                   jax.ShapeDtypeStruct((n_seq, POOL_HIST, D), F32)],
        scratch_shapes=[pltpu.VMEM((nb, POOL_PAD + tl, D), F32)],
        compiler_params=_cparams(("parallel", "arbitrary")),
        name="pool_mix_ln",
    )(x, state, w, scale, g, bb)


def _trunk(x3, conv_prev, shift_prev, wkv_prev, pool_prev, start, prm, tiles):
    n_seq, seq_len, D = x3.shape
    T = n_seq * seq_len
    x = x3.reshape(T, D)
    alpha = prm["alpha"]
    tm = tiles["tm"]
    proj_a, proj_b = _inproj(x, prm["w_in_a"], prm["w_in_b"], tm)
    a_out, new_conv = _conv_branch(proj_a.reshape(n_seq, seq_len, -1), conv_prev, prm["conv_w"], prm["conv_b"],
                                   prm["conv_ln_g"], prm["conv_ln_b"], tiles["seq_nb"], tiles["conv_tl"])
    b_out, new_shift, new_wkv = _rwkv_branch(proj_b, shift_prev, wkv_prev, prm["rwkv"], n_seq, seq_len,
                                             tiles["rwkv_nsb"], tiles["rwkv_tb"])
    x = _outproj(a_out.reshape(T, -1), b_out, x, prm["w_out_a"], prm["w_out_b"], prm["ln_mix_g"][0],
                 prm["ln_mix_b"][0], alpha, tm)
    x = _ffn(x, prm["ffn_gate"][0], prm["ffn_up"][0], prm["ffn_down"][0], prm["ln_ffn_g"][0], prm["ln_ffn_b"][0],
             alpha, tm, tiles["tf"])
    x, new_pool = _pool_mix(x.reshape(n_seq, seq_len, D), pool_prev, prm["pool_w"], prm["pool_scale"],
                            prm["ln_mix_g"][1], prm["ln_mix_b"][1], tiles["seq_nb"], tiles["pool_tl"], start, alpha)
    x = _ffn(x.reshape(T, D), prm["ffn_gate"][1], prm["ffn_up"][1], prm["ffn_down"][1], prm["ln_ffn_g"][1],
             prm["ln_ffn_b"][1], alpha, tm, tiles["tf"])
    return (x.reshape(n_seq, seq_len, D), new_conv[None], new_shift.reshape(1, n_seq, -1), new_wkv[None],
            new_pool[None])


def kernel(x_prompt, x_sample, state_conv, state_shift, state_wkv, state_pool, w_in, conv_w, conv_b, conv_ln_g, conv_ln_b, rwkv_mu, rwkv_w0, rwkv_w2, rwkv_a0, rwkv_a2, rwkv_g2, rwkv_kk, rwkv_ka, rwkv_rk, rwkv_lnx_g, rwkv_lnx_b, w_out, pool_w, pool_scale, ln_mix_g, ln_mix_b, ffn_gate, ffn_up, ffn_down, ln_ffn_g, ln_ffn_b):
    depth = ln_mix_g.shape[0]
    assert depth == 2 and w_in.shape[0] == 1 and pool_w.shape[0] == 1
    B, L, D = x_prompt.shape
    Bs, Ls, _ = x_sample.shape
    a_w = conv_w.shape[2]
    b_w = rwkv_w0.shape[1]
    n_heads = b_w // HEAD
    alpha = float((2 * depth) ** 0.25)

    row = lambda a: a.reshape(1, -1)
    prm = dict(
        alpha=alpha,
        w_in_a=w_in[0, :, :2 * a_w].astype(BF16), w_in_b=w_in[0, :, 2 * a_w:].astype(BF16),
        conv_w=conv_w[0], conv_b=row(conv_b[0]), conv_ln_g=row(conv_ln_g[0]), conv_ln_b=row(conv_ln_b[0]),
        rwkv=dict(mu=row(rwkv_mu[0]), w0=row(rwkv_w0[0]),
                  w2a2=jnp.concatenate([rwkv_w2[0], rwkv_a2[0]], axis=0), a0=row(rwkv_a0[0]), g2=rwkv_g2[0],
                  kk=row(rwkv_kk[0]), ka=row(rwkv_ka[0]), rk=row(rwkv_rk[0]),
                  lng=row(rwkv_lnx_g[0]), lnb=row(rwkv_lnx_b[0])),
        w_out_a=w_out[0, :a_w].astype(BF16), w_out_b=w_out[0, a_w:].astype(BF16),
        pool_w=pool_w[0].astype(BF16), pool_scale=row(pool_scale[0]),
        ln_mix_g=[row(ln_mix_g[i]) for i in range(depth)], ln_mix_b=[row(ln_mix_b[i]) for i in range(depth)],
        ffn_gate=[ffn_gate[i].astype(BF16) for i in range(depth)],
        ffn_up=[ffn_up[i].astype(BF16) for i in range(depth)],
        ffn_down=[ffn_down[i].astype(BF16) for i in range(depth)],
        ln_ffn_g=[row(ln_ffn_g[i]) for i in range(depth)], ln_ffn_b=[row(ln_ffn_b[i]) for i in range(depth)],
    )
    dt = x_prompt.dtype
    z_conv = jnp.zeros((B, CONV_HIST, a_w), dt)
    z_shift = jnp.zeros((B, state_shift.shape[-1]), dt)
    z_wkv = jnp.zeros((B, n_heads, HEAD, HEAD), state_wkv.dtype)
    z_pool = jnp.zeros((B, POOL_HIST, D), dt)
    tiles_p = dict(tm=1024, tf=256, seq_nb=1, conv_tl=256, pool_tl=512, rwkv_nsb=1, rwkv_tb=4 * CHUNK)
    tiles_s = dict(tm=1024, tf=256, seq_nb=32, conv_tl=Ls, pool_tl=Ls, rwkv_nsb=2 * CHUNK // Ls, rwkv_tb=Ls)
    yp = _trunk(x_prompt, z_conv, z_shift, z_wkv, z_pool, 0, prm, tiles_p)
    ys = _trunk(x_sample, state_conv[0], state_shift[0], state_wkv[0], state_pool[0], PAST_LEN, prm, tiles_s)
    return (yp[0], ys[0], yp[1], yp[2], yp[3], yp[4], ys[1], ys[2], ys[3], ys[4])
```

```python
import functools

import jax
import jax.numpy as jnp
from jax import lax
from jax.experimental import pallas as pl
from jax.experimental.pallas import tpu as pltpu

F32 = jnp.float32
BF16 = jnp.bfloat16

HEAD = 64
PAIR = 2 * HEAD
CHUNK = 64
SOLVE_BLOCK = 16
SUBLANES = 8
CONV_TAPS = 31
CONV_HIST = CONV_TAPS - 1
CONV_PAD = 32
POOL_WINDOWS = (2, 4, 8, 16)
POOL_HIST = max(POOL_WINDOWS) - 1
POOL_PAD = 16
PAST_LEN = 16384
LN_EPS = 1e-5
GN_EPS = 64e-5
VMEM_LIMIT = 56 * 1024 * 1024


def _ln(z, g, b, eps):
    mu = jnp.mean(z, axis=-1, keepdims=True)
    zc = z - mu
    var = jnp.mean(zc * zc, axis=-1, keepdims=True)
    return zc * lax.rsqrt(var + eps) * g + b


def _sigmoid(x):
    return 1.0 / (1.0 + jnp.exp(-x))


def _cparams(sem):
    return pltpu.CompilerParams(dimension_semantics=sem, vmem_limit_bytes=VMEM_LIMIT)


def _log2(n):
    assert n & (n - 1) == 0
    return n.bit_length() - 1


def _inproj_kernel(x_ref, wa_ref, wb_ref, oa_ref, ob_ref):
    xb = x_ref[...].astype(BF16)
    oa_ref[...] = jnp.dot(xb, wa_ref[...], preferred_element_type=F32)
    ob_ref[...] = jnp.dot(xb, wb_ref[...], preferred_element_type=F32)


def _inproj(x, wa, wb, tm):
    T, D = x.shape
    na, nb = wa.shape[1], wb.shape[1]
    return pl.pallas_call(
        _inproj_kernel,
        grid=(T // tm,),
        in_specs=[pl.BlockSpec((tm, D), lambda i: (i, 0)),
                  pl.BlockSpec((D, na), lambda i: (0, 0)),
                  pl.BlockSpec((D, nb), lambda i: (0, 0))],
        out_specs=[pl.BlockSpec((tm, na), lambda i: (i, 0)),
                   pl.BlockSpec((tm, nb), lambda i: (i, 0))],
        out_shape=[jax.ShapeDtypeStruct((T, na), F32), jax.ShapeDtypeStruct((T, nb), F32)],
        compiler_params=_cparams(("parallel",)),
        name="in_proj",
    )(x, wa, wb)


def _conv_kernel(p_ref, st_ref, w_ref, b_ref, g_ref, bb_ref, o_ref, nst_ref, ext_ref, *, tl, width):
    c = pl.program_id(1)

    base = CONV_PAD - CONV_HIST

    @pl.when(c == 0)
    def _():
        ext_ref[:, base:CONV_PAD, :] = st_ref[...]
        ext_ref[:, CONV_PAD + tl:CONV_PAD + tl + SUBLANES, :] = jnp.zeros((p_ref.shape[0], SUBLANES, width), F32)

    p = p_ref[...]
    u = p[..., :width] * _sigmoid(p[..., width:])
    ext_ref[:, CONV_PAD:CONV_PAD + tl, :] = u
    acc = None
    for rho in range(SUBLANES):
        part = None
        for m in range((base + CONV_TAPS) // SUBLANES + 1):
            j = SUBLANES * m + rho - base
            if 0 <= j < CONV_TAPS:
                term = ext_ref[:, SUBLANES * m:SUBLANES * m + tl + SUBLANES, :] * w_ref[j:j + 1, :]
                part = term if part is None else part + term
        shifted = part[:, rho:rho + tl, :]
        acc = shifted if acc is None else acc + shifted
    acc = acc + b_ref[...]
    y = _ln(acc, g_ref[...], bb_ref[...], LN_EPS)
    o_ref[...] = y * _sigmoid(y)
    tail = ext_ref[:, tl + CONV_PAD - CONV_HIST:tl + CONV_PAD, :]
    ext_ref[:, CONV_PAD - CONV_HIST:CONV_PAD, :] = tail
    nst_ref[...] = tail


def _conv_branch(proj_a, state, w, b, g, bb, nb, tl):
    n_seq, seq_len, two_w = proj_a.shape
    width = two_w // 2
    vec = pl.BlockSpec((1, width), lambda n, c: (0, 0))
    return pl.pallas_call(
        functools.partial(_conv_kernel, tl=tl, width=width),
        grid=(n_seq // nb, seq_len // tl),
        in_specs=[pl.BlockSpec((nb, tl, two_w), lambda n, c: (n, c, 0)),
                  pl.BlockSpec((nb, CONV_HIST, width), lambda n, c: (n, 0, 0)),
                  pl.BlockSpec((CONV_TAPS, width), lambda n, c: (0, 0)),
                  vec, vec, vec],
        out_specs=[pl.BlockSpec((nb, tl, width), lambda n, c: (n, c, 0)),
                   pl.BlockSpec((nb, CONV_HIST, width), lambda n, c: (n, 0, 0))],
        out_shape=[jax.ShapeDtypeStruct((n_seq, seq_len, width), F32),
                   jax.ShapeDtypeStruct((n_seq, CONV_HIST, width), F32)],
        scratch_shapes=[pltpu.VMEM((nb, CONV_PAD + tl + SUBLANES, width), F32)],
        compiler_params=_cparams(("parallel", "arbitrary")),
        name="conv_branch",
    )(proj_a, state, w, b, g, bb)


def _split(x):
    hi = x.astype(BF16)
    return hi, (x - hi.astype(F32)).astype(BF16)


def _dotb(a, b):
    return jnp.dot(a, b, preferred_element_type=F32)


def _dotb_nt(a, b):
    return lax.dot_general(a, b, (((1,), (1,)), ((), ())), preferred_element_type=F32)


def _b(x):
    return x.astype(BF16)


def _rwkv_kernel(p_ref, sh_ref, wkv_ref, mu_ref, w0_ref, w2a2_ref, a0_ref, g2_ref, kk_ref, ka_ref, rk_ref,
                 lng_ref, lnb_ref, o_ref, nsh_ref, nwkv_ref, s_scr, prev_scr, *, nsb, tb, bw, lw):
    c = pl.program_id(1)
    n_pairs = bw // PAIR
    C = CHUNK
    R = nsb * tb
    nq = R // C
    ts = min(tb, C)
    ns = C // ts
    assert HEAD == C and nq * C == R and (nsb == 1 or tb == ts)
    ts_bits = _log2(ts)
    sb = min(SOLVE_BLOCK, ts)
    sb_bits = _log2(sb)

    @pl.when(c == 0)
    def _():
        prev_scr[...] = sh_ref[...]
        s_scr[...] = jnp.zeros_like(s_scr)
        for j in range(nsb):
            for h in range(2 * n_pairs):
                lo = (h % 2) * HEAD
                s_scr[j, h // 2, lo:lo + HEAD, lo:lo + HEAD] = wkv_ref[j, h]

    p = p_ref[...]
    pw = p.shape[1]
    row = lax.broadcasted_iota(jnp.int32, (R, 1), 0)
    if nsb == 1:
        first = prev_scr[0]
    else:
        first = jnp.concatenate([jnp.broadcast_to(prev_scr[j], (tb, pw)) for j in range(nsb)], axis=0)
    prev = jnp.where((row & (tb - 1)) == 0, first, pltpu.roll(p, 1, 0))
    for j in range(nsb):
        last = p[(j + 1) * tb - 1:(j + 1) * tb, :]
        prev_scr[j] = last
        nsh_ref[j] = last
    xm = p + mu_ref[...] * (prev - p)
    r = xm[:, 0:bw]
    k = xm[:, bw:2 * bw]
    v = xm[:, 2 * bw:3 * bw]
    dwa = xm[:, 3 * bw:3 * bw + 2 * lw]
    dg = xm[:, 3 * bw + 2 * lw:]
    lane_l = lax.broadcasted_iota(jnp.int32, (1, 2 * lw), 1)
    is_w = lane_l < lw
    w2a2 = _b(w2a2_ref[...])
    wpre = w0_ref[...] + _dotb(_b(jnp.where(is_w, jnp.tanh(dwa), 0.0)), w2a2)
    apre = a0_ref[...] + _dotb(_b(jnp.where(is_w, 0.0, dwa)), w2a2)
    z = -wpre
    wlog = -(jnp.maximum(z, 0.0) + jnp.log1p(jnp.exp(-jnp.abs(z)))) - 0.5
    logw = -jnp.exp(wlog)
    a_lr = _sigmoid(apre)
    gate = _dotb(_b(_sigmoid(dg)), _b(g2_ref[...]))
    kkv = k * kk_ref[...]
    k2 = k * (1.0 + (a_lr - 1.0) * ka_ref[...])
    rk2 = r * k2 * rk_ref[...]

    lane = lax.broadcasted_iota(jnp.int32, (1, PAIR), 1)
    m_l = lane < HEAD
    rowp = lax.broadcasted_iota(jnp.int32, (PAIR, PAIR), 0)
    colp = lax.broadcasted_iota(jnp.int32, (PAIR, PAIR), 1)
    bd = (rowp >= HEAD) == (colp >= HEAD)
    bd2 = jnp.concatenate([bd, bd], axis=0).astype(BF16)
    t_i = lax.broadcasted_iota(jnp.int32, (C, PAIR), 0)
    s_i = lax.broadcasted_iota(jnp.int32, (C, PAIR), 1) & (HEAD - 1)
    same = (s_i >> ts_bits) == (t_i >> ts_bits)
    strict = same & (s_i < t_i)
    incl = same & (s_i <= t_i)
    eye = (s_i == t_i).astype(F32)
    dblk = (s_i >> sb_bits) == (t_i >> sb_bits)
    t_c = lax.broadcasted_iota(jnp.int32, (R, R), 0)
    s_c = lax.broadcasted_iota(jnp.int32, (R, R), 1)
    same_c = (s_c >> ts_bits) == (t_c >> ts_bits)
    tri_b = (same_c & (s_c <= t_c)).astype(BF16)
    ones_b = same_c.astype(BF16)

    def vs(y):
        return jnp.concatenate([jnp.where(m_l, y, 0.0), jnp.where(m_l, 0.0, y)], axis=0)

    def pm(a, y):
        return _dotb(_b(a), vs(_b(y)))

    def pmm(lhs, y):
        out = _dotb(_b(jnp.concatenate(lhs, axis=0)), vs(_b(y)))
        return [out[k * C:(k + 1) * C] for k in range(len(lhs))]

    def bdsum(xs):
        hi, lo = _split(jnp.concatenate(xs, axis=0))
        out = _dotb(jnp.concatenate([hi, lo], axis=1), bd2)
        rows = xs[0].shape[0]
        return [out[k * rows:(k + 1) * rows] for k in range(len(xs))]

    P = range(n_pairs)
    units = [(q, i) for q in range(nq) for i in P]
    NU = range(len(units))

    def blk(x, n):
        q, i = units[n]
        return x[q * C:(q + 1) * C, i * PAIR:(i + 1) * PAIR]

    def neumann(m, n_terms):
        t = [eye + m[n] for n in NU]
        levels = _log2(n_terms)
        if levels == 1:
            return t
        p = [pm(m[n], m[n]) for n in NU]
        for _ in range(levels - 2):
            both = [pmm([p[n], t[n]], p[n]) for n in NU]
            t = [t[n] + both[n][1] for n in NU]
            p = [both[n][0] for n in NU]
        return [t[n] + pm(t[n], p[n]) for n in NU]

    lw_hi, lw_lo = _split(logw)
    lw2 = jnp.concatenate([lw_hi, lw_lo], axis=0)
    cs = _dotb(jnp.concatenate([tri_b, tri_b], axis=1), lw2)
    g_in = jnp.exp(cs)
    g_ex = jnp.exp(cs - logw)
    g_inv = jnp.exp(-cs)
    g_end = jnp.exp(_dotb(jnp.concatenate([ones_b, ones_b], axis=1), lw2))

    sums = bdsum([blk(kkv, n) * blk(kkv, n) for n in NU] + [blk(rk2, n) for n in NU])
    kk_n = [blk(kkv, n) / jnp.maximum(jnp.sqrt(sums[n]), 1e-12) for n in NU]
    v_p = [blk(v, n) for n in NU]
    at = [-kk_n[n] * blk(g_ex, n) for n in NU]
    rt = [blk(r, n) * blk(g_in, n) for n in NU]
    bt = [kk_n[n] * blk(a_lr, n) * blk(g_inv, n) for n in NU]
    kt = [blk(k2, n) * blk(g_inv, n) for n in NU]
    sm = [_dotb_nt(_b(jnp.concatenate([at[n], rt[n]], axis=0)),
                   jnp.concatenate([vs(_b(bt[n])), vs(_b(kt[n]))], axis=0)) for n in NU]
    lab = [jnp.where(strict, sm[n][0:C, 0:PAIR], 0.0) for n in NU]
    lak = [jnp.where(strict, sm[n][0:C, PAIR:2 * PAIR], 0.0) for n in NU]
    mrb = [jnp.where(incl, sm[n][C:2 * C, 0:PAIR], 0.0) for n in NU]
    mrk = [jnp.where(incl, sm[n][C:2 * C, PAIR:2 * PAIR], 0.0) for n in NU]
    d1 = [jnp.where(dblk, lab[n], 0.0) for n in NU]
    tinv = neumann(d1, sb)
    if sb < ts:
        e1 = [pm(tinv[n], lab[n] - d1[n]) for n in NU]
        te = neumann(e1, ts // sb)
        tinv = [pm(te[n], tinv[n]) for n in NU]
    res = [eye - tinv[n] + pm(lab[n], tinv[n]) for n in NU]
    tinv = [_b(tinv[n] + pm(tinv[n], res[n])) for n in NU]
    xv = [pmm([lak[n], mrk[n]], v_p[n]) for n in NU]
    x0 = [xv[n][0] for n in NU]
    s_cur = [[s_scr[sq, i] for i in P] for sq in range(nsb)]
    col_seq = (lax.broadcasted_iota(jnp.int32, (1, PAIR), 1) & (HEAD - 1)) >> ts_bits
    o = []
    for q in range(nq):
        un = [q * n_pairs + i for i in P]
        seq = [q * ns + j if nsb > 1 else 0 for j in range(ns)]
        a_s, r_s = [], []
        for i in P:
            parts = []
            for j in range(ns):
                rows = slice(j * ts, (j + 1) * ts)
                lhs = _b(jnp.concatenate([at[un[i]][rows], rt[un[i]][rows]], axis=0))
                parts.append(_dotb_nt(lhs, _b(s_cur[seq[j]][i])))
            a_s.append(jnp.concatenate([x[0:ts] for x in parts], axis=0))
            r_s.append(jnp.concatenate([x[ts:2 * ts] for x in parts], axis=0))
        u = [_dotb(tinv[un[i]], vs(_b(a_s[i] + x0[un[i]]))) for i in P]
        for i in P:
            n = un[i]
            ge = blk(g_end, n)
            uv = _b(jnp.concatenate([u[i], v_p[n]], axis=0).T)
            bk = _b(jnp.concatenate([bt[n] * ge, kt[n] * ge], axis=0))
            for j in range(ns):
                uv_j = uv if ns == 1 else jnp.where(col_seq == j, uv, 0.0)
                upd = _dotb(uv_j, bk)
                s_cur[seq[j]][i] = s_cur[seq[j]][i] * ge[j * ts:j * ts + 1, :] + jnp.where(bd, upd, 0.0)
        o += [r_s[i] + pm(mrb[un[i]], u[i]) + xv[un[i]][1] for i in P]
    for sq in range(nsb):
        for i in P:
            s_scr[sq, i] = s_cur[sq][i]
    om = bdsum(o)
    oc = [o[n] - om[n] * (1.0 / HEAD) for n in NU]
    ov = bdsum([oc[n] * oc[n] for n in NU])
    for n in NU:
        q, i = units[n]
        ln = slice(i * PAIR, (i + 1) * PAIR)
        bonus = sums[len(units) + n] * v_p[n]
        y = oc[n] * lax.rsqrt(ov[n] * (1.0 / HEAD) + GN_EPS) * lng_ref[:, ln] + lnb_ref[:, ln] + bonus
        o_ref[q * C:(q + 1) * C, ln] = y * blk(gate, n)

    @pl.when(c == pl.num_programs(1) - 1)
    def _():
        for j in range(nsb):
            for h in range(2 * n_pairs):
                lo = (h % 2) * HEAD
                nwkv_ref[j, h] = s_scr[j, h // 2, lo:lo + HEAD, lo:lo + HEAD]


def _rwkv_branch(proj_b, shift, wkv, prm, n_seq, seq_len, nsb, tb):
    T, pw = proj_b.shape
    n_heads = wkv.shape[1]
    bw = n_heads * HEAD
    lw = prm["w2a2"].shape[0] // 2
    nc = seq_len // tb
    assert nsb == 1 or nc == 1
    rows = nsb * tb

    def full(a):
        return pl.BlockSpec(a.shape, lambda n, c: (0,) * a.ndim)

    names = ("mu", "w0", "w2a2", "a0", "g2", "kk", "ka", "rk", "lng", "lnb")
    return pl.pallas_call(
        functools.partial(_rwkv_kernel, nsb=nsb, tb=tb, bw=bw, lw=lw),
        grid=(n_seq // nsb, nc),
        in_specs=[pl.BlockSpec((rows, pw), lambda n, c: (n * nc + c, 0)),
                  pl.BlockSpec((nsb, 1, pw), lambda n, c: (n, 0, 0)),
                  pl.BlockSpec((nsb, n_heads, HEAD, HEAD), lambda n, c: (n, 0, 0, 0))]
                 + [full(prm[k]) for k in names],
        out_specs=[pl.BlockSpec((rows, bw), lambda n, c: (n * nc + c, 0)),
                   pl.BlockSpec((nsb, 1, pw), lambda n, c: (n, 0, 0)),
                   pl.BlockSpec((nsb, n_heads, HEAD, HEAD), lambda n, c: (n, 0, 0, 0))],
        out_shape=[jax.ShapeDtypeStruct((T, bw), F32),
                   jax.ShapeDtypeStruct((n_seq, 1, pw), F32),
                   jax.ShapeDtypeStruct((n_seq, n_heads, HEAD, HEAD), F32)],
        scratch_shapes=[pltpu.VMEM((nsb, n_heads // 2, PAIR, PAIR), F32), pltpu.VMEM((nsb, 1, pw), F32)],
        compiler_params=_cparams(("parallel", "arbitrary")),
        name="rwkv_branch",
    )(proj_b, shift.reshape(n_seq, 1, pw), wkv, *[prm[k] for k in names])


def _outproj_kernel(a_ref, b_ref, x_ref, wa_ref, wb_ref, g_ref, bb_ref, o_ref, *, alpha):
    mix = jnp.dot(a_ref[...].astype(BF16), wa_ref[...], preferred_element_type=F32)
    mix = mix + jnp.dot(b_ref[...].astype(BF16), wb_ref[...], preferred_element_type=F32)
    o_ref[...] = _ln(alpha * x_ref[...] + mix, g_ref[...], bb_ref[...], LN_EPS)


def _outproj(a, b, x, wa, wb, g, bb, alpha, tm):
    T, D = x.shape
    wdt = a.shape[1]
    vec = pl.BlockSpec((1, D), lambda i: (0, 0))
    return pl.pallas_call(
        functools.partial(_outproj_kernel, alpha=alpha),
        grid=(T // tm,),
        in_specs=[pl.BlockSpec((tm, wdt), lambda i: (i, 0)),
                  pl.BlockSpec((tm, wdt), lambda i: (i, 0)),
                  pl.BlockSpec((tm, D), lambda i: (i, 0)),
                  pl.BlockSpec((wdt, D), lambda i: (0, 0)),
                  pl.BlockSpec((wdt, D), lambda i: (0, 0)),
                  vec, vec],
        out_specs=pl.BlockSpec((tm, D), lambda i: (i, 0)),
        out_shape=jax.ShapeDtypeStruct((T, D), F32),
        compiler_params=_cparams(("parallel",)),
        name="out_proj_ln",
    )(a, b, x, wa, wb, g, bb)


def _ffn_kernel(x_ref, wg_ref, wu_ref, wd_ref, g_ref, bb_ref, o_ref, acc_ref, xb_ref, *, alpha):
    j = pl.program_id(1)

    @pl.when(j == 0)
    def _():
        xb_ref[...] = x_ref[...].astype(BF16)
        acc_ref[...] = jnp.zeros_like(acc_ref)

    xb = xb_ref[...]
    hg = jnp.dot(xb, wg_ref[...], preferred_element_type=F32)
    hu = jnp.dot(xb, wu_ref[...], preferred_element_type=F32)
    h = (hg * _sigmoid(hg) * hu).astype(BF16)
    acc_ref[...] += jnp.dot(h, wd_ref[...], preferred_element_type=F32)

    @pl.when(j == pl.num_programs(1) - 1)
    def _():
        o_ref[...] = _ln(alpha * x_ref[...] + acc_ref[...], g_ref[...], bb_ref[...], LN_EPS)


def _ffn(x, wg, wu, wd, g, bb, alpha, tm, tf):
    T, D = x.shape
    Fd = wg.shape[1]
    vec = pl.BlockSpec((1, D), lambda i, j: (0, 0))
    return pl.pallas_call(
        functools.partial(_ffn_kernel, alpha=alpha),
        grid=(T // tm, Fd // tf),
        in_specs=[pl.BlockSpec((tm, D), lambda i, j: (i, 0)),
                  pl.BlockSpec((D, tf), lambda i, j: (0, j)),
                  pl.BlockSpec((D, tf), lambda i, j: (0, j)),
                  pl.BlockSpec((tf, D), lambda i, j: (j, 0)),
                  vec, vec],
        out_specs=pl.BlockSpec((tm, D), lambda i, j: (i, 0)),
        out_shape=jax.ShapeDtypeStruct((T, D), F32),
        scratch_shapes=[pltpu.VMEM((tm, D), F32), pltpu.VMEM((tm, D), BF16)],
        compiler_params=_cparams(("parallel", "arbitrary")),
        name="ffn_ln",
    )(x, wg, wu, wd, g, bb)


def _pool_kernel(x_ref, st_ref, w_ref, sc_ref, g_ref, bb_ref, o_ref, nst_ref, ext_ref, *, tl, start, alpha):
    c = pl.program_id(1)

    @pl.when(c == 0)
    def _():
        ext_ref[:, POOL_PAD - POOL_HIST:POOL_PAD, :] = st_ref[...]

    x = x_ref[...]
    nb, _, D = x.shape
    gw = D // len(POOL_WINDOWS)
    ext_ref[:, POOL_PAD:POOL_PAD + tl, :] = x
    pos = start + c * tl + lax.broadcasted_iota(jnp.int32, (1, tl, 1), 1)
    outs = []
    for gi, wdw in enumerate(POOL_WINDOWS):
        lo = gi * gw
        ssum = x[..., lo:lo + gw]
        for j in range(1, wdw):
            ssum = ssum + ext_ref[:, POOL_PAD - j:POOL_PAD - j + tl, lo:lo + gw]
        cnt = jnp.minimum(wdw, pos + 1).astype(F32)
        pooled = ssum / cnt - x[..., lo:lo + gw]
        y = jnp.dot(pooled.reshape(nb * tl, gw).astype(BF16), w_ref[gi], preferred_element_type=F32)
        outs.append(y.reshape(nb, tl, gw))
    y = jnp.concatenate(outs, axis=-1) * sc_ref[...]
    o_ref[...] = _ln(alpha * x + y, g_ref[...], bb_ref[...], LN_EPS)
    tail = ext_ref[:, tl + POOL_PAD - POOL_HIST:tl + POOL_PAD, :]
    ext_ref[:, POOL_PAD - POOL_HIST:POOL_PAD, :] = tail
    nst_ref[...] = tail


def _pool_mix(x, state, w, scale, g, bb, nb, tl, start, alpha):
    n_seq, seq_len, D = x.shape
    vec = pl.BlockSpec((1, D), lambda n, c: (0, 0))
    return pl.pallas_call(
        functools.partial(_pool_kernel, tl=tl, start=start, alpha=alpha),
        grid=(n_seq // nb, seq_len // tl),
        in_specs=[pl.BlockSpec((nb, tl, D), lambda n, c: (n, c, 0)),
                  pl.BlockSpec((nb, POOL_HIST, D), lambda n, c: (n, 0, 0)),
                  pl.BlockSpec(w.shape, lambda n, c: (0, 0, 0)),
                  vec, vec, vec],
        out_specs=[pl.BlockSpec((nb, tl, D), lambda n, c: (n, c, 0)),
                   pl.BlockSpec((nb, POOL_HIST, D), lambda n, c: (n, 0, 0))],
        out_shape=[jax.ShapeDtypeStruct((n_seq, seq_len, D), F32),
                   jax.ShapeDtypeStruct((n_seq, POOL_HIST, D), F32)],
        scratch_shapes=[pltpu.VMEM((nb, POOL_PAD + tl, D), F32)],
        compiler_params=_cparams(("parallel", "arbitrary")),
        name="pool_mix_ln",
    )(x, state, w, scale, g, bb)


def _trunk(x3, conv_prev, shift_prev, wkv_prev, pool_prev, start, prm, tiles):
    n_seq, seq_len, D = x3.shape
    T = n_seq * seq_len
    x = x3.reshape(T, D)
    alpha = prm["alpha"]
    tm = tiles["tm"]
    proj_a, proj_b = _inproj(x, prm["w_in_a"], prm["w_in_b"], tm)
    a_out, new_conv = _conv_branch(proj_a.reshape(n_seq, seq_len, -1), conv_prev, prm["conv_w"], prm["conv_b"],
                                   prm["conv_ln_g"], prm["conv_ln_b"], tiles["seq_nb"], tiles["conv_tl"])
    b_out, new_shift, new_wkv = _rwkv_branch(proj_b, shift_prev, wkv_prev, prm["rwkv"], n_seq, seq_len,
                                             tiles["rwkv_nsb"], tiles["rwkv_tb"])
    x = _outproj(a_out.reshape(T, -1), b_out, x, prm["w_out_a"], prm["w_out_b"], prm["ln_mix_g"][0],
                 prm["ln_mix_b"][0], alpha, tm)
    x = _ffn(x, prm["ffn_gate"][0], prm["ffn_up"][0], prm["ffn_down"][0], prm["ln_ffn_g"][0], prm["ln_ffn_b"][0],
             alpha, tm, tiles["tf"])
    x, new_pool = _pool_mix(x.reshape(n_seq, seq_len, D), pool_prev, prm["pool_w"], prm["pool_scale"],
                            prm["ln_mix_g"][1], prm["ln_mix_b"][1], tiles["seq_nb"], tiles["pool_tl"], start, alpha)
    x = _ffn(x.reshape(T, D), prm["ffn_gate"][1], prm["ffn_up"][1], prm["ffn_down"][1], prm["ln_ffn_g"][1],
             prm["ln_ffn_b"][1], alpha, tm, tiles["tf"])
    return (x.reshape(n_seq, seq_len, D), new_conv[None], new_shift.reshape(1, n_seq, -1), new_wkv[None],
            new_pool[None])


def kernel(x_prompt, x_sample, state_conv, state_shift, state_wkv, state_pool, w_in, conv_w, conv_b, conv_ln_g, conv_ln_b, rwkv_mu, rwkv_w0, rwkv_w2, rwkv_a0, rwkv_a2, rwkv_g2, rwkv_kk, rwkv_ka, rwkv_rk, rwkv_lnx_g, rwkv_lnx_b, w_out, pool_w, pool_scale, ln_mix_g, ln_mix_b, ffn_gate, ffn_up, ffn_down, ln_ffn_g, ln_ffn_b):
    depth = ln_mix_g.shape[0]
    assert depth == 2 and w_in.shape[0] == 1 and pool_w.shape[0] == 1
    B, L, D = x_prompt.shape
    Bs, Ls, _ = x_sample.shape
    a_w = conv_w.shape[2]
    b_w = rwkv_w0.shape[1]
    n_heads = b_w // HEAD
    alpha = float((2 * depth) ** 0.25)

    row = lambda a: a.reshape(1, -1)
    prm = dict(
        alpha=alpha,
        w_in_a=w_in[0, :, :2 * a_w].astype(BF16), w_in_b=w_in[0, :, 2 * a_w:].astype(BF16),
        conv_w=conv_w[0], conv_b=row(conv_b[0]), conv_ln_g=row(conv_ln_g[0]), conv_ln_b=row(conv_ln_b[0]),
        rwkv=dict(mu=row(rwkv_mu[0]), w0=row(rwkv_w0[0]),
                  w2a2=jnp.concatenate([rwkv_w2[0], rwkv_a2[0]], axis=0), a0=row(rwkv_a0[0]), g2=rwkv_g2[0],
                  kk=row(rwkv_kk[0]), ka=row(rwkv_ka[0]), rk=row(rwkv_rk[0]),
                  lng=row(rwkv_lnx_g[0]), lnb=row(rwkv_lnx_b[0])),
        w_out_a=w_out[0, :a_w].astype(BF16), w_out_b=w_out[0, a_w:].astype(BF16),
        pool_w=pool_w[0].astype(BF16), pool_scale=row(pool_scale[0]),
        ln_mix_g=[row(ln_mix_g[i]) for i in range(depth)], ln_mix_b=[row(ln_mix_b[i]) for i in range(depth)],
        ffn_gate=[ffn_gate[i].astype(BF16) for i in range(depth)],
        ffn_up=[ffn_up[i].astype(BF16) for i in range(depth)],
        ffn_down=[ffn_down[i].astype(BF16) for i in range(depth)],
        ln_ffn_g=[row(ln_ffn_g[i]) for i in range(depth)], ln_ffn_b=[row(ln_ffn_b[i]) for i in range(depth)],
    )
    dt = x_prompt.dtype
    z_conv = jnp.zeros((B, CONV_HIST, a_w), dt)
    z_shift = jnp.zeros((B, state_shift.shape[-1]), dt)
    z_wkv = jnp.zeros((B, n_heads, HEAD, HEAD), state_wkv.dtype)
    z_pool = jnp.zeros((B, POOL_HIST, D), dt)
    tiles_p = dict(tm=1024, tf=256, seq_nb=1, conv_tl=256, pool_tl=512, rwkv_nsb=1, rwkv_tb=4 * CHUNK)
    tiles_s = dict(tm=1024, tf=256, seq_nb=32, conv_tl=Ls, pool_tl=Ls, rwkv_nsb=4 * CHUNK // Ls, rwkv_tb=Ls)
    yp = _trunk(x_prompt, z_conv, z_shift, z_wkv, z_pool, 0, prm, tiles_p)
    ys = _trunk(x_sample, state_conv[0], state_shift[0], state_wkv[0], state_pool[0], PAST_LEN, prm, tiles_s)
    return (yp[0], ys[0], yp[1], yp[2], yp[3], yp[4], ys[1], ys[2], ys[3], ys[4])
```

```python
import functools

import jax
import jax.numpy as jnp
from jax import lax
from jax.experimental import pallas as pl
from jax.experimental.pallas import tpu as pltpu

F32 = jnp.float32
BF16 = jnp.bfloat16

HEAD = 64
PAIR = 2 * HEAD
CHUNK = 64
SOLVE_BLOCK = 16
SUBLANES = 8
CONV_TAPS = 31
CONV_HIST = CONV_TAPS - 1
CONV_PAD = 32
POOL_WINDOWS = (2, 4, 8, 16)
POOL_HIST = max(POOL_WINDOWS) - 1
POOL_PAD = 16
PAST_LEN = 16384
LN_EPS = 1e-5
GN_EPS = 64e-5
VMEM_LIMIT = 56 * 1024 * 1024


def _ln(z, g, b, eps):
    mu = jnp.mean(z, axis=-1, keepdims=True)
    zc = z - mu
    var = jnp.mean(zc * zc, axis=-1, keepdims=True)
    return zc * lax.rsqrt(var + eps) * g + b


def _sigmoid(x):
    return 1.0 / (1.0 + jnp.exp(-x))


def _cparams(sem):
    return pltpu.CompilerParams(dimension_semantics=sem, vmem_limit_bytes=VMEM_LIMIT)


def _log2(n):
    assert n & (n - 1) == 0
    return n.bit_length() - 1


def _inproj_kernel(x_ref, wa_ref, wb_ref, oa_ref, ob_ref):
    xb = x_ref[...].astype(BF16)
    oa_ref[...] = jnp.dot(xb, wa_ref[...], preferred_element_type=F32)
    ob_ref[...] = jnp.dot(xb, wb_ref[...], preferred_element_type=F32)


def _inproj(x, wa, wb, tm):
    T, D = x.shape
    na, nb = wa.shape[1], wb.shape[1]
    return pl.pallas_call(
        _inproj_kernel,
        grid=(T // tm,),
        in_specs=[pl.BlockSpec((tm, D), lambda i: (i, 0)),
                  pl.BlockSpec((D, na), lambda i: (0, 0)),
                  pl.BlockSpec((D, nb), lambda i: (0, 0))],
        out_specs=[pl.BlockSpec((tm, na), lambda i: (i, 0)),
                   pl.BlockSpec((tm, nb), lambda i: (i, 0))],
        out_shape=[jax.ShapeDtypeStruct((T, na), F32), jax.ShapeDtypeStruct((T, nb), F32)],
        compiler_params=_cparams(("parallel",)),
        name="in_proj",
    )(x, wa, wb)


def _conv_kernel(p_ref, st_ref, w_ref, b_ref, g_ref, bb_ref, o_ref, nst_ref, ext_ref, *, tl, width):
    c = pl.program_id(1)

    base = CONV_PAD - CONV_HIST

    @pl.when(c == 0)
    def _():
        ext_ref[:, base:CONV_PAD, :] = st_ref[...]
        ext_ref[:, CONV_PAD + tl:CONV_PAD + tl + SUBLANES, :] = jnp.zeros((p_ref.shape[0], SUBLANES, width), F32)

    p = p_ref[...]
    u = p[..., :width] * _sigmoid(p[..., width:])
    ext_ref[:, CONV_PAD:CONV_PAD + tl, :] = u
    acc = None
    for rho in range(SUBLANES):
        part = None
        for m in range((base + CONV_TAPS) // SUBLANES + 1):
            j = SUBLANES * m + rho - base
            if 0 <= j < CONV_TAPS:
                term = ext_ref[:, SUBLANES * m:SUBLANES * m + tl + SUBLANES, :] * w_ref[j:j + 1, :]
                part = term if part is None else part + term
        shifted = part[:, rho:rho + tl, :]
        acc = shifted if acc is None else acc + shifted
    acc = acc + b_ref[...]
    y = _ln(acc, g_ref[...], bb_ref[...], LN_EPS)
    o_ref[...] = y * _sigmoid(y)
    tail = ext_ref[:, tl + CONV_PAD - CONV_HIST:tl + CONV_PAD, :]
    ext_ref[:, CONV_PAD - CONV_HIST:CONV_PAD, :] = tail
    nst_ref[...] = tail


def _conv_branch(proj_a, state, w, b, g, bb, nb, tl):
    n_seq, seq_len, two_w = proj_a.shape
    width = two_w // 2
    vec = pl.BlockSpec((1, width), lambda n, c: (0, 0))
    return pl.pallas_call(
        functools.partial(_conv_kernel, tl=tl, width=width),
        grid=(n_seq // nb, seq_len // tl),
        in_specs=[pl.BlockSpec((nb, tl, two_w), lambda n, c: (n, c, 0)),
                  pl.BlockSpec((nb, CONV_HIST, width), lambda n, c: (n, 0, 0)),
                  pl.BlockSpec((CONV_TAPS, width), lambda n, c: (0, 0)),
                  vec, vec, vec],
        out_specs=[pl.BlockSpec((nb, tl, width), lambda n, c: (n, c, 0)),
                   pl.BlockSpec((nb, CONV_HIST, width), lambda n, c: (n, 0, 0))],
        out_shape=[jax.ShapeDtypeStruct((n_seq, seq_len, width), F32),
                   jax.ShapeDtypeStruct((n_seq, CONV_HIST, width), F32)],
        scratch_shapes=[pltpu.VMEM((nb, CONV_PAD + tl + SUBLANES, width), F32)],
        compiler_params=_cparams(("parallel", "arbitrary")),
        name="conv_branch",
    )(proj_a, state, w, b, g, bb)


def _split(x):
    hi = x.astype(BF16)
    return hi, (x - hi.astype(F32)).astype(BF16)


def _dotb(a, b):
    return jnp.dot(a, b, preferred_element_type=F32)


def _dotb_nt(a, b):
    return lax.dot_general(a, b, (((1,), (1,)), ((), ())), preferred_element_type=F32)


def _b(x):
    return x.astype(BF16)


def _rwkv_kernel(p_ref, sh_ref, wkv_ref, mu_ref, w0_ref, w2a2_ref, a0_ref, g2_ref, kk_ref, ka_ref, rk_ref,
                 lng_ref, lnb_ref, o_ref, nsh_ref, nwkv_ref, s_scr, prev_scr, *, nsb, tb, bw, lw):
    c = pl.program_id(1)
    n_pairs = bw // PAIR
    C = CHUNK
    R = nsb * tb
    nq = R // C
    ts = min(tb, C)
    ns = C // ts
    assert HEAD == C and nq * C == R and (nsb == 1 or tb == ts)
    ts_bits = _log2(ts)
    sb = min(SOLVE_BLOCK, ts)
    sb_bits = _log2(sb)

    @pl.when(c == 0)
    def _():
        prev_scr[...] = sh_ref[...]
        s_scr[...] = jnp.zeros_like(s_scr)
        for j in range(nsb):
            for h in range(2 * n_pairs):
                lo = (h % 2) * HEAD
                s_scr[j, h // 2, lo:lo + HEAD, lo:lo + HEAD] = wkv_ref[j, h]

    p = p_ref[...]
    pw = p.shape[1]
    row = lax.broadcasted_iota(jnp.int32, (R, 1), 0)
    if nsb == 1:
        first = prev_scr[0]
    else:
        first = jnp.concatenate([jnp.broadcast_to(prev_scr[j], (tb, pw)) for j in range(nsb)], axis=0)
    prev = jnp.where((row & (tb - 1)) == 0, first, pltpu.roll(p, 1, 0))
    for j in range(nsb):
        last = p[(j + 1) * tb - 1:(j + 1) * tb, :]
        prev_scr[j] = last
        nsh_ref[j] = last
    xm = p + mu_ref[...] * (prev - p)
    r = xm[:, 0:bw]
    k = xm[:, bw:2 * bw]
    v = xm[:, 2 * bw:3 * bw]
    dwa = xm[:, 3 * bw:3 * bw + 2 * lw]
    dg = xm[:, 3 * bw + 2 * lw:]
    lane_l = lax.broadcasted_iota(jnp.int32, (1, 2 * lw), 1)
    is_w = lane_l < lw
    w2a2 = _b(w2a2_ref[...])
    wpre = w0_ref[...] + _dotb(_b(jnp.where(is_w, jnp.tanh(dwa), 0.0)), w2a2)
    apre = a0_ref[...] + _dotb(_b(jnp.where(is_w, 0.0, dwa)), w2a2)
    z = -wpre
    wlog = -(jnp.maximum(z, 0.0) + jnp.log1p(jnp.exp(-jnp.abs(z)))) - 0.5
    logw = -jnp.exp(wlog)
    a_lr = _sigmoid(apre)
    gate = _dotb(_b(_sigmoid(dg)), _b(g2_ref[...]))
    kkv = k * kk_ref[...]
    k2 = k * (1.0 + (a_lr - 1.0) * ka_ref[...])
    rk2 = r * k2 * rk_ref[...]

    lane = lax.broadcasted_iota(jnp.int32, (1, PAIR), 1)
    m_l = lane < HEAD
    rowp = lax.broadcasted_iota(jnp.int32, (PAIR, PAIR), 0)
    colp = lax.broadcasted_iota(jnp.int32, (PAIR, PAIR), 1)
    bd = (rowp >= HEAD) == (colp >= HEAD)
    bd2 = jnp.concatenate([bd, bd], axis=0).astype(BF16)
    t_i = lax.broadcasted_iota(jnp.int32, (C, PAIR), 0)
    s_i = lax.broadcasted_iota(jnp.int32, (C, PAIR), 1) & (HEAD - 1)
    same = (s_i >> ts_bits) == (t_i >> ts_bits)
    strict = same & (s_i < t_i)
    incl = same & (s_i <= t_i)
    eye = (s_i == t_i).astype(F32)
    dblk = (s_i >> sb_bits) == (t_i >> sb_bits)
    t_c = lax.broadcasted_iota(jnp.int32, (R, R), 0)
    s_c = lax.broadcasted_iota(jnp.int32, (R, R), 1)
    same_c = (s_c >> ts_bits) == (t_c >> ts_bits)
    tri_b = (same_c & (s_c <= t_c)).astype(BF16)
    ones_b = same_c.astype(BF16)

    def vs(y):
        return jnp.concatenate([jnp.where(m_l, y, 0.0), jnp.where(m_l, 0.0, y)], axis=0)

    def pm(a, y):
        return _dotb(_b(a), vs(_b(y)))

    def pmm(lhs, y):
        out = _dotb(_b(jnp.concatenate(lhs, axis=0)), vs(_b(y)))
        return [out[k * C:(k + 1) * C] for k in range(len(lhs))]

    def bdsum(xs):
        hi, lo = _split(jnp.concatenate(xs, axis=0))
        out = _dotb(jnp.concatenate([hi, lo], axis=1), bd2)
        rows = xs[0].shape[0]
        return [out[k * rows:(k + 1) * rows] for k in range(len(xs))]

    P = range(n_pairs)
    units = [(q, i) for q in range(nq) for i in P]
    NU = range(len(units))

    def blk(x, n):
        q, i = units[n]
        return x[q * C:(q + 1) * C, i * PAIR:(i + 1) * PAIR]

    def neumann(m, n_terms):
        t = [eye + m[n] for n in NU]
        levels = _log2(n_terms)
        if levels == 1:
            return t
        p = [pm(m[n], m[n]) for n in NU]
        for _ in range(levels - 2):
            both = [pmm([p[n], t[n]], p[n]) for n in NU]
            t = [t[n] + both[n][1] for n in NU]
            p = [both[n][0] for n in NU]
        return [t[n] + pm(t[n], p[n]) for n in NU]

    lw_hi, lw_lo = _split(logw)
    lw2 = jnp.concatenate([lw_hi, lw_lo], axis=0)
    cs = _dotb(jnp.concatenate([tri_b, tri_b], axis=1), lw2)
    g_in = jnp.exp(cs)
    g_ex = jnp.exp(cs - logw)
    g_inv = jnp.exp(-cs)
    g_end = jnp.exp(_dotb(jnp.concatenate([ones_b, ones_b], axis=1), lw2))

    sums = bdsum([blk(kkv, n) * blk(kkv, n) for n in NU] + [blk(rk2, n) for n in NU])
    kk_n = [blk(kkv, n) / jnp.maximum(jnp.sqrt(sums[n]), 1e-12) for n in NU]
    v_p = [blk(v, n) for n in NU]
    at = [-kk_n[n] * blk(g_ex, n) for n in NU]
    rt = [blk(r, n) * blk(g_in, n) for n in NU]
    bt = [kk_n[n] * blk(a_lr, n) * blk(g_inv, n) for n in NU]
    kt = [blk(k2, n) * blk(g_inv, n) for n in NU]
    sm = [_dotb_nt(_b(jnp.concatenate([at[n], rt[n]], axis=0)),
                   jnp.concatenate([vs(_b(bt[n])), vs(_b(kt[n]))], axis=0)) for n in NU]
    lab = [jnp.where(strict, sm[n][0:C, 0:PAIR], 0.0) for n in NU]
    lak = [jnp.where(strict, sm[n][0:C, PAIR:2 * PAIR], 0.0) for n in NU]
    mrb = [jnp.where(incl, sm[n][C:2 * C, 0:PAIR], 0.0) for n in NU]
    mrk = [jnp.where(incl, sm[n][C:2 * C, PAIR:2 * PAIR], 0.0) for n in NU]
    d1 = [jnp.where(dblk, lab[n], 0.0) for n in NU]
    tinv = neumann(d1, sb)
    if sb < ts:
        e1 = [pm(tinv[n], lab[n] - d1[n]) for n in NU]
        te = neumann(e1, ts // sb)
        tinv = [pm(te[n], tinv[n]) for n in NU]
    res = [eye - tinv[n] + pm(lab[n], tinv[n]) for n in NU]
    tinv = [_b(tinv[n] + pm(tinv[n], res[n])) for n in NU]
    xv = [pmm([lak[n], mrk[n]], v_p[n]) for n in NU]
    x0 = [xv[n][0] for n in NU]
    s_cur = [[s_scr[sq, i] for i in P] for sq in range(nsb)]
    col_seq = (lax.broadcasted_iota(jnp.int32, (1, PAIR), 1) & (HEAD - 1)) >> ts_bits
    o = []
    for q in range(nq):
        un = [q * n_pairs + i for i in P]
        seq = [q * ns + j if nsb > 1 else 0 for j in range(ns)]
        a_s, r_s = [], []
        for i in P:
            parts = []
            for j in range(ns):
                rows = slice(j * ts, (j + 1) * ts)
                lhs = _b(jnp.concatenate([at[un[i]][rows], rt[un[i]][rows]], axis=0))
                parts.append(_dotb_nt(lhs, _b(s_cur[seq[j]][i])))
            a_s.append(jnp.concatenate([x[0:ts] for x in parts], axis=0))
            r_s.append(jnp.concatenate([x[ts:2 * ts] for x in parts], axis=0))
        u = [_dotb(tinv[un[i]], vs(_b(a_s[i] + x0[un[i]]))) for i in P]
        for i in P:
            n = un[i]
            ge = blk(g_end, n)
            uv = _b(jnp.concatenate([u[i], v_p[n]], axis=0).T)
            bk = _b(jnp.concatenate([bt[n] * ge, kt[n] * ge], axis=0))
            for j in range(ns):
                uv_j = uv if ns == 1 else jnp.where(col_seq == j, uv, 0.0)
                upd = _dotb(uv_j, bk)
                s_cur[seq[j]][i] = s_cur[seq[j]][i] * ge[j * ts:j * ts + 1, :] + jnp.where(bd, upd, 0.0)
        o += [r_s[i] + pm(mrb[un[i]], u[i]) + xv[un[i]][1] for i in P]
    for sq in range(nsb):
        for i in P:
            s_scr[sq, i] = s_cur[sq][i]
    om = bdsum(o)
    oc = [o[n] - om[n] * (1.0 / HEAD) for n in NU]
    ov = bdsum([oc[n] * oc[n] for n in NU])
    for n in NU:
        q, i = units[n]
        ln = slice(i * PAIR, (i + 1) * PAIR)
        bonus = sums[len(units) + n] * v_p[n]
        y = oc[n] * lax.rsqrt(ov[n] * (1.0 / HEAD) + GN_EPS) * lng_ref[:, ln] + lnb_ref[:, ln] + bonus
        o_ref[q * C:(q + 1) * C, ln] = y * blk(gate, n)

    @pl.when(c == pl.num_programs(1) - 1)
    def _():
        for j in range(nsb):
            for h in range(2 * n_pairs):
                lo = (h % 2) * HEAD
                nwkv_ref[j, h] = s_scr[j, h // 2, lo:lo + HEAD, lo:lo + HEAD]


def _rwkv_branch(proj_b, shift, wkv, prm, n_seq, seq_len, nsb, tb):
    T, pw = proj_b.shape
    n_heads = wkv.shape[1]
    bw = n_heads * HEAD
    lw = prm["w2a2"].shape[0] // 2
    nc = seq_len // tb
    assert nsb == 1 or nc == 1
    rows = nsb * tb

    def full(a):
        return pl.BlockSpec(a.shape, lambda n, c: (0,) * a.ndim)

    names = ("mu", "w0", "w2a2", "a0", "g2", "kk", "ka", "rk", "lng", "lnb")
    return pl.pallas_call(
        functools.partial(_rwkv_kernel, nsb=nsb, tb=tb, bw=bw, lw=lw),
        grid=(n_seq // nsb, nc),
        in_specs=[pl.BlockSpec((rows, pw), lambda n, c: (n * nc + c, 0)),
                  pl.BlockSpec((nsb, 1, pw), lambda n, c: (n, 0, 0)),
                  pl.BlockSpec((nsb, n_heads, HEAD, HEAD), lambda n, c: (n, 0, 0, 0))]
                 + [full(prm[k]) for k in names],
        out_specs=[pl.BlockSpec((rows, bw), lambda n, c: (n * nc + c, 0)),
                   pl.BlockSpec((nsb, 1, pw), lambda n, c: (n, 0, 0)),
                   pl.BlockSpec((nsb, n_heads, HEAD, HEAD), lambda n, c: (n, 0, 0, 0))],
        out_shape=[jax.ShapeDtypeStruct((T, bw), F32),
                   jax.ShapeDtypeStruct((n_seq, 1, pw), F32),
                   jax.ShapeDtypeStruct((n_seq, n_heads, HEAD, HEAD), F32)],
        scratch_shapes=[pltpu.VMEM((nsb, n_heads // 2, PAIR, PAIR), F32), pltpu.VMEM((nsb, 1, pw), F32)],
        compiler_params=_cparams(("parallel", "arbitrary")),
        name="rwkv_branch",
    )(proj_b, shift.reshape(n_seq, 1, pw), wkv, *[prm[k] for k in names])


def _outproj_kernel(a_ref, b_ref, x_ref, wa_ref, wb_ref, g_ref, bb_ref, o_ref, *, alpha):
    mix = jnp.dot(a_ref[...].astype(BF16), wa_ref[...], preferred_element_type=F32)
    mix = mix + jnp.dot(b_ref[...].astype(BF16), wb_ref[...], preferred_element_type=F32)
    o_ref[...] = _ln(alpha * x_ref[...] + mix, g_ref[...], bb_ref[...], LN_EPS)


def _outproj(a, b, x, wa, wb, g, bb, alpha, tm):
    T, D = x.shape
    wdt = a.shape[1]
    vec = pl.BlockSpec((1, D), lambda i: (0, 0))
    return pl.pallas_call(
        functools.partial(_outproj_kernel, alpha=alpha),
        grid=(T // tm,),
        in_specs=[pl.BlockSpec((tm, wdt), lambda i: (i, 0)),
                  pl.BlockSpec((tm, wdt), lambda i: (i, 0)),
                  pl.BlockSpec((tm, D), lambda i: (i, 0)),
                  pl.BlockSpec((wdt, D), lambda i: (0, 0)),
                  pl.BlockSpec((wdt, D), lambda i: (0, 0)),
                  vec, vec],
        out_specs=pl.BlockSpec((tm, D), lambda i: (i, 0)),
        out_shape=jax.ShapeDtypeStruct((T, D), F32),
        compiler_params=_cparams(("parallel",)),
        name="out_proj_ln",
    )(a, b, x, wa, wb, g, bb)


def _ffn_kernel(x_ref, wg_ref, wu_ref, wd_ref, g_ref, bb_ref, o_ref, acc_ref, xb_ref, wg_s, wu_s, wd_s, *, alpha):
    i = pl.program_id(0)
    j = pl.program_id(1)

    @pl.when(i == 0)
    def _():
        wg_s[j] = wg_ref[...].astype(BF16)
        wu_s[j] = wu_ref[...].astype(BF16)
        wd_s[j] = wd_ref[...].astype(BF16)

    @pl.when(j == 0)
    def _():
        xb_ref[...] = x_ref[...].astype(BF16)
        acc_ref[...] = jnp.zeros_like(acc_ref)

    xb = xb_ref[...]
    hg = jnp.dot(xb, wg_s[j], preferred_element_type=F32)
    hu = jnp.dot(xb, wu_s[j], preferred_element_type=F32)
    h = (hg * _sigmoid(hg) * hu).astype(BF16)
    acc_ref[...] += jnp.dot(h, wd_s[j], preferred_element_type=F32)

    @pl.when(j == pl.num_programs(1) - 1)
    def _():
        o_ref[...] = _ln(alpha * x_ref[...] + acc_ref[...], g_ref[...], bb_ref[...], LN_EPS)


def _ffn(x, wg, wu, wd, layer, g, bb, alpha, tm, tf):
    T, D = x.shape
    Fd = wg.shape[2]
    nj = Fd // tf
    vec = pl.BlockSpec((1, D), lambda i, j: (0, 0))

    def wj(i, j):
        return jnp.where(i == 0, j, nj - 1)

    return pl.pallas_call(
        functools.partial(_ffn_kernel, alpha=alpha),
        grid=(T // tm, nj),
        in_specs=[pl.BlockSpec((tm, D), lambda i, j: (i, 0)),
                  pl.BlockSpec((None, D, tf), lambda i, j: (layer, 0, wj(i, j))),
                  pl.BlockSpec((None, D, tf), lambda i, j: (layer, 0, wj(i, j))),
                  pl.BlockSpec((None, tf, D), lambda i, j: (layer, wj(i, j), 0)),
                  vec, vec],
        out_specs=pl.BlockSpec((tm, D), lambda i, j: (i, 0)),
        out_shape=jax.ShapeDtypeStruct((T, D), F32),
        scratch_shapes=[pltpu.VMEM((tm, D), F32), pltpu.VMEM((tm, D), BF16),
                        pltpu.VMEM((nj, D, tf), BF16), pltpu.VMEM((nj, D, tf), BF16), pltpu.VMEM((nj, tf, D), BF16)],
        compiler_params=_cparams(("arbitrary", "arbitrary")),
        name="ffn_ln",
    )(x, wg, wu, wd, g, bb)


def _pool_kernel(x_ref, st_ref, w_ref, sc_ref, g_ref, bb_ref, o_ref, nst_ref, ext_ref, *, tl, start, alpha):
    c = pl.program_id(1)

    x = x_ref[...]
    nb, _, D = x.shape
    gw = D // len(POOL_WINDOWS)
    n_ext = POOL_PAD + tl

    @pl.when(c == 0)
    def _():
        ext_ref[:, 0:POOL_PAD - POOL_HIST, :] = jnp.zeros((nb, POOL_PAD - POOL_HIST, D), F32)
        ext_ref[:, POOL_PAD - POOL_HIST:POOL_PAD, :] = st_ref[...]

    ext_ref[:, POOL_PAD:n_ext, :] = x
    pos = start + c * tl + lax.broadcasted_iota(jnp.int32, (1, tl, 1), 1)
    outs = []
    for gi, wdw in enumerate(POOL_WINDOWS):
        lo = gi * gw
        acc = ext_ref[:, 1:n_ext, lo:lo + gw] + ext_ref[:, 0:n_ext - 1, lo:lo + gw]
        h = 2
        while h < wdw:
            rows = acc.shape[1]
            acc = acc[:, h:rows, :] + acc[:, 0:rows - h, :]
            h *= 2
        first = POOL_PAD + 1 - wdw
        ssum = acc[:, first:first + tl, :]
        cnt = jnp.minimum(wdw, pos + 1).astype(F32)
        pooled = ssum / cnt - x[..., lo:lo + gw]
        y = jnp.dot(pooled.reshape(nb * tl, gw).astype(BF16), w_ref[gi], preferred_element_type=F32)
        outs.append(y.reshape(nb, tl, gw))
    y = jnp.concatenate(outs, axis=-1) * sc_ref[...]
    o_ref[...] = _ln(alpha * x + y, g_ref[...], bb_ref[...], LN_EPS)
    tail = ext_ref[:, tl + POOL_PAD - POOL_HIST:tl + POOL_PAD, :]
    ext_ref[:, POOL_PAD - POOL_HIST:POOL_PAD, :] = tail
    nst_ref[...] = tail


def _pool_mix(x, state, w, scale, g, bb, nb, tl, start, alpha):
    n_seq, seq_len, D = x.shape
    vec = pl.BlockSpec((1, D), lambda n, c: (0, 0))
    return pl.pallas_call(
        functools.partial(_pool_kernel, tl=tl, start=start, alpha=alpha),
        grid=(n_seq // nb, seq_len // tl),
        in_specs=[pl.BlockSpec((nb, tl, D), lambda n, c: (n, c, 0)),
                  pl.BlockSpec((nb, POOL_HIST, D), lambda n, c: (n, 0, 0)),
                  pl.BlockSpec(w.shape, lambda n, c: (0, 0, 0)),
                  vec, vec, vec],
        out_specs=[pl.BlockSpec((nb, tl, D), lambda n, c: (n, c, 0)),
                   pl.BlockSpec((nb, POOL_HIST, D), lambda n, c: (n, 0, 0))],
        out_shape=[jax.ShapeDtypeStruct((n_seq, seq_len, D), F32),
                   jax.ShapeDtypeStruct((n_seq, POOL_HIST, D), F32)],
        scratch_shapes=[pltpu.VMEM((nb, POOL_PAD + tl, D), F32)],
        compiler_params=_cparams(("parallel", "arbitrary")),
        name="pool_mix_ln",
    )(x, state, w, scale, g, bb)


def _trunk(x3, conv_prev, shift_prev, wkv_prev, pool_prev, start, prm, tiles):
    n_seq, seq_len, D = x3.shape
    T = n_seq * seq_len
    x = x3.reshape(T, D)
    alpha = prm["alpha"]
    tm = tiles["tm"]
    proj_a, proj_b = _inproj(x, prm["w_in_a"], prm["w_in_b"], tm)
    a_out, new_conv = _conv_branch(proj_a.reshape(n_seq, seq_len, -1), conv_prev, prm["conv_w"], prm["conv_b"],
                                   prm["conv_ln_g"], prm["conv_ln_b"], tiles["seq_nb"], tiles["conv_tl"])
    b_out, new_shift, new_wkv = _rwkv_branch(proj_b, shift_prev, wkv_prev, prm["rwkv"], n_seq, seq_len,
                                             tiles["rwkv_nsb"], tiles["rwkv_tb"])
    x = _outproj(a_out.reshape(T, -1), b_out, x, prm["w_out_a"], prm["w_out_b"], prm["ln_mix_g"][0],
                 prm["ln_mix_b"][0], alpha, tm)
    x = _ffn(x, prm["ffn_gate"], prm["ffn_up"], prm["ffn_down"], 0, prm["ln_ffn_g"][0], prm["ln_ffn_b"][0],
             alpha, tm, tiles["tf"])
    x, new_pool = _pool_mix(x.reshape(n_seq, seq_len, D), pool_prev, prm["pool_w"], prm["pool_scale"],
                            prm["ln_mix_g"][1], prm["ln_mix_b"][1], tiles["seq_nb"], tiles["pool_tl"], start, alpha)
    x = _ffn(x.reshape(T, D), prm["ffn_gate"], prm["ffn_up"], prm["ffn_down"], 1, prm["ln_ffn_g"][1],
             prm["ln_ffn_b"][1], alpha, tm, tiles["tf"])
    return (x.reshape(n_seq, seq_len, D), new_conv[None], new_shift.reshape(1, n_seq, -1), new_wkv[None],
            new_pool[None])


def kernel(x_prompt, x_sample, state_conv, state_shift, state_wkv, state_pool, w_in, conv_w, conv_b, conv_ln_g, conv_ln_b, rwkv_mu, rwkv_w0, rwkv_w2, rwkv_a0, rwkv_a2, rwkv_g2, rwkv_kk, rwkv_ka, rwkv_rk, rwkv_lnx_g, rwkv_lnx_b, w_out, pool_w, pool_scale, ln_mix_g, ln_mix_b, ffn_gate, ffn_up, ffn_down, ln_ffn_g, ln_ffn_b):
    depth = ln_mix_g.shape[0]
    assert depth == 2 and w_in.shape[0] == 1 and pool_w.shape[0] == 1
    B, L, D = x_prompt.shape
    Bs, Ls, _ = x_sample.shape
    a_w = conv_w.shape[2]
    b_w = rwkv_w0.shape[1]
    n_heads = b_w // HEAD
    alpha = float((2 * depth) ** 0.25)

    row = lambda a: a.reshape(1, -1)
    prm = dict(
        alpha=alpha,
        w_in_a=w_in[0, :, :2 * a_w].astype(BF16), w_in_b=w_in[0, :, 2 * a_w:].astype(BF16),
        conv_w=conv_w[0], conv_b=row(conv_b[0]), conv_ln_g=row(conv_ln_g[0]), conv_ln_b=row(conv_ln_b[0]),
        rwkv=dict(mu=row(rwkv_mu[0]), w0=row(rwkv_w0[0]),
                  w2a2=jnp.concatenate([rwkv_w2[0], rwkv_a2[0]], axis=0), a0=row(rwkv_a0[0]), g2=rwkv_g2[0],
                  kk=row(rwkv_kk[0]), ka=row(rwkv_ka[0]), rk=row(rwkv_rk[0]),
                  lng=row(rwkv_lnx_g[0]), lnb=row(rwkv_lnx_b[0])),
        w_out_a=w_out[0, :a_w].astype(BF16), w_out_b=w_out[0, a_w:].astype(BF16),
        pool_w=pool_w[0].astype(BF16), pool_scale=row(pool_scale[0]),
        ln_mix_g=[row(ln_mix_g[i]) for i in range(depth)], ln_mix_b=[row(ln_mix_b[i]) for i in range(depth)],
        ffn_gate=ffn_gate, ffn_up=ffn_up, ffn_down=ffn_down,
        ln_ffn_g=[row(ln_ffn_g[i]) for i in range(depth)], ln_ffn_b=[row(ln_ffn_b[i]) for i in range(depth)],
    )
    dt = x_prompt.dtype
    z_conv = jnp.zeros((B, CONV_HIST, a_w), dt)
    z_shift = jnp.zeros((B, state_shift.shape[-1]), dt)
    z_wkv = jnp.zeros((B, n_heads, HEAD, HEAD), state_wkv.dtype)
    z_pool = jnp.zeros((B, POOL_HIST, D), dt)
    tiles_p = dict(tm=1024, tf=256, seq_nb=1, conv_tl=256, pool_tl=512, rwkv_nsb=1, rwkv_tb=4 * CHUNK)
    tiles_s = dict(tm=1024, tf=256, seq_nb=32, conv_tl=Ls, pool_tl=Ls, rwkv_nsb=4 * CHUNK // Ls, rwkv_tb=Ls)
    yp = _trunk(x_prompt, z_conv, z_shift, z_wkv, z_pool, 0, prm, tiles_p)
    ys = _trunk(x_sample, state_conv[0], state_shift[0], state_wkv[0], state_pool[0], PAST_LEN, prm, tiles_s)
    return (yp[0], ys[0], yp[1], yp[2], yp[3], yp[4], ys[1], ys[2], ys[3], ys[4])
```

```python
import functools

import jax
import jax.numpy as jnp
from jax import lax
from jax.experimental import pallas as pl
from jax.experimental.pallas import tpu as pltpu

F32 = jnp.float32
BF16 = jnp.bfloat16

HEAD = 64
PAIR = 2 * HEAD
CHUNK = 64
SOLVE_BLOCK = 16
SUBLANES = 8
CONV_TAPS = 31
CONV_HIST = CONV_TAPS - 1
CONV_PAD = 32
POOL_WINDOWS = (2, 4, 8, 16)
POOL_HIST = max(POOL_WINDOWS) - 1
POOL_PAD = 16
PAST_LEN = 16384
LN_EPS = 1e-5
GN_EPS = 64e-5
VMEM_LIMIT = 56 * 1024 * 1024


def _ln(z, g, b, eps):
    mu = jnp.mean(z, axis=-1, keepdims=True)
    zc = z - mu
    var = jnp.mean(zc * zc, axis=-1, keepdims=True)
    return zc * lax.rsqrt(var + eps) * g + b


def _sigmoid(x):
    return 1.0 / (1.0 + jnp.exp(-x))


def _cparams(sem):
    return pltpu.CompilerParams(dimension_semantics=sem, vmem_limit_bytes=VMEM_LIMIT)


def _log2(n):
    assert n & (n - 1) == 0
    return n.bit_length() - 1


def _inproj_kernel(x_ref, w_ref, o_ref):
    o_ref[...] = jnp.dot(x_ref[...].astype(BF16), w_ref[...], preferred_element_type=F32)


def _inproj(x, w, tm):
    T, D = x.shape
    n = w.shape[1]
    return pl.pallas_call(
        _inproj_kernel,
        grid=(T // tm,),
        in_specs=[pl.BlockSpec((tm, D), lambda i: (i, 0)),
                  pl.BlockSpec((D, n), lambda i: (0, 0))],
        out_specs=pl.BlockSpec((tm, n), lambda i: (i, 0)),
        out_shape=jax.ShapeDtypeStruct((T, n), F32),
        compiler_params=_cparams(("parallel",)),
        name="in_proj",
    )(x, w)


def _conv_kernel(x_ref, wp_ref, st_ref, w_ref, b_ref, g_ref, bb_ref, o_ref, nst_ref, ext_ref, *, tl, width):
    c = pl.program_id(1)

    base = CONV_PAD - CONV_HIST

    @pl.when(c == 0)
    def _():
        ext_ref[:, base:CONV_PAD, :] = st_ref[...]
        ext_ref[:, CONV_PAD + tl:CONV_PAD + tl + SUBLANES, :] = jnp.zeros((x_ref.shape[0], SUBLANES, width), F32)

    nb, _, d_in = x_ref.shape
    p = jnp.dot(x_ref[...].reshape(nb * tl, d_in).astype(BF16), wp_ref[...], preferred_element_type=F32)
    p = p.reshape(nb, tl, 2 * width)
    u = p[..., :width] * _sigmoid(p[..., width:])
    ext_ref[:, CONV_PAD:CONV_PAD + tl, :] = u
    acc = None
    for rho in range(SUBLANES):
        part = None
        for m in range((base + CONV_TAPS) // SUBLANES + 1):
            j = SUBLANES * m + rho - base
            if 0 <= j < CONV_TAPS:
                term = ext_ref[:, SUBLANES * m:SUBLANES * m + tl + SUBLANES, :] * w_ref[j:j + 1, :]
                part = term if part is None else part + term
        shifted = part[:, rho:rho + tl, :]
        acc = shifted if acc is None else acc + shifted
    acc = acc + b_ref[...]
    y = _ln(acc, g_ref[...], bb_ref[...], LN_EPS)
    o_ref[...] = y * _sigmoid(y)
    tail = ext_ref[:, tl + CONV_PAD - CONV_HIST:tl + CONV_PAD, :]
    ext_ref[:, CONV_PAD - CONV_HIST:CONV_PAD, :] = tail
    nst_ref[...] = tail


def _conv_branch(x, w_proj, state, w, b, g, bb, nb, tl):
    n_seq, seq_len, d_in = x.shape
    width = w_proj.shape[1] // 2
    vec = pl.BlockSpec((1, width), lambda n, c: (0, 0))
    return pl.pallas_call(
        functools.partial(_conv_kernel, tl=tl, width=width),
        grid=(n_seq // nb, seq_len // tl),
        in_specs=[pl.BlockSpec((nb, tl, d_in), lambda n, c: (n, c, 0)),
                  pl.BlockSpec(w_proj.shape, lambda n, c: (0, 0)),
                  pl.BlockSpec((nb, CONV_HIST, width), lambda n, c: (n, 0, 0)),
                  pl.BlockSpec((CONV_TAPS, width), lambda n, c: (0, 0)),
                  vec, vec, vec],
        out_specs=[pl.BlockSpec((nb, tl, width), lambda n, c: (n, c, 0)),
                   pl.BlockSpec((nb, CONV_HIST, width), lambda n, c: (n, 0, 0))],
        out_shape=[jax.ShapeDtypeStruct((n_seq, seq_len, width), F32),
                   jax.ShapeDtypeStruct((n_seq, CONV_HIST, width), F32)],
        scratch_shapes=[pltpu.VMEM((nb, CONV_PAD + tl + SUBLANES, width), F32)],
        compiler_params=_cparams(("parallel", "arbitrary")),
        name="conv_branch",
    )(x, w_proj, state, w, b, g, bb)


def _split(x):
    hi = x.astype(BF16)
    return hi, (x - hi.astype(F32)).astype(BF16)


def _dotb(a, b):
    return jnp.dot(a, b, preferred_element_type=F32)


def _dotb_nt(a, b):
    return lax.dot_general(a, b, (((1,), (1,)), ((), ())), preferred_element_type=F32)


def _b(x):
    return x.astype(BF16)


def _rwkv_kernel(p_ref, sh_ref, wkv_ref, mu_ref, w0_ref, w2a2_ref, a0_ref, g2_ref, kk_ref, ka_ref, rk_ref,
                 lng_ref, lnb_ref, o_ref, nsh_ref, nwkv_ref, s_scr, prev_scr, *, nsb, tb, bw, lw):
    c = pl.program_id(1)
    n_pairs = bw // PAIR
    C = CHUNK
    R = nsb * tb
    nq = R // C
    ts = min(tb, C)
    ns = C // ts
    assert HEAD == C and nq * C == R and (nsb == 1 or tb == ts)
    ts_bits = _log2(ts)
    sb = min(SOLVE_BLOCK, ts)
    sb_bits = _log2(sb)

    @pl.when(c == 0)
    def _():
        prev_scr[...] = sh_ref[...]
        s_scr[...] = jnp.zeros_like(s_scr)
        for j in range(nsb):
            for h in range(2 * n_pairs):
                lo = (h % 2) * HEAD
                s_scr[j, h // 2, lo:lo + HEAD, lo:lo + HEAD] = wkv_ref[j, h]

    p = p_ref[...]
    pw = p.shape[1]
    row = lax.broadcasted_iota(jnp.int32, (R, 1), 0)
    if nsb == 1:
        first = prev_scr[0]
    else:
        first = jnp.concatenate([jnp.broadcast_to(prev_scr[j], (tb, pw)) for j in range(nsb)], axis=0)
    prev = jnp.where((row & (tb - 1)) == 0, first, pltpu.roll(p, 1, 0))
    for j in range(nsb):
        last = p[(j + 1) * tb - 1:(j + 1) * tb, :]
        prev_scr[j] = last
        nsh_ref[j] = last
    xm = p + mu_ref[...] * (prev - p)
    r = xm[:, 0:bw]
    k = xm[:, bw:2 * bw]
    v = xm[:, 2 * bw:3 * bw]
    dwa = xm[:, 3 * bw:3 * bw + 2 * lw]
    dg = xm[:, 3 * bw + 2 * lw:]
    lane_l = lax.broadcasted_iota(jnp.int32, (1, 2 * lw), 1)
    is_w = lane_l < lw
    w2a2 = _b(w2a2_ref[...])
    wpre = w0_ref[...] + _dotb(_b(jnp.where(is_w, jnp.tanh(dwa), 0.0)), w2a2)
    apre = a0_ref[...] + _dotb(_b(jnp.where(is_w, 0.0, dwa)), w2a2)
    z = -wpre
    wlog = -(jnp.maximum(z, 0.0) + jnp.log1p(jnp.exp(-jnp.abs(z)))) - 0.5
    logw = -jnp.exp(wlog)
    a_lr = _sigmoid(apre)
    gate = _dotb(_b(_sigmoid(dg)), _b(g2_ref[...]))
    kkv = k * kk_ref[...]
    k2 = k * (1.0 + (a_lr - 1.0) * ka_ref[...])
    rk2 = r * k2 * rk_ref[...]

    lane = lax.broadcasted_iota(jnp.int32, (1, PAIR), 1)
    m_l = lane < HEAD
    rowp = lax.broadcasted_iota(jnp.int32, (PAIR, PAIR), 0)
    colp = lax.broadcasted_iota(jnp.int32, (PAIR, PAIR), 1)
    bd = (rowp >= HEAD) == (colp >= HEAD)
    bd2 = jnp.concatenate([bd, bd], axis=0).astype(BF16)
    t_i = lax.broadcasted_iota(jnp.int32, (C, PAIR), 0)
    s_i = lax.broadcasted_iota(jnp.int32, (C, PAIR), 1) & (HEAD - 1)
    same = (s_i >> ts_bits) == (t_i >> ts_bits)
    strict = same & (s_i < t_i)
    incl = same & (s_i <= t_i)
    eye = (s_i == t_i).astype(F32)
    dblk = (s_i >> sb_bits) == (t_i >> sb_bits)
    t_c = lax.broadcasted_iota(jnp.int32, (R, R), 0)
    s_c = lax.broadcasted_iota(jnp.int32, (R, R), 1)
    same_c = (s_c >> ts_bits) == (t_c >> ts_bits)
    tri_b = (same_c & (s_c <= t_c)).astype(BF16)
    ones_b = same_c.astype(BF16)

    def vs(y):
        return jnp.concatenate([jnp.where(m_l, y, 0.0), jnp.where(m_l, 0.0, y)], axis=0)

    def pm(a, y):
        return _dotb(_b(a), vs(_b(y)))

    def pmm(lhs, y):
        out = _dotb(_b(jnp.concatenate(lhs, axis=0)), vs(_b(y)))
        return [out[k * C:(k + 1) * C] for k in range(len(lhs))]

    def bdsum(xs):
        hi, lo = _split(jnp.concatenate(xs, axis=0))
        out = _dotb(jnp.concatenate([hi, lo], axis=1), bd2)
        rows = xs[0].shape[0]
        return [out[k * rows:(k + 1) * rows] for k in range(len(xs))]

    P = range(n_pairs)
    units = [(q, i) for q in range(nq) for i in P]
    NU = range(len(units))

    def blk(x, n):
        q, i = units[n]
        return x[q * C:(q + 1) * C, i * PAIR:(i + 1) * PAIR]

    def neumann(m, n_terms):
        t = [eye + m[n] for n in NU]
        levels = _log2(n_terms)
        if levels == 1:
            return t
        p = [pm(m[n], m[n]) for n in NU]
        for _ in range(levels - 2):
            both = [pmm([p[n], t[n]], p[n]) for n in NU]
            t = [t[n] + both[n][1] for n in NU]
            p = [both[n][0] for n in NU]
        return [t[n] + pm(t[n], p[n]) for n in NU]

    lw_hi, lw_lo = _split(logw)
    lw2 = jnp.concatenate([lw_hi, lw_lo], axis=0)
    cs = _dotb(jnp.concatenate([tri_b, tri_b], axis=1), lw2)
    g_in = jnp.exp(cs)
    g_ex = jnp.exp(cs - logw)
    g_inv = jnp.exp(-cs)
    g_end = jnp.exp(_dotb(jnp.concatenate([ones_b, ones_b], axis=1), lw2))

    sums = bdsum([blk(kkv, n) * blk(kkv, n) for n in NU] + [blk(rk2, n) for n in NU])
    kk_n = [blk(kkv, n) / jnp.maximum(jnp.sqrt(sums[n]), 1e-12) for n in NU]
    v_p = [blk(v, n) for n in NU]
    at = [-kk_n[n] * blk(g_ex, n) for n in NU]
    rt = [blk(r, n) * blk(g_in, n) for n in NU]
    bt = [kk_n[n] * blk(a_lr, n) * blk(g_inv, n) for n in NU]
    kt = [blk(k2, n) * blk(g_inv, n) for n in NU]
    sm = [_dotb_nt(_b(jnp.concatenate([at[n], rt[n]], axis=0)),
                   jnp.concatenate([vs(_b(bt[n])), vs(_b(kt[n]))], axis=0)) for n in NU]
    lab = [jnp.where(strict, sm[n][0:C, 0:PAIR], 0.0) for n in NU]
    lak = [jnp.where(strict, sm[n][0:C, PAIR:2 * PAIR], 0.0) for n in NU]
    mrb = [jnp.where(incl, sm[n][C:2 * C, 0:PAIR], 0.0) for n in NU]
    mrk = [jnp.where(incl, sm[n][C:2 * C, PAIR:2 * PAIR], 0.0) for n in NU]
    d1 = [jnp.where(dblk, lab[n], 0.0) for n in NU]
    tinv = neumann(d1, sb)
    if sb < ts:
        fp = [pm(lab[n] - d1[n], tinv[n]) for n in NU]
        levels = _log2(ts // sb)
        for lev in range(levels):
            if lev < levels - 1:
                both = [pmm([fp[n], tinv[n]], fp[n]) for n in NU]
                tinv = [tinv[n] + both[n][1] for n in NU]
                fp = [both[n][0] for n in NU]
            else:
                tinv = [tinv[n] + pm(tinv[n], fp[n]) for n in NU]
    tinv = [_b(tinv[n]) for n in NU]
    xv = [pmm([lak[n], mrk[n]], v_p[n]) for n in NU]
    x0 = [xv[n][0] for n in NU]
    s_cur = [[s_scr[sq, i] for i in P] for sq in range(nsb)]
    col_seq = (lax.broadcasted_iota(jnp.int32, (1, PAIR), 1) & (HEAD - 1)) >> ts_bits
    o = []
    for q in range(nq):
        un = [q * n_pairs + i for i in P]
        seq = [q * ns + j if nsb > 1 else 0 for j in range(ns)]
        a_s, r_s = [], []
        for i in P:
            parts = []
            for j in range(ns):
                rows = slice(j * ts, (j + 1) * ts)
                lhs = _b(jnp.concatenate([at[un[i]][rows], rt[un[i]][rows]], axis=0))
                parts.append(_dotb_nt(lhs, _b(s_cur[seq[j]][i])))
            a_s.append(jnp.concatenate([x[0:ts] for x in parts], axis=0))
            r_s.append(jnp.concatenate([x[ts:2 * ts] for x in parts], axis=0))
        u = [_dotb(tinv[un[i]], vs(_b(a_s[i] + x0[un[i]]))) for i in P]
        for i in P:
            n = un[i]
            ge = blk(g_end, n)
            uv = _b(jnp.concatenate([u[i], v_p[n]], axis=0).T)
            bk = _b(jnp.concatenate([bt[n] * ge, kt[n] * ge], axis=0))
            for j in range(ns):
                uv_j = uv if ns == 1 else jnp.where(col_seq == j, uv, 0.0)
                upd = _dotb(uv_j, bk)
                s_cur[seq[j]][i] = s_cur[seq[j]][i] * ge[j * ts:j * ts + 1, :] + jnp.where(bd, upd, 0.0)
        o += [r_s[i] + pm(mrb[un[i]], u[i]) + xv[un[i]][1] for i in P]
    for sq in range(nsb):
        for i in P:
            s_scr[sq, i] = s_cur[sq][i]
    om = bdsum(o)
    oc = [o[n] - om[n] * (1.0 / HEAD) for n in NU]
    ov = bdsum([oc[n] * oc[n] for n in NU])
    for n in NU:
        q, i = units[n]
        ln = slice(i * PAIR, (i + 1) * PAIR)
        bonus = sums[len(units) + n] * v_p[n]
        y = oc[n] * lax.rsqrt(ov[n] * (1.0 / HEAD) + GN_EPS) * lng_ref[:, ln] + lnb_ref[:, ln] + bonus
        o_ref[q * C:(q + 1) * C, ln] = y * blk(gate, n)

    @pl.when(c == pl.num_programs(1) - 1)
    def _():
        for j in range(nsb):
            for h in range(2 * n_pairs):
                lo = (h % 2) * HEAD
                nwkv_ref[j, h] = s_scr[j, h // 2, lo:lo + HEAD, lo:lo + HEAD]


def _rwkv_branch(proj_b, shift, wkv, prm, n_seq, seq_len, nsb, tb):
    T, pw = proj_b.shape
    n_heads = wkv.shape[1]
    bw = n_heads * HEAD
    lw = prm["w2a2"].shape[0] // 2
    nc = seq_len // tb
    assert nsb == 1 or nc == 1
    rows = nsb * tb

    def full(a):
        return pl.BlockSpec(a.shape, lambda n, c: (0,) * a.ndim)

    names = ("mu", "w0", "w2a2", "a0", "g2", "kk", "ka", "rk", "lng", "lnb")
    return pl.pallas_call(
        functools.partial(_rwkv_kernel, nsb=nsb, tb=tb, bw=bw, lw=lw),
        grid=(n_seq // nsb, nc),
        in_specs=[pl.BlockSpec((rows, pw), lambda n, c: (n * nc + c, 0)),
                  pl.BlockSpec((nsb, 1, pw), lambda n, c: (n, 0, 0)),
                  pl.BlockSpec((nsb, n_heads, HEAD, HEAD), lambda n, c: (n, 0, 0, 0))]
                 + [full(prm[k]) for k in names],
        out_specs=[pl.BlockSpec((rows, bw), lambda n, c: (n * nc + c, 0)),
                   pl.BlockSpec((nsb, 1, pw), lambda n, c: (n, 0, 0)),
                   pl.BlockSpec((nsb, n_heads, HEAD, HEAD), lambda n, c: (n, 0, 0, 0))],
        out_shape=[jax.ShapeDtypeStruct((T, bw), F32),
                   jax.ShapeDtypeStruct((n_seq, 1, pw), F32),
                   jax.ShapeDtypeStruct((n_seq, n_heads, HEAD, HEAD), F32)],
        scratch_shapes=[pltpu.VMEM((nsb, n_heads // 2, PAIR, PAIR), F32), pltpu.VMEM((nsb, 1, pw), F32)],
        compiler_params=_cparams(("parallel", "arbitrary")),
        name="rwkv_branch",
    )(proj_b, shift.reshape(n_seq, 1, pw), wkv, *[prm[k] for k in names])


def _outproj_kernel(a_ref, b_ref, x_ref, wa_ref, wb_ref, g_ref, bb_ref, o_ref, *, alpha):
    mix = jnp.dot(a_ref[...].astype(BF16), wa_ref[...], preferred_element_type=F32)
    mix = mix + jnp.dot(b_ref[...].astype(BF16), wb_ref[...], preferred_element_type=F32)
    o_ref[...] = _ln(alpha * x_ref[...] + mix, g_ref[...], bb_ref[...], LN_EPS)


def _outproj(a, b, x, wa, wb, g, bb, alpha, tm):
    T, D = x.shape
    wdt = a.shape[1]
    vec = pl.BlockSpec((1, D), lambda i: (0, 0))
    return pl.pallas_call(
        functools.partial(_outproj_kernel, alpha=alpha),
        grid=(T // tm,),
        in_specs=[pl.BlockSpec((tm, wdt), lambda i: (i, 0)),
                  pl.BlockSpec((tm, wdt), lambda i: (i, 0)),
                  pl.BlockSpec((tm, D), lambda i: (i, 0)),
                  pl.BlockSpec((wdt, D), lambda i: (0, 0)),
                  pl.BlockSpec((wdt, D), lambda i: (0, 0)),
                  vec, vec],
        out_specs=pl.BlockSpec((tm, D), lambda i: (i, 0)),
        out_shape=jax.ShapeDtypeStruct((T, D), F32),
        compiler_params=_cparams(("parallel",)),
        name="out_proj_ln",
    )(a, b, x, wa, wb, g, bb)


def _ffn_kernel(x_ref, wg_ref, wu_ref, wd_ref, g_ref, bb_ref, o_ref, acc_ref, xb_ref, wg_s, wu_s, wd_s, *, alpha):
    i = pl.program_id(0)
    j = pl.program_id(1)

    @pl.when(i == 0)
    def _():
        wg_s[j] = wg_ref[...].astype(BF16)
        wu_s[j] = wu_ref[...].astype(BF16)
        wd_s[j] = wd_ref[...].astype(BF16)

    @pl.when(j == 0)
    def _():
        xb_ref[...] = x_ref[...].astype(BF16)
        acc_ref[...] = jnp.zeros_like(acc_ref)

    xb = xb_ref[...]
    hg = jnp.dot(xb, wg_s[j], preferred_element_type=F32)
    hu = jnp.dot(xb, wu_s[j], preferred_element_type=F32)
    h = (hg * _sigmoid(hg) * hu).astype(BF16)
    acc_ref[...] += jnp.dot(h, wd_s[j], preferred_element_type=F32)

    @pl.when(j == pl.num_programs(1) - 1)
    def _():
        o_ref[...] = _ln(alpha * x_ref[...] + acc_ref[...], g_ref[...], bb_ref[...], LN_EPS)


def _ffn(x, wg, wu, wd, layer, g, bb, alpha, tm, tf):
    T, D = x.shape
    Fd = wg.shape[2]
    nj = Fd // tf
    vec = pl.BlockSpec((1, D), lambda i, j: (0, 0))

    def wj(i, j):
        return jnp.where(i == 0, j, nj - 1)

    return pl.pallas_call(
        functools.partial(_ffn_kernel, alpha=alpha),
        grid=(T // tm, nj),
        in_specs=[pl.BlockSpec((tm, D), lambda i, j: (i, 0)),
                  pl.BlockSpec((None, D, tf), lambda i, j: (layer, 0, wj(i, j))),
                  pl.BlockSpec((None, D, tf), lambda i, j: (layer, 0, wj(i, j))),
                  pl.BlockSpec((None, tf, D), lambda i, j: (layer, wj(i, j), 0)),
                  vec, vec],
        out_specs=pl.BlockSpec((tm, D), lambda i, j: (i, 0)),
        out_shape=jax.ShapeDtypeStruct((T, D), F32),
        scratch_shapes=[pltpu.VMEM((tm, D), F32), pltpu.VMEM((tm, D), BF16),
                        pltpu.VMEM((nj, D, tf), BF16), pltpu.VMEM((nj, D, tf), BF16), pltpu.VMEM((nj, tf, D), BF16)],
        compiler_params=_cparams(("arbitrary", "arbitrary")),
        name="ffn_ln",
    )(x, wg, wu, wd, g, bb)


def _pool_kernel(x_ref, st_ref, w_ref, sc_ref, g_ref, bb_ref, o_ref, nst_ref, ext_ref, *, tl, start, alpha):
    c = pl.program_id(1)

    x = x_ref[...]
    nb, _, D = x.shape
    gw = D // len(POOL_WINDOWS)
    n_ext = POOL_PAD + tl

    @pl.when(c == 0)
    def _():
        ext_ref[:, 0:POOL_PAD - POOL_HIST, :] = jnp.zeros((nb, POOL_PAD - POOL_HIST, D), F32)
        ext_ref[:, POOL_PAD - POOL_HIST:POOL_PAD, :] = st_ref[...]

    ext_ref[:, POOL_PAD:n_ext, :] = x
    pos = start + c * tl + lax.broadcasted_iota(jnp.int32, (1, tl, 1), 1)
    outs = []
    for gi, wdw in enumerate(POOL_WINDOWS):
        lo = gi * gw
        acc = ext_ref[:, 1:n_ext, lo:lo + gw] + ext_ref[:, 0:n_ext - 1, lo:lo + gw]
        h = 2
        while h < wdw:
            rows = acc.shape[1]
            acc = acc[:, h:rows, :] + acc[:, 0:rows - h, :]
            h *= 2
        first = POOL_PAD + 1 - wdw
        ssum = acc[:, first:first + tl, :]
        cnt = jnp.minimum(wdw, pos + 1).astype(F32)
        pooled = ssum / cnt - x[..., lo:lo + gw]
        y = jnp.dot(pooled.reshape(nb * tl, gw).astype(BF16), w_ref[gi], preferred_element_type=F32)
        outs.append(y.reshape(nb, tl, gw))
    y = jnp.concatenate(outs, axis=-1) * sc_ref[...]
    o_ref[...] = _ln(alpha * x + y, g_ref[...], bb_ref[...], LN_EPS)
    tail = ext_ref[:, tl + POOL_PAD - POOL_HIST:tl + POOL_PAD, :]
    ext_ref[:, POOL_PAD - POOL_HIST:POOL_PAD, :] = tail
    nst_ref[...] = tail


def _pool_mix(x, state, w, scale, g, bb, nb, tl, start, alpha):
    n_seq, seq_len, D = x.shape
    vec = pl.BlockSpec((1, D), lambda n, c: (0, 0))
    return pl.pallas_call(
        functools.partial(_pool_kernel, tl=tl, start=start, alpha=alpha),
        grid=(n_seq // nb, seq_len // tl),
        in_specs=[pl.BlockSpec((nb, tl, D), lambda n, c: (n, c, 0)),
                  pl.BlockSpec((nb, POOL_HIST, D), lambda n, c: (n, 0, 0)),
                  pl.BlockSpec(w.shape, lambda n, c: (0, 0, 0)),
                  vec, vec, vec],
        out_specs=[pl.BlockSpec((nb, tl, D), lambda n, c: (n, c, 0)),
                   pl.BlockSpec((nb, POOL_HIST, D), lambda n, c: (n, 0, 0))],
        out_shape=[jax.ShapeDtypeStruct((n_seq, seq_len, D), F32),
                   jax.ShapeDtypeStruct((n_seq, POOL_HIST, D), F32)],
        scratch_shapes=[pltpu.VMEM((nb, POOL_PAD + tl, D), F32)],
        compiler_params=_cparams(("parallel", "arbitrary")),
        name="pool_mix_ln",
    )(x, state, w, scale, g, bb)


def _trunk(x3, conv_prev, shift_prev, wkv_prev, pool_prev, start, prm, tiles):
    n_seq, seq_len, D = x3.shape
    T = n_seq * seq_len
    x = x3.reshape(T, D)
    alpha = prm["alpha"]
    tm = tiles["tm"]
    a_out, new_conv = _conv_branch(x3, prm["w_in_a"], conv_prev, prm["conv_w"], prm["conv_b"],
                                   prm["conv_ln_g"], prm["conv_ln_b"], tiles["seq_nb"], tiles["conv_tl"])
    proj_b = _inproj(x, prm["w_in_b"], tm)
    b_out, new_shift, new_wkv = _rwkv_branch(proj_b, shift_prev, wkv_prev, prm["rwkv"], n_seq, seq_len,
                                             tiles["rwkv_nsb"], tiles["rwkv_tb"])
    x = _outproj(a_out.reshape(T, -1), b_out, x, prm["w_out_a"], prm["w_out_b"], prm["ln_mix_g"][0],
                 prm["ln_mix_b"][0], alpha, tm)
    x = _ffn(x, prm["ffn_gate"], prm["ffn_up"], prm["ffn_down"], 0, prm["ln_ffn_g"][0], prm["ln_ffn_b"][0],
             alpha, tm, tiles["tf"])
    x, new_pool = _pool_mix(x.reshape(n_seq, seq_len, D), pool_prev, prm["pool_w"], prm["pool_scale"],
                            prm["ln_mix_g"][1], prm["ln_mix_b"][1], tiles["seq_nb"], tiles["pool_tl"], start, alpha)
    x = _ffn(x.reshape(T, D), prm["ffn_gate"], prm["ffn_up"], prm["ffn_down"], 1, prm["ln_ffn_g"][1],
             prm["ln_ffn_b"][1], alpha, tm, tiles["tf"])
    return (x.reshape(n_seq, seq_len, D), new_conv[None], new_shift.reshape(1, n_seq, -1), new_wkv[None],
            new_pool[None])


def kernel(x_prompt, x_sample, state_conv, state_shift, state_wkv, state_pool, w_in, conv_w, conv_b, conv_ln_g, conv_ln_b, rwkv_mu, rwkv_w0, rwkv_w2, rwkv_a0, rwkv_a2, rwkv_g2, rwkv_kk, rwkv_ka, rwkv_rk, rwkv_lnx_g, rwkv_lnx_b, w_out, pool_w, pool_scale, ln_mix_g, ln_mix_b, ffn_gate, ffn_up, ffn_down, ln_ffn_g, ln_ffn_b):
    depth = ln_mix_g.shape[0]
    assert depth == 2 and w_in.shape[0] == 1 and pool_w.shape[0] == 1
    B, L, D = x_prompt.shape
    Bs, Ls, _ = x_sample.shape
    a_w = conv_w.shape[2]
    b_w = rwkv_w0.shape[1]
    n_heads = b_w // HEAD
    alpha = float((2 * depth) ** 0.25)

    row = lambda a: a.reshape(1, -1)
    prm = dict(
        alpha=alpha,
        w_in_a=w_in[0, :, :2 * a_w].astype(BF16), w_in_b=w_in[0, :, 2 * a_w:].astype(BF16),
        conv_w=conv_w[0], conv_b=row(conv_b[0]), conv_ln_g=row(conv_ln_g[0]), conv_ln_b=row(conv_ln_b[0]),
        rwkv=dict(mu=row(rwkv_mu[0]), w0=row(rwkv_w0[0]),
                  w2a2=jnp.concatenate([rwkv_w2[0], rwkv_a2[0]], axis=0), a0=row(rwkv_a0[0]), g2=rwkv_g2[0],
                  kk=row(rwkv_kk[0]), ka=row(rwkv_ka[0]), rk=row(rwkv_rk[0]),
                  lng=row(rwkv_lnx_g[0]), lnb=row(rwkv_lnx_b[0])),
        w_out_a=w_out[0, :a_w].astype(BF16), w_out_b=w_out[0, a_w:].astype(BF16),
        pool_w=pool_w[0].astype(BF16), pool_scale=row(pool_scale[0]),
        ln_mix_g=[row(ln_mix_g[i]) for i in range(depth)], ln_mix_b=[row(ln_mix_b[i]) for i in range(depth)],
        ffn_gate=ffn_gate, ffn_up=ffn_up, ffn_down=ffn_down,
        ln_ffn_g=[row(ln_ffn_g[i]) for i in range(depth)], ln_ffn_b=[row(ln_ffn_b[i]) for i in range(depth)],
    )
    dt = x_prompt.dtype
    z_conv = jnp.zeros((B, CONV_HIST, a_w), dt)
    z_shift = jnp.zeros((B, state_shift.shape[-1]), dt)
    z_wkv = jnp.zeros((B, n_heads, HEAD, HEAD), state_wkv.dtype)
    z_pool = jnp.zeros((B, POOL_HIST, D), dt)
    tiles_p = dict(tm=1024, tf=256, seq_nb=1, conv_tl=256, pool_tl=512, rwkv_nsb=1, rwkv_tb=4 * CHUNK)
    tiles_s = dict(tm=1024, tf=256, seq_nb=32, conv_tl=Ls, pool_tl=Ls, rwkv_nsb=4 * CHUNK // Ls, rwkv_tb=Ls)
    yp = _trunk(x_prompt, z_conv, z_shift, z_wkv, z_pool, 0, prm, tiles_p)
    ys = _trunk(x_sample, state_conv[0], state_shift[0], state_wkv[0], state_pool[0], PAST_LEN, prm, tiles_s)
    return (yp[0], ys[0], yp[1], yp[2], yp[3], yp[4], ys[1], ys[2], ys[3], ys[4])
```

```python
import functools

import jax
import jax.numpy as jnp
from jax import lax
from jax.experimental import pallas as pl
from jax.experimental.pallas import tpu as pltpu

F32 = jnp.float32
BF16 = jnp.bfloat16

HEAD = 64
PAIR = 2 * HEAD
CHUNK = 64
SOLVE_BLOCK = 16
SUBLANES = 8
CONV_TAPS = 31
CONV_HIST = CONV_TAPS - 1
CONV_PAD = 32
POOL_WINDOWS = (2, 4, 8, 16)
POOL_HIST = max(POOL_WINDOWS) - 1
POOL_PAD = 16
PAST_LEN = 16384
LN_EPS = 1e-5
GN_EPS = 64e-5
VMEM_LIMIT = 56 * 1024 * 1024


def _ln(z, g, b, eps):
    mu = jnp.mean(z, axis=-1, keepdims=True)
    zc = z - mu
    var = jnp.mean(zc * zc, axis=-1, keepdims=True)
    return zc * lax.rsqrt(var + eps) * g + b


def _sigmoid(x):
    return 1.0 / (1.0 + jnp.exp(-x))


def _cparams(sem):
    return pltpu.CompilerParams(dimension_semantics=sem, vmem_limit_bytes=VMEM_LIMIT)


def _log2(n):
    assert n & (n - 1) == 0
    return n.bit_length() - 1


def _inproj_kernel(x_ref, wa_ref, wb_ref, oa_ref, ob_ref):
    xb = x_ref[...].astype(BF16)
    oa_ref[...] = jnp.dot(xb, wa_ref[...], preferred_element_type=F32)
    ob_ref[...] = jnp.dot(xb, wb_ref[...], preferred_element_type=F32)


def _inproj(x, wa, wb, tm):
    T, D = x.shape
    na, nb = wa.shape[1], wb.shape[1]
    return pl.pallas_call(
        _inproj_kernel,
        grid=(T // tm,),
        in_specs=[pl.BlockSpec((tm, D), lambda i: (i, 0)),
                  pl.BlockSpec((D, na), lambda i: (0, 0)),
                  pl.BlockSpec((D, nb), lambda i: (0, 0))],
        out_specs=[pl.BlockSpec((tm, na), lambda i: (i, 0)),
                   pl.BlockSpec((tm, nb), lambda i: (i, 0))],
        out_shape=[jax.ShapeDtypeStruct((T, na), F32), jax.ShapeDtypeStruct((T, nb), F32)],
        compiler_params=_cparams(("parallel",)),
        name="in_proj",
    )(x, wa, wb)


def _conv_kernel(p_ref, st_ref, w_ref, b_ref, g_ref, bb_ref, o_ref, nst_ref, ext_ref, *, tl, width):
    c = pl.program_id(1)

    base = CONV_PAD - CONV_HIST

    @pl.when(c == 0)
    def _():
        ext_ref[:, base:CONV_PAD, :] = st_ref[...]
        ext_ref[:, CONV_PAD + tl:CONV_PAD + tl + SUBLANES, :] = jnp.zeros((p_ref.shape[0], SUBLANES, width), F32)

    p = p_ref[...]
    u = p[..., :width] * _sigmoid(p[..., width:])
    ext_ref[:, CONV_PAD:CONV_PAD + tl, :] = u
    acc = None
    for rho in range(SUBLANES):
        part = None
        for m in range((base + CONV_TAPS) // SUBLANES + 1):
            j = SUBLANES * m + rho - base
            if 0 <= j < CONV_TAPS:
                term = ext_ref[:, SUBLANES * m:SUBLANES * m + tl + SUBLANES, :] * w_ref[j:j + 1, :]
                part = term if part is None else part + term
        shifted = part[:, rho:rho + tl, :]
        acc = shifted if acc is None else acc + shifted
    acc = acc + b_ref[...]
    y = _ln(acc, g_ref[...], bb_ref[...], LN_EPS)
    o_ref[...] = (y * _sigmoid(y)).astype(o_ref.dtype)
    tail = ext_ref[:, tl + CONV_PAD - CONV_HIST:tl + CONV_PAD, :]
    ext_ref[:, CONV_PAD - CONV_HIST:CONV_PAD, :] = tail
    nst_ref[...] = tail


def _conv_branch(proj_a, state, w, b, g, bb, nb, tl):
    n_seq, seq_len, two_w = proj_a.shape
    width = two_w // 2
    vec = pl.BlockSpec((1, width), lambda n, c: (0, 0))
    return pl.pallas_call(
        functools.partial(_conv_kernel, tl=tl, width=width),
        grid=(n_seq // nb, seq_len // tl),
        in_specs=[pl.BlockSpec((nb, tl, two_w), lambda n, c: (n, c, 0)),
                  pl.BlockSpec((nb, CONV_HIST, width), lambda n, c: (n, 0, 0)),
                  pl.BlockSpec((CONV_TAPS, width), lambda n, c: (0, 0)),
                  vec, vec, vec],
        out_specs=[pl.BlockSpec((nb, tl, width), lambda n, c: (n, c, 0)),
                   pl.BlockSpec((nb, CONV_HIST, width), lambda n, c: (n, 0, 0))],
        out_shape=[jax.ShapeDtypeStruct((n_seq, seq_len, width), BF16 if tl % (2 * SUBLANES) == 0 else F32),
                   jax.ShapeDtypeStruct((n_seq, CONV_HIST, width), F32)],
        scratch_shapes=[pltpu.VMEM((nb, CONV_PAD + tl + SUBLANES, width), F32)],
        compiler_params=_cparams(("parallel", "arbitrary")),
        name="conv_branch",
    )(proj_a, state, w, b, g, bb)


def _dotb(a, b):
    return jnp.dot(a, b, preferred_element_type=F32)


def _dotb_nt(a, b):
    return lax.dot_general(a, b, (((1,), (1,)), ((), ())), preferred_element_type=F32)


def _b(x):
    return x.astype(BF16)


def _rwkv_kernel(p_ref, sh_ref, wkv_ref, mu_ref, w0_ref, w2a2_ref, a0_ref, g2_ref, kk_ref, ka_ref, rk_ref,
                 lng_ref, lnb_ref, o_ref, nsh_ref, nwkv_ref, s_scr, prev_scr, *, nsb, tb, bw, lw):
    c = pl.program_id(1)
    n_pairs = bw // PAIR
    C = CHUNK
    R = nsb * tb
    nq = R // C
    ts = min(tb, C)
    ns = C // ts
    assert HEAD == C and nq * C == R and (nsb == 1 or tb == ts)
    ts_bits = _log2(ts)
    sb = min(SOLVE_BLOCK, ts)
    sb_bits = _log2(sb)

    @pl.when(c == 0)
    def _():
        prev_scr[...] = sh_ref[...]
        s_scr[...] = jnp.zeros_like(s_scr)
        for j in range(nsb):
            for h in range(2 * n_pairs):
                lo = (h % 2) * HEAD
                s_scr[j, h // 2, lo:lo + HEAD, lo:lo + HEAD] = wkv_ref[j, h]

    p = p_ref[...]
    pw = p.shape[1]
    row = lax.broadcasted_iota(jnp.int32, (R, 1), 0)
    if nsb == 1:
        first = prev_scr[0]
    else:
        first = jnp.concatenate([jnp.broadcast_to(prev_scr[j], (tb, pw)) for j in range(nsb)], axis=0)
    prev = jnp.where((row & (tb - 1)) == 0, first, pltpu.roll(p, 1, 0))
    for j in range(nsb):
        last = p[(j + 1) * tb - 1:(j + 1) * tb, :]
        prev_scr[j] = last
        nsh_ref[j] = last
    xm = p + mu_ref[...] * (prev - p)
    r = xm[:, 0:bw]
    k = xm[:, bw:2 * bw]
    v = xm[:, 2 * bw:3 * bw]
    dwa = xm[:, 3 * bw:3 * bw + 2 * lw]
    dg = xm[:, 3 * bw + 2 * lw:]
    lane_l = lax.broadcasted_iota(jnp.int32, (1, 2 * lw), 1)
    is_w = lane_l < lw
    w2a2 = _b(w2a2_ref[...])
    wpre = w0_ref[...] + _dotb(_b(jnp.where(is_w, jnp.tanh(dwa), 0.0)), w2a2)
    apre = a0_ref[...] + _dotb(_b(jnp.where(is_w, 0.0, dwa)), w2a2)
    z = -wpre
    wlog = -(jnp.maximum(z, 0.0) + jnp.log1p(jnp.exp(-jnp.abs(z)))) - 0.5
    logw = -jnp.exp(wlog)
    a_lr = _sigmoid(apre)
    gate = _dotb(_b(_sigmoid(dg)), _b(g2_ref[...]))
    kkv = k * kk_ref[...]
    k2 = k * (1.0 + (a_lr - 1.0) * ka_ref[...])
    rk2 = r * k2 * rk_ref[...]

    lane = lax.broadcasted_iota(jnp.int32, (1, PAIR), 1)
    m_l = lane < HEAD
    rowp = lax.broadcasted_iota(jnp.int32, (PAIR, PAIR), 0)
    colp = lax.broadcasted_iota(jnp.int32, (PAIR, PAIR), 1)
    bd = (rowp >= HEAD) == (colp >= HEAD)
    t_i = lax.broadcasted_iota(jnp.int32, (C, PAIR), 0)
    s_i = lax.broadcasted_iota(jnp.int32, (C, PAIR), 1) & (HEAD - 1)
    same = (s_i >> ts_bits) == (t_i >> ts_bits)
    strict = same & (s_i < t_i)
    incl = same & (s_i <= t_i)
    eye = (s_i == t_i).astype(F32)
    dblk = (s_i >> sb_bits) == (t_i >> sb_bits)

    def vs(y):
        return jnp.concatenate([jnp.where(m_l, y, 0.0), jnp.where(m_l, 0.0, y)], axis=0)

    def pm(a, y):
        return _dotb(_b(a), vs(_b(y)))

    def pmm(lhs, y):
        out = _dotb(_b(jnp.concatenate(lhs, axis=0)), vs(_b(y)))
        return [out[k * C:(k + 1) * C] for k in range(len(lhs))]

    def bdsum(xs):
        out = []
        for x in xs:
            s0 = jnp.sum(jnp.where(m_l, x, 0.0), axis=1, keepdims=True)
            s1 = jnp.sum(jnp.where(m_l, 0.0, x), axis=1, keepdims=True)
            out.append(jnp.where(m_l, s0, s1))
        return out

    P = range(n_pairs)
    units = [(q, i) for q in range(nq) for i in P]
    NU = range(len(units))

    def blk(x, n):
        q, i = units[n]
        return x[q * C:(q + 1) * C, i * PAIR:(i + 1) * PAIR]

    def neumann(m, n_terms):
        t = [eye + m[n] for n in NU]
        levels = _log2(n_terms)
        if levels == 1:
            return t
        p = [pm(m[n], m[n]) for n in NU]
        for _ in range(levels - 2):
            both = [pmm([p[n], t[n]], p[n]) for n in NU]
            t = [t[n] + both[n][1] for n in NU]
            p = [both[n][0] for n in NU]
        return [t[n] + pm(t[n], p[n]) for n in NU]

    t_run = row & (ts - 1)
    cs = logw
    tot = logw
    for lev in range(ts_bits):
        s = 1 << lev
        cs = cs + jnp.where(t_run >= s, pltpu.roll(cs, s, 0), 0.0)
        tot = tot + jnp.where(t_run >= s, pltpu.roll(tot, s, 0), pltpu.roll(tot, R + s - ts, 0))
    g_in = jnp.exp(cs)
    g_ex = jnp.exp(cs - logw)
    g_inv = jnp.exp(-cs)
    g_end = jnp.exp(tot)

    sums = bdsum([blk(kkv, n) * blk(kkv, n) for n in NU] + [blk(rk2, n) for n in NU])
    kk_n = [blk(kkv, n) / jnp.maximum(jnp.sqrt(sums[n]), 1e-12) for n in NU]
    v_p = [blk(v, n) for n in NU]
    at = [-kk_n[n] * blk(g_ex, n) for n in NU]
    rt = [blk(r, n) * blk(g_in, n) for n in NU]
    bt = [kk_n[n] * blk(a_lr, n) * blk(g_inv, n) for n in NU]
    kt = [blk(k2, n) * blk(g_inv, n) for n in NU]
    sm = [_dotb_nt(_b(jnp.concatenate([at[n], rt[n]], axis=0)),
                   jnp.concatenate([vs(_b(bt[n])), vs(_b(kt[n]))], axis=0)) for n in NU]
    lab = [jnp.where(strict, sm[n][0:C, 0:PAIR], 0.0) for n in NU]
    lak = [jnp.where(strict, sm[n][0:C, PAIR:2 * PAIR], 0.0) for n in NU]
    mrb = [jnp.where(incl, sm[n][C:2 * C, 0:PAIR], 0.0) for n in NU]
    mrk = [jnp.where(incl, sm[n][C:2 * C, PAIR:2 * PAIR], 0.0) for n in NU]
    d1 = [jnp.where(dblk, lab[n], 0.0) for n in NU]
    tinv = neumann(d1, sb)
    if sb < ts:
        fp = [pm(lab[n] - d1[n], tinv[n]) for n in NU]
        levels = _log2(ts // sb)
        for lev in range(levels):
            if lev < levels - 1:
                both = [pmm([fp[n], tinv[n]], fp[n]) for n in NU]
                tinv = [tinv[n] + both[n][1] for n in NU]
                fp = [both[n][0] for n in NU]
            else:
                tinv = [tinv[n] + pm(tinv[n], fp[n]) for n in NU]
    tinv = [_b(tinv[n]) for n in NU]
    xv = [pmm([lak[n], mrk[n]], v_p[n]) for n in NU]
    x0 = [xv[n][0] for n in NU]
    s_cur = [[s_scr[sq, i] for i in P] for sq in range(nsb)]
    col_seq = (lax.broadcasted_iota(jnp.int32, (1, PAIR), 1) & (HEAD - 1)) >> ts_bits
    o = []
    for q in range(nq):
        un = [q * n_pairs + i for i in P]
        seq = [q * ns + j if nsb > 1 else 0 for j in range(ns)]
        a_s, r_s = [], []
        for i in P:
            parts = []
            for j in range(ns):
                rows = slice(j * ts, (j + 1) * ts)
                lhs = _b(jnp.concatenate([at[un[i]][rows], rt[un[i]][rows]], axis=0))
                parts.append(_dotb_nt(lhs, _b(s_cur[seq[j]][i])))
            a_s.append(jnp.concatenate([x[0:ts] for x in parts], axis=0))
            r_s.append(jnp.concatenate([x[ts:2 * ts] for x in parts], axis=0))
        u = [_dotb(tinv[un[i]], vs(_b(a_s[i] + x0[un[i]]))) for i in P]
        for i in P:
            n = un[i]
            ge = blk(g_end, n)
            uv = _b(jnp.concatenate([u[i], v_p[n]], axis=0).T)
            bk = _b(jnp.concatenate([bt[n] * ge, kt[n] * ge], axis=0))
            for j in range(ns):
                uv_j = uv if ns == 1 else jnp.where(col_seq == j, uv, 0.0)
                upd = _dotb(uv_j, bk)
                s_cur[seq[j]][i] = s_cur[seq[j]][i] * ge[j * ts:j * ts + 1, :] + jnp.where(bd, upd, 0.0)
        o += [r_s[i] + pm(mrb[un[i]], u[i]) + xv[un[i]][1] for i in P]
    for sq in range(nsb):
        for i in P:
            s_scr[sq, i] = s_cur[sq][i]
    om = bdsum(o)
    oc = [o[n] - om[n] * (1.0 / HEAD) for n in NU]
    ov = bdsum([oc[n] * oc[n] for n in NU])
    for n in NU:
        q, i = units[n]
        ln = slice(i * PAIR, (i + 1) * PAIR)
        bonus = sums[len(units) + n] * v_p[n]
        y = oc[n] * lax.rsqrt(ov[n] * (1.0 / HEAD) + GN_EPS) * lng_ref[:, ln] + lnb_ref[:, ln] + bonus
        o_ref[q * C:(q + 1) * C, ln] = (y * blk(gate, n)).astype(o_ref.dtype)

    @pl.when(c == pl.num_programs(1) - 1)
    def _():
        for j in range(nsb):
            for h in range(2 * n_pairs):
                lo = (h % 2) * HEAD
                nwkv_ref[j, h] = s_scr[j, h // 2, lo:lo + HEAD, lo:lo + HEAD]


def _rwkv_branch(proj_b, shift, wkv, prm, n_seq, seq_len, nsb, tb):
    T, pw = proj_b.shape
    n_heads = wkv.shape[1]
    bw = n_heads * HEAD
    lw = prm["w2a2"].shape[0] // 2
    nc = seq_len // tb
    assert nsb == 1 or nc == 1
    rows = nsb * tb

    def full(a):
        return pl.BlockSpec(a.shape, lambda n, c: (0,) * a.ndim)

    names = ("mu", "w0", "w2a2", "a0", "g2", "kk", "ka", "rk", "lng", "lnb")
    return pl.pallas_call(
        functools.partial(_rwkv_kernel, nsb=nsb, tb=tb, bw=bw, lw=lw),
        grid=(n_seq // nsb, nc),
        in_specs=[pl.BlockSpec((rows, pw), lambda n, c: (n * nc + c, 0)),
                  pl.BlockSpec((nsb, 1, pw), lambda n, c: (n, 0, 0)),
                  pl.BlockSpec((nsb, n_heads, HEAD, HEAD), lambda n, c: (n, 0, 0, 0))]
                 + [full(prm[k]) for k in names],
        out_specs=[pl.BlockSpec((rows, bw), lambda n, c: (n * nc + c, 0)),
                   pl.BlockSpec((nsb, 1, pw), lambda n, c: (n, 0, 0)),
                   pl.BlockSpec((nsb, n_heads, HEAD, HEAD), lambda n, c: (n, 0, 0, 0))],
        out_shape=[jax.ShapeDtypeStruct((T, bw), BF16),
                   jax.ShapeDtypeStruct((n_seq, 1, pw), F32),
                   jax.ShapeDtypeStruct((n_seq, n_heads, HEAD, HEAD), F32)],
        scratch_shapes=[pltpu.VMEM((nsb, n_heads // 2, PAIR, PAIR), F32), pltpu.VMEM((nsb, 1, pw), F32)],
        compiler_params=_cparams(("parallel", "arbitrary")),
        name="rwkv_branch",
    )(proj_b, shift.reshape(n_seq, 1, pw), wkv, *[prm[k] for k in names])


def _outproj_kernel(a_ref, b_ref, x_ref, wa_ref, wb_ref, g_ref, bb_ref, o_ref, *, alpha):
    mix = jnp.dot(a_ref[...].astype(BF16), wa_ref[...], preferred_element_type=F32)
    mix = mix + jnp.dot(b_ref[...].astype(BF16), wb_ref[...], preferred_element_type=F32)
    o_ref[...] = _ln(alpha * x_ref[...] + mix, g_ref[...], bb_ref[...], LN_EPS)


def _outproj(a, b, x, wa, wb, g, bb, alpha, tm):
    T, D = x.shape
    wdt = a.shape[1]
    vec = pl.BlockSpec((1, D), lambda i: (0, 0))
    return pl.pallas_call(
        functools.partial(_outproj_kernel, alpha=alpha),
        grid=(T // tm,),
        in_specs=[pl.BlockSpec((tm, wdt), lambda i: (i, 0)),
                  pl.BlockSpec((tm, wdt), lambda i: (i, 0)),
                  pl.BlockSpec((tm, D), lambda i: (i, 0)),
                  pl.BlockSpec((wdt, D), lambda i: (0, 0)),
                  pl.BlockSpec((wdt, D), lambda i: (0, 0)),
                  vec, vec],
        out_specs=pl.BlockSpec((tm, D), lambda i: (i, 0)),
        out_shape=jax.ShapeDtypeStruct((T, D), F32),
        compiler_params=_cparams(("parallel",)),
        name="out_proj_ln",
    )(a, b, x, wa, wb, g, bb)


def _ffn_kernel(x_ref, wg_ref, wu_ref, wd_ref, g_ref, bb_ref, o_ref, acc_ref, xb_ref, wg_s, wu_s, wd_s, *, alpha):
    i = pl.program_id(0)
    j = pl.program_id(1)

    @pl.when(i == 0)
    def _():
        wg_s[j] = wg_ref[...].astype(BF16)
        wu_s[j] = wu_ref[...].astype(BF16)
        wd_s[j] = wd_ref[...].astype(BF16)

    @pl.when(j == 0)
    def _():
        xb_ref[...] = x_ref[...].astype(BF16)
        acc_ref[...] = jnp.zeros_like(acc_ref)

    xb = xb_ref[...]
    hg = jnp.dot(xb, wg_s[j], preferred_element_type=F32)
    hu = jnp.dot(xb, wu_s[j], preferred_element_type=F32)
    h = (hg * _sigmoid(hg) * hu).astype(BF16)
    acc_ref[...] += jnp.dot(h, wd_s[j], preferred_element_type=F32)

    @pl.when(j == pl.num_programs(1) - 1)
    def _():
        o_ref[...] = _ln(alpha * x_ref[...] + acc_ref[...], g_ref[...], bb_ref[...], LN_EPS)


def _ffn(x, wg, wu, wd, layer, g, bb, alpha, tm, tf):
    T, D = x.shape
    Fd = wg.shape[2]
    nj = Fd // tf
    vec = pl.BlockSpec((1, D), lambda i, j: (0, 0))

    def wj(i, j):
        return jnp.where(i == 0, j, nj - 1)

    return pl.pallas_call(
        functools.partial(_ffn_kernel, alpha=alpha),
        grid=(T // tm, nj),
        in_specs=[pl.BlockSpec((tm, D), lambda i, j: (i, 0)),
                  pl.BlockSpec((None, D, tf), lambda i, j: (layer, 0, wj(i, j))),
                  pl.BlockSpec((None, D, tf), lambda i, j: (layer, 0, wj(i, j))),
                  pl.BlockSpec((None, tf, D), lambda i, j: (layer, wj(i, j), 0)),
                  vec, vec],
        out_specs=pl.BlockSpec((tm, D), lambda i, j: (i, 0)),
        out_shape=jax.ShapeDtypeStruct((T, D), F32),
        scratch_shapes=[pltpu.VMEM((tm, D), F32), pltpu.VMEM((tm, D), BF16),
                        pltpu.VMEM((nj, D, tf), BF16), pltpu.VMEM((nj, D, tf), BF16), pltpu.VMEM((nj, tf, D), BF16)],
        compiler_params=_cparams(("arbitrary", "arbitrary")),
        name="ffn_ln",
    )(x, wg, wu, wd, g, bb)


def _pool_kernel(x_ref, st_ref, w_ref, sc_ref, g_ref, bb_ref, o_ref, nst_ref, ext_ref, *, tl, start, alpha):
    c = pl.program_id(1)

    x = x_ref[...]
    nb, _, D = x.shape
    gw = D // len(POOL_WINDOWS)
    n_ext = POOL_PAD + tl

    @pl.when(c == 0)
    def _():
        ext_ref[:, 0:POOL_PAD - POOL_HIST, :] = jnp.zeros((nb, POOL_PAD - POOL_HIST, D), F32)
        ext_ref[:, POOL_PAD - POOL_HIST:POOL_PAD, :] = st_ref[...]

    ext_ref[:, POOL_PAD:n_ext, :] = x
    pos = start + c * tl + lax.broadcasted_iota(jnp.int32, (1, tl, 1), 1)
    outs = []
    for gi, wdw in enumerate(POOL_WINDOWS):
        lo = gi * gw
        acc = ext_ref[:, 1:n_ext, lo:lo + gw] + ext_ref[:, 0:n_ext - 1, lo:lo + gw]
        h = 2
        while h < wdw:
            rows = acc.shape[1]
            acc = acc[:, h:rows, :] + acc[:, 0:rows - h, :]
            h *= 2
        first = POOL_PAD + 1 - wdw
        ssum = acc[:, first:first + tl, :]
        cnt = jnp.minimum(wdw, pos + 1).astype(F32)
        pooled = ssum / cnt - x[..., lo:lo + gw]
        y = jnp.dot(pooled.reshape(nb * tl, gw).astype(BF16), w_ref[gi], preferred_element_type=F32)
        outs.append(y.reshape(nb, tl, gw))
    y = jnp.concatenate(outs, axis=-1) * sc_ref[...]
    o_ref[...] = _ln(alpha * x + y, g_ref[...], bb_ref[...], LN_EPS)
    tail = ext_ref[:, tl + POOL_PAD - POOL_HIST:tl + POOL_PAD, :]
    ext_ref[:, POOL_PAD - POOL_HIST:POOL_PAD, :] = tail
    nst_ref[...] = tail


def _pool_mix(x, state, w, scale, g, bb, nb, tl, start, alpha):
    n_seq, seq_len, D = x.shape
    vec = pl.BlockSpec((1, D), lambda n, c: (0, 0))
    return pl.pallas_call(
        functools.partial(_pool_kernel, tl=tl, start=start, alpha=alpha),
        grid=(n_seq // nb, seq_len // tl),
        in_specs=[pl.BlockSpec((nb, tl, D), lambda n, c: (n, c, 0)),
                  pl.BlockSpec((nb, POOL_HIST, D), lambda n, c: (n, 0, 0)),
                  pl.BlockSpec(w.shape, lambda n, c: (0, 0, 0)),
                  vec, vec, vec],
        out_specs=[pl.BlockSpec((nb, tl, D), lambda n, c: (n, c, 0)),
                   pl.BlockSpec((nb, POOL_HIST, D), lambda n, c: (n, 0, 0))],
        out_shape=[jax.ShapeDtypeStruct((n_seq, seq_len, D), F32),
                   jax.ShapeDtypeStruct((n_seq, POOL_HIST, D), F32)],
        scratch_shapes=[pltpu.VMEM((nb, POOL_PAD + tl, D), F32)],
        compiler_params=_cparams(("parallel", "arbitrary")),
        name="pool_mix_ln",
    )(x, state, w, scale, g, bb)


def _trunk(x3, conv_prev, shift_prev, wkv_prev, pool_prev, start, prm, tiles):
    n_seq, seq_len, D = x3.shape
    T = n_seq * seq_len
    x = x3.reshape(T, D)
    alpha = prm["alpha"]
    tm = tiles["tm"]
    proj_a, proj_b = _inproj(x, prm["w_in_a"], prm["w_in_b"], tm)
    a_out, new_conv = _conv_branch(proj_a.reshape(n_seq, seq_len, -1), conv_prev, prm["conv_w"], prm["conv_b"],
                                   prm["conv_ln_g"], prm["conv_ln_b"], tiles["seq_nb"], tiles["conv_tl"])
    b_out, new_shift, new_wkv = _rwkv_branch(proj_b, shift_prev, wkv_prev, prm["rwkv"], n_seq, seq_len,
                                             tiles["rwkv_nsb"], tiles["rwkv_tb"])
    x = _outproj(a_out.reshape(T, -1), b_out, x, prm["w_out_a"], prm["w_out_b"], prm["ln_mix_g"][0],
                 prm["ln_mix_b"][0], alpha, tm)
    x = _ffn(x, prm["ffn_gate"], prm["ffn_up"], prm["ffn_down"], 0, prm["ln_ffn_g"][0], prm["ln_ffn_b"][0],
             alpha, tm, tiles["tf"])
    x, new_pool = _pool_mix(x.reshape(n_seq, seq_len, D), pool_prev, prm["pool_w"], prm["pool_scale"],
                            prm["ln_mix_g"][1], prm["ln_mix_b"][1], tiles["seq_nb"], tiles["pool_tl"], start, alpha)
    x = _ffn(x.reshape(T, D), prm["ffn_gate"], prm["ffn_up"], prm["ffn_down"], 1, prm["ln_ffn_g"][1],
             prm["ln_ffn_b"][1], alpha, tm, tiles["tf"])
    return (x.reshape(n_seq, seq_len, D), new_conv[None], new_shift.reshape(1, n_seq, -1), new_wkv[None],
            new_pool[None])


def kernel(x_prompt, x_sample, state_conv, state_shift, state_wkv, state_pool, w_in, conv_w, conv_b, conv_ln_g, conv_ln_b, rwkv_mu, rwkv_w0, rwkv_w2, rwkv_a0, rwkv_a2, rwkv_g2, rwkv_kk, rwkv_ka, rwkv_rk, rwkv_lnx_g, rwkv_lnx_b, w_out, pool_w, pool_scale, ln_mix_g, ln_mix_b, ffn_gate, ffn_up, ffn_down, ln_ffn_g, ln_ffn_b):
    depth = ln_mix_g.shape[0]
    assert depth == 2 and w_in.shape[0] == 1 and pool_w.shape[0] == 1
    B, L, D = x_prompt.shape
    Bs, Ls, _ = x_sample.shape
    a_w = conv_w.shape[2]
    b_w = rwkv_w0.shape[1]
    n_heads = b_w // HEAD
    alpha = float((2 * depth) ** 0.25)

    row = lambda a: a.reshape(1, -1)
    prm = dict(
        alpha=alpha,
        w_in_a=w_in[0, :, :2 * a_w].astype(BF16), w_in_b=w_in[0, :, 2 * a_w:].astype(BF16),
        conv_w=conv_w[0], conv_b=row(conv_b[0]), conv_ln_g=row(conv_ln_g[0]), conv_ln_b=row(conv_ln_b[0]),
        rwkv=dict(mu=row(rwkv_mu[0]), w0=row(rwkv_w0[0]),
                  w2a2=jnp.concatenate([rwkv_w2[0], rwkv_a2[0]], axis=0), a0=row(rwkv_a0[0]), g2=rwkv_g2[0],
                  kk=row(rwkv_kk[0]), ka=row(rwkv_ka[0]), rk=row(rwkv_rk[0]),
                  lng=row(rwkv_lnx_g[0]), lnb=row(rwkv_lnx_b[0])),
        w_out_a=w_out[0, :a_w].astype(BF16), w_out_b=w_out[0, a_w:].astype(BF16),
        pool_w=pool_w[0].astype(BF16), pool_scale=row(pool_scale[0]),
        ln_mix_g=[row(ln_mix_g[i]) for i in range(depth)], ln_mix_b=[row(ln_mix_b[i]) for i in range(depth)],
        ffn_gate=ffn_gate, ffn_up=ffn_up, ffn_down=ffn_down,
        ln_ffn_g=[row(ln_ffn_g[i]) for i in range(depth)], ln_ffn_b=[row(ln_ffn_b[i]) for i in range(depth)],
    )
    dt = x_prompt.dtype
    z_conv = jnp.zeros((B, CONV_HIST, a_w), dt)
    z_shift = jnp.zeros((B, state_shift.shape[-1]), dt)
    z_wkv = jnp.zeros((B, n_heads, HEAD, HEAD), state_wkv.dtype)
    z_pool = jnp.zeros((B, POOL_HIST, D), dt)
    tiles_p = dict(tm=1024, tf=256, seq_nb=1, conv_tl=256, pool_tl=512, rwkv_nsb=1, rwkv_tb=4 * CHUNK)
    tiles_s = dict(tm=1024, tf=256, seq_nb=32, conv_tl=Ls, pool_tl=Ls, rwkv_nsb=4 * CHUNK // Ls, rwkv_tb=Ls)
    yp = _trunk(x_prompt, z_conv, z_shift, z_wkv, z_pool, 0, prm, tiles_p)
    ys = _trunk(x_sample, state_conv[0], state_shift[0], state_wkv[0], state_pool[0], PAST_LEN, prm, tiles_s)
    return (yp[0], ys[0], yp[1], yp[2], yp[3], yp[4], ys[1], ys[2], ys[3], ys[4])
```

```python
import functools

import jax
import jax.numpy as jnp
from jax import lax
from jax.experimental import pallas as pl
from jax.experimental.pallas import tpu as pltpu

F32 = jnp.float32
BF16 = jnp.bfloat16

HEAD = 64
PAIR = 2 * HEAD
CHUNK = 64
SOLVE_BLOCK = 16
SUBLANES = 8
CONV_TAPS = 31
CONV_HIST = CONV_TAPS - 1
CONV_PAD = 32
POOL_WINDOWS = (2, 4, 8, 16)
POOL_HIST = max(POOL_WINDOWS) - 1
POOL_PAD = 16
PAST_LEN = 16384
LN_EPS = 1e-5
GN_EPS = 64e-5
VMEM_LIMIT = 56 * 1024 * 1024


def _ln(z, g, b, eps):
    mu = jnp.mean(z, axis=-1, keepdims=True)
    zc = z - mu
    var = jnp.mean(zc * zc, axis=-1, keepdims=True)
    return zc * lax.rsqrt(var + eps) * g + b


def _sigmoid(x):
    return 1.0 / (1.0 + jnp.exp(-x))


def _cparams(sem):
    return pltpu.CompilerParams(dimension_semantics=sem, vmem_limit_bytes=VMEM_LIMIT)


def _log2(n):
    assert n & (n - 1) == 0
    return n.bit_length() - 1


def _inproj_kernel(x_ref, wa_ref, wb_ref, oa_ref, ob_ref):
    xb = x_ref[...].astype(BF16)
    oa_ref[...] = jnp.dot(xb, wa_ref[...], preferred_element_type=F32)
    ob_ref[...] = jnp.dot(xb, wb_ref[...], preferred_element_type=F32)


def _inproj(x, wa, wb, tm):
    T, D = x.shape
    na, nb = wa.shape[1], wb.shape[1]
    return pl.pallas_call(
        _inproj_kernel,
        grid=(T // tm,),
        in_specs=[pl.BlockSpec((tm, D), lambda i: (i, 0)),
                  pl.BlockSpec((D, na), lambda i: (0, 0)),
                  pl.BlockSpec((D, nb), lambda i: (0, 0))],
        out_specs=[pl.BlockSpec((tm, na), lambda i: (i, 0)),
                   pl.BlockSpec((tm, nb), lambda i: (i, 0))],
        out_shape=[jax.ShapeDtypeStruct((T, na), F32), jax.ShapeDtypeStruct((T, nb), F32)],
        compiler_params=_cparams(("parallel",)),
        name="in_proj",
    )(x, wa, wb)


def _conv_kernel(p_ref, st_ref, w_ref, b_ref, g_ref, bb_ref, o_ref, nst_ref, ext_ref, *, tl, width):
    c = pl.program_id(1)

    base = CONV_PAD - CONV_HIST

    @pl.when(c == 0)
    def _():
        ext_ref[:, base:CONV_PAD, :] = st_ref[...]
        ext_ref[:, CONV_PAD + tl:CONV_PAD + tl + SUBLANES, :] = jnp.zeros((p_ref.shape[0], SUBLANES, width), F32)

    p = p_ref[...]
    u = p[..., :width] * _sigmoid(p[..., width:])
    ext_ref[:, CONV_PAD:CONV_PAD + tl, :] = u
    acc = None
    for rho in range(SUBLANES):
        part = None
        for m in range((base + CONV_TAPS) // SUBLANES + 1):
            j = SUBLANES * m + rho - base
            if 0 <= j < CONV_TAPS:
                term = ext_ref[:, SUBLANES * m:SUBLANES * m + tl + SUBLANES, :] * w_ref[j:j + 1, :]
                part = term if part is None else part + term
        shifted = part[:, rho:rho + tl, :]
        acc = shifted if acc is None else acc + shifted
    acc = acc + b_ref[...]
    y = _ln(acc, g_ref[...], bb_ref[...], LN_EPS)
    o_ref[...] = (y * _sigmoid(y)).astype(o_ref.dtype)
    tail = ext_ref[:, tl + CONV_PAD - CONV_HIST:tl + CONV_PAD, :]
    ext_ref[:, CONV_PAD - CONV_HIST:CONV_PAD, :] = tail
    nst_ref[...] = tail


def _conv_branch(proj_a, state, w, b, g, bb, nb, tl):
    n_seq, seq_len, two_w = proj_a.shape
    width = two_w // 2
    vec = pl.BlockSpec((1, width), lambda n, c: (0, 0))
    return pl.pallas_call(
        functools.partial(_conv_kernel, tl=tl, width=width),
        grid=(n_seq // nb, seq_len // tl),
        in_specs=[pl.BlockSpec((nb, tl, two_w), lambda n, c: (n, c, 0)),
                  pl.BlockSpec((nb, CONV_HIST, width), lambda n, c: (n, 0, 0)),
                  pl.BlockSpec((CONV_TAPS, width), lambda n, c: (0, 0)),
                  vec, vec, vec],
        out_specs=[pl.BlockSpec((nb, tl, width), lambda n, c: (n, c, 0)),
                   pl.BlockSpec((nb, CONV_HIST, width), lambda n, c: (n, 0, 0))],
        out_shape=[jax.ShapeDtypeStruct((n_seq, seq_len, width), BF16 if tl % (2 * SUBLANES) == 0 else F32),
                   jax.ShapeDtypeStruct((n_seq, CONV_HIST, width), F32)],
        scratch_shapes=[pltpu.VMEM((nb, CONV_PAD + tl + SUBLANES, width), F32)],
        compiler_params=_cparams(("parallel", "arbitrary")),
        name="conv_branch",
    )(proj_a, state, w, b, g, bb)


def _dotb(a, b):
    return jnp.dot(a, b, preferred_element_type=F32)


def _dotb_nt(a, b):
    return lax.dot_general(a, b, (((1,), (1,)), ((), ())), preferred_element_type=F32)


def _b(x):
    return x.astype(BF16)


def _rwkv_kernel(p_ref, sh_ref, wkv_ref, mu_ref, w0_ref, w2a2_ref, a0_ref, g2_ref, kk_ref, ka_ref, rk_ref,
                 lng_ref, lnb_ref, o_ref, nsh_ref, nwkv_ref, s_scr, prev_scr, *, nsb, tb, bw, lw):
    c = pl.program_id(1)
    n_pairs = bw // PAIR
    C = CHUNK
    R = nsb * tb
    nq = R // C
    ts = min(tb, C)
    ns = C // ts
    assert HEAD == C and nq * C == R and (nsb == 1 or tb == ts)
    ts_bits = _log2(ts)
    sb = min(SOLVE_BLOCK, ts)
    sb_bits = _log2(sb)

    @pl.when(c == 0)
    def _():
        prev_scr[...] = sh_ref[...]
        s_scr[...] = jnp.zeros_like(s_scr)
        for j in range(nsb):
            for h in range(2 * n_pairs):
                lo = (h % 2) * HEAD
                s_scr[j, h // 2, lo:lo + HEAD, lo:lo + HEAD] = wkv_ref[j, h]

    p = p_ref[...]
    pw = p.shape[1]
    row = lax.broadcasted_iota(jnp.int32, (R, 1), 0)
    if nsb == 1:
        first = prev_scr[0]
    else:
        first = jnp.concatenate([jnp.broadcast_to(prev_scr[j], (tb, pw)) for j in range(nsb)], axis=0)
    prev = jnp.where((row & (tb - 1)) == 0, first, pltpu.roll(p, 1, 0))
    for j in range(nsb):
        last = p[(j + 1) * tb - 1:(j + 1) * tb, :]
        prev_scr[j] = last
        nsh_ref[j] = last
    xm = p + mu_ref[...] * (prev - p)
    r = xm[:, 0:bw]
    k = xm[:, bw:2 * bw]
    v = xm[:, 2 * bw:3 * bw]
    dwa = xm[:, 3 * bw:3 * bw + 2 * lw]
    dg = xm[:, 3 * bw + 2 * lw:]
    lane_l = lax.broadcasted_iota(jnp.int32, (1, 2 * lw), 1)
    is_w = lane_l < lw
    w2a2 = _b(w2a2_ref[...])
    wpre = w0_ref[...] + _dotb(_b(jnp.where(is_w, jnp.tanh(dwa), 0.0)), w2a2)
    apre = a0_ref[...] + _dotb(_b(jnp.where(is_w, 0.0, dwa)), w2a2)
    z = -wpre
    wlog = -(jnp.maximum(z, 0.0) + jnp.log(1.0 + jnp.exp(-jnp.abs(z)))) - 0.5
    logw = -jnp.exp(wlog)
    a_lr = _sigmoid(apre)
    gate = _dotb(_b(_sigmoid(dg)), _b(g2_ref[...]))
    kkv = k * kk_ref[...]
    k2 = k * (1.0 + (a_lr - 1.0) * ka_ref[...])
    rk2 = r * k2 * rk_ref[...]

    lane = lax.broadcasted_iota(jnp.int32, (1, PAIR), 1)
    m_l = lane < HEAD
    rowp = lax.broadcasted_iota(jnp.int32, (PAIR, PAIR), 0)
    colp = lax.broadcasted_iota(jnp.int32, (PAIR, PAIR), 1)
    bd = (rowp >= HEAD) == (colp >= HEAD)
    t_i = lax.broadcasted_iota(jnp.int32, (C, PAIR), 0)
    s_i = lax.broadcasted_iota(jnp.int32, (C, PAIR), 1) & (HEAD - 1)
    same = (s_i >> ts_bits) == (t_i >> ts_bits)
    strict = same & (s_i < t_i)
    incl = same & (s_i <= t_i)
    eye = (s_i == t_i).astype(F32)
    dblk = (s_i >> sb_bits) == (t_i >> sb_bits)

    def vs(y):
        return jnp.concatenate([jnp.where(m_l, y, 0.0), jnp.where(m_l, 0.0, y)], axis=0)

    def pm(a, y):
        return _dotb(_b(a), vs(_b(y)))

    def pmm(lhs, y):
        out = _dotb(_b(jnp.concatenate(lhs, axis=0)), vs(_b(y)))
        return [out[k * C:(k + 1) * C] for k in range(len(lhs))]

    def bdsum(xs):
        out = []
        for x in xs:
            s0 = jnp.sum(jnp.where(m_l, x, 0.0), axis=1, keepdims=True)
            s1 = jnp.sum(jnp.where(m_l, 0.0, x), axis=1, keepdims=True)
            out.append(jnp.where(m_l, s0, s1))
        return out

    P = range(n_pairs)
    units = [(q, i) for q in range(nq) for i in P]
    NU = range(len(units))

    def blk(x, n):
        q, i = units[n]
        return x[q * C:(q + 1) * C, i * PAIR:(i + 1) * PAIR]

    def neumann(m, n_terms):
        t = [eye + m[n] for n in NU]
        levels = _log2(n_terms)
        if levels == 1:
            return t
        p = [pm(m[n], m[n]) for n in NU]
        for _ in range(levels - 2):
            both = [pmm([p[n], t[n]], p[n]) for n in NU]
            t = [t[n] + both[n][1] for n in NU]
            p = [both[n][0] for n in NU]
        return [t[n] + pm(t[n], p[n]) for n in NU]

    t_run = row & (ts - 1)
    cs = logw
    for lev in range(ts_bits):
        s = 1 << lev
        cs = cs + jnp.where(t_run >= s, pltpu.roll(cs, s, 0), 0.0)
    g_in = jnp.exp(cs)
    g_ex = jnp.exp(cs - logw)
    g_inv = jnp.exp(-cs)
    runs = g_in.reshape(R // ts, ts, bw)
    g_end = jnp.broadcast_to(runs[:, ts - 1:ts, :], runs.shape).reshape(R, bw)

    sums = bdsum([blk(kkv, n) * blk(kkv, n) for n in NU] + [blk(rk2, n) for n in NU])
    kk_n = [blk(kkv, n) / jnp.maximum(jnp.sqrt(sums[n]), 1e-12) for n in NU]
    v_p = [blk(v, n) for n in NU]
    at = [-kk_n[n] * blk(g_ex, n) for n in NU]
    rt = [blk(r, n) * blk(g_in, n) for n in NU]
    bt = [kk_n[n] * blk(a_lr, n) * blk(g_inv, n) for n in NU]
    kt = [blk(k2, n) * blk(g_inv, n) for n in NU]
    sm = [_dotb_nt(_b(jnp.concatenate([at[n], rt[n]], axis=0)),
                   jnp.concatenate([vs(_b(bt[n])), vs(_b(kt[n]))], axis=0)) for n in NU]
    lab = [jnp.where(strict, sm[n][0:C, 0:PAIR], 0.0) for n in NU]
    lak = [jnp.where(strict, sm[n][0:C, PAIR:2 * PAIR], 0.0) for n in NU]
    mrb = [jnp.where(incl, sm[n][C:2 * C, 0:PAIR], 0.0) for n in NU]
    mrk = [jnp.where(incl, sm[n][C:2 * C, PAIR:2 * PAIR], 0.0) for n in NU]
    d1 = [jnp.where(dblk, lab[n], 0.0) for n in NU]
    tinv = neumann(d1, sb)
    if sb < ts:
        fp = [pm(lab[n] - d1[n], tinv[n]) for n in NU]
        levels = _log2(ts // sb)
        for lev in range(levels):
            if lev < levels - 1:
                both = [pmm([fp[n], tinv[n]], fp[n]) for n in NU]
                tinv = [tinv[n] + both[n][1] for n in NU]
                fp = [both[n][0] for n in NU]
            else:
                tinv = [tinv[n] + pm(tinv[n], fp[n]) for n in NU]
    tinv = [_b(tinv[n]) for n in NU]
    xv = [pmm([lak[n], mrk[n]], v_p[n]) for n in NU]
    x0 = [xv[n][0] for n in NU]
    s_cur = [[s_scr[sq, i] for i in P] for sq in range(nsb)]
    col_seq = (lax.broadcasted_iota(jnp.int32, (1, PAIR), 1) & (HEAD - 1)) >> ts_bits

    def finish(q, o_q):
        un = [q * n_pairs + i for i in P]
        om = bdsum(o_q)
        oc = [o_q[i] - om[i] * (1.0 / HEAD) for i in P]
        ov = bdsum([oc[i] * oc[i] for i in P])
        for i in P:
            n = un[i]
            ln = slice(i * PAIR, (i + 1) * PAIR)
            bonus = sums[len(units) + n] * v_p[n]
            y = oc[i] * lax.rsqrt(ov[i] * (1.0 / HEAD) + GN_EPS) * lng_ref[:, ln] + lnb_ref[:, ln] + bonus
            o_ref[q * C:(q + 1) * C, ln] = (y * blk(gate, n)).astype(o_ref.dtype)

    o_prev = None
    for q in range(nq):
        un = [q * n_pairs + i for i in P]
        seq = [q * ns + j if nsb > 1 else 0 for j in range(ns)]
        a_s, r_s = [], []
        for i in P:
            parts = []
            for j in range(ns):
                rows = slice(j * ts, (j + 1) * ts)
                lhs = _b(jnp.concatenate([at[un[i]][rows], rt[un[i]][rows]], axis=0))
                parts.append(_dotb_nt(lhs, _b(s_cur[seq[j]][i])))
            a_s.append(jnp.concatenate([x[0:ts] for x in parts], axis=0))
            r_s.append(jnp.concatenate([x[ts:2 * ts] for x in parts], axis=0))
        if o_prev is not None:
            finish(q - 1, o_prev)
        u = [_dotb(tinv[un[i]], vs(_b(a_s[i] + x0[un[i]]))) for i in P]
        for i in P:
            n = un[i]
            ge = blk(g_end, n)
            uv = _b(jnp.concatenate([u[i], v_p[n]], axis=0).T)
            bk = _b(jnp.concatenate([bt[n] * ge, kt[n] * ge], axis=0))
            for j in range(ns):
                uv_j = uv if ns == 1 else jnp.where(col_seq == j, uv, 0.0)
                upd = _dotb(uv_j, bk)
                s_cur[seq[j]][i] = s_cur[seq[j]][i] * ge[j * ts:j * ts + 1, :] + jnp.where(bd, upd, 0.0)
        o_prev = [r_s[i] + pm(mrb[un[i]], u[i]) + xv[un[i]][1] for i in P]
    for sq in range(nsb):
        for i in P:
            s_scr[sq, i] = s_cur[sq][i]
    finish(nq - 1, o_prev)

    @pl.when(c == pl.num_programs(1) - 1)
    def _():
        for j in range(nsb):
            for h in range(2 * n_pairs):
                lo = (h % 2) * HEAD
                nwkv_ref[j, h] = s_scr[j, h // 2, lo:lo + HEAD, lo:lo + HEAD]


def _rwkv_branch(proj_b, shift, wkv, prm, n_seq, seq_len, nsb, tb):
    T, pw = proj_b.shape
    n_heads = wkv.shape[1]
    bw = n_heads * HEAD
    lw = prm["w2a2"].shape[0] // 2
    nc = seq_len // tb
    assert nsb == 1 or nc == 1
    rows = nsb * tb

    def full(a):
        return pl.BlockSpec(a.shape, lambda n, c: (0,) * a.ndim)

    names = ("mu", "w0", "w2a2", "a0", "g2", "kk", "ka", "rk", "lng", "lnb")
    return pl.pallas_call(
        functools.partial(_rwkv_kernel, nsb=nsb, tb=tb, bw=bw, lw=lw),
        grid=(n_seq // nsb, nc),
        in_specs=[pl.BlockSpec((rows, pw), lambda n, c: (n * nc + c, 0)),
                  pl.BlockSpec((nsb, 1, pw), lambda n, c: (n, 0, 0)),
                  pl.BlockSpec((nsb, n_heads, HEAD, HEAD), lambda n, c: (n, 0, 0, 0))]
                 + [full(prm[k]) for k in names],
        out_specs=[pl.BlockSpec((rows, bw), lambda n, c: (n * nc + c, 0)),
                   pl.BlockSpec((nsb, 1, pw), lambda n, c: (n, 0, 0)),
                   pl.BlockSpec((nsb, n_heads, HEAD, HEAD), lambda n, c: (n, 0, 0, 0))],
        out_shape=[jax.ShapeDtypeStruct((T, bw), BF16),
                   jax.ShapeDtypeStruct((n_seq, 1, pw), F32),
                   jax.ShapeDtypeStruct((n_seq, n_heads, HEAD, HEAD), F32)],
        scratch_shapes=[pltpu.VMEM((nsb, n_heads // 2, PAIR, PAIR), F32), pltpu.VMEM((nsb, 1, pw), F32)],
        compiler_params=_cparams(("parallel", "arbitrary")),
        name="rwkv_branch",
    )(proj_b, shift.reshape(n_seq, 1, pw), wkv, *[prm[k] for k in names])


def _outproj_kernel(a_ref, b_ref, x_ref, wa_ref, wb_ref, g_ref, bb_ref, o_ref, *, alpha):
    mix = jnp.dot(a_ref[...].astype(BF16), wa_ref[...], preferred_element_type=F32)
    mix = mix + jnp.dot(b_ref[...].astype(BF16), wb_ref[...], preferred_element_type=F32)
    o_ref[...] = _ln(alpha * x_ref[...] + mix, g_ref[...], bb_ref[...], LN_EPS)


def _outproj(a, b, x, wa, wb, g, bb, alpha, tm):
    T, D = x.shape
    wdt = a.shape[1]
    vec = pl.BlockSpec((1, D), lambda i: (0, 0))
    return pl.pallas_call(
        functools.partial(_outproj_kernel, alpha=alpha),
        grid=(T // tm,),
        in_specs=[pl.BlockSpec((tm, wdt), lambda i: (i, 0)),
                  pl.BlockSpec((tm, wdt), lambda i: (i, 0)),
                  pl.BlockSpec((tm, D), lambda i: (i, 0)),
                  pl.BlockSpec((wdt, D), lambda i: (0, 0)),
                  pl.BlockSpec((wdt, D), lambda i: (0, 0)),
                  vec, vec],
        out_specs=pl.BlockSpec((tm, D), lambda i: (i, 0)),
        out_shape=jax.ShapeDtypeStruct((T, D), F32),
        compiler_params=_cparams(("parallel",)),
        name="out_proj_ln",
    )(a, b, x, wa, wb, g, bb)


def _ffn_kernel(x_ref, wg_ref, wu_ref, wd_ref, g_ref, bb_ref, o_ref, acc_ref, xb_ref, wg_s, wu_s, wd_s, *, alpha):
    i = pl.program_id(0)
    j = pl.program_id(1)

    @pl.when(i == 0)
    def _():
        wg_s[j] = wg_ref[...].astype(BF16)
        wu_s[j] = wu_ref[...].astype(BF16)
        wd_s[j] = wd_ref[...].astype(BF16)

    @pl.when(j == 0)
    def _():
        xb_ref[...] = x_ref[...].astype(BF16)
        acc_ref[...] = jnp.zeros_like(acc_ref)

    xb = xb_ref[...]
    hg = jnp.dot(xb, wg_s[j], preferred_element_type=F32)
    hu = jnp.dot(xb, wu_s[j], preferred_element_type=F32)
    h = (hg * _sigmoid(hg) * hu).astype(BF16)
    acc_ref[...] += jnp.dot(h, wd_s[j], preferred_element_type=F32)

    @pl.when(j == pl.num_programs(1) - 1)
    def _():
        o_ref[...] = _ln(alpha * x_ref[...] + acc_ref[...], g_ref[...], bb_ref[...], LN_EPS)


def _ffn(x, wg, wu, wd, layer, g, bb, alpha, tm, tf):
    T, D = x.shape
    Fd = wg.shape[2]
    nj = Fd // tf
    vec = pl.BlockSpec((1, D), lambda i, j: (0, 0))

    def wj(i, j):
        return jnp.where(i == 0, j, nj - 1)

    return pl.pallas_call(
        functools.partial(_ffn_kernel, alpha=alpha),
        grid=(T // tm, nj),
        in_specs=[pl.BlockSpec((tm, D), lambda i, j: (i, 0)),
                  pl.BlockSpec((None, D, tf), lambda i, j: (layer, 0, wj(i, j))),
                  pl.BlockSpec((None, D, tf), lambda i, j: (layer, 0, wj(i, j))),
                  pl.BlockSpec((None, tf, D), lambda i, j: (layer, wj(i, j), 0)),
                  vec, vec],
        out_specs=pl.BlockSpec((tm, D), lambda i, j: (i, 0)),
        out_shape=jax.ShapeDtypeStruct((T, D), F32),
        scratch_shapes=[pltpu.VMEM((tm, D), F32), pltpu.VMEM((tm, D), BF16),
                        pltpu.VMEM((nj, D, tf), BF16), pltpu.VMEM((nj, D, tf), BF16), pltpu.VMEM((nj, tf, D), BF16)],
        compiler_params=_cparams(("arbitrary", "arbitrary")),
        name="ffn_ln",
    )(x, wg, wu, wd, g, bb)


def _pool_kernel(x_ref, st_ref, w_ref, sc_ref, g_ref, bb_ref, o_ref, nst_ref, ext_ref, *, tl, start, alpha):
    c = pl.program_id(1)

    x = x_ref[...]
    nb, _, D = x.shape
    gw = D // len(POOL_WINDOWS)
    n_ext = POOL_PAD + tl

    @pl.when(c == 0)
    def _():
        ext_ref[:, 0:POOL_PAD - POOL_HIST, :] = jnp.zeros((nb, POOL_PAD - POOL_HIST, D), F32)
        ext_ref[:, POOL_PAD - POOL_HIST:POOL_PAD, :] = st_ref[...]

    ext_ref[:, POOL_PAD:n_ext, :] = x
    pos = start + c * tl + lax.broadcasted_iota(jnp.int32, (1, tl, 1), 1)
    outs = []
    for gi, wdw in enumerate(POOL_WINDOWS):
        lo = gi * gw
        acc = ext_ref[:, 1:n_ext, lo:lo + gw] + ext_ref[:, 0:n_ext - 1, lo:lo + gw]
        h = 2
        while h < wdw:
            rows = acc.shape[1]
            acc = acc[:, h:rows, :] + acc[:, 0:rows - h, :]
            h *= 2
        first = POOL_PAD + 1 - wdw
        ssum = acc[:, first:first + tl, :]
        cnt = jnp.minimum(wdw, pos + 1).astype(F32)
        pooled = ssum / cnt - x[..., lo:lo + gw]
        y = jnp.dot(pooled.reshape(nb * tl, gw).astype(BF16), w_ref[gi], preferred_element_type=F32)
        outs.append(y.reshape(nb, tl, gw))
    y = jnp.concatenate(outs, axis=-1) * sc_ref[...]
    o_ref[...] = _ln(alpha * x + y, g_ref[...], bb_ref[...], LN_EPS)
    tail = ext_ref[:, tl + POOL_PAD - POOL_HIST:tl + POOL_PAD, :]
    ext_ref[:, POOL_PAD - POOL_HIST:POOL_PAD, :] = tail
    nst_ref[...] = tail


def _pool_mix(x, state, w, scale, g, bb, nb, tl, start, alpha):
    n_seq, seq_len, D = x.shape
    vec = pl.BlockSpec((1, D), lambda n, c: (0, 0))
    return pl.pallas_call(
        functools.partial(_pool_kernel, tl=tl, start=start, alpha=alpha),
        grid=(n_seq // nb, seq_len // tl),
        in_specs=[pl.BlockSpec((nb, tl, D), lambda n, c: (n, c, 0)),
                  pl.BlockSpec((nb, POOL_HIST, D), lambda n, c: (n, 0, 0)),
                  pl.BlockSpec(w.shape, lambda n, c: (0, 0, 0)),
                  vec, vec, vec],
        out_specs=[pl.BlockSpec((nb, tl, D), lambda n, c: (n, c, 0)),
                   pl.BlockSpec((nb, POOL_HIST, D), lambda n, c: (n, 0, 0))],
        out_shape=[jax.ShapeDtypeStruct((n_seq, seq_len, D), F32),
                   jax.ShapeDtypeStruct((n_seq, POOL_HIST, D), F32)],
        scratch_shapes=[pltpu.VMEM((nb, POOL_PAD + tl, D), F32)],
        compiler_params=_cparams(("parallel", "arbitrary")),
        name="pool_mix_ln",
    )(x, state, w, scale, g, bb)


def _trunk(x3, conv_prev, shift_prev, wkv_prev, pool_prev, start, prm, tiles):
    n_seq, seq_len, D = x3.shape
    T = n_seq * seq_len
    x = x3.reshape(T, D)
    alpha = prm["alpha"]
    tm = tiles["tm"]
    proj_a, proj_b = _inproj(x, prm["w_in_a"], prm["w_in_b"], tm)
    a_out, new_conv = _conv_branch(proj_a.reshape(n_seq, seq_len, -1), conv_prev, prm["conv_w"], prm["conv_b"],
                                   prm["conv_ln_g"], prm["conv_ln_b"], tiles["seq_nb"], tiles["conv_tl"])
    b_out, new_shift, new_wkv = _rwkv_branch(proj_b, shift_prev, wkv_prev, prm["rwkv"], n_seq, seq_len,
                                             tiles["rwkv_nsb"], tiles["rwkv_tb"])
    x = _outproj(a_out.reshape(T, -1), b_out, x, prm["w_out_a"], prm["w_out_b"], prm["ln_mix_g"][0],
                 prm["ln_mix_b"][0], alpha, tm)
    x = _ffn(x, prm["ffn_gate"], prm["ffn_up"], prm["ffn_down"], 0, prm["ln_ffn_g"][0], prm["ln_ffn_b"][0],
             alpha, tm, tiles["tf"])
    x, new_pool = _pool_mix(x.reshape(n_seq, seq_len, D), pool_prev, prm["pool_w"], prm["pool_scale"],
                            prm["ln_mix_g"][1], prm["ln_mix_b"][1], tiles["seq_nb"], tiles["pool_tl"], start, alpha)
    x = _ffn(x.reshape(T, D), prm["ffn_gate"], prm["ffn_up"], prm["ffn_down"], 1, prm["ln_ffn_g"][1],
             prm["ln_ffn_b"][1], alpha, tm, tiles["tf"])
    return (x.reshape(n_seq, seq_len, D), new_conv[None], new_shift.reshape(1, n_seq, -1), new_wkv[None],
            new_pool[None])


def kernel(x_prompt, x_sample, state_conv, state_shift, state_wkv, state_pool, w_in, conv_w, conv_b, conv_ln_g, conv_ln_b, rwkv_mu, rwkv_w0, rwkv_w2, rwkv_a0, rwkv_a2, rwkv_g2, rwkv_kk, rwkv_ka, rwkv_rk, rwkv_lnx_g, rwkv_lnx_b, w_out, pool_w, pool_scale, ln_mix_g, ln_mix_b, ffn_gate, ffn_up, ffn_down, ln_ffn_g, ln_ffn_b):
    depth = ln_mix_g.shape[0]
    assert depth == 2 and w_in.shape[0] == 1 and pool_w.shape[0] == 1
    B, L, D = x_prompt.shape
    Bs, Ls, _ = x_sample.shape
    a_w = conv_w.shape[2]
    b_w = rwkv_w0.shape[1]
    n_heads = b_w // HEAD
    alpha = float((2 * depth) ** 0.25)

    row = lambda a: a.reshape(1, -1)
    prm = dict(
        alpha=alpha,
        w_in_a=w_in[0, :, :2 * a_w].astype(BF16), w_in_b=w_in[0, :, 2 * a_w:].astype(BF16),
        conv_w=conv_w[0], conv_b=row(conv_b[0]), conv_ln_g=row(conv_ln_g[0]), conv_ln_b=row(conv_ln_b[0]),
        rwkv=dict(mu=row(rwkv_mu[0]), w0=row(rwkv_w0[0]),
                  w2a2=jnp.concatenate([rwkv_w2[0], rwkv_a2[0]], axis=0), a0=row(rwkv_a0[0]), g2=rwkv_g2[0],
                  kk=row(rwkv_kk[0]), ka=row(rwkv_ka[0]), rk=row(rwkv_rk[0]),
                  lng=row(rwkv_lnx_g[0]), lnb=row(rwkv_lnx_b[0])),
        w_out_a=w_out[0, :a_w].astype(BF16), w_out_b=w_out[0, a_w:].astype(BF16),
        pool_w=pool_w[0].astype(BF16), pool_scale=row(pool_scale[0]),
        ln_mix_g=[row(ln_mix_g[i]) for i in range(depth)], ln_mix_b=[row(ln_mix_b[i]) for i in range(depth)],
        ffn_gate=ffn_gate, ffn_up=ffn_up, ffn_down=ffn_down,
        ln_ffn_g=[row(ln_ffn_g[i]) for i in range(depth)], ln_ffn_b=[row(ln_ffn_b[i]) for i in range(depth)],
    )
    dt = x_prompt.dtype
    z_conv = jnp.zeros((B, CONV_HIST, a_w), dt)
    z_shift = jnp.zeros((B, state_shift.shape[-1]), dt)
    z_wkv = jnp.zeros((B, n_heads, HEAD, HEAD), state_wkv.dtype)
    z_pool = jnp.zeros((B, POOL_HIST, D), dt)
    tiles_p = dict(tm=1024, tf=256, seq_nb=1, conv_tl=512, pool_tl=1024, rwkv_nsb=1, rwkv_tb=8 * CHUNK)
    tiles_s = dict(tm=1024, tf=256, seq_nb=32, conv_tl=Ls, pool_tl=Ls, rwkv_nsb=4 * CHUNK // Ls, rwkv_tb=Ls)
    yp = _trunk(x_prompt, z_conv, z_shift, z_wkv, z_pool, 0, prm, tiles_p)
    ys = _trunk(x_sample, state_conv[0], state_shift[0], state_wkv[0], state_pool[0], PAST_LEN, prm, tiles_s)
    return (yp[0], ys[0], yp[1], yp[2], yp[3], yp[4], ys[1], ys[2], ys[3], ys[4])
```

```python
import functools

import jax
import jax.numpy as jnp
from jax import lax
from jax.experimental import pallas as pl
from jax.experimental.pallas import tpu as pltpu

F32 = jnp.float32
BF16 = jnp.bfloat16

HEAD = 64
PAIR = 2 * HEAD
CHUNK = 64
SOLVE_BLOCK = 16
SUBLANES = 8
CONV_TAPS = 31
CONV_HIST = CONV_TAPS - 1
CONV_PAD = 32
POOL_WINDOWS = (2, 4, 8, 16)
POOL_HIST = max(POOL_WINDOWS) - 1
POOL_PAD = 16
PAST_LEN = 16384
LN_EPS = 1e-5
GN_EPS = 64e-5
VMEM_LIMIT = 56 * 1024 * 1024


def _ln(z, g, b, eps):
    mu = jnp.mean(z, axis=-1, keepdims=True)
    zc = z - mu
    var = jnp.mean(zc * zc, axis=-1, keepdims=True)
    return zc * lax.rsqrt(var + eps) * g + b


def _sigmoid(x):
    return 1.0 / (1.0 + jnp.exp(-x))


def _cparams(sem):
    return pltpu.CompilerParams(dimension_semantics=sem, vmem_limit_bytes=VMEM_LIMIT)


def _log2(n):
    assert n & (n - 1) == 0
    return n.bit_length() - 1


def _inproj_kernel(x_ref, wa_ref, wb_ref, oa_ref, ob_ref):
    xb = x_ref[...].astype(BF16)
    pa = jnp.dot(xb, wa_ref[...], preferred_element_type=F32)
    width = pa.shape[1] // 2
    oa_ref[...] = pa[:, :width] * _sigmoid(pa[:, width:])
    ob_ref[...] = jnp.dot(xb, wb_ref[...], preferred_element_type=F32)


def _inproj(x, wa, wb, tm):
    T, D = x.shape
    na, nb = wa.shape[1] // 2, wb.shape[1]
    return pl.pallas_call(
        _inproj_kernel,
        grid=(T // tm,),
        in_specs=[pl.BlockSpec((tm, D), lambda i: (i, 0)),
                  pl.BlockSpec(wa.shape, lambda i: (0, 0)),
                  pl.BlockSpec((D, nb), lambda i: (0, 0))],
        out_specs=[pl.BlockSpec((tm, na), lambda i: (i, 0)),
                   pl.BlockSpec((tm, nb), lambda i: (i, 0))],
        out_shape=[jax.ShapeDtypeStruct((T, na), F32), jax.ShapeDtypeStruct((T, nb), F32)],
        compiler_params=_cparams(("parallel",)),
        name="in_proj",
    )(x, wa, wb)


def _conv_kernel(u_ref, st_ref, w_ref, b_ref, g_ref, bb_ref, o_ref, nst_ref, ext_ref, *, tl, width):
    c = pl.program_id(1)

    base = CONV_PAD - CONV_HIST

    @pl.when(c == 0)
    def _():
        ext_ref[:, base:CONV_PAD, :] = st_ref[...]
        ext_ref[:, CONV_PAD + tl:CONV_PAD + tl + SUBLANES, :] = jnp.zeros((u_ref.shape[0], SUBLANES, width), F32)

    ext_ref[:, CONV_PAD:CONV_PAD + tl, :] = u_ref[...]
    acc = None
    for rho in range(SUBLANES):
        part = None
        for m in range((base + CONV_TAPS) // SUBLANES + 1):
            j = SUBLANES * m + rho - base
            if 0 <= j < CONV_TAPS:
                term = ext_ref[:, SUBLANES * m:SUBLANES * m + tl + SUBLANES, :] * w_ref[j:j + 1, :]
                part = term if part is None else part + term
        shifted = part[:, rho:rho + tl, :]
        acc = shifted if acc is None else acc + shifted
    acc = acc + b_ref[...]
    y = _ln(acc, g_ref[...], bb_ref[...], LN_EPS)
    o_ref[...] = (y * _sigmoid(y)).astype(o_ref.dtype)
    tail = ext_ref[:, tl + CONV_PAD - CONV_HIST:tl + CONV_PAD, :]
    ext_ref[:, CONV_PAD - CONV_HIST:CONV_PAD, :] = tail
    nst_ref[...] = tail


def _conv_branch(glu, state, w, b, g, bb, nb, tl):
    n_seq, seq_len, width = glu.shape
    vec = pl.BlockSpec((1, width), lambda n, c: (0, 0))
    return pl.pallas_call(
        functools.partial(_conv_kernel, tl=tl, width=width),
        grid=(n_seq // nb, seq_len // tl),
        in_specs=[pl.BlockSpec((nb, tl, width), lambda n, c: (n, c, 0)),
                  pl.BlockSpec((nb, CONV_HIST, width), lambda n, c: (n, 0, 0)),
                  pl.BlockSpec((CONV_TAPS, width), lambda n, c: (0, 0)),
                  vec, vec, vec],
        out_specs=[pl.BlockSpec((nb, tl, width), lambda n, c: (n, c, 0)),
                   pl.BlockSpec((nb, CONV_HIST, width), lambda n, c: (n, 0, 0))],
        out_shape=[jax.ShapeDtypeStruct((n_seq, seq_len, width), BF16 if tl % (2 * SUBLANES) == 0 else F32),
                   jax.ShapeDtypeStruct((n_seq, CONV_HIST, width), F32)],
        scratch_shapes=[pltpu.VMEM((nb, CONV_PAD + tl + SUBLANES, width), F32)],
        compiler_params=_cparams(("parallel", "arbitrary")),
        name="conv_branch",
    )(glu, state, w, b, g, bb)


def _dotb(a, b):
    return jnp.dot(a, b, preferred_element_type=F32)


def _dotb_nt(a, b):
    return lax.dot_general(a, b, (((1,), (1,)), ((), ())), preferred_element_type=F32)


def _b(x):
    return x.astype(BF16)


def _rwkv_kernel(p_ref, sh_ref, wkv_ref, mu_ref, w0_ref, w2a2_ref, a0_ref, g2_ref, kk_ref, ka_ref, rk_ref,
                 lng_ref, lnb_ref, o_ref, nsh_ref, nwkv_ref, s_scr, prev_scr, *, nsb, tb, bw, lw):
    c = pl.program_id(1)
    n_pairs = bw // PAIR
    C = CHUNK
    R = nsb * tb
    nq = R // C
    ts = min(tb, C)
    ns = C // ts
    assert HEAD == C and nq * C == R and (nsb == 1 or tb == ts)
    ts_bits = _log2(ts)
    sb = min(SOLVE_BLOCK, ts)
    sb_bits = _log2(sb)

    @pl.when(c == 0)
    def _():
        prev_scr[...] = sh_ref[...]
        s_scr[...] = jnp.zeros_like(s_scr)
        for j in range(nsb):
            for h in range(2 * n_pairs):
                lo = (h % 2) * HEAD
                s_scr[j, h // 2, lo:lo + HEAD, lo:lo + HEAD] = wkv_ref[j, h]

    p = p_ref[...]
    pw = p.shape[1]
    row = lax.broadcasted_iota(jnp.int32, (R, 1), 0)
    if nsb == 1:
        first = prev_scr[0]
    else:
        first = jnp.concatenate([jnp.broadcast_to(prev_scr[j], (tb, pw)) for j in range(nsb)], axis=0)
    prev = jnp.where((row & (tb - 1)) == 0, first, pltpu.roll(p, 1, 0))
    for j in range(nsb):
        last = p[(j + 1) * tb - 1:(j + 1) * tb, :]
        prev_scr[j] = last
        nsh_ref[j] = last
    xm = p + mu_ref[...] * (prev - p)
    r = xm[:, 0:bw]
    k = xm[:, bw:2 * bw]
    v = xm[:, 2 * bw:3 * bw]
    dwa = xm[:, 3 * bw:3 * bw + 2 * lw]
    dg = xm[:, 3 * bw + 2 * lw:]
    lane_l = lax.broadcasted_iota(jnp.int32, (1, 2 * lw), 1)
    is_w = lane_l < lw
    w2a2 = _b(w2a2_ref[...])
    wpre = w0_ref[...] + _dotb(_b(jnp.where(is_w, jnp.tanh(dwa), 0.0)), w2a2)
    apre = a0_ref[...] + _dotb(_b(jnp.where(is_w, 0.0, dwa)), w2a2)
    z = -wpre
    wlog = -(jnp.maximum(z, 0.0) + jnp.log(1.0 + jnp.exp(-jnp.abs(z)))) - 0.5
    logw = -jnp.exp(wlog)
    a_lr = _sigmoid(apre)
    gate = _dotb(_b(_sigmoid(dg)), _b(g2_ref[...]))
    kkv = k * kk_ref[...]
    k2 = k * (1.0 + (a_lr - 1.0) * ka_ref[...])
    rk2 = r * k2 * rk_ref[...]

    lane = lax.broadcasted_iota(jnp.int32, (1, PAIR), 1)
    m_l = lane < HEAD
    rowp = lax.broadcasted_iota(jnp.int32, (PAIR, PAIR), 0)
    colp = lax.broadcasted_iota(jnp.int32, (PAIR, PAIR), 1)
    bd = (rowp >= HEAD) == (colp >= HEAD)
    t_i = lax.broadcasted_iota(jnp.int32, (C, PAIR), 0)
    s_i = lax.broadcasted_iota(jnp.int32, (C, PAIR), 1) & (HEAD - 1)
    same = (s_i >> ts_bits) == (t_i >> ts_bits)
    strict = same & (s_i < t_i)
    incl = same & (s_i <= t_i)
    eye = (s_i == t_i).astype(F32)
    dblk = (s_i >> sb_bits) == (t_i >> sb_bits)

    def vs(y):
        return jnp.concatenate([jnp.where(m_l, y, 0.0), jnp.where(m_l, 0.0, y)], axis=0)

    def pm(a, y):
        return _dotb(_b(a), vs(_b(y)))

    def pmm(lhs, y):
        out = _dotb(_b(jnp.concatenate(lhs, axis=0)), vs(_b(y)))
        return [out[k * C:(k + 1) * C] for k in range(len(lhs))]

    def bdsum(xs):
        out = []
        for x in xs:
            s0 = jnp.sum(jnp.where(m_l, x, 0.0), axis=1, keepdims=True)
            s1 = jnp.sum(jnp.where(m_l, 0.0, x), axis=1, keepdims=True)
            out.append(jnp.where(m_l, s0, s1))
        return out

    P = range(n_pairs)
    units = [(q, i) for q in range(nq) for i in P]
    NU = range(len(units))

    def blk(x, n):
        q, i = units[n]
        return x[q * C:(q + 1) * C, i * PAIR:(i + 1) * PAIR]

    def neumann(m, n_terms):
        t = [eye + m[n] for n in NU]
        levels = _log2(n_terms)
        if levels == 1:
            return t
        p = [pm(m[n], m[n]) for n in NU]
        for _ in range(levels - 2):
            both = [pmm([p[n], t[n]], p[n]) for n in NU]
            t = [t[n] + both[n][1] for n in NU]
            p = [both[n][0] for n in NU]
        return [t[n] + pm(t[n], p[n]) for n in NU]

    t_run = row & (ts - 1)
    cs = logw
    for lev in range(ts_bits):
        s = 1 << lev
        cs = cs + jnp.where(t_run >= s, pltpu.roll(cs, s, 0), 0.0)
    g_in = jnp.exp(cs)
    g_ex = jnp.exp(cs - logw)
    g_inv = jnp.exp(-cs)
    runs = g_in.reshape(R // ts, ts, bw)
    g_end = jnp.broadcast_to(runs[:, ts - 1:ts, :], runs.shape).reshape(R, bw)

    sums = bdsum([blk(kkv, n) * blk(kkv, n) for n in NU] + [blk(rk2, n) for n in NU])
    kk_n = [blk(kkv, n) / jnp.maximum(jnp.sqrt(sums[n]), 1e-12) for n in NU]
    v_p = [blk(v, n) for n in NU]
    at = [-kk_n[n] * blk(g_ex, n) for n in NU]
    rt = [blk(r, n) * blk(g_in, n) for n in NU]
    bt = [kk_n[n] * blk(a_lr, n) * blk(g_inv, n) for n in NU]
    kt = [blk(k2, n) * blk(g_inv, n) for n in NU]
    sm = [_dotb_nt(_b(jnp.concatenate([at[n], rt[n]], axis=0)),
                   jnp.concatenate([vs(_b(bt[n])), vs(_b(kt[n]))], axis=0)) for n in NU]
    lab = [jnp.where(strict, sm[n][0:C, 0:PAIR], 0.0) for n in NU]
    lak = [jnp.where(strict, sm[n][0:C, PAIR:2 * PAIR], 0.0) for n in NU]
    mrb = [jnp.where(incl, sm[n][C:2 * C, 0:PAIR], 0.0) for n in NU]
    mrk = [jnp.where(incl, sm[n][C:2 * C, PAIR:2 * PAIR], 0.0) for n in NU]
    d1 = [jnp.where(dblk, lab[n], 0.0) for n in NU]
    tinv = neumann(d1, sb)
    if sb < ts:
        fp = [pm(lab[n] - d1[n], tinv[n]) for n in NU]
        levels = _log2(ts // sb)
        for lev in range(levels):
            if lev < levels - 1:
                both = [pmm([fp[n], tinv[n]], fp[n]) for n in NU]
                tinv = [tinv[n] + both[n][1] for n in NU]
                fp = [both[n][0] for n in NU]
            else:
                tinv = [tinv[n] + pm(tinv[n], fp[n]) for n in NU]
    tinv = [_b(tinv[n]) for n in NU]
    xv = [pmm([lak[n], mrk[n]], v_p[n]) for n in NU]
    x0 = [xv[n][0] for n in NU]
    s_cur = [[s_scr[sq, i] for i in P] for sq in range(nsb)]
    col_seq = (lax.broadcasted_iota(jnp.int32, (1, PAIR), 1) & (HEAD - 1)) >> ts_bits

    def finish(q, o_q):
        un = [q * n_pairs + i for i in P]
        om = bdsum(o_q)
        oc = [o_q[i] - om[i] * (1.0 / HEAD) for i in P]
        ov = bdsum([oc[i] * oc[i] for i in P])
        for i in P:
            n = un[i]
            ln = slice(i * PAIR, (i + 1) * PAIR)
            bonus = sums[len(units) + n] * v_p[n]
            y = oc[i] * lax.rsqrt(ov[i] * (1.0 / HEAD) + GN_EPS) * lng_ref[:, ln] + lnb_ref[:, ln] + bonus
            o_ref[q * C:(q + 1) * C, ln] = (y * blk(gate, n)).astype(o_ref.dtype)

    o_prev = None
    for q in range(nq):
        un = [q * n_pairs + i for i in P]
        seq = [q * ns + j if nsb > 1 else 0 for j in range(ns)]
        a_s, r_s = [], []
        for i in P:
            parts = []
            for j in range(ns):
                rows = slice(j * ts, (j + 1) * ts)
                lhs = _b(jnp.concatenate([at[un[i]][rows], rt[un[i]][rows]], axis=0))
                parts.append(_dotb_nt(lhs, _b(s_cur[seq[j]][i])))
            a_s.append(jnp.concatenate([x[0:ts] for x in parts], axis=0))
            r_s.append(jnp.concatenate([x[ts:2 * ts] for x in parts], axis=0))
        if o_prev is not None:
            finish(q - 1, o_prev)
        u = [_dotb(tinv[un[i]], vs(_b(a_s[i] + x0[un[i]]))) for i in P]
        for i in P:
            n = un[i]
            ge = blk(g_end, n)
            uv = _b(jnp.concatenate([u[i], v_p[n]], axis=0).T)
            bk = _b(jnp.concatenate([bt[n] * ge, kt[n] * ge], axis=0))
            for j in range(ns):
                uv_j = uv if ns == 1 else jnp.where(col_seq == j, uv, 0.0)
                upd = _dotb(uv_j, bk)
                s_cur[seq[j]][i] = s_cur[seq[j]][i] * ge[j * ts:j * ts + 1, :] + jnp.where(bd, upd, 0.0)
        o_prev = [r_s[i] + pm(mrb[un[i]], u[i]) + xv[un[i]][1] for i in P]
    for sq in range(nsb):
        for i in P:
            s_scr[sq, i] = s_cur[sq][i]
    finish(nq - 1, o_prev)

    @pl.when(c == pl.num_programs(1) - 1)
    def _():
        for j in range(nsb):
            for h in range(2 * n_pairs):
                lo = (h % 2) * HEAD
                nwkv_ref[j, h] = s_scr[j, h // 2, lo:lo + HEAD, lo:lo + HEAD]


def _rwkv_branch(proj_b, shift, wkv, prm, n_seq, seq_len, nsb, tb):
    T, pw = proj_b.shape
    n_heads = wkv.shape[1]
    bw = n_heads * HEAD
    lw = prm["w2a2"].shape[0] // 2
    nc = seq_len // tb
    assert nsb == 1 or nc == 1
    rows = nsb * tb

    def full(a):
        return pl.BlockSpec(a.shape, lambda n, c: (0,) * a.ndim)

    names = ("mu", "w0", "w2a2", "a0", "g2", "kk", "ka", "rk", "lng", "lnb")
    return pl.pallas_call(
        functools.partial(_rwkv_kernel, nsb=nsb, tb=tb, bw=bw, lw=lw),
        grid=(n_seq // nsb, nc),
        in_specs=[pl.BlockSpec((rows, pw), lambda n, c: (n * nc + c, 0)),
                  pl.BlockSpec((nsb, 1, pw), lambda n, c: (n, 0, 0)),
                  pl.BlockSpec((nsb, n_heads, HEAD, HEAD), lambda n, c: (n, 0, 0, 0))]
                 + [full(prm[k]) for k in names],
        out_specs=[pl.BlockSpec((rows, bw), lambda n, c: (n * nc + c, 0)),
                   pl.BlockSpec((nsb, 1, pw), lambda n, c: (n, 0, 0)),
                   pl.BlockSpec((nsb, n_heads, HEAD, HEAD), lambda n, c: (n, 0, 0, 0))],
        out_shape=[jax.ShapeDtypeStruct((T, bw), BF16),
                   jax.ShapeDtypeStruct((n_seq, 1, pw), F32),
                   jax.ShapeDtypeStruct((n_seq, n_heads, HEAD, HEAD), F32)],
        scratch_shapes=[pltpu.VMEM((nsb, n_heads // 2, PAIR, PAIR), F32), pltpu.VMEM((nsb, 1, pw), F32)],
        compiler_params=_cparams(("parallel", "arbitrary")),
        name="rwkv_branch",
    )(proj_b, shift.reshape(n_seq, 1, pw), wkv, *[prm[k] for k in names])


def _outproj_kernel(a_ref, b_ref, x_ref, wa_ref, wb_ref, g_ref, bb_ref, o_ref, *, alpha):
    mix = jnp.dot(a_ref[...].astype(BF16), wa_ref[...], preferred_element_type=F32)
    mix = mix + jnp.dot(b_ref[...].astype(BF16), wb_ref[...], preferred_element_type=F32)
    o_ref[...] = _ln(alpha * x_ref[...] + mix, g_ref[...], bb_ref[...], LN_EPS)


def _outproj(a, b, x, wa, wb, g, bb, alpha, tm):
    T, D = x.shape
    wdt = a.shape[1]
    vec = pl.BlockSpec((1, D), lambda i: (0, 0))
    return pl.pallas_call(
        functools.partial(_outproj_kernel, alpha=alpha),
        grid=(T // tm,),
        in_specs=[pl.BlockSpec((tm, wdt), lambda i: (i, 0)),
                  pl.BlockSpec((tm, wdt), lambda i: (i, 0)),
                  pl.BlockSpec((tm, D), lambda i: (i, 0)),
                  pl.BlockSpec((wdt, D), lambda i: (0, 0)),
                  pl.BlockSpec((wdt, D), lambda i: (0, 0)),
                  vec, vec],
        out_specs=pl.BlockSpec((tm, D), lambda i: (i, 0)),
        out_shape=jax.ShapeDtypeStruct((T, D), F32),
        compiler_params=_cparams(("parallel",)),
        name="out_proj_ln",
    )(a, b, x, wa, wb, g, bb)


def _ffn_kernel(x_ref, wg_ref, wu_ref, wd_ref, g_ref, bb_ref, o_ref, acc_ref, xb_ref, wg_s, wu_s, wd_s,
                *, alpha, ln_rows):
    i = pl.program_id(0)
    j = pl.program_id(1)
    n_i = pl.num_programs(0) - 1
    cur = i % 2
    prv = 1 - cur

    def ln_slice():
        r0 = pl.multiple_of(jnp.minimum(j * ln_rows, acc_ref.shape[1] - ln_rows), SUBLANES)
        o_ref[pl.ds(r0, ln_rows), :] = _ln(acc_ref[prv, pl.ds(r0, ln_rows), :], g_ref[...], bb_ref[...], LN_EPS)

    @pl.when(i == 0)
    def _():
        wg_s[j] = wg_ref[...].astype(BF16)
        wu_s[j] = wu_ref[...].astype(BF16)
        wd_s[j] = wd_ref[...].astype(BF16)

    @pl.when((i == 0) & (j == 0))
    def _():
        acc_ref[1] = jnp.zeros(acc_ref.shape[1:], F32)

    @pl.when((j == 0) & (i < n_i))
    def _():
        x = x_ref[...]
        xb_ref[...] = x.astype(BF16)
        acc_ref[cur] = alpha * x

    @pl.when(i < n_i)
    def _():
        ln_slice()
        xb = xb_ref[...]
        hg = jnp.dot(xb, wg_s[j], preferred_element_type=F32)
        hu = jnp.dot(xb, wu_s[j], preferred_element_type=F32)
        h = (hg * _sigmoid(hg) * hu).astype(BF16)
        acc_ref[cur] += jnp.dot(h, wd_s[j], preferred_element_type=F32)

    @pl.when(i == n_i)
    def _():
        ln_slice()


def _ffn(x, wg, wu, wd, layer, g, bb, alpha, tm, tf):
    T, D = x.shape
    Fd = wg.shape[2]
    nj = Fd // tf
    n_i = T // tm
    ln_rows = -(-tm // (nj * SUBLANES)) * SUBLANES
    vec = pl.BlockSpec((1, D), lambda i, j: (0, 0))

    def wj(i, j):
        return jnp.where(i == 0, j, nj - 1)

    return pl.pallas_call(
        functools.partial(_ffn_kernel, alpha=alpha, ln_rows=ln_rows),
        grid=(n_i + 1, nj),
        in_specs=[pl.BlockSpec((tm, D), lambda i, j: (jnp.minimum(i, n_i - 1), 0)),
                  pl.BlockSpec((None, D, tf), lambda i, j: (layer, 0, wj(i, j))),
                  pl.BlockSpec((None, D, tf), lambda i, j: (layer, 0, wj(i, j))),
                  pl.BlockSpec((None, tf, D), lambda i, j: (layer, wj(i, j), 0)),
                  vec, vec],
        out_specs=pl.BlockSpec((tm, D), lambda i, j: (jnp.maximum(i - 1, 0), 0)),
        out_shape=jax.ShapeDtypeStruct((T, D), F32),
        scratch_shapes=[pltpu.VMEM((2, tm, D), F32), pltpu.VMEM((tm, D), BF16),
                        pltpu.VMEM((nj, D, tf), BF16), pltpu.VMEM((nj, D, tf), BF16), pltpu.VMEM((nj, tf, D), BF16)],
        compiler_params=_cparams(("arbitrary", "arbitrary")),
        name="ffn_ln",
    )(x, wg, wu, wd, g, bb)


def _pool_kernel(x_ref, st_ref, w_ref, sc_ref, g_ref, bb_ref, o_ref, nst_ref, ext_ref, *, tl, start, alpha):
    c = pl.program_id(1)

    x = x_ref[...]
    nb, _, D = x.shape
    gw = D // len(POOL_WINDOWS)
    n_ext = POOL_PAD + tl

    @pl.when(c == 0)
    def _():
        ext_ref[:, 0:POOL_PAD - POOL_HIST, :] = jnp.zeros((nb, POOL_PAD - POOL_HIST, D), F32)
        ext_ref[:, POOL_PAD - POOL_HIST:POOL_PAD, :] = st_ref[...]

    ext_ref[:, POOL_PAD:n_ext, :] = x
    pos = start + c * tl + lax.broadcasted_iota(jnp.int32, (1, tl, 1), 1)
    outs = []
    for gi, wdw in enumerate(POOL_WINDOWS):
        lo = gi * gw
        acc = ext_ref[:, 1:n_ext, lo:lo + gw] + ext_ref[:, 0:n_ext - 1, lo:lo + gw]
        h = 2
        while h < wdw:
            rows = acc.shape[1]
            acc = acc[:, h:rows, :] + acc[:, 0:rows - h, :]
            h *= 2
        first = POOL_PAD + 1 - wdw
        ssum = acc[:, first:first + tl, :]
        cnt = jnp.minimum(wdw, pos + 1).astype(F32)
        pooled = ssum / cnt - x[..., lo:lo + gw]
        y = jnp.dot(pooled.reshape(nb * tl, gw).astype(BF16), w_ref[gi], preferred_element_type=F32)
        outs.append(y.reshape(nb, tl, gw))
    y = jnp.concatenate(outs, axis=-1) * sc_ref[...]
    o_ref[...] = _ln(alpha * x + y, g_ref[...], bb_ref[...], LN_EPS)
    tail = ext_ref[:, tl + POOL_PAD - POOL_HIST:tl + POOL_PAD, :]
    ext_ref[:, POOL_PAD - POOL_HIST:POOL_PAD, :] = tail
    nst_ref[...] = tail


def _pool_mix(x, state, w, scale, g, bb, nb, tl, start, alpha):
    n_seq, seq_len, D = x.shape
    vec = pl.BlockSpec((1, D), lambda n, c: (0, 0))
    return pl.pallas_call(
        functools.partial(_pool_kernel, tl=tl, start=start, alpha=alpha),
        grid=(n_seq // nb, seq_len // tl),
        in_specs=[pl.BlockSpec((nb, tl, D), lambda n, c: (n, c, 0)),
                  pl.BlockSpec((nb, POOL_HIST, D), lambda n, c: (n, 0, 0)),
                  pl.BlockSpec(w.shape, lambda n, c: (0, 0, 0)),
                  vec, vec, vec],
        out_specs=[pl.BlockSpec((nb, tl, D), lambda n, c: (n, c, 0)),
                   pl.BlockSpec((nb, POOL_HIST, D), lambda n, c: (n, 0, 0))],
        out_shape=[jax.ShapeDtypeStruct((n_seq, seq_len, D), F32),
                   jax.ShapeDtypeStruct((n_seq, POOL_HIST, D), F32)],
        scratch_shapes=[pltpu.VMEM((nb, POOL_PAD + tl, D), F32)],
        compiler_params=_cparams(("parallel", "arbitrary")),
        name="pool_mix_ln",
    )(x, state, w, scale, g, bb)


def _trunk(x3, conv_prev, shift_prev, wkv_prev, pool_prev, start, prm, tiles):
    n_seq, seq_len, D = x3.shape
    T = n_seq * seq_len
    x = x3.reshape(T, D)
    alpha = prm["alpha"]
    tm = tiles["tm"]
    glu, proj_b = _inproj(x, prm["w_in_a"], prm["w_in_b"], tm)
    a_out, new_conv = _conv_branch(glu.reshape(n_seq, seq_len, -1), conv_prev, prm["conv_w"], prm["conv_b"],
                                   prm["conv_ln_g"], prm["conv_ln_b"], tiles["seq_nb"], tiles["conv_tl"])
    b_out, new_shift, new_wkv = _rwkv_branch(proj_b, shift_prev, wkv_prev, prm["rwkv"], n_seq, seq_len,
                                             tiles["rwkv_nsb"], tiles["rwkv_tb"])
    x = _outproj(a_out.reshape(T, -1), b_out, x, prm["w_out_a"], prm["w_out_b"], prm["ln_mix_g"][0],
                 prm["ln_mix_b"][0], alpha, tm)
    x = _ffn(x, prm["ffn_gate"], prm["ffn_up"], prm["ffn_down"], 0, prm["ln_ffn_g"][0], prm["ln_ffn_b"][0],
             alpha, tm, tiles["tf"])
    x, new_pool = _pool_mix(x.reshape(n_seq, seq_len, D), pool_prev, prm["pool_w"], prm["pool_scale"],
                            prm["ln_mix_g"][1], prm["ln_mix_b"][1], tiles["seq_nb"], tiles["pool_tl"], start, alpha)
    x = _ffn(x.reshape(T, D), prm["ffn_gate"], prm["ffn_up"], prm["ffn_down"], 1, prm["ln_ffn_g"][1],
             prm["ln_ffn_b"][1], alpha, tm, tiles["tf"])
    return (x.reshape(n_seq, seq_len, D), new_conv[None], new_shift.reshape(1, n_seq, -1), new_wkv[None],
            new_pool[None])


def kernel(x_prompt, x_sample, state_conv, state_shift, state_wkv, state_pool, w_in, conv_w, conv_b, conv_ln_g, conv_ln_b, rwkv_mu, rwkv_w0, rwkv_w2, rwkv_a0, rwkv_a2, rwkv_g2, rwkv_kk, rwkv_ka, rwkv_rk, rwkv_lnx_g, rwkv_lnx_b, w_out, pool_w, pool_scale, ln_mix_g, ln_mix_b, ffn_gate, ffn_up, ffn_down, ln_ffn_g, ln_ffn_b):
    depth = ln_mix_g.shape[0]
    assert depth == 2 and w_in.shape[0] == 1 and pool_w.shape[0] == 1
    B, L, D = x_prompt.shape
    Bs, Ls, _ = x_sample.shape
    a_w = conv_w.shape[2]
    b_w = rwkv_w0.shape[1]
    n_heads = b_w // HEAD
    alpha = float((2 * depth) ** 0.25)

    row = lambda a: a.reshape(1, -1)
    prm = dict(
        alpha=alpha,
        w_in_a=w_in[0, :, :2 * a_w].astype(BF16), w_in_b=w_in[0, :, 2 * a_w:].astype(BF16),
        conv_w=conv_w[0], conv_b=row(conv_b[0]), conv_ln_g=row(conv_ln_g[0]), conv_ln_b=row(conv_ln_b[0]),
        rwkv=dict(mu=row(rwkv_mu[0]), w0=row(rwkv_w0[0]),
                  w2a2=jnp.concatenate([rwkv_w2[0], rwkv_a2[0]], axis=0), a0=row(rwkv_a0[0]), g2=rwkv_g2[0],
                  kk=row(rwkv_kk[0]), ka=row(rwkv_ka[0]), rk=row(rwkv_rk[0]),
                  lng=row(rwkv_lnx_g[0]), lnb=row(rwkv_lnx_b[0])),
        w_out_a=w_out[0, :a_w].astype(BF16), w_out_b=w_out[0, a_w:].astype(BF16),
        pool_w=pool_w[0].astype(BF16), pool_scale=row(pool_scale[0]),
        ln_mix_g=[row(ln_mix_g[i]) for i in range(depth)], ln_mix_b=[row(ln_mix_b[i]) for i in range(depth)],
        ffn_gate=ffn_gate, ffn_up=ffn_up, ffn_down=ffn_down,
        ln_ffn_g=[row(ln_ffn_g[i]) for i in range(depth)], ln_ffn_b=[row(ln_ffn_b[i]) for i in range(depth)],
    )
    dt = x_prompt.dtype
    z_conv = jnp.zeros((B, CONV_HIST, a_w), dt)
    z_shift = jnp.zeros((B, state_shift.shape[-1]), dt)
    z_wkv = jnp.zeros((B, n_heads, HEAD, HEAD), state_wkv.dtype)
    z_pool = jnp.zeros((B, POOL_HIST, D), dt)
    tiles_p = dict(tm=1024, tf=256, seq_nb=1, conv_tl=512, pool_tl=1024, rwkv_nsb=1, rwkv_tb=8 * CHUNK)
    tiles_s = dict(tm=1024, tf=256, seq_nb=32, conv_tl=Ls, pool_tl=Ls, rwkv_nsb=4 * CHUNK // Ls, rwkv_tb=Ls)
    yp = _trunk(x_prompt, z_conv, z_shift, z_wkv, z_pool, 0, prm, tiles_p)
    ys = _trunk(x_sample, state_conv[0], state_shift[0], state_wkv[0], state_pool[0], PAST_LEN, prm, tiles_s)
    return (yp[0], ys[0], yp[1], yp[2], yp[3], yp[4], ys[1], ys[2], ys[3], ys[4])
```

```python
import functools

import jax
import jax.numpy as jnp
from jax import lax
from jax.experimental import pallas as pl
from jax.experimental.pallas import tpu as pltpu

F32 = jnp.float32
BF16 = jnp.bfloat16

HEAD = 64
PAIR = 2 * HEAD
CHUNK = 64
SOLVE_BLOCK = 16
SUBLANES = 8
CONV_TAPS = 31
CONV_HIST = CONV_TAPS - 1
CONV_PAD = 32
POOL_WINDOWS = (2, 4, 8, 16)
POOL_HIST = max(POOL_WINDOWS) - 1
POOL_PAD = 16
PAST_LEN = 16384
LN_EPS = 1e-5
GN_EPS = 64e-5
VMEM_LIMIT = 56 * 1024 * 1024


def _ln(z, g, b, eps):
    mu = jnp.mean(z, axis=-1, keepdims=True)
    zc = z - mu
    var = jnp.mean(zc * zc, axis=-1, keepdims=True)
    return zc * lax.rsqrt(var + eps) * g + b


def _sigmoid(x):
    return 1.0 / (1.0 + jnp.exp(-x))


def _cparams(sem):
    return pltpu.CompilerParams(dimension_semantics=sem, vmem_limit_bytes=VMEM_LIMIT)


def _log2(n):
    assert n & (n - 1) == 0
    return n.bit_length() - 1


def _inproj_kernel(x_ref, wa_ref, wb_ref, oa_ref, ob_ref):
    xb = x_ref[...].astype(BF16)
    pa = jnp.dot(xb, wa_ref[...], preferred_element_type=F32)
    width = pa.shape[1] // 2
    oa_ref[...] = pa[:, :width] * _sigmoid(pa[:, width:])
    ob_ref[...] = jnp.dot(xb, wb_ref[...], preferred_element_type=F32)


def _inproj(x, wa, wb, tm):
    T, D = x.shape
    na, nb = wa.shape[1] // 2, wb.shape[1]
    return pl.pallas_call(
        _inproj_kernel,
        grid=(T // tm,),
        in_specs=[pl.BlockSpec((tm, D), lambda i: (i, 0)),
                  pl.BlockSpec(wa.shape, lambda i: (0, 0)),
                  pl.BlockSpec((D, nb), lambda i: (0, 0))],
        out_specs=[pl.BlockSpec((tm, na), lambda i: (i, 0)),
                   pl.BlockSpec((tm, nb), lambda i: (i, 0))],
        out_shape=[jax.ShapeDtypeStruct((T, na), F32), jax.ShapeDtypeStruct((T, nb), F32)],
        compiler_params=_cparams(("parallel",)),
        name="in_proj",
    )(x, wa, wb)


def _conv_kernel(u_ref, st_ref, w_ref, b_ref, g_ref, bb_ref, o_ref, nst_ref, ext_ref, *, tl, width):
    c = pl.program_id(1)

    base = CONV_PAD - CONV_HIST

    @pl.when(c == 0)
    def _():
        ext_ref[:, base:CONV_PAD, :] = st_ref[...]
        ext_ref[:, CONV_PAD + tl:CONV_PAD + tl + SUBLANES, :] = jnp.zeros((u_ref.shape[0], SUBLANES, width), F32)

    ext_ref[:, CONV_PAD:CONV_PAD + tl, :] = u_ref[...]
    acc = None
    for rho in range(SUBLANES):
        part = None
        for m in range((base + CONV_TAPS) // SUBLANES + 1):
            j = SUBLANES * m + rho - base
            if 0 <= j < CONV_TAPS:
                term = ext_ref[:, SUBLANES * m:SUBLANES * m + tl + SUBLANES, :] * w_ref[j:j + 1, :]
                part = term if part is None else part + term
        shifted = part[:, rho:rho + tl, :]
        acc = shifted if acc is None else acc + shifted
    acc = acc + b_ref[...]
    y = _ln(acc, g_ref[...], bb_ref[...], LN_EPS)
    o_ref[...] = (y * _sigmoid(y)).astype(o_ref.dtype)
    tail = ext_ref[:, tl + CONV_PAD - CONV_HIST:tl + CONV_PAD, :]
    ext_ref[:, CONV_PAD - CONV_HIST:CONV_PAD, :] = tail
    nst_ref[...] = tail


def _conv_branch(glu, state, w, b, g, bb, nb, tl):
    n_seq, seq_len, width = glu.shape
    vec = pl.BlockSpec((1, width), lambda n, c: (0, 0))
    return pl.pallas_call(
        functools.partial(_conv_kernel, tl=tl, width=width),
        grid=(n_seq // nb, seq_len // tl),
        in_specs=[pl.BlockSpec((nb, tl, width), lambda n, c: (n, c, 0)),
                  pl.BlockSpec((nb, CONV_HIST, width), lambda n, c: (n, 0, 0)),
                  pl.BlockSpec((CONV_TAPS, width), lambda n, c: (0, 0)),
                  vec, vec, vec],
        out_specs=[pl.BlockSpec((nb, tl, width), lambda n, c: (n, c, 0)),
                   pl.BlockSpec((nb, CONV_HIST, width), lambda n, c: (n, 0, 0))],
        out_shape=[jax.ShapeDtypeStruct((n_seq, seq_len, width), BF16 if tl % (2 * SUBLANES) == 0 else F32),
                   jax.ShapeDtypeStruct((n_seq, CONV_HIST, width), F32)],
        scratch_shapes=[pltpu.VMEM((nb, CONV_PAD + tl + SUBLANES, width), F32)],
        compiler_params=_cparams(("parallel", "arbitrary")),
        name="conv_branch",
    )(glu, state, w, b, g, bb)


def _dotb(a, b):
    return jnp.dot(a, b, preferred_element_type=F32)


def _dotb_nt(a, b):
    return lax.dot_general(a, b, (((1,), (1,)), ((), ())), preferred_element_type=F32)


def _b(x):
    return x.astype(BF16)


def _rwkv_kernel(p_ref, sh_ref, wkv_ref, mu_ref, w0_ref, w2a2_ref, a0_ref, g2_ref, kk_ref, ka_ref, rk_ref,
                 lng_ref, lnb_ref, o_ref, nsh_ref, nwkv_ref, s_scr, prev_scr, *, nsb, tb, bw, lw):
    c = pl.program_id(1)
    n_pairs = bw // PAIR
    C = CHUNK
    R = nsb * tb
    nq = R // C
    ts = min(tb, C)
    ns = C // ts
    assert HEAD == C and nq * C == R and (nsb == 1 or tb == ts)
    ts_bits = _log2(ts)
    sb = min(SOLVE_BLOCK, ts)
    sb_bits = _log2(sb)

    @pl.when(c == 0)
    def _():
        prev_scr[...] = sh_ref[...]
        s_scr[...] = jnp.zeros_like(s_scr)
        for j in range(nsb):
            for h in range(2 * n_pairs):
                lo = (h % 2) * HEAD
                s_scr[j, h // 2, lo:lo + HEAD, lo:lo + HEAD] = wkv_ref[j, h]

    p = p_ref[...]
    pw = p.shape[1]
    row = lax.broadcasted_iota(jnp.int32, (R, 1), 0)
    if nsb == 1:
        first = prev_scr[0]
    else:
        first = jnp.concatenate([jnp.broadcast_to(prev_scr[j], (tb, pw)) for j in range(nsb)], axis=0)
    prev = jnp.where((row & (tb - 1)) == 0, first, pltpu.roll(p, 1, 0))
    for j in range(nsb):
        last = p[(j + 1) * tb - 1:(j + 1) * tb, :]
        prev_scr[j] = last
        nsh_ref[j] = last
    xm = p + mu_ref[...] * (prev - p)
    r = xm[:, 0:bw]
    k = xm[:, bw:2 * bw]
    v = xm[:, 2 * bw:3 * bw]
    dwa = xm[:, 3 * bw:3 * bw + 2 * lw]
    dg = xm[:, 3 * bw + 2 * lw:]
    lane_l = lax.broadcasted_iota(jnp.int32, (1, 2 * lw), 1)
    is_w = lane_l < lw
    w2a2 = _b(w2a2_ref[...])
    wpre = w0_ref[...] + _dotb(_b(jnp.where(is_w, jnp.tanh(dwa), 0.0)), w2a2)
    apre = a0_ref[...] + _dotb(_b(jnp.where(is_w, 0.0, dwa)), w2a2)
    z = -wpre
    wlog = -(jnp.maximum(z, 0.0) + jnp.log(1.0 + jnp.exp(-jnp.abs(z)))) - 0.5
    logw = -jnp.exp(wlog)
    a_lr = _sigmoid(apre)
    gate = _dotb(_b(_sigmoid(dg)), _b(g2_ref[...]))
    kkv = k * kk_ref[...]
    k2 = k * (1.0 + (a_lr - 1.0) * ka_ref[...])
    rk2 = r * k2 * rk_ref[...]

    lane = lax.broadcasted_iota(jnp.int32, (1, PAIR), 1)
    m_l = lane < HEAD
    rowp = lax.broadcasted_iota(jnp.int32, (PAIR, PAIR), 0)
    colp = lax.broadcasted_iota(jnp.int32, (PAIR, PAIR), 1)
    bd = (rowp >= HEAD) == (colp >= HEAD)
    t_i = lax.broadcasted_iota(jnp.int32, (C, PAIR), 0)
    s_i = lax.broadcasted_iota(jnp.int32, (C, PAIR), 1) & (HEAD - 1)
    same = (s_i >> ts_bits) == (t_i >> ts_bits)
    strict = same & (s_i < t_i)
    incl = same & (s_i <= t_i)
    eye = (s_i == t_i).astype(F32)
    dblk = (s_i >> sb_bits) == (t_i >> sb_bits)

    def vs(y):
        return jnp.concatenate([jnp.where(m_l, y, 0.0), jnp.where(m_l, 0.0, y)], axis=0)

    def pm(a, y):
        return _dotb(_b(a), vs(_b(y)))

    def pmm(lhs, y):
        out = _dotb(_b(jnp.concatenate(lhs, axis=0)), vs(_b(y)))
        return [out[k * C:(k + 1) * C] for k in range(len(lhs))]

    def bdsum(xs):
        out = []
        for x in xs:
            s0 = jnp.sum(jnp.where(m_l, x, 0.0), axis=1, keepdims=True)
            s1 = jnp.sum(jnp.where(m_l, 0.0, x), axis=1, keepdims=True)
            out.append(jnp.where(m_l, s0, s1))
        return out

    P = range(n_pairs)
    units = [(q, i) for q in range(nq) for i in P]
    NU = range(len(units))

    def blk(x, n):
        q, i = units[n]
        return x[q * C:(q + 1) * C, i * PAIR:(i + 1) * PAIR]

    def neumann(m, n_terms):
        t = [eye + m[n] for n in NU]
        levels = _log2(n_terms)
        if levels == 1:
            return t
        p = [pm(m[n], m[n]) for n in NU]
        for _ in range(levels - 2):
            both = [pmm([p[n], t[n]], p[n]) for n in NU]
            t = [t[n] + both[n][1] for n in NU]
            p = [both[n][0] for n in NU]
        return [t[n] + pm(t[n], p[n]) for n in NU]

    t_run = row & (ts - 1)
    cs = logw
    for lev in range(ts_bits):
        s = 1 << lev
        cs = cs + jnp.where(t_run >= s, pltpu.roll(cs, s, 0), 0.0)
    g_in = jnp.exp(cs)
    g_ex = jnp.exp(cs - logw)
    g_inv = jnp.exp(-cs)
    runs = g_in.reshape(R // ts, ts, bw)
    g_end = jnp.broadcast_to(runs[:, ts - 1:ts, :], runs.shape).reshape(R, bw)

    sums = bdsum([blk(kkv, n) * blk(kkv, n) for n in NU] + [blk(rk2, n) for n in NU])
    kk_n = [blk(kkv, n) / jnp.maximum(jnp.sqrt(sums[n]), 1e-12) for n in NU]
    v_p = [blk(v, n) for n in NU]
    at = [-kk_n[n] * blk(g_ex, n) for n in NU]
    rt = [blk(r, n) * blk(g_in, n) for n in NU]
    bt = [kk_n[n] * blk(a_lr, n) * blk(g_inv, n) for n in NU]
    kt = [blk(k2, n) * blk(g_inv, n) for n in NU]
    sm = [_dotb_nt(_b(jnp.concatenate([at[n], rt[n]], axis=0)),
                   jnp.concatenate([vs(_b(bt[n])), vs(_b(kt[n]))], axis=0)) for n in NU]
    lab = [jnp.where(strict, sm[n][0:C, 0:PAIR], 0.0) for n in NU]
    lak = [jnp.where(strict, sm[n][0:C, PAIR:2 * PAIR], 0.0) for n in NU]
    mrb = [jnp.where(incl, sm[n][C:2 * C, 0:PAIR], 0.0) for n in NU]
    mrk = [jnp.where(incl, sm[n][C:2 * C, PAIR:2 * PAIR], 0.0) for n in NU]
    d1 = [jnp.where(dblk, lab[n], 0.0) for n in NU]
    tinv = neumann(d1, sb)
    if sb < ts:
        fp = [pm(lab[n] - d1[n], tinv[n]) for n in NU]
        levels = _log2(ts // sb)
        for lev in range(levels):
            if lev < levels - 1:
                both = [pmm([fp[n], tinv[n]], fp[n]) for n in NU]
                tinv = [tinv[n] + both[n][1] for n in NU]
                fp = [both[n][0] for n in NU]
            else:
                tinv = [tinv[n] + pm(tinv[n], fp[n]) for n in NU]
    tinv = [_b(tinv[n]) for n in NU]
    xv = [pmm([lak[n], mrk[n]], v_p[n]) for n in NU]
    x0 = [xv[n][0] for n in NU]
    s_cur = [[s_scr[sq, i] for i in P] for sq in range(nsb)]
    col_seq = (lax.broadcasted_iota(jnp.int32, (1, PAIR), 1) & (HEAD - 1)) >> ts_bits

    def finish(q, o_q):
        un = [q * n_pairs + i for i in P]
        om = bdsum(o_q)
        oc = [o_q[i] - om[i] * (1.0 / HEAD) for i in P]
        ov = bdsum([oc[i] * oc[i] for i in P])
        for i in P:
            n = un[i]
            ln = slice(i * PAIR, (i + 1) * PAIR)
            bonus = sums[len(units) + n] * v_p[n]
            y = oc[i] * lax.rsqrt(ov[i] * (1.0 / HEAD) + GN_EPS) * lng_ref[:, ln] + lnb_ref[:, ln] + bonus
            o_ref[q * C:(q + 1) * C, ln] = (y * blk(gate, n)).astype(o_ref.dtype)

    o_prev = None
    for q in range(nq):
        un = [q * n_pairs + i for i in P]
        seq = [q * ns + j if nsb > 1 else 0 for j in range(ns)]
        a_s, r_s = [], []
        for i in P:
            parts = []
            for j in range(ns):
                rows = slice(j * ts, (j + 1) * ts)
                lhs = _b(jnp.concatenate([at[un[i]][rows], rt[un[i]][rows]], axis=0))
                parts.append(_dotb_nt(lhs, _b(s_cur[seq[j]][i])))
            a_s.append(jnp.concatenate([x[0:ts] for x in parts], axis=0))
            r_s.append(jnp.concatenate([x[ts:2 * ts] for x in parts], axis=0))
        if o_prev is not None:
            finish(q - 1, o_prev)
        u = [_dotb(tinv[un[i]], vs(_b(a_s[i] + x0[un[i]]))) for i in P]
        for i in P:
            n = un[i]
            ge = blk(g_end, n)
            uv = _b(jnp.concatenate([u[i], v_p[n]], axis=0).T)
            bk = _b(jnp.concatenate([bt[n] * ge, kt[n] * ge], axis=0))
            for j in range(ns):
                uv_j = uv if ns == 1 else jnp.where(col_seq == j, uv, 0.0)
                upd = _dotb(uv_j, bk)
                s_cur[seq[j]][i] = s_cur[seq[j]][i] * ge[j * ts:j * ts + 1, :] + jnp.where(bd, upd, 0.0)
        o_prev = [r_s[i] + pm(mrb[un[i]], u[i]) + xv[un[i]][1] for i in P]
    for sq in range(nsb):
        for i in P:
            s_scr[sq, i] = s_cur[sq][i]
    finish(nq - 1, o_prev)

    @pl.when(c == pl.num_programs(1) - 1)
    def _():
        for j in range(nsb):
            for h in range(2 * n_pairs):
                lo = (h % 2) * HEAD
                nwkv_ref[j, h] = s_scr[j, h // 2, lo:lo + HEAD, lo:lo + HEAD]


def _rwkv_branch(proj_b, shift, wkv, prm, n_seq, seq_len, nsb, tb):
    T, pw = proj_b.shape
    n_heads = wkv.shape[1]
    bw = n_heads * HEAD
    lw = prm["w2a2"].shape[0] // 2
    nc = seq_len // tb
    assert nsb == 1 or nc == 1
    rows = nsb * tb

    def full(a):
        return pl.BlockSpec(a.shape, lambda n, c: (0,) * a.ndim)

    names = ("mu", "w0", "w2a2", "a0", "g2", "kk", "ka", "rk", "lng", "lnb")
    return pl.pallas_call(
        functools.partial(_rwkv_kernel, nsb=nsb, tb=tb, bw=bw, lw=lw),
        grid=(n_seq // nsb, nc),
        in_specs=[pl.BlockSpec((rows, pw), lambda n, c: (n * nc + c, 0)),
                  pl.BlockSpec((nsb, 1, pw), lambda n, c: (n, 0, 0)),
                  pl.BlockSpec((nsb, n_heads, HEAD, HEAD), lambda n, c: (n, 0, 0, 0))]
                 + [full(prm[k]) for k in names],
        out_specs=[pl.BlockSpec((rows, bw), lambda n, c: (n * nc + c, 0)),
                   pl.BlockSpec((nsb, 1, pw), lambda n, c: (n, 0, 0)),
                   pl.BlockSpec((nsb, n_heads, HEAD, HEAD), lambda n, c: (n, 0, 0, 0))],
        out_shape=[jax.ShapeDtypeStruct((T, bw), BF16),
                   jax.ShapeDtypeStruct((n_seq, 1, pw), F32),
                   jax.ShapeDtypeStruct((n_seq, n_heads, HEAD, HEAD), F32)],
        scratch_shapes=[pltpu.VMEM((nsb, n_heads // 2, PAIR, PAIR), F32), pltpu.VMEM((nsb, 1, pw), F32)],
        compiler_params=_cparams(("parallel", "arbitrary")),
        name="rwkv_branch",
    )(proj_b, shift.reshape(n_seq, 1, pw), wkv, *[prm[k] for k in names])


def _outproj_kernel(a_ref, b_ref, x_ref, wa_ref, wb_ref, g_ref, bb_ref, o_ref, *, alpha):
    mix = jnp.dot(a_ref[...].astype(BF16), wa_ref[...], preferred_element_type=F32)
    mix = mix + jnp.dot(b_ref[...].astype(BF16), wb_ref[...], preferred_element_type=F32)
    o_ref[...] = _ln(alpha * x_ref[...] + mix, g_ref[...], bb_ref[...], LN_EPS)


def _outproj(a, b, x, wa, wb, g, bb, alpha, tm):
    T, D = x.shape
    wdt = a.shape[1]
    vec = pl.BlockSpec((1, D), lambda i: (0, 0))
    return pl.pallas_call(
        functools.partial(_outproj_kernel, alpha=alpha),
        grid=(T // tm,),
        in_specs=[pl.BlockSpec((tm, wdt), lambda i: (i, 0)),
                  pl.BlockSpec((tm, wdt), lambda i: (i, 0)),
                  pl.BlockSpec((tm, D), lambda i: (i, 0)),
                  pl.BlockSpec((wdt, D), lambda i: (0, 0)),
                  pl.BlockSpec((wdt, D), lambda i: (0, 0)),
                  vec, vec],
        out_specs=pl.BlockSpec((tm, D), lambda i: (i, 0)),
        out_shape=jax.ShapeDtypeStruct((T, D), F32),
        compiler_params=_cparams(("parallel",)),
        name="out_proj_ln",
    )(a, b, x, wa, wb, g, bb)


def _ffn_kernel(x_ref, wg_ref, wu_ref, wd_ref, g_ref, bb_ref, o_ref, acc_ref, xb_ref, wg_s, wu_s, wd_s, *, alpha):
    i = pl.program_id(0)
    j = pl.program_id(1)

    @pl.when(i == 0)
    def _():
        wg_s[j] = wg_ref[...].astype(BF16)
        wu_s[j] = wu_ref[...].astype(BF16)
        wd_s[j] = wd_ref[...].astype(BF16)

    @pl.when(j == 0)
    def _():
        xb_ref[...] = x_ref[...].astype(BF16)
        acc_ref[...] = jnp.zeros_like(acc_ref)

    xb = xb_ref[...]
    hg = jnp.dot(xb, wg_s[j], preferred_element_type=F32)
    hu = jnp.dot(xb, wu_s[j], preferred_element_type=F32)
    h = (hg * _sigmoid(hg) * hu).astype(BF16)
    acc_ref[...] += jnp.dot(h, wd_s[j], preferred_element_type=F32)

    @pl.when(j == pl.num_programs(1) - 1)
    def _():
        o_ref[...] = _ln(alpha * x_ref[...] + acc_ref[...], g_ref[...], bb_ref[...], LN_EPS)


def _ffn(x, wg, wu, wd, layer, g, bb, alpha, tm, tf):
    T, D = x.shape
    Fd = wg.shape[2]
    nj = Fd // tf
    vec = pl.BlockSpec((1, D), lambda i, j: (0, 0))

    def wj(i, j):
        return jnp.where(i == 0, j, nj - 1)

    return pl.pallas_call(
        functools.partial(_ffn_kernel, alpha=alpha),
        grid=(T // tm, nj),
        in_specs=[pl.BlockSpec((tm, D), lambda i, j: (i, 0)),
                  pl.BlockSpec((None, D, tf), lambda i, j: (layer, 0, wj(i, j))),
                  pl.BlockSpec((None, D, tf), lambda i, j: (layer, 0, wj(i, j))),
                  pl.BlockSpec((None, tf, D), lambda i, j: (layer, wj(i, j), 0)),
                  vec, vec],
        out_specs=pl.BlockSpec((tm, D), lambda i, j: (i, 0)),
        out_shape=jax.ShapeDtypeStruct((T, D), F32),
        scratch_shapes=[pltpu.VMEM((tm, D), F32), pltpu.VMEM((tm, D), BF16),
                        pltpu.VMEM((nj, D, tf), BF16), pltpu.VMEM((nj, D, tf), BF16), pltpu.VMEM((nj, tf, D), BF16)],
        compiler_params=_cparams(("arbitrary", "arbitrary")),
        name="ffn_ln",
    )(x, wg, wu, wd, g, bb)


def _pool_kernel(x_ref, st_ref, w_ref, sc_ref, g_ref, bb_ref, o_ref, nst_ref, ext_ref, *, tl, start, alpha):
    c = pl.program_id(1)

    x = x_ref[...]
    nb, _, D = x.shape
    gw = D // len(POOL_WINDOWS)
    n_ext = POOL_PAD + tl

    @pl.when(c == 0)
    def _():
        ext_ref[:, 0:POOL_PAD - POOL_HIST, :] = jnp.zeros((nb, POOL_PAD - POOL_HIST, D), F32)
        ext_ref[:, POOL_PAD - POOL_HIST:POOL_PAD, :] = st_ref[...]

    ext_ref[:, POOL_PAD:n_ext, :] = x
    pos = start + c * tl + lax.broadcasted_iota(jnp.int32, (1, tl, 1), 1)
    outs = []
    for gi, wdw in enumerate(POOL_WINDOWS):
        lo = gi * gw
        acc = ext_ref[:, 1:n_ext, lo:lo + gw] + ext_ref[:, 0:n_ext - 1, lo:lo + gw]
        h = 2
        while h < wdw:
            rows = acc.shape[1]
            acc = acc[:, h:rows, :] + acc[:, 0:rows - h, :]
            h *= 2
        first = POOL_PAD + 1 - wdw
        ssum = acc[:, first:first + tl, :]
        cnt = jnp.minimum(wdw, pos + 1).astype(F32)
        pooled = ssum / cnt - x[..., lo:lo + gw]
        y = jnp.dot(pooled.reshape(nb * tl, gw).astype(BF16), w_ref[gi], preferred_element_type=F32)
        outs.append(y.reshape(nb, tl, gw))
    y = jnp.concatenate(outs, axis=-1) * sc_ref[...]
    o_ref[...] = _ln(alpha * x + y, g_ref[...], bb_ref[...], LN_EPS)
    tail = ext_ref[:, tl + POOL_PAD - POOL_HIST:tl + POOL_PAD, :]
    ext_ref[:, POOL_PAD - POOL_HIST:POOL_PAD, :] = tail
    nst_ref[...] = tail


def _pool_mix(x, state, w, scale, g, bb, nb, tl, start, alpha):
    n_seq, seq_len, D = x.shape
    vec = pl.BlockSpec((1, D), lambda n, c: (0, 0))
    return pl.pallas_call(
        functools.partial(_pool_kernel, tl=tl, start=start, alpha=alpha),
        grid=(n_seq // nb, seq_len // tl),
        in_specs=[pl.BlockSpec((nb, tl, D), lambda n, c: (n, c, 0)),
                  pl.BlockSpec((nb, POOL_HIST, D), lambda n, c: (n, 0, 0)),
                  pl.BlockSpec(w.shape, lambda n, c: (0, 0, 0)),
                  vec, vec, vec],
        out_specs=[pl.BlockSpec((nb, tl, D), lambda n, c: (n, c, 0)),
                   pl.BlockSpec((nb, POOL_HIST, D), lambda n, c: (n, 0, 0))],
        out_shape=[jax.ShapeDtypeStruct((n_seq, seq_len, D), F32),
                   jax.ShapeDtypeStruct((n_seq, POOL_HIST, D), F32)],
        scratch_shapes=[pltpu.VMEM((nb, POOL_PAD + tl, D), F32)],
        compiler_params=_cparams(("parallel", "arbitrary")),
        name="pool_mix_ln",
    )(x, state, w, scale, g, bb)


def _trunk(x3, conv_prev, shift_prev, wkv_prev, pool_prev, start, prm, tiles):
    n_seq, seq_len, D = x3.shape
    T = n_seq * seq_len
    x = x3.reshape(T, D)
    alpha = prm["alpha"]
    tm = tiles["tm"]
    glu, proj_b = _inproj(x, prm["w_in_a"], prm["w_in_b"], tm)
    a_out, new_conv = _conv_branch(glu.reshape(n_seq, seq_len, -1), conv_prev, prm["conv_w"], prm["conv_b"],
                                   prm["conv_ln_g"], prm["conv_ln_b"], tiles["seq_nb"], tiles["conv_tl"])
    b_out, new_shift, new_wkv = _rwkv_branch(proj_b, shift_prev, wkv_prev, prm["rwkv"], n_seq, seq_len,
                                             tiles["rwkv_nsb"], tiles["rwkv_tb"])
    x = _outproj(a_out.reshape(T, -1), b_out, x, prm["w_out_a"], prm["w_out_b"], prm["ln_mix_g"][0],
                 prm["ln_mix_b"][0], alpha, tm)
    x = _ffn(x, prm["ffn_gate"], prm["ffn_up"], prm["ffn_down"], 0, prm["ln_ffn_g"][0], prm["ln_ffn_b"][0],
             alpha, tm, tiles["tf"])
    x, new_pool = _pool_mix(x.reshape(n_seq, seq_len, D), pool_prev, prm["pool_w"], prm["pool_scale"],
                            prm["ln_mix_g"][1], prm["ln_mix_b"][1], tiles["seq_nb"], tiles["pool_tl"], start, alpha)
    x = _ffn(x.reshape(T, D), prm["ffn_gate"], prm["ffn_up"], prm["ffn_down"], 1, prm["ln_ffn_g"][1],
             prm["ln_ffn_b"][1], alpha, tm, tiles["tf"])
    return (x.reshape(n_seq, seq_len, D), new_conv[None], new_shift.reshape(1, n_seq, -1), new_wkv[None],
            new_pool[None])


def kernel(x_prompt, x_sample, state_conv, state_shift, state_wkv, state_pool, w_in, conv_w, conv_b, conv_ln_g, conv_ln_b, rwkv_mu, rwkv_w0, rwkv_w2, rwkv_a0, rwkv_a2, rwkv_g2, rwkv_kk, rwkv_ka, rwkv_rk, rwkv_lnx_g, rwkv_lnx_b, w_out, pool_w, pool_scale, ln_mix_g, ln_mix_b, ffn_gate, ffn_up, ffn_down, ln_ffn_g, ln_ffn_b):
    depth = ln_mix_g.shape[0]
    assert depth == 2 and w_in.shape[0] == 1 and pool_w.shape[0] == 1
    B, L, D = x_prompt.shape
    Bs, Ls, _ = x_sample.shape
    a_w = conv_w.shape[2]
    b_w = rwkv_w0.shape[1]
    n_heads = b_w // HEAD
    alpha = float((2 * depth) ** 0.25)

    row = lambda a: a.reshape(1, -1)
    prm = dict(
        alpha=alpha,
        w_in_a=w_in[0, :, :2 * a_w].astype(BF16), w_in_b=w_in[0, :, 2 * a_w:].astype(BF16),
        conv_w=conv_w[0], conv_b=row(conv_b[0]), conv_ln_g=row(conv_ln_g[0]), conv_ln_b=row(conv_ln_b[0]),
        rwkv=dict(mu=row(rwkv_mu[0]), w0=row(rwkv_w0[0]),
                  w2a2=jnp.concatenate([rwkv_w2[0], rwkv_a2[0]], axis=0), a0=row(rwkv_a0[0]), g2=rwkv_g2[0],
                  kk=row(rwkv_kk[0]), ka=row(rwkv_ka[0]), rk=row(rwkv_rk[0]),
                  lng=row(rwkv_lnx_g[0]), lnb=row(rwkv_lnx_b[0])),
        w_out_a=w_out[0, :a_w].astype(BF16), w_out_b=w_out[0, a_w:].astype(BF16),
        pool_w=pool_w[0].astype(BF16), pool_scale=row(pool_scale[0]),
        ln_mix_g=[row(ln_mix_g[i]) for i in range(depth)], ln_mix_b=[row(ln_mix_b[i]) for i in range(depth)],
        ffn_gate=ffn_gate, ffn_up=ffn_up, ffn_down=ffn_down,
        ln_ffn_g=[row(ln_ffn_g[i]) for i in range(depth)], ln_ffn_b=[row(ln_ffn_b[i]) for i in range(depth)],
    )
    dt = x_prompt.dtype
    z_conv = jnp.zeros((B, CONV_HIST, a_w), dt)
    z_shift = jnp.zeros((B, state_shift.shape[-1]), dt)
    z_wkv = jnp.zeros((B, n_heads, HEAD, HEAD), state_wkv.dtype)
    z_pool = jnp.zeros((B, POOL_HIST, D), dt)
    tiles_p = dict(tm=1024, tf=256, seq_nb=1, conv_tl=512, pool_tl=1024, rwkv_nsb=1, rwkv_tb=8 * CHUNK)
    tiles_s = dict(tm=1024, tf=256, seq_nb=32, conv_tl=Ls, pool_tl=Ls, rwkv_nsb=4 * CHUNK // Ls, rwkv_tb=Ls)
    yp = _trunk(x_prompt, z_conv, z_shift, z_wkv, z_pool, 0, prm, tiles_p)
    ys = _trunk(x_sample, state_conv[0], state_shift[0], state_wkv[0], state_pool[0], PAST_LEN, prm, tiles_s)
    return (yp[0], ys[0], yp[1], yp[2], yp[3], yp[4], ys[1], ys[2], ys[3], ys[4])
```

```python
import functools

import jax
import jax.numpy as jnp
from jax import lax
from jax.experimental import pallas as pl
from jax.experimental.pallas import tpu as pltpu

F32 = jnp.float32
BF16 = jnp.bfloat16

HEAD = 64
PAIR = 2 * HEAD
CHUNK = 64
SOLVE_BLOCK = 16
SUBLANES = 8
CONV_TAPS = 31
CONV_HIST = CONV_TAPS - 1
CONV_PAD = 32
POOL_WINDOWS = (2, 4, 8, 16)
POOL_HIST = max(POOL_WINDOWS) - 1
POOL_PAD = 16
PAST_LEN = 16384
LN_EPS = 1e-5
GN_EPS = 64e-5
DECAY_SCALE = 0.6065306597126334
KK_NORM_FLOOR = 1e-12
VMEM_LIMIT = 56 * 1024 * 1024


def _ln(z, g, b, eps):
    mu = jnp.mean(z, axis=-1, keepdims=True)
    zc = z - mu
    var = jnp.mean(zc * zc, axis=-1, keepdims=True)
    return zc * lax.rsqrt(var + eps) * g + b


def _sigmoid(x):
    return 1.0 / (1.0 + jnp.exp(-x))


def _cparams(sem):
    return pltpu.CompilerParams(dimension_semantics=sem, vmem_limit_bytes=VMEM_LIMIT)


def _log2(n):
    assert n & (n - 1) == 0
    return n.bit_length() - 1


def _inproj_kernel(x_ref, wa_ref, wb_ref, oa_ref, ob_ref):
    xb = x_ref[...].astype(BF16)
    pa = jnp.dot(xb, wa_ref[...], preferred_element_type=F32)
    width = pa.shape[1] // 2
    oa_ref[...] = pa[:, :width] * _sigmoid(pa[:, width:])
    ob_ref[...] = jnp.dot(xb, wb_ref[...], preferred_element_type=F32)


def _inproj(x, wa, wb, tm):
    T, D = x.shape
    na, nb = wa.shape[1] // 2, wb.shape[1]
    return pl.pallas_call(
        _inproj_kernel,
        grid=(T // tm,),
        in_specs=[pl.BlockSpec((tm, D), lambda i: (i, 0)),
                  pl.BlockSpec(wa.shape, lambda i: (0, 0)),
                  pl.BlockSpec((D, nb), lambda i: (0, 0))],
        out_specs=[pl.BlockSpec((tm, na), lambda i: (i, 0)),
                   pl.BlockSpec((tm, nb), lambda i: (i, 0))],
        out_shape=[jax.ShapeDtypeStruct((T, na), F32), jax.ShapeDtypeStruct((T, nb), F32)],
        compiler_params=_cparams(("parallel",)),
        name="in_proj",
    )(x, wa, wb)


def _conv_kernel(u_ref, st_ref, w_ref, b_ref, g_ref, bb_ref, o_ref, nst_ref, ext_ref, *, tl, width):
    c = pl.program_id(1)

    base = CONV_PAD - CONV_HIST

    @pl.when(c == 0)
    def _():
        ext_ref[:, base:CONV_PAD, :] = st_ref[...]
        ext_ref[:, CONV_PAD + tl:CONV_PAD + tl + SUBLANES, :] = jnp.zeros((u_ref.shape[0], SUBLANES, width), F32)

    ext_ref[:, CONV_PAD:CONV_PAD + tl, :] = u_ref[...]
    acc = None
    for rho in range(SUBLANES):
        part = None
        for m in range((base + CONV_TAPS) // SUBLANES + 1):
            j = SUBLANES * m + rho - base
            if 0 <= j < CONV_TAPS:
                term = ext_ref[:, SUBLANES * m:SUBLANES * m + tl + SUBLANES, :] * w_ref[j:j + 1, :]
                part = term if part is None else part + term
        shifted = part[:, rho:rho + tl, :]
        acc = shifted if acc is None else acc + shifted
    acc = acc + b_ref[...]
    y = _ln(acc, g_ref[...], bb_ref[...], LN_EPS)
    o_ref[...] = (y * _sigmoid(y)).astype(o_ref.dtype)
    tail = ext_ref[:, tl + CONV_PAD - CONV_HIST:tl + CONV_PAD, :]
    ext_ref[:, CONV_PAD - CONV_HIST:CONV_PAD, :] = tail
    nst_ref[...] = tail


def _conv_branch(glu, state, w, b, g, bb, nb, tl):
    n_seq, seq_len, width = glu.shape
    vec = pl.BlockSpec((1, width), lambda n, c: (0, 0))
    return pl.pallas_call(
        functools.partial(_conv_kernel, tl=tl, width=width),
        grid=(n_seq // nb, seq_len // tl),
        in_specs=[pl.BlockSpec((nb, tl, width), lambda n, c: (n, c, 0)),
                  pl.BlockSpec((nb, CONV_HIST, width), lambda n, c: (n, 0, 0)),
                  pl.BlockSpec((CONV_TAPS, width), lambda n, c: (0, 0)),
                  vec, vec, vec],
        out_specs=[pl.BlockSpec((nb, tl, width), lambda n, c: (n, c, 0)),
                   pl.BlockSpec((nb, CONV_HIST, width), lambda n, c: (n, 0, 0))],
        out_shape=[jax.ShapeDtypeStruct((n_seq, seq_len, width), BF16 if tl % (2 * SUBLANES) == 0 else F32),
                   jax.ShapeDtypeStruct((n_seq, CONV_HIST, width), F32)],
        scratch_shapes=[pltpu.VMEM((nb, CONV_PAD + tl + SUBLANES, width), F32)],
        compiler_params=_cparams(("parallel", "arbitrary")),
        name="conv_branch",
    )(glu, state, w, b, g, bb)


def _dotb(a, b):
    return jnp.dot(a, b, preferred_element_type=F32)


def _dotb_nt(a, b):
    return lax.dot_general(a, b, (((1,), (1,)), ((), ())), preferred_element_type=F32)


def _b(x):
    return x.astype(BF16)


def _rwkv_kernel(p_ref, sh_ref, wkv_ref, mu_ref, w0_ref, w2a2_ref, a0_ref, g2_ref, kk_ref, ka_ref, rk_ref,
                 lng_ref, lnb_ref, o_ref, nsh_ref, nwkv_ref, s_scr, prev_scr, *, nsb, tb, bw, lw):
    c = pl.program_id(1)
    n_pairs = bw // PAIR
    C = CHUNK
    R = nsb * tb
    nq = R // C
    ts = min(tb, C)
    ns = C // ts
    assert HEAD == C and nq * C == R and (nsb == 1 or tb == ts)
    ts_bits = _log2(ts)
    sb = min(SOLVE_BLOCK, ts)
    sb_bits = _log2(sb)

    @pl.when(c == 0)
    def _():
        prev_scr[...] = sh_ref[...]
        s_scr[...] = jnp.zeros_like(s_scr)
        for j in range(nsb):
            for h in range(2 * n_pairs):
                lo = (h % 2) * HEAD
                s_scr[j, h // 2, lo:lo + HEAD, lo:lo + HEAD] = wkv_ref[j, h]

    p = p_ref[...]
    pw = p.shape[1]
    row = lax.broadcasted_iota(jnp.int32, (R, 1), 0)
    shifted = pltpu.roll(p, 1, 0)
    if nsb == 1:
        xm = p + mu_ref[...] * (shifted - p)
        head = p[0:SUBLANES]
        prev_h = jnp.where(row[0:SUBLANES] == 0, prev_scr[0], shifted[0:SUBLANES])
        xm = jnp.concatenate([head + mu_ref[...] * (prev_h - head), xm[SUBLANES:]], axis=0)
    else:
        first = jnp.concatenate([jnp.broadcast_to(prev_scr[j], (tb, pw)) for j in range(nsb)], axis=0)
        prev = jnp.where((row & (tb - 1)) == 0, first, shifted)
        xm = p + mu_ref[...] * (prev - p)
    for j in range(nsb):
        last = p[(j + 1) * tb - 1:(j + 1) * tb, :]
        prev_scr[j] = last
        nsh_ref[j] = last
    r = xm[:, 0:bw]
    k = xm[:, bw:2 * bw]
    v = xm[:, 2 * bw:3 * bw]
    dwa = xm[:, 3 * bw:3 * bw + 2 * lw]
    dg = xm[:, 3 * bw + 2 * lw:]
    lane_l = lax.broadcasted_iota(jnp.int32, (1, 2 * lw), 1)
    is_w = lane_l < lw
    w2a2 = _b(w2a2_ref[...])
    wpre = w0_ref[...] + _dotb(_b(jnp.where(is_w, jnp.tanh(dwa), 0.0)), w2a2)
    apre = a0_ref[...] + _dotb(_b(jnp.where(is_w, 0.0, dwa)), w2a2)
    logw = -DECAY_SCALE * _sigmoid(wpre)
    a_lr = _sigmoid(apre)
    gate = _dotb(_b(_sigmoid(dg)), _b(g2_ref[...]))
    kkv = k * kk_ref[...]
    k2 = k * (1.0 + (a_lr - 1.0) * ka_ref[...])
    rk2 = r * k2 * rk_ref[...]

    lane = lax.broadcasted_iota(jnp.int32, (1, PAIR), 1)
    m_l = lane < HEAD
    rowp = lax.broadcasted_iota(jnp.int32, (PAIR, PAIR), 0)
    colp = lax.broadcasted_iota(jnp.int32, (PAIR, PAIR), 1)
    bd = (rowp >= HEAD) == (colp >= HEAD)
    t_i = lax.broadcasted_iota(jnp.int32, (C, PAIR), 0)
    s_i = lax.broadcasted_iota(jnp.int32, (C, PAIR), 1) & (HEAD - 1)
    same = (s_i >> ts_bits) == (t_i >> ts_bits)
    strict = same & (s_i < t_i)
    incl = same & (s_i <= t_i)
    eye = (s_i == t_i).astype(F32)
    dblk = (s_i >> sb_bits) == (t_i >> sb_bits)

    def vs(y):
        return jnp.concatenate([jnp.where(m_l, y, 0.0), jnp.where(m_l, 0.0, y)], axis=0)

    def pm(a, y):
        return _dotb(_b(a), vs(_b(y)))

    def pmm(lhs, y):
        out = _dotb(_b(jnp.concatenate(lhs, axis=0)), vs(_b(y)))
        return [out[k * C:(k + 1) * C] for k in range(len(lhs))]

    def bdsum(xs):
        out = []
        for x in xs:
            s0 = jnp.sum(jnp.where(m_l, x, 0.0), axis=1, keepdims=True)
            s1 = jnp.sum(jnp.where(m_l, 0.0, x), axis=1, keepdims=True)
            out.append(jnp.where(m_l, s0, s1))
        return out

    P = range(n_pairs)
    units = [(q, i) for q in range(nq) for i in P]
    NU = range(len(units))

    def blk(x, n):
        q, i = units[n]
        return x[q * C:(q + 1) * C, i * PAIR:(i + 1) * PAIR]

    def neumann(m, n_terms):
        t = [eye + m[n] for n in NU]
        levels = _log2(n_terms)
        if levels == 1:
            return t
        p = [pm(m[n], m[n]) for n in NU]
        for _ in range(levels - 2):
            both = [pmm([p[n], t[n]], p[n]) for n in NU]
            t = [t[n] + both[n][1] for n in NU]
            p = [both[n][0] for n in NU]
        return [t[n] + pm(t[n], p[n]) for n in NU]

    t_c = lax.broadcasted_iota(jnp.int32, (C, C), 0)
    s_c = lax.broadcasted_iota(jnp.int32, (C, C), 1)
    tri_b = (((s_c >> ts_bits) == (t_c >> ts_bits)) & (s_c <= t_c)).astype(BF16)
    tri2 = jnp.concatenate([tri_b, tri_b], axis=1)
    lw_hi = _b(logw)
    lw_lo = _b(logw - lw_hi.astype(F32))
    cs = jnp.concatenate(
        [_dotb(tri2, jnp.concatenate([lw_hi[q * C:(q + 1) * C], lw_lo[q * C:(q + 1) * C]], axis=0))
         for q in range(nq)], axis=0)
    g_in = jnp.exp(cs)
    g_ex = jnp.exp(cs - logw)
    g_inv = jnp.exp(-cs)
    runs = g_in.reshape(R // ts, ts, bw)
    g_end = jnp.broadcast_to(runs[:, ts - 1:ts, :], runs.shape).reshape(R, bw)

    sums = bdsum([blk(kkv, n) * blk(kkv, n) for n in NU] + [blk(rk2, n) for n in NU])
    kk_n = [blk(kkv, n) * lax.rsqrt(jnp.maximum(sums[n], KK_NORM_FLOOR * KK_NORM_FLOOR)) for n in NU]
    v_p = [blk(v, n) for n in NU]
    at = [-kk_n[n] * blk(g_ex, n) for n in NU]
    rt = [blk(r, n) * blk(g_in, n) for n in NU]
    bt = [kk_n[n] * blk(a_lr, n) * blk(g_inv, n) for n in NU]
    kt = [blk(k2, n) * blk(g_inv, n) for n in NU]
    sm = [_dotb_nt(_b(jnp.concatenate([at[n], rt[n]], axis=0)),
                   jnp.concatenate([vs(_b(bt[n])), vs(_b(kt[n]))], axis=0)) for n in NU]
    lab = [jnp.where(strict, sm[n][0:C, 0:PAIR], 0.0) for n in NU]
    lak = [jnp.where(strict, sm[n][0:C, PAIR:2 * PAIR], 0.0) for n in NU]
    mrb = [jnp.where(incl, sm[n][C:2 * C, 0:PAIR], 0.0) for n in NU]
    mrk = [jnp.where(incl, sm[n][C:2 * C, PAIR:2 * PAIR], 0.0) for n in NU]
    d1 = [jnp.where(dblk, lab[n], 0.0) for n in NU]
    tinv = neumann(d1, sb)
    if sb < ts:
        fp = [pm(lab[n] - d1[n], tinv[n]) for n in NU]
        levels = _log2(ts // sb)
        for lev in range(levels):
            if lev < levels - 1:
                both = [pmm([fp[n], tinv[n]], fp[n]) for n in NU]
                tinv = [tinv[n] + both[n][1] for n in NU]
                fp = [both[n][0] for n in NU]
            else:
                tinv = [tinv[n] + pm(tinv[n], fp[n]) for n in NU]
    tinv = [_b(tinv[n]) for n in NU]
    xv = [pmm([lak[n], mrk[n]], v_p[n]) for n in NU]
    x0 = [xv[n][0] for n in NU]
    s_cur = [[s_scr[sq, i] for i in P] for sq in range(nsb)]
    col_seq = (lax.broadcasted_iota(jnp.int32, (1, PAIR), 1) & (HEAD - 1)) >> ts_bits

    def finish(q, o_q):
        un = [q * n_pairs + i for i in P]
        om = bdsum(o_q)
        oc = [o_q[i] - om[i] * (1.0 / HEAD) for i in P]
        ov = bdsum([oc[i] * oc[i] for i in P])
        for i in P:
            n = un[i]
            ln = slice(i * PAIR, (i + 1) * PAIR)
            bonus = sums[len(units) + n] * v_p[n]
            y = oc[i] * lax.rsqrt(ov[i] * (1.0 / HEAD) + GN_EPS) * lng_ref[:, ln] + lnb_ref[:, ln] + bonus
            o_ref[q * C:(q + 1) * C, ln] = (y * blk(gate, n)).astype(o_ref.dtype)

    o_prev = None
    for q in range(nq):
        un = [q * n_pairs + i for i in P]
        seq = [q * ns + j if nsb > 1 else 0 for j in range(ns)]
        a_s, r_s = [], []
        for i in P:
            parts = []
            for j in range(ns):
                rows = slice(j * ts, (j + 1) * ts)
                lhs = _b(jnp.concatenate([at[un[i]][rows], rt[un[i]][rows]], axis=0))
                parts.append(_dotb_nt(lhs, _b(s_cur[seq[j]][i])))
            a_s.append(jnp.concatenate([x[0:ts] for x in parts], axis=0))
            r_s.append(jnp.concatenate([x[ts:2 * ts] for x in parts], axis=0))
        if o_prev is not None:
            finish(q - 1, o_prev)
        u = [_dotb(tinv[un[i]], vs(_b(a_s[i] + x0[un[i]]))) for i in P]
        for i in P:
            n = un[i]
            ge = blk(g_end, n)
            uv = _b(jnp.concatenate([u[i], v_p[n]], axis=0).T)
            bk = _b(jnp.concatenate([bt[n] * ge, kt[n] * ge], axis=0))
            for j in range(ns):
                uv_j = uv if ns == 1 else jnp.where(col_seq == j, uv, 0.0)
                upd = _dotb(uv_j, bk)
                s_cur[seq[j]][i] = s_cur[seq[j]][i] * ge[j * ts:j * ts + 1, :] + jnp.where(bd, upd, 0.0)
        o_prev = [r_s[i] + pm(mrb[un[i]], u[i]) + xv[un[i]][1] for i in P]
    for sq in range(nsb):
        for i in P:
            s_scr[sq, i] = s_cur[sq][i]
    finish(nq - 1, o_prev)

    @pl.when(c == pl.num_programs(1) - 1)
    def _():
        for j in range(nsb):
            for h in range(2 * n_pairs):
                lo = (h % 2) * HEAD
                nwkv_ref[j, h] = s_scr[j, h // 2, lo:lo + HEAD, lo:lo + HEAD]


def _rwkv_branch(proj_b, shift, wkv, prm, n_seq, seq_len, nsb, tb):
    T, pw = proj_b.shape
    n_heads = wkv.shape[1]
    bw = n_heads * HEAD
    lw = prm["w2a2"].shape[0] // 2
    nc = seq_len // tb
    assert nsb == 1 or nc == 1
    rows = nsb * tb

    def full(a):
        return pl.BlockSpec(a.shape, lambda n, c: (0,) * a.ndim)

    names = ("mu", "w0", "w2a2", "a0", "g2", "kk", "ka", "rk", "lng", "lnb")
    return pl.pallas_call(
        functools.partial(_rwkv_kernel, nsb=nsb, tb=tb, bw=bw, lw=lw),
        grid=(n_seq // nsb, nc),
        in_specs=[pl.BlockSpec((rows, pw), lambda n, c: (n * nc + c, 0)),
                  pl.BlockSpec((nsb, 1, pw), lambda n, c: (n, 0, 0)),
                  pl.BlockSpec((nsb, n_heads, HEAD, HEAD), lambda n, c: (n, 0, 0, 0))]
                 + [full(prm[k]) for k in names],
        out_specs=[pl.BlockSpec((rows, bw), lambda n, c: (n * nc + c, 0)),
                   pl.BlockSpec((nsb, 1, pw), lambda n, c: (n, 0, 0)),
                   pl.BlockSpec((nsb, n_heads, HEAD, HEAD), lambda n, c: (n, 0, 0, 0))],
        out_shape=[jax.ShapeDtypeStruct((T, bw), BF16),
                   jax.ShapeDtypeStruct((n_seq, 1, pw), F32),
                   jax.ShapeDtypeStruct((n_seq, n_heads, HEAD, HEAD), F32)],
        scratch_shapes=[pltpu.VMEM((nsb, n_heads // 2, PAIR, PAIR), F32), pltpu.VMEM((nsb, 1, pw), F32)],
        compiler_params=_cparams(("parallel", "arbitrary")),
        name="rwkv_branch",
    )(proj_b, shift.reshape(n_seq, 1, pw), wkv, *[prm[k] for k in names])


def _outproj_kernel(a_ref, b_ref, x_ref, wa_ref, wb_ref, g_ref, bb_ref, o_ref, *, alpha):
    mix = jnp.dot(a_ref[...].astype(BF16), wa_ref[...], preferred_element_type=F32)
    mix = mix + jnp.dot(b_ref[...].astype(BF16), wb_ref[...], preferred_element_type=F32)
    o_ref[...] = _ln(alpha * x_ref[...] + mix, g_ref[...], bb_ref[...], LN_EPS)


def _outproj(a, b, x, wa, wb, g, bb, alpha, tm):
    T, D = x.shape
    wdt = a.shape[1]
    vec = pl.BlockSpec((1, D), lambda i: (0, 0))
    return pl.pallas_call(
        functools.partial(_outproj_kernel, alpha=alpha),
        grid=(T // tm,),
        in_specs=[pl.BlockSpec((tm, wdt), lambda i: (i, 0)),
                  pl.BlockSpec((tm, wdt), lambda i: (i, 0)),
                  pl.BlockSpec((tm, D), lambda i: (i, 0)),
                  pl.BlockSpec((wdt, D), lambda i: (0, 0)),
                  pl.BlockSpec((wdt, D), lambda i: (0, 0)),
                  vec, vec],
        out_specs=pl.BlockSpec((tm, D), lambda i: (i, 0)),
        out_shape=jax.ShapeDtypeStruct((T, D), F32),
        compiler_params=_cparams(("parallel",)),
        name="out_proj_ln",
    )(a, b, x, wa, wb, g, bb)


def _ffn_kernel(x_ref, wg_ref, wu_ref, wd_ref, g_ref, bb_ref, o_ref, acc_ref, xb_ref, wg_s, wu_s, wd_s, *, alpha):
    i = pl.program_id(0)
    j = pl.program_id(1)

    @pl.when(i == 0)
    def _():
        wg_s[j] = wg_ref[...].astype(BF16)
        wu_s[j] = wu_ref[...].astype(BF16)
        wd_s[j] = wd_ref[...].astype(BF16)

    @pl.when(j == 0)
    def _():
        xb_ref[...] = x_ref[...].astype(BF16)
        acc_ref[...] = jnp.zeros_like(acc_ref)

    xb = xb_ref[...]
    hg = jnp.dot(xb, wg_s[j], preferred_element_type=F32)
    hu = jnp.dot(xb, wu_s[j], preferred_element_type=F32)
    h = (hg * _sigmoid(hg) * hu).astype(BF16)
    acc_ref[...] += jnp.dot(h, wd_s[j], preferred_element_type=F32)

    @pl.when(j == pl.num_programs(1) - 1)
    def _():
        o_ref[...] = _ln(alpha * x_ref[...] + acc_ref[...], g_ref[...], bb_ref[...], LN_EPS)


def _ffn(x, wg, wu, wd, layer, g, bb, alpha, tm, tf):
    T, D = x.shape
    Fd = wg.shape[2]
    nj = Fd // tf
    vec = pl.BlockSpec((1, D), lambda i, j: (0, 0))

    def wj(i, j):
        return jnp.where(i == 0, j, nj - 1)

    return pl.pallas_call(
        functools.partial(_ffn_kernel, alpha=alpha),
        grid=(T // tm, nj),
        in_specs=[pl.BlockSpec((tm, D), lambda i, j: (i, 0)),
                  pl.BlockSpec((None, D, tf), lambda i, j: (layer, 0, wj(i, j))),
                  pl.BlockSpec((None, D, tf), lambda i, j: (layer, 0, wj(i, j))),
                  pl.BlockSpec((None, tf, D), lambda i, j: (layer, wj(i, j), 0)),
                  vec, vec],
        out_specs=pl.BlockSpec((tm, D), lambda i, j: (i, 0)),
        out_shape=jax.ShapeDtypeStruct((T, D), F32),
        scratch_shapes=[pltpu.VMEM((tm, D), F32), pltpu.VMEM((tm, D), BF16),
                        pltpu.VMEM((nj, D, tf), BF16), pltpu.VMEM((nj, D, tf), BF16), pltpu.VMEM((nj, tf, D), BF16)],
        compiler_params=_cparams(("arbitrary", "arbitrary")),
        name="ffn_ln",
    )(x, wg, wu, wd, g, bb)


def _pool_kernel(x_ref, st_ref, w_ref, sc_ref, g_ref, bb_ref, o_ref, nst_ref, ext_ref, *, tl, start, alpha):
    c = pl.program_id(1)

    x = x_ref[...]
    nb, _, D = x.shape
    gw = D // len(POOL_WINDOWS)
    n_ext = POOL_PAD + tl

    @pl.when(c == 0)
    def _():
        ext_ref[:, 0:POOL_PAD - POOL_HIST, :] = jnp.zeros((nb, POOL_PAD - POOL_HIST, D), F32)
        ext_ref[:, POOL_PAD - POOL_HIST:POOL_PAD, :] = st_ref[...]

    ext_ref[:, POOL_PAD:n_ext, :] = x
    pos = start + c * tl + lax.broadcasted_iota(jnp.int32, (1, tl, 1), 1)
    outs = []
    for gi, wdw in enumerate(POOL_WINDOWS):
        lo = gi * gw
        acc = ext_ref[:, 1:n_ext, lo:lo + gw] + ext_ref[:, 0:n_ext - 1, lo:lo + gw]
        h = 2
        while h < wdw:
            rows = acc.shape[1]
            acc = acc[:, h:rows, :] + acc[:, 0:rows - h, :]
            h *= 2
        first = POOL_PAD + 1 - wdw
        ssum = acc[:, first:first + tl, :]
        cnt = jnp.minimum(wdw, pos + 1).astype(F32)
        pooled = ssum / cnt - x[..., lo:lo + gw]
        y = jnp.dot(pooled.reshape(nb * tl, gw).astype(BF16), w_ref[gi], preferred_element_type=F32)
        outs.append(y.reshape(nb, tl, gw))
    y = jnp.concatenate(outs, axis=-1) * sc_ref[...]
    o_ref[...] = _ln(alpha * x + y, g_ref[...], bb_ref[...], LN_EPS)
    tail = ext_ref[:, tl + POOL_PAD - POOL_HIST:tl + POOL_PAD, :]
    ext_ref[:, POOL_PAD - POOL_HIST:POOL_PAD, :] = tail
    nst_ref[...] = tail


def _pool_mix(x, state, w, scale, g, bb, nb, tl, start, alpha):
    n_seq, seq_len, D = x.shape
    vec = pl.BlockSpec((1, D), lambda n, c: (0, 0))
    return pl.pallas_call(
        functools.partial(_pool_kernel, tl=tl, start=start, alpha=alpha),
        grid=(n_seq // nb, seq_len // tl),
        in_specs=[pl.BlockSpec((nb, tl, D), lambda n, c: (n, c, 0)),
                  pl.BlockSpec((nb, POOL_HIST, D), lambda n, c: (n, 0, 0)),
                  pl.BlockSpec(w.shape, lambda n, c: (0, 0, 0)),
                  vec, vec, vec],
        out_specs=[pl.BlockSpec((nb, tl, D), lambda n, c: (n, c, 0)),
                   pl.BlockSpec((nb, POOL_HIST, D), lambda n, c: (n, 0, 0))],
        out_shape=[jax.ShapeDtypeStruct((n_seq, seq_len, D), F32),
                   jax.ShapeDtypeStruct((n_seq, POOL_HIST, D), F32)],
        scratch_shapes=[pltpu.VMEM((nb, POOL_PAD + tl, D), F32)],
        compiler_params=_cparams(("parallel", "arbitrary")),
        name="pool_mix_ln",
    )(x, state, w, scale, g, bb)


def _trunk(x3, conv_prev, shift_prev, wkv_prev, pool_prev, start, prm, tiles):
    n_seq, seq_len, D = x3.shape
    T = n_seq * seq_len
    x = x3.reshape(T, D)
    alpha = prm["alpha"]
    tm = tiles["tm"]
    glu, proj_b = _inproj(x, prm["w_in_a"], prm["w_in_b"], tm)
    a_out, new_conv = _conv_branch(glu.reshape(n_seq, seq_len, -1), conv_prev, prm["conv_w"], prm["conv_b"],
                                   prm["conv_ln_g"], prm["conv_ln_b"], tiles["seq_nb"], tiles["conv_tl"])
    b_out, new_shift, new_wkv = _rwkv_branch(proj_b, shift_prev, wkv_prev, prm["rwkv"], n_seq, seq_len,
                                             tiles["rwkv_nsb"], tiles["rwkv_tb"])
    x = _outproj(a_out.reshape(T, -1), b_out, x, prm["w_out_a"], prm["w_out_b"], prm["ln_mix_g"][0],
                 prm["ln_mix_b"][0], alpha, tm)
    x = _ffn(x, prm["ffn_gate"], prm["ffn_up"], prm["ffn_down"], 0, prm["ln_ffn_g"][0], prm["ln_ffn_b"][0],
             alpha, tm, tiles["tf"])
    x, new_pool = _pool_mix(x.reshape(n_seq, seq_len, D), pool_prev, prm["pool_w"], prm["pool_scale"],
                            prm["ln_mix_g"][1], prm["ln_mix_b"][1], tiles["seq_nb"], tiles["pool_tl"], start, alpha)
    x = _ffn(x.reshape(T, D), prm["ffn_gate"], prm["ffn_up"], prm["ffn_down"], 1, prm["ln_ffn_g"][1],
             prm["ln_ffn_b"][1], alpha, tm, tiles["tf"])
    return (x.reshape(n_seq, seq_len, D), new_conv[None], new_shift.reshape(1, n_seq, -1), new_wkv[None],
            new_pool[None])


def kernel(x_prompt, x_sample, state_conv, state_shift, state_wkv, state_pool, w_in, conv_w, conv_b, conv_ln_g, conv_ln_b, rwkv_mu, rwkv_w0, rwkv_w2, rwkv_a0, rwkv_a2, rwkv_g2, rwkv_kk, rwkv_ka, rwkv_rk, rwkv_lnx_g, rwkv_lnx_b, w_out, pool_w, pool_scale, ln_mix_g, ln_mix_b, ffn_gate, ffn_up, ffn_down, ln_ffn_g, ln_ffn_b):
    depth = ln_mix_g.shape[0]
    assert depth == 2 and w_in.shape[0] == 1 and pool_w.shape[0] == 1
    B, L, D = x_prompt.shape
    Bs, Ls, _ = x_sample.shape
    a_w = conv_w.shape[2]
    b_w = rwkv_w0.shape[1]
    n_heads = b_w // HEAD
    alpha = float((2 * depth) ** 0.25)

    row = lambda a: a.reshape(1, -1)
    prm = dict(
        alpha=alpha,
        w_in_a=w_in[0, :, :2 * a_w].astype(BF16), w_in_b=w_in[0, :, 2 * a_w:].astype(BF16),
        conv_w=conv_w[0], conv_b=row(conv_b[0]), conv_ln_g=row(conv_ln_g[0]), conv_ln_b=row(conv_ln_b[0]),
        rwkv=dict(mu=row(rwkv_mu[0]), w0=row(rwkv_w0[0]),
                  w2a2=jnp.concatenate([rwkv_w2[0], rwkv_a2[0]], axis=0), a0=row(rwkv_a0[0]), g2=rwkv_g2[0],
                  kk=row(rwkv_kk[0]), ka=row(rwkv_ka[0]), rk=row(rwkv_rk[0]),
                  lng=row(rwkv_lnx_g[0]), lnb=row(rwkv_lnx_b[0])),
        w_out_a=w_out[0, :a_w].astype(BF16), w_out_b=w_out[0, a_w:].astype(BF16),
        pool_w=pool_w[0].astype(BF16), pool_scale=row(pool_scale[0]),
        ln_mix_g=[row(ln_mix_g[i]) for i in range(depth)], ln_mix_b=[row(ln_mix_b[i]) for i in range(depth)],
        ffn_gate=ffn_gate, ffn_up=ffn_up, ffn_down=ffn_down,
        ln_ffn_g=[row(ln_ffn_g[i]) for i in range(depth)], ln_ffn_b=[row(ln_ffn_b[i]) for i in range(depth)],
    )
    dt = x_prompt.dtype
    z_conv = jnp.zeros((B, CONV_HIST, a_w), dt)
    z_shift = jnp.zeros((B, state_shift.shape[-1]), dt)
    z_wkv = jnp.zeros((B, n_heads, HEAD, HEAD), state_wkv.dtype)
    z_pool = jnp.zeros((B, POOL_HIST, D), dt)
    tiles_p = dict(tm=1024, tf=256, seq_nb=1, conv_tl=512, pool_tl=1024, rwkv_nsb=1, rwkv_tb=8 * CHUNK)
    tiles_s = dict(tm=1024, tf=256, seq_nb=32, conv_tl=Ls, pool_tl=Ls, rwkv_nsb=4 * CHUNK // Ls, rwkv_tb=Ls)
    yp = _trunk(x_prompt, z_conv, z_shift, z_wkv, z_pool, 0, prm, tiles_p)
    ys = _trunk(x_sample, state_conv[0], state_shift[0], state_wkv[0], state_pool[0], PAST_LEN, prm, tiles_s)
    return (yp[0], ys[0], yp[1], yp[2], yp[3], yp[4], ys[1], ys[2], ys[3], ys[4])
```

```python
import functools

import jax
import jax.numpy as jnp
from jax import lax
from jax.experimental import pallas as pl
from jax.experimental.pallas import tpu as pltpu

F32 = jnp.float32
BF16 = jnp.bfloat16

HEAD = 64
PAIR = 2 * HEAD
CHUNK = 64
SOLVE_BLOCK = 16
SUBLANES = 8
CONV_TAPS = 31
CONV_HIST = CONV_TAPS - 1
CONV_PAD = 32
POOL_WINDOWS = (2, 4, 8, 16)
POOL_HIST = max(POOL_WINDOWS) - 1
POOL_PAD = 16
PAST_LEN = 16384
LN_EPS = 1e-5
GN_EPS = 64e-5
DECAY_SCALE = 0.6065306597126334
KK_NORM_FLOOR = 1e-12
VMEM_LIMIT = 56 * 1024 * 1024


def _ln(z, g, b, eps):
    mu = jnp.mean(z, axis=-1, keepdims=True)
    zc = z - mu
    var = jnp.mean(zc * zc, axis=-1, keepdims=True)
    return zc * lax.rsqrt(var + eps) * g + b


def _sigmoid(x):
    return 1.0 / (1.0 + jnp.exp(-x))


def _cparams(sem):
    return pltpu.CompilerParams(dimension_semantics=sem, vmem_limit_bytes=VMEM_LIMIT)


def _log2(n):
    assert n & (n - 1) == 0
    return n.bit_length() - 1


def _inproj_kernel(x_ref, wa_ref, wb_ref, oa_ref, ob_ref):
    xb = x_ref[...].astype(BF16)
    pa = jnp.dot(xb, wa_ref[...], preferred_element_type=F32)
    width = pa.shape[1] // 2
    oa_ref[...] = pa[:, :width] * _sigmoid(pa[:, width:])
    ob_ref[...] = jnp.dot(xb, wb_ref[...], preferred_element_type=F32)


def _inproj(x, wa, wb, tm):
    T, D = x.shape
    na, nb = wa.shape[1] // 2, wb.shape[1]
    return pl.pallas_call(
        _inproj_kernel,
        grid=(T // tm,),
        in_specs=[pl.BlockSpec((tm, D), lambda i: (i, 0)),
                  pl.BlockSpec(wa.shape, lambda i: (0, 0)),
                  pl.BlockSpec((D, nb), lambda i: (0, 0))],
        out_specs=[pl.BlockSpec((tm, na), lambda i: (i, 0)),
                   pl.BlockSpec((tm, nb), lambda i: (i, 0))],
        out_shape=[jax.ShapeDtypeStruct((T, na), F32), jax.ShapeDtypeStruct((T, nb), F32)],
        compiler_params=_cparams(("parallel",)),
        name="in_proj",
    )(x, wa, wb)


def _conv_kernel(u_ref, st_ref, w_ref, b_ref, g_ref, bb_ref, o_ref, nst_ref, ext_ref, *, tl, width):
    c = pl.program_id(1)

    base = CONV_PAD - CONV_HIST

    @pl.when(c == 0)
    def _():
        ext_ref[:, base:CONV_PAD, :] = st_ref[...]
        ext_ref[:, CONV_PAD + tl:CONV_PAD + tl + SUBLANES, :] = jnp.zeros((u_ref.shape[0], SUBLANES, width), F32)

    ext_ref[:, CONV_PAD:CONV_PAD + tl, :] = u_ref[...]
    acc = None
    for rho in range(SUBLANES):
        part = None
        for m in range((base + CONV_TAPS) // SUBLANES + 1):
            j = SUBLANES * m + rho - base
            if 0 <= j < CONV_TAPS:
                term = ext_ref[:, SUBLANES * m:SUBLANES * m + tl + SUBLANES, :] * w_ref[j:j + 1, :]
                part = term if part is None else part + term
        shifted = part[:, rho:rho + tl, :]
        acc = shifted if acc is None else acc + shifted
    acc = acc + b_ref[...]
    y = _ln(acc, g_ref[...], bb_ref[...], LN_EPS)
    o_ref[...] = (y * _sigmoid(y)).astype(o_ref.dtype)
    tail = ext_ref[:, tl + CONV_PAD - CONV_HIST:tl + CONV_PAD, :]
    ext_ref[:, CONV_PAD - CONV_HIST:CONV_PAD, :] = tail
    nst_ref[...] = tail


def _conv_branch(glu, state, w, b, g, bb, nb, tl):
    n_seq, seq_len, width = glu.shape
    vec = pl.BlockSpec((1, width), lambda n, c: (0, 0))
    return pl.pallas_call(
        functools.partial(_conv_kernel, tl=tl, width=width),
        grid=(n_seq // nb, seq_len // tl),
        in_specs=[pl.BlockSpec((nb, tl, width), lambda n, c: (n, c, 0)),
                  pl.BlockSpec((nb, CONV_HIST, width), lambda n, c: (n, 0, 0)),
                  pl.BlockSpec((CONV_TAPS, width), lambda n, c: (0, 0)),
                  vec, vec, vec],
        out_specs=[pl.BlockSpec((nb, tl, width), lambda n, c: (n, c, 0)),
                   pl.BlockSpec((nb, CONV_HIST, width), lambda n, c: (n, 0, 0))],
        out_shape=[jax.ShapeDtypeStruct((n_seq, seq_len, width), BF16 if tl % (2 * SUBLANES) == 0 else F32),
                   jax.ShapeDtypeStruct((n_seq, CONV_HIST, width), F32)],
        scratch_shapes=[pltpu.VMEM((nb, CONV_PAD + tl + SUBLANES, width), F32)],
        compiler_params=_cparams(("parallel", "arbitrary")),
        name="conv_branch",
    )(glu, state, w, b, g, bb)


def _dotb(a, b):
    return jnp.dot(a, b, preferred_element_type=F32)


def _dotb_nt(a, b):
    return lax.dot_general(a, b, (((1,), (1,)), ((), ())), preferred_element_type=F32)


def _b(x):
    return x.astype(BF16)


def _rwkv_kernel(p_ref, sh_ref, wkv_ref, mu_ref, w0_ref, w2a2_ref, a0_ref, g2_ref, kk_ref, ka_ref, rk_ref,
                 lng_ref, lnb_ref, o_ref, nsh_ref, nwkv_ref, s_scr, prev_scr, *, nsb, tb, bw, lw):
    c = pl.program_id(1)
    n_pairs = bw // PAIR
    C = CHUNK
    R = nsb * tb
    nq = R // C
    ts = min(tb, C)
    ns = C // ts
    assert HEAD == C and nq * C == R and (nsb == 1 or tb == ts)
    ts_bits = _log2(ts)
    sb = min(SOLVE_BLOCK, ts)
    sb_bits = _log2(sb)

    @pl.when(c == 0)
    def _():
        prev_scr[...] = sh_ref[...]
        s_scr[...] = jnp.zeros_like(s_scr)
        for j in range(nsb):
            for h in range(2 * n_pairs):
                lo = (h % 2) * HEAD
                s_scr[j, h // 2, lo:lo + HEAD, lo:lo + HEAD] = wkv_ref[j, h]

    p = p_ref[...]
    pw = p.shape[1]
    row = lax.broadcasted_iota(jnp.int32, (R, 1), 0)
    shifted = pltpu.roll(p, 1, 0)
    if nsb == 1:
        xm = p + mu_ref[...] * (shifted - p)
        head = p[0:SUBLANES]
        prev_h = jnp.where(row[0:SUBLANES] == 0, prev_scr[0], shifted[0:SUBLANES])
        xm = jnp.concatenate([head + mu_ref[...] * (prev_h - head), xm[SUBLANES:]], axis=0)
    else:
        first = jnp.concatenate([jnp.broadcast_to(prev_scr[j], (tb, pw)) for j in range(nsb)], axis=0)
        prev = jnp.where((row & (tb - 1)) == 0, first, shifted)
        xm = p + mu_ref[...] * (prev - p)
    for j in range(nsb):
        last = p[(j + 1) * tb - 1:(j + 1) * tb, :]
        prev_scr[j] = last
        nsh_ref[j] = last
    r = xm[:, 0:bw]
    k = xm[:, bw:2 * bw]
    v = xm[:, 2 * bw:3 * bw]
    dwa = xm[:, 3 * bw:3 * bw + 2 * lw]
    dg = xm[:, 3 * bw + 2 * lw:]
    lane_l = lax.broadcasted_iota(jnp.int32, (1, 2 * lw), 1)
    is_w = lane_l < lw
    w2a2 = _b(w2a2_ref[...])
    wpre = w0_ref[...] + _dotb(_b(jnp.where(is_w, jnp.tanh(dwa), 0.0)), w2a2)
    apre = a0_ref[...] + _dotb(_b(jnp.where(is_w, 0.0, dwa)), w2a2)
    logw = -DECAY_SCALE * _sigmoid(wpre)
    a_lr = _sigmoid(apre)
    gate = _dotb(_b(_sigmoid(dg)), _b(g2_ref[...]))
    kkv = k * kk_ref[...]
    k2 = k * (1.0 + (a_lr - 1.0) * ka_ref[...])
    rk2 = r * k2 * rk_ref[...]

    lane = lax.broadcasted_iota(jnp.int32, (1, PAIR), 1)
    m_l = lane < HEAD
    rowp = lax.broadcasted_iota(jnp.int32, (PAIR, PAIR), 0)
    colp = lax.broadcasted_iota(jnp.int32, (PAIR, PAIR), 1)
    bd = (rowp >= HEAD) == (colp >= HEAD)
    t_i = lax.broadcasted_iota(jnp.int32, (C, PAIR), 0)
    s_i = lax.broadcasted_iota(jnp.int32, (C, PAIR), 1) & (HEAD - 1)
    same = (s_i >> ts_bits) == (t_i >> ts_bits)
    strict = same & (s_i < t_i)
    incl = same & (s_i <= t_i)
    eye = (s_i == t_i).astype(F32)
    dblk = (s_i >> sb_bits) == (t_i >> sb_bits)

    def vs(y):
        return jnp.concatenate([jnp.where(m_l, y, 0.0), jnp.where(m_l, 0.0, y)], axis=0)

    def pm(a, y):
        return _dotb(_b(a), vs(_b(y)))

    def pmm(lhs, y):
        out = _dotb(_b(jnp.concatenate(lhs, axis=0)), vs(_b(y)))
        return [out[k * C:(k + 1) * C] for k in range(len(lhs))]

    def bdsum(xs):
        out = []
        for x in xs:
            s0 = jnp.sum(jnp.where(m_l, x, 0.0), axis=1, keepdims=True)
            s1 = jnp.sum(jnp.where(m_l, 0.0, x), axis=1, keepdims=True)
            out.append(jnp.where(m_l, s0, s1))
        return out

    P = range(n_pairs)
    units = [(q, i) for q in range(nq) for i in P]
    NU = range(len(units))

    def blk(x, n):
        q, i = units[n]
        return x[q * C:(q + 1) * C, i * PAIR:(i + 1) * PAIR]

    def neumann(m, n_terms):
        t = [eye + m[n] for n in NU]
        levels = _log2(n_terms)
        if levels == 1:
            return t
        p = [pm(m[n], m[n]) for n in NU]
        for _ in range(levels - 2):
            both = [pmm([p[n], t[n]], p[n]) for n in NU]
            t = [t[n] + both[n][1] for n in NU]
            p = [both[n][0] for n in NU]
        return [t[n] + pm(t[n], p[n]) for n in NU]

    t_c = lax.broadcasted_iota(jnp.int32, (C, C), 0)
    s_c = lax.broadcasted_iota(jnp.int32, (C, C), 1)
    tri_b = (((s_c >> ts_bits) == (t_c >> ts_bits)) & (s_c <= t_c)).astype(BF16)
    tri2 = jnp.concatenate([tri_b, tri_b], axis=1)
    lw_hi = _b(logw)
    lw_lo = _b(logw - lw_hi.astype(F32))
    cs = jnp.concatenate(
        [_dotb(tri2, jnp.concatenate([lw_hi[q * C:(q + 1) * C], lw_lo[q * C:(q + 1) * C]], axis=0))
         for q in range(nq)], axis=0)
    g_in = jnp.exp(cs)
    g_ex = jnp.exp(cs - logw)
    g_inv = jnp.exp(-cs)
    runs = g_in.reshape(R // ts, ts, bw)
    g_end = jnp.broadcast_to(runs[:, ts - 1:ts, :], runs.shape).reshape(R, bw)

    sums = bdsum([blk(kkv, n) * blk(kkv, n) for n in NU] + [blk(rk2, n) for n in NU])
    kk_n = [blk(kkv, n) * lax.rsqrt(jnp.maximum(sums[n], KK_NORM_FLOOR * KK_NORM_FLOOR)) for n in NU]
    v_p = [blk(v, n) for n in NU]
    at = [-kk_n[n] * blk(g_ex, n) for n in NU]
    rt = [blk(r, n) * blk(g_in, n) for n in NU]
    bt = [kk_n[n] * blk(a_lr, n) * blk(g_inv, n) for n in NU]
    kt = [blk(k2, n) * blk(g_inv, n) for n in NU]
    sm = [_dotb_nt(_b(jnp.concatenate([at[n], rt[n]], axis=0)),
                   jnp.concatenate([vs(_b(bt[n])), vs(_b(kt[n]))], axis=0)) for n in NU]
    lab = [jnp.where(strict, sm[n][0:C, 0:PAIR], 0.0) for n in NU]
    lak = [jnp.where(strict, sm[n][0:C, PAIR:2 * PAIR], 0.0) for n in NU]
    mrb = [jnp.where(incl, sm[n][C:2 * C, 0:PAIR], 0.0) for n in NU]
    mrk = [jnp.where(incl, sm[n][C:2 * C, PAIR:2 * PAIR], 0.0) for n in NU]
    d1 = [jnp.where(dblk, lab[n], 0.0) for n in NU]
    tinv = neumann(d1, sb)
    if sb < ts:
        fp = [pm(lab[n] - d1[n], tinv[n]) for n in NU]
        levels = _log2(ts // sb)
        for lev in range(levels):
            if lev < levels - 1:
                both = [pmm([fp[n], tinv[n]], fp[n]) for n in NU]
                tinv = [tinv[n] + both[n][1] for n in NU]
                fp = [both[n][0] for n in NU]
            else:
                tinv = [tinv[n] + pm(tinv[n], fp[n]) for n in NU]
    tinv = [_b(tinv[n]) for n in NU]
    xv = [pmm([lak[n], mrk[n]], v_p[n]) for n in NU]
    x0 = [xv[n][0] for n in NU]
    s_cur = [[s_scr[sq, i] for i in P] for sq in range(nsb)]
    col_seq = (lax.broadcasted_iota(jnp.int32, (1, PAIR), 1) & (HEAD - 1)) >> ts_bits

    def finish(q, o_q):
        un = [q * n_pairs + i for i in P]
        om = bdsum(o_q)
        oc = [o_q[i] - om[i] * (1.0 / HEAD) for i in P]
        ov = bdsum([oc[i] * oc[i] for i in P])
        for i in P:
            n = un[i]
            ln = slice(i * PAIR, (i + 1) * PAIR)
            bonus = sums[len(units) + n] * v_p[n]
            y = oc[i] * lax.rsqrt(ov[i] * (1.0 / HEAD) + GN_EPS) * lng_ref[:, ln] + lnb_ref[:, ln] + bonus
            o_ref[q * C:(q + 1) * C, ln] = (y * blk(gate, n)).astype(o_ref.dtype)

    o_prev = None
    for q in range(nq):
        un = [q * n_pairs + i for i in P]
        seq = [q * ns + j if nsb > 1 else 0 for j in range(ns)]
        a_s, r_s = [], []
        for i in P:
            parts = []
            for j in range(ns):
                rows = slice(j * ts, (j + 1) * ts)
                lhs = _b(jnp.concatenate([at[un[i]][rows], rt[un[i]][rows]], axis=0))
                parts.append(_dotb_nt(lhs, _b(s_cur[seq[j]][i])))
            a_s.append(jnp.concatenate([x[0:ts] for x in parts], axis=0))
            r_s.append(jnp.concatenate([x[ts:2 * ts] for x in parts], axis=0))
        if o_prev is not None:
            finish(q - 1, o_prev)
        u = [_dotb(tinv[un[i]], vs(_b(a_s[i] + x0[un[i]]))) for i in P]
        for i in P:
            n = un[i]
            ge = blk(g_end, n)
            uv = _b(jnp.concatenate([u[i], v_p[n]], axis=0).T)
            bk = _b(jnp.concatenate([bt[n] * ge, kt[n] * ge], axis=0))
            for j in range(ns):
                uv_j = uv if ns == 1 else jnp.where(col_seq == j, uv, 0.0)
                upd = _dotb(uv_j, bk)
                s_cur[seq[j]][i] = s_cur[seq[j]][i] * ge[j * ts:j * ts + 1, :] + jnp.where(bd, upd, 0.0)
        o_prev = [r_s[i] + pm(mrb[un[i]], u[i]) + xv[un[i]][1] for i in P]
    for sq in range(nsb):
        for i in P:
            s_scr[sq, i] = s_cur[sq][i]
    finish(nq - 1, o_prev)

    @pl.when(c == pl.num_programs(1) - 1)
    def _():
        for j in range(nsb):
            for h in range(2 * n_pairs):
                lo = (h % 2) * HEAD
                nwkv_ref[j, h] = s_scr[j, h // 2, lo:lo + HEAD, lo:lo + HEAD]


def _rwkv_branch(proj_b, shift, wkv, prm, n_seq, seq_len, nsb, tb):
    T, pw = proj_b.shape
    n_heads = wkv.shape[1]
    bw = n_heads * HEAD
    lw = prm["w2a2"].shape[0] // 2
    nc = seq_len // tb
    assert nsb == 1 or nc == 1
    rows = nsb * tb

    def full(a):
        return pl.BlockSpec(a.shape, lambda n, c: (0,) * a.ndim)

    names = ("mu", "w0", "w2a2", "a0", "g2", "kk", "ka", "rk", "lng", "lnb")
    return pl.pallas_call(
        functools.partial(_rwkv_kernel, nsb=nsb, tb=tb, bw=bw, lw=lw),
        grid=(n_seq // nsb, nc),
        in_specs=[pl.BlockSpec((rows, pw), lambda n, c: (n * nc + c, 0)),
                  pl.BlockSpec((nsb, 1, pw), lambda n, c: (n, 0, 0)),
                  pl.BlockSpec((nsb, n_heads, HEAD, HEAD), lambda n, c: (n, 0, 0, 0))]
                 + [full(prm[k]) for k in names],
        out_specs=[pl.BlockSpec((rows, bw), lambda n, c: (n * nc + c, 0)),
                   pl.BlockSpec((nsb, 1, pw), lambda n, c: (n, 0, 0)),
                   pl.BlockSpec((nsb, n_heads, HEAD, HEAD), lambda n, c: (n, 0, 0, 0))],
        out_shape=[jax.ShapeDtypeStruct((T, bw), BF16),
                   jax.ShapeDtypeStruct((n_seq, 1, pw), F32),
                   jax.ShapeDtypeStruct((n_seq, n_heads, HEAD, HEAD), F32)],
        scratch_shapes=[pltpu.VMEM((nsb, n_heads // 2, PAIR, PAIR), F32), pltpu.VMEM((nsb, 1, pw), F32)],
        compiler_params=_cparams(("parallel", "arbitrary")),
        name="rwkv_branch",
    )(proj_b, shift.reshape(n_seq, 1, pw), wkv, *[prm[k] for k in names])


def _outproj_kernel(a_ref, b_ref, x_ref, wa_ref, wb_ref, g_ref, bb_ref, o_ref, *, alpha):
    mix = jnp.dot(a_ref[...].astype(BF16), wa_ref[...], preferred_element_type=F32)
    mix = mix + jnp.dot(b_ref[...].astype(BF16), wb_ref[...], preferred_element_type=F32)
    o_ref[...] = _ln(alpha * x_ref[...] + mix, g_ref[...], bb_ref[...], LN_EPS)


def _outproj(a, b, x, wa, wb, g, bb, alpha, tm):
    T, D = x.shape
    wdt = a.shape[1]
    vec = pl.BlockSpec((1, D), lambda i: (0, 0))
    return pl.pallas_call(
        functools.partial(_outproj_kernel, alpha=alpha),
        grid=(T // tm,),
        in_specs=[pl.BlockSpec((tm, wdt), lambda i: (i, 0)),
                  pl.BlockSpec((tm, wdt), lambda i: (i, 0)),
                  pl.BlockSpec((tm, D), lambda i: (i, 0)),
                  pl.BlockSpec((wdt, D), lambda i: (0, 0)),
                  pl.BlockSpec((wdt, D), lambda i: (0, 0)),
                  vec, vec],
        out_specs=pl.BlockSpec((tm, D), lambda i: (i, 0)),
        out_shape=jax.ShapeDtypeStruct((T, D), F32),
        compiler_params=_cparams(("parallel",)),
        name="out_proj_ln",
    )(a, b, x, wa, wb, g, bb)


def _ffn_kernel(x_ref, wg_ref, wu_ref, wd_ref, g_ref, bb_ref, o_ref, acc_ref, xb_ref, wg_s, wu_s, wd_s, *, alpha):
    i = pl.program_id(0)
    j = pl.program_id(1)

    @pl.when(i == 0)
    def _():
        wg_s[j] = wg_ref[...].astype(BF16)
        wu_s[j] = wu_ref[...].astype(BF16)
        wd_s[j] = wd_ref[...].astype(BF16)

    @pl.when(j == 0)
    def _():
        xb_ref[...] = x_ref[...].astype(BF16)
        acc_ref[...] = jnp.zeros_like(acc_ref)

    xb = xb_ref[...]
    hg = jnp.dot(xb, wg_s[j], preferred_element_type=F32)
    hu = jnp.dot(xb, wu_s[j], preferred_element_type=F32)
    h = (hg * _sigmoid(hg) * hu).astype(BF16)
    acc_ref[...] += jnp.dot(h, wd_s[j], preferred_element_type=F32)

    @pl.when(j == pl.num_programs(1) - 1)
    def _():
        o_ref[...] = _ln(alpha * x_ref[...] + acc_ref[...], g_ref[...], bb_ref[...], LN_EPS)


def _ffn(x, wg, wu, wd, layer, g, bb, alpha, tm, tf):
    T, D = x.shape
    Fd = wg.shape[2]
    nj = Fd // tf
    vec = pl.BlockSpec((1, D), lambda i, j: (0, 0))

    def wj(i, j):
        return jnp.where(i == 0, j, nj - 1)

    return pl.pallas_call(
        functools.partial(_ffn_kernel, alpha=alpha),
        grid=(T // tm, nj),
        in_specs=[pl.BlockSpec((tm, D), lambda i, j: (i, 0)),
                  pl.BlockSpec((None, D, tf), lambda i, j: (layer, 0, wj(i, j))),
                  pl.BlockSpec((None, D, tf), lambda i, j: (layer, 0, wj(i, j))),
                  pl.BlockSpec((None, tf, D), lambda i, j: (layer, wj(i, j), 0)),
                  vec, vec],
        out_specs=pl.BlockSpec((tm, D), lambda i, j: (i, 0)),
        out_shape=jax.ShapeDtypeStruct((T, D), F32),
        scratch_shapes=[pltpu.VMEM((tm, D), F32), pltpu.VMEM((tm, D), BF16),
                        pltpu.VMEM((nj, D, tf), BF16), pltpu.VMEM((nj, D, tf), BF16), pltpu.VMEM((nj, tf, D), BF16)],
        compiler_params=_cparams(("arbitrary", "arbitrary")),
        name="ffn_ln",
    )(x, wg, wu, wd, g, bb)


def _pool_kernel(x_ref, st_ref, w_ref, sc_ref, g_ref, bb_ref, o_ref, nst_ref, ext_ref, *, tl, start, alpha):
    c = pl.program_id(1)

    x = x_ref[...]
    nb, _, D = x.shape
    gw = D // len(POOL_WINDOWS)
    n_ext = POOL_PAD + tl

    @pl.when(c == 0)
    def _():
        ext_ref[:, 0:POOL_PAD - POOL_HIST, :] = jnp.zeros((nb, POOL_PAD - POOL_HIST, D), F32)
        ext_ref[:, POOL_PAD - POOL_HIST:POOL_PAD, :] = st_ref[...]

    ext_ref[:, POOL_PAD:n_ext, :] = x
    pos = start + c * tl + lax.broadcasted_iota(jnp.int32, (1, tl, 1), 1)
    outs = []
    for gi, wdw in enumerate(POOL_WINDOWS):
        lo = gi * gw
        acc = ext_ref[:, 1:n_ext, lo:lo + gw] + ext_ref[:, 0:n_ext - 1, lo:lo + gw]
        h = 2
        while h < wdw:
            rows = acc.shape[1]
            acc = acc[:, h:rows, :] + acc[:, 0:rows - h, :]
            h *= 2
        first = POOL_PAD + 1 - wdw
        ssum = acc[:, first:first + tl, :]
        cnt = jnp.minimum(wdw, pos + 1).astype(F32)
        pooled = ssum / cnt - x[..., lo:lo + gw]
        y = jnp.dot(pooled.reshape(nb * tl, gw).astype(BF16), w_ref[gi], preferred_element_type=F32)
        outs.append(y.reshape(nb, tl, gw))
    y = jnp.concatenate(outs, axis=-1) * sc_ref[...]
    o_ref[...] = _ln(alpha * x + y, g_ref[...], bb_ref[...], LN_EPS)
    tail = ext_ref[:, tl + POOL_PAD - POOL_HIST:tl + POOL_PAD, :]
    ext_ref[:, POOL_PAD - POOL_HIST:POOL_PAD, :] = tail
    nst_ref[...] = tail


def _pool_mix(x, state, w, scale, g, bb, nb, tl, start, alpha):
    n_seq, seq_len, D = x.shape
    vec = pl.BlockSpec((1, D), lambda n, c: (0, 0))
    return pl.pallas_call(
        functools.partial(_pool_kernel, tl=tl, start=start, alpha=alpha),
        grid=(n_seq // nb, seq_len // tl),
        in_specs=[pl.BlockSpec((nb, tl, D), lambda n, c: (n, c, 0)),
                  pl.BlockSpec((nb, POOL_HIST, D), lambda n, c: (n, 0, 0)),
                  pl.BlockSpec(w.shape, lambda n, c: (0, 0, 0)),
                  vec, vec, vec],
        out_specs=[pl.BlockSpec((nb, tl, D), lambda n, c: (n, c, 0)),
                   pl.BlockSpec((nb, POOL_HIST, D), lambda n, c: (n, 0, 0))],
        out_shape=[jax.ShapeDtypeStruct((n_seq, seq_len, D), F32),
                   jax.ShapeDtypeStruct((n_seq, POOL_HIST, D), F32)],
        scratch_shapes=[pltpu.VMEM((nb, POOL_PAD + tl, D), F32)],
        compiler_params=_cparams(("parallel", "arbitrary")),
        name="pool_mix_ln",
    )(x, state, w, scale, g, bb)


def _trunk(x3, conv_prev, shift_prev, wkv_prev, pool_prev, start, prm, tiles):
    n_seq, seq_len, D = x3.shape
    T = n_seq * seq_len
    x = x3.reshape(T, D)
    alpha = prm["alpha"]
    tm = tiles["tm"]
    glu, proj_b = _inproj(x, prm["w_in_a"], prm["w_in_b"], tm)
    a_out, new_conv = _conv_branch(glu.reshape(n_seq, seq_len, -1), conv_prev, prm["conv_w"], prm["conv_b"],
                                   prm["conv_ln_g"], prm["conv_ln_b"], tiles["seq_nb"], tiles["conv_tl"])
    b_out, new_shift, new_wkv = _rwkv_branch(proj_b, shift_prev, wkv_prev, prm["rwkv"], n_seq, seq_len,
                                             tiles["rwkv_nsb"], tiles["rwkv_tb"])
    x = _outproj(a_out.reshape(T, -1), b_out, x, prm["w_out_a"], prm["w_out_b"], prm["ln_mix_g"][0],
                 prm["ln_mix_b"][0], alpha, tiles["out_tm"])
    x = _ffn(x, prm["ffn_gate"], prm["ffn_up"], prm["ffn_down"], 0, prm["ln_ffn_g"][0], prm["ln_ffn_b"][0],
             alpha, tm, tiles["tf"])
    x, new_pool = _pool_mix(x.reshape(n_seq, seq_len, D), pool_prev, prm["pool_w"], prm["pool_scale"],
                            prm["ln_mix_g"][1], prm["ln_mix_b"][1], tiles["seq_nb"], tiles["pool_tl"], start, alpha)
    x = _ffn(x.reshape(T, D), prm["ffn_gate"], prm["ffn_up"], prm["ffn_down"], 1, prm["ln_ffn_g"][1],
             prm["ln_ffn_b"][1], alpha, tm, tiles["tf"])
    return (x.reshape(n_seq, seq_len, D), new_conv[None], new_shift.reshape(1, n_seq, -1), new_wkv[None],
            new_pool[None])


def kernel(x_prompt, x_sample, state_conv, state_shift, state_wkv, state_pool, w_in, conv_w, conv_b, conv_ln_g, conv_ln_b, rwkv_mu, rwkv_w0, rwkv_w2, rwkv_a0, rwkv_a2, rwkv_g2, rwkv_kk, rwkv_ka, rwkv_rk, rwkv_lnx_g, rwkv_lnx_b, w_out, pool_w, pool_scale, ln_mix_g, ln_mix_b, ffn_gate, ffn_up, ffn_down, ln_ffn_g, ln_ffn_b):
    depth = ln_mix_g.shape[0]
    assert depth == 2 and w_in.shape[0] == 1 and pool_w.shape[0] == 1
    B, L, D = x_prompt.shape
    Bs, Ls, _ = x_sample.shape
    a_w = conv_w.shape[2]
    b_w = rwkv_w0.shape[1]
    n_heads = b_w // HEAD
    alpha = float((2 * depth) ** 0.25)

    row = lambda a: a.reshape(1, -1)
    prm = dict(
        alpha=alpha,
        w_in_a=w_in[0, :, :2 * a_w].astype(BF16), w_in_b=w_in[0, :, 2 * a_w:].astype(BF16),
        conv_w=conv_w[0], conv_b=row(conv_b[0]), conv_ln_g=row(conv_ln_g[0]), conv_ln_b=row(conv_ln_b[0]),
        rwkv=dict(mu=row(rwkv_mu[0]), w0=row(rwkv_w0[0]),
                  w2a2=jnp.concatenate([rwkv_w2[0], rwkv_a2[0]], axis=0), a0=row(rwkv_a0[0]), g2=rwkv_g2[0],
                  kk=row(rwkv_kk[0]), ka=row(rwkv_ka[0]), rk=row(rwkv_rk[0]),
                  lng=row(rwkv_lnx_g[0]), lnb=row(rwkv_lnx_b[0])),
        w_out_a=w_out[0, :a_w].astype(BF16), w_out_b=w_out[0, a_w:].astype(BF16),
        pool_w=pool_w[0].astype(BF16), pool_scale=row(pool_scale[0]),
        ln_mix_g=[row(ln_mix_g[i]) for i in range(depth)], ln_mix_b=[row(ln_mix_b[i]) for i in range(depth)],
        ffn_gate=ffn_gate, ffn_up=ffn_up, ffn_down=ffn_down,
        ln_ffn_g=[row(ln_ffn_g[i]) for i in range(depth)], ln_ffn_b=[row(ln_ffn_b[i]) for i in range(depth)],
    )
    dt = x_prompt.dtype
    z_conv = jnp.zeros((B, CONV_HIST, a_w), dt)
    z_shift = jnp.zeros((B, state_shift.shape[-1]), dt)
    z_wkv = jnp.zeros((B, n_heads, HEAD, HEAD), state_wkv.dtype)
    z_pool = jnp.zeros((B, POOL_HIST, D), dt)
    tiles_p = dict(tm=1024, out_tm=2048, tf=256, seq_nb=1, conv_tl=512, pool_tl=1024, rwkv_nsb=1,
                   rwkv_tb=16 * CHUNK)
    tiles_s = dict(tm=1024, out_tm=1024, tf=256, seq_nb=32, conv_tl=Ls, pool_tl=Ls, rwkv_nsb=4 * CHUNK // Ls,
                   rwkv_tb=Ls)
    yp = _trunk(x_prompt, z_conv, z_shift, z_wkv, z_pool, 0, prm, tiles_p)
    ys = _trunk(x_sample, state_conv[0], state_shift[0], state_wkv[0], state_pool[0], PAST_LEN, prm, tiles_s)
    return (yp[0], ys[0], yp[1], yp[2], yp[3], yp[4], ys[1], ys[2], ys[3], ys[4])
```

```python
import functools

import jax
import jax.numpy as jnp
from jax import lax
from jax.experimental import pallas as pl
from jax.experimental.pallas import tpu as pltpu

F32 = jnp.float32
BF16 = jnp.bfloat16

HEAD = 64
PAIR = 2 * HEAD
CHUNK = 64
SOLVE_BLOCK = 16
SUBLANES = 8
CONV_TAPS = 31
CONV_HIST = CONV_TAPS - 1
CONV_PAD = 32
POOL_WINDOWS = (2, 4, 8, 16)
POOL_HIST = max(POOL_WINDOWS) - 1
POOL_PAD = 16
PAST_LEN = 16384
LN_EPS = 1e-5
GN_EPS = 64e-5
DECAY_SCALE = 0.6065306597126334
KK_NORM_FLOOR = 1e-12
VMEM_LIMIT = 56 * 1024 * 1024


def _ln(z, g, b, eps):
    mu = jnp.mean(z, axis=-1, keepdims=True)
    zc = z - mu
    var = jnp.mean(zc * zc, axis=-1, keepdims=True)
    return zc * lax.rsqrt(var + eps) * g + b


def _sigmoid(x):
    return 1.0 / (1.0 + jnp.exp(-x))


def _cparams(sem):
    return pltpu.CompilerParams(dimension_semantics=sem, vmem_limit_bytes=VMEM_LIMIT)


def _log2(n):
    assert n & (n - 1) == 0
    return n.bit_length() - 1


def _inproj_kernel(x_ref, wa_ref, wb_ref, oa_ref, ob_ref):
    xb = x_ref[...].astype(BF16)
    pa = jnp.dot(xb, wa_ref[...], preferred_element_type=F32)
    width = pa.shape[1] // 2
    oa_ref[...] = pa[:, :width] * _sigmoid(pa[:, width:])
    ob_ref[...] = jnp.dot(xb, wb_ref[...], preferred_element_type=F32)


def _inproj(x, wa, wb, tm):
    T, D = x.shape
    na, nb = wa.shape[1] // 2, wb.shape[1]
    return pl.pallas_call(
        _inproj_kernel,
        grid=(T // tm,),
        in_specs=[pl.BlockSpec((tm, D), lambda i: (i, 0)),
                  pl.BlockSpec(wa.shape, lambda i: (0, 0)),
                  pl.BlockSpec((D, nb), lambda i: (0, 0))],
        out_specs=[pl.BlockSpec((tm, na), lambda i: (i, 0)),
                   pl.BlockSpec((tm, nb), lambda i: (i, 0))],
        out_shape=[jax.ShapeDtypeStruct((T, na), F32), jax.ShapeDtypeStruct((T, nb), F32)],
        compiler_params=_cparams(("parallel",)),
        name="in_proj",
    )(x, wa, wb)


def _conv_kernel(u_ref, st_ref, w_ref, b_ref, g_ref, bb_ref, o_ref, nst_ref, ext_ref, *, tl, width):
    c = pl.program_id(1)

    base = CONV_PAD - CONV_HIST

    @pl.when(c == 0)
    def _():
        ext_ref[:, base:CONV_PAD, :] = st_ref[...]
        ext_ref[:, CONV_PAD + tl:CONV_PAD + tl + SUBLANES, :] = jnp.zeros((u_ref.shape[0], SUBLANES, width), F32)

    ext_ref[:, CONV_PAD:CONV_PAD + tl, :] = u_ref[...]
    acc = None
    for rho in range(SUBLANES):
        part = None
        for m in range((base + CONV_TAPS) // SUBLANES + 1):
            j = SUBLANES * m + rho - base
            if 0 <= j < CONV_TAPS:
                term = ext_ref[:, SUBLANES * m:SUBLANES * m + tl + SUBLANES, :] * w_ref[j:j + 1, :]
                part = term if part is None else part + term
        shifted = part[:, rho:rho + tl, :]
        acc = shifted if acc is None else acc + shifted
    acc = acc + b_ref[...]
    y = _ln(acc, g_ref[...], bb_ref[...], LN_EPS)
    o_ref[...] = (y * _sigmoid(y)).astype(o_ref.dtype)
    tail = ext_ref[:, tl + CONV_PAD - CONV_HIST:tl + CONV_PAD, :]
    ext_ref[:, CONV_PAD - CONV_HIST:CONV_PAD, :] = tail
    nst_ref[...] = tail


def _conv_branch(glu, state, w, b, g, bb, nb, tl):
    n_seq, seq_len, width = glu.shape
    vec = pl.BlockSpec((1, width), lambda n, c: (0, 0))
    return pl.pallas_call(
        functools.partial(_conv_kernel, tl=tl, width=width),
        grid=(n_seq // nb, seq_len // tl),
        in_specs=[pl.BlockSpec((nb, tl, width), lambda n, c: (n, c, 0)),
                  pl.BlockSpec((nb, CONV_HIST, width), lambda n, c: (n, 0, 0)),
                  pl.BlockSpec((CONV_TAPS, width), lambda n, c: (0, 0)),
                  vec, vec, vec],
        out_specs=[pl.BlockSpec((nb, tl, width), lambda n, c: (n, c, 0)),
                   pl.BlockSpec((nb, CONV_HIST, width), lambda n, c: (n, 0, 0))],
        out_shape=[jax.ShapeDtypeStruct((n_seq, seq_len, width), BF16 if tl % (2 * SUBLANES) == 0 else F32),
                   jax.ShapeDtypeStruct((n_seq, CONV_HIST, width), F32)],
        scratch_shapes=[pltpu.VMEM((nb, CONV_PAD + tl + SUBLANES, width), F32)],
        compiler_params=_cparams(("parallel", "arbitrary")),
        name="conv_branch",
    )(glu, state, w, b, g, bb)


def _dotb(a, b):
    return jnp.dot(a, b, preferred_element_type=F32)


def _dotb_nt(a, b):
    return lax.dot_general(a, b, (((1,), (1,)), ((), ())), preferred_element_type=F32)


def _b(x):
    return x.astype(BF16)


def _rwkv_kernel(p_ref, sh_ref, wkv_ref, mu_ref, w0_ref, w2a2_ref, a0_ref, g2_ref, kk_ref, ka_ref, rk_ref,
                 lng_ref, lnb_ref, o_ref, nsh_ref, nwkv_ref, s_scr, prev_scr, *, nsb, tb, bw, lw):
    c = pl.program_id(1)
    n_pairs = bw // PAIR
    C = CHUNK
    R = nsb * tb
    nq = R // C
    ts = min(tb, C)
    ns = C // ts
    assert HEAD == C and nq * C == R and (nsb == 1 or tb == ts)
    ts_bits = _log2(ts)
    sb = min(SOLVE_BLOCK, ts)
    sb_bits = _log2(sb)

    @pl.when(c == 0)
    def _():
        prev_scr[...] = sh_ref[...]
        s_scr[...] = jnp.zeros_like(s_scr)
        for j in range(nsb):
            for h in range(2 * n_pairs):
                lo = (h % 2) * HEAD
                s_scr[j, h // 2, lo:lo + HEAD, lo:lo + HEAD] = wkv_ref[j, h]

    p = p_ref[...]
    pw = p.shape[1]
    row = lax.broadcasted_iota(jnp.int32, (R, 1), 0)
    shifted = pltpu.roll(p, 1, 0)
    if nsb == 1:
        xm = p + mu_ref[...] * (shifted - p)
        head = p[0:SUBLANES]
        prev_h = jnp.where(row[0:SUBLANES] == 0, prev_scr[0], shifted[0:SUBLANES])
        xm = jnp.concatenate([head + mu_ref[...] * (prev_h - head), xm[SUBLANES:]], axis=0)
    else:
        first = jnp.concatenate([jnp.broadcast_to(prev_scr[j], (tb, pw)) for j in range(nsb)], axis=0)
        prev = jnp.where((row & (tb - 1)) == 0, first, shifted)
        xm = p + mu_ref[...] * (prev - p)
    for j in range(nsb):
        last = p[(j + 1) * tb - 1:(j + 1) * tb, :]
        prev_scr[j] = last
        nsh_ref[j] = last
    r = xm[:, 0:bw]
    k = xm[:, bw:2 * bw]
    v = xm[:, 2 * bw:3 * bw]
    dwa = xm[:, 3 * bw:3 * bw + 2 * lw]
    dg = xm[:, 3 * bw + 2 * lw:]
    lane_l = lax.broadcasted_iota(jnp.int32, (1, 2 * lw), 1)
    is_w = lane_l < lw
    w2a2 = _b(w2a2_ref[...])
    wpre = w0_ref[...] + _dotb(_b(jnp.where(is_w, jnp.tanh(dwa), 0.0)), w2a2)
    apre = a0_ref[...] + _dotb(_b(jnp.where(is_w, 0.0, dwa)), w2a2)
    logw = -DECAY_SCALE * _sigmoid(wpre)
    a_lr = _sigmoid(apre)
    gate = _dotb(_b(_sigmoid(dg)), _b(g2_ref[...]))
    kkv = k * kk_ref[...]
    k2 = k * (1.0 + (a_lr - 1.0) * ka_ref[...])
    rk2 = r * k2 * rk_ref[...]

    lane = lax.broadcasted_iota(jnp.int32, (1, PAIR), 1)
    m_l = lane < HEAD
    rowp = lax.broadcasted_iota(jnp.int32, (PAIR, PAIR), 0)
    colp = lax.broadcasted_iota(jnp.int32, (PAIR, PAIR), 1)
    bd = (rowp >= HEAD) == (colp >= HEAD)
    t_i = lax.broadcasted_iota(jnp.int32, (C, PAIR), 0)
    s_i = lax.broadcasted_iota(jnp.int32, (C, PAIR), 1) & (HEAD - 1)
    same = (s_i >> ts_bits) == (t_i >> ts_bits)
    strict = same & (s_i < t_i)
    incl = same & (s_i <= t_i)
    eye = (s_i == t_i).astype(F32)
    dblk = (s_i >> sb_bits) == (t_i >> sb_bits)

    def vs(y):
        return jnp.concatenate([jnp.where(m_l, y, 0.0), jnp.where(m_l, 0.0, y)], axis=0)

    def pm(a, y):
        return _dotb(_b(a), vs(_b(y)))

    def pmm(lhs, y):
        out = _dotb(_b(jnp.concatenate(lhs, axis=0)), vs(_b(y)))
        return [out[k * C:(k + 1) * C] for k in range(len(lhs))]

    def bdsum(xs):
        out = []
        for x in xs:
            s0 = jnp.sum(jnp.where(m_l, x, 0.0), axis=1, keepdims=True)
            s1 = jnp.sum(jnp.where(m_l, 0.0, x), axis=1, keepdims=True)
            out.append(jnp.where(m_l, s0, s1))
        return out

    P = range(n_pairs)
    units = [(q, i) for q in range(nq) for i in P]
    NU = range(len(units))

    def blk(x, n):
        q, i = units[n]
        return x[q * C:(q + 1) * C, i * PAIR:(i + 1) * PAIR]

    def neumann(m, n_terms):
        t = [eye + m[n] for n in NU]
        levels = _log2(n_terms)
        if levels == 1:
            return t
        p = [pm(m[n], m[n]) for n in NU]
        for _ in range(levels - 2):
            both = [pmm([p[n], t[n]], p[n]) for n in NU]
            t = [t[n] + both[n][1] for n in NU]
            p = [both[n][0] for n in NU]
        return [t[n] + pm(t[n], p[n]) for n in NU]

    t_c = lax.broadcasted_iota(jnp.int32, (C, C), 0)
    s_c = lax.broadcasted_iota(jnp.int32, (C, C), 1)
    tri_b = (((s_c >> ts_bits) == (t_c >> ts_bits)) & (s_c <= t_c)).astype(BF16)
    tri2 = jnp.concatenate([tri_b, tri_b], axis=1)
    lw_hi = _b(logw)
    lw_lo = _b(logw - lw_hi.astype(F32))
    cs = jnp.concatenate(
        [_dotb(tri2, jnp.concatenate([lw_hi[q * C:(q + 1) * C], lw_lo[q * C:(q + 1) * C]], axis=0))
         for q in range(nq)], axis=0)
    g_in = jnp.exp(cs)
    g_ex = jnp.exp(cs - logw)
    g_inv = jnp.exp(-cs)
    runs = g_in.reshape(R // ts, ts, bw)
    g_end = jnp.broadcast_to(runs[:, ts - 1:ts, :], runs.shape).reshape(R, bw)

    sums = bdsum([blk(kkv, n) * blk(kkv, n) for n in NU] + [blk(rk2, n) for n in NU])
    kk_n = [blk(kkv, n) * lax.rsqrt(jnp.maximum(sums[n], KK_NORM_FLOOR * KK_NORM_FLOOR)) for n in NU]
    v_p = [blk(v, n) for n in NU]
    at = [-kk_n[n] * blk(g_ex, n) for n in NU]
    rt = [blk(r, n) * blk(g_in, n) for n in NU]
    bt = [kk_n[n] * blk(a_lr, n) * blk(g_inv, n) for n in NU]
    kt = [blk(k2, n) * blk(g_inv, n) for n in NU]
    sm = [_dotb_nt(_b(jnp.concatenate([at[n], rt[n]], axis=0)),
                   jnp.concatenate([vs(_b(bt[n])), vs(_b(kt[n]))], axis=0)) for n in NU]
    lab = [jnp.where(strict, sm[n][0:C, 0:PAIR], 0.0) for n in NU]
    lak = [jnp.where(strict, sm[n][0:C, PAIR:2 * PAIR], 0.0) for n in NU]
    mrb = [jnp.where(incl, sm[n][C:2 * C, 0:PAIR], 0.0) for n in NU]
    mrk = [jnp.where(incl, sm[n][C:2 * C, PAIR:2 * PAIR], 0.0) for n in NU]
    d1 = [jnp.where(dblk, lab[n], 0.0) for n in NU]
    tinv = neumann(d1, sb)
    if sb < ts:
        fp = [pm(lab[n] - d1[n], tinv[n]) for n in NU]
        levels = _log2(ts // sb)
        for lev in range(levels):
            if lev < levels - 1:
                both = [pmm([fp[n], tinv[n]], fp[n]) for n in NU]
                tinv = [tinv[n] + both[n][1] for n in NU]
                fp = [both[n][0] for n in NU]
            else:
                tinv = [tinv[n] + pm(tinv[n], fp[n]) for n in NU]
    tinv = [_b(tinv[n]) for n in NU]
    xv = [pmm([lak[n], mrk[n]], v_p[n]) for n in NU]
    x0 = [xv[n][0] for n in NU]
    s_cur = [[s_scr[sq, i] for i in P] for sq in range(nsb)]
    col_seq = (lax.broadcasted_iota(jnp.int32, (1, PAIR), 1) & (HEAD - 1)) >> ts_bits

    def finish(q, o_q):
        un = [q * n_pairs + i for i in P]
        om = bdsum(o_q)
        oc = [o_q[i] - om[i] * (1.0 / HEAD) for i in P]
        ov = bdsum([oc[i] * oc[i] for i in P])
        for i in P:
            n = un[i]
            ln = slice(i * PAIR, (i + 1) * PAIR)
            bonus = sums[len(units) + n] * v_p[n]
            y = oc[i] * lax.rsqrt(ov[i] * (1.0 / HEAD) + GN_EPS) * lng_ref[:, ln] + lnb_ref[:, ln] + bonus
            o_ref[q * C:(q + 1) * C, ln] = (y * blk(gate, n)).astype(o_ref.dtype)

    o_prev = None
    for q in range(nq):
        un = [q * n_pairs + i for i in P]
        seq = [q * ns + j if nsb > 1 else 0 for j in range(ns)]
        a_s, r_s = [], []
        for i in P:
            parts = []
            for j in range(ns):
                rows = slice(j * ts, (j + 1) * ts)
                lhs = _b(jnp.concatenate([at[un[i]][rows], rt[un[i]][rows]], axis=0))
                parts.append(_dotb_nt(lhs, _b(s_cur[seq[j]][i])))
            a_s.append(jnp.concatenate([x[0:ts] for x in parts], axis=0))
            r_s.append(jnp.concatenate([x[ts:2 * ts] for x in parts], axis=0))
        if o_prev is not None:
            finish(q - 1, o_prev)
        u = [_dotb(tinv[un[i]], vs(_b(a_s[i] + x0[un[i]]))) for i in P]
        for i in P:
            n = un[i]
            ge = blk(g_end, n)
            uv = _b(jnp.concatenate([u[i], v_p[n]], axis=0).T)
            bk = _b(jnp.concatenate([bt[n] * ge, kt[n] * ge], axis=0))
            for j in range(ns):
                uv_j = uv if ns == 1 else jnp.where(col_seq == j, uv, 0.0)
                upd = _dotb(uv_j, bk)
                s_cur[seq[j]][i] = s_cur[seq[j]][i] * ge[j * ts:j * ts + 1, :] + jnp.where(bd, upd, 0.0)
        o_prev = [r_s[i] + pm(mrb[un[i]], u[i]) + xv[un[i]][1] for i in P]
    for sq in range(nsb):
        for i in P:
            s_scr[sq, i] = s_cur[sq][i]
    finish(nq - 1, o_prev)

    @pl.when(c == pl.num_programs(1) - 1)
    def _():
        for j in range(nsb):
            for h in range(2 * n_pairs):
                lo = (h % 2) * HEAD
                nwkv_ref[j, h] = s_scr[j, h // 2, lo:lo + HEAD, lo:lo + HEAD]


def _rwkv_branch(proj_b, shift, wkv, prm, n_seq, seq_len, nsb, tb):
    T, pw = proj_b.shape
    n_heads = wkv.shape[1]
    bw = n_heads * HEAD
    lw = prm["w2a2"].shape[0] // 2
    nc = seq_len // tb
    assert nsb == 1 or nc == 1
    rows = nsb * tb

    def full(a):
        return pl.BlockSpec(a.shape, lambda n, c: (0,) * a.ndim)

    names = ("mu", "w0", "w2a2", "a0", "g2", "kk", "ka", "rk", "lng", "lnb")
    return pl.pallas_call(
        functools.partial(_rwkv_kernel, nsb=nsb, tb=tb, bw=bw, lw=lw),
        grid=(n_seq // nsb, nc),
        in_specs=[pl.BlockSpec((rows, pw), lambda n, c: (n * nc + c, 0)),
                  pl.BlockSpec((nsb, 1, pw), lambda n, c: (n, 0, 0)),
                  pl.BlockSpec((nsb, n_heads, HEAD, HEAD), lambda n, c: (n, 0, 0, 0))]
                 + [full(prm[k]) for k in names],
        out_specs=[pl.BlockSpec((rows, bw), lambda n, c: (n * nc + c, 0)),
                   pl.BlockSpec((nsb, 1, pw), lambda n, c: (n, 0, 0)),
                   pl.BlockSpec((nsb, n_heads, HEAD, HEAD), lambda n, c: (n, 0, 0, 0))],
        out_shape=[jax.ShapeDtypeStruct((T, bw), BF16),
                   jax.ShapeDtypeStruct((n_seq, 1, pw), F32),
                   jax.ShapeDtypeStruct((n_seq, n_heads, HEAD, HEAD), F32)],
        scratch_shapes=[pltpu.VMEM((nsb, n_heads // 2, PAIR, PAIR), F32), pltpu.VMEM((nsb, 1, pw), F32)],
        compiler_params=_cparams(("parallel", "arbitrary")),
        name="rwkv_branch",
    )(proj_b, shift.reshape(n_seq, 1, pw), wkv, *[prm[k] for k in names])


def _outproj_kernel(a_ref, b_ref, x_ref, wa_ref, wb_ref, g_ref, bb_ref, o_ref, *, alpha):
    mix = jnp.dot(a_ref[...].astype(BF16), wa_ref[...], preferred_element_type=F32)
    mix = mix + jnp.dot(b_ref[...].astype(BF16), wb_ref[...], preferred_element_type=F32)
    o_ref[...] = _ln(alpha * x_ref[...] + mix, g_ref[...], bb_ref[...], LN_EPS)


def _outproj(a, b, x, wa, wb, g, bb, alpha, tm):
    T, D = x.shape
    wdt = a.shape[1]
    vec = pl.BlockSpec((1, D), lambda i: (0, 0))
    return pl.pallas_call(
        functools.partial(_outproj_kernel, alpha=alpha),
        grid=(T // tm,),
        in_specs=[pl.BlockSpec((tm, wdt), lambda i: (i, 0)),
                  pl.BlockSpec((tm, wdt), lambda i: (i, 0)),
                  pl.BlockSpec((tm, D), lambda i: (i, 0)),
                  pl.BlockSpec((wdt, D), lambda i: (0, 0)),
                  pl.BlockSpec((wdt, D), lambda i: (0, 0)),
                  vec, vec],
        out_specs=pl.BlockSpec((tm, D), lambda i: (i, 0)),
        out_shape=jax.ShapeDtypeStruct((T, D), F32),
        compiler_params=_cparams(("parallel",)),
        name="out_proj_ln",
    )(a, b, x, wa, wb, g, bb)


def _ffn_kernel(x_ref, wg_ref, wu_ref, wd_ref, g_ref, bb_ref, o_ref, acc_ref, xb_ref, wg_s, wu_s, wd_s, *, alpha):
    i = pl.program_id(0)
    j = pl.program_id(1)

    @pl.when(i == 0)
    def _():
        wg_s[j] = wg_ref[...].astype(BF16)
        wu_s[j] = wu_ref[...].astype(BF16)
        wd_s[j] = wd_ref[...].astype(BF16)

    @pl.when(j == 0)
    def _():
        xb_ref[...] = x_ref[...].astype(BF16)
        acc_ref[...] = jnp.zeros_like(acc_ref)

    xb = xb_ref[...]
    hg = jnp.dot(xb, wg_s[j], preferred_element_type=F32)
    hu = jnp.dot(xb, wu_s[j], preferred_element_type=F32)
    h = (hg * _sigmoid(hg) * hu).astype(BF16)
    acc_ref[...] += jnp.dot(h, wd_s[j], preferred_element_type=F32)

    @pl.when(j == pl.num_programs(1) - 1)
    def _():
        o_ref[...] = _ln(alpha * x_ref[...] + acc_ref[...], g_ref[...], bb_ref[...], LN_EPS)


def _ffn(x, wg, wu, wd, layer, g, bb, alpha, tm, tf):
    T, D = x.shape
    Fd = wg.shape[2]
    nj = Fd // tf
    vec = pl.BlockSpec((1, D), lambda i, j: (0, 0))

    def wj(i, j):
        return jnp.where(i == 0, j, nj - 1)

    return pl.pallas_call(
        functools.partial(_ffn_kernel, alpha=alpha),
        grid=(T // tm, nj),
        in_specs=[pl.BlockSpec((tm, D), lambda i, j: (i, 0)),
                  pl.BlockSpec((None, D, tf), lambda i, j: (layer, 0, wj(i, j))),
                  pl.BlockSpec((None, D, tf), lambda i, j: (layer, 0, wj(i, j))),
                  pl.BlockSpec((None, tf, D), lambda i, j: (layer, wj(i, j), 0)),
                  vec, vec],
        out_specs=pl.BlockSpec((tm, D), lambda i, j: (i, 0)),
        out_shape=jax.ShapeDtypeStruct((T, D), F32),
        scratch_shapes=[pltpu.VMEM((tm, D), F32), pltpu.VMEM((tm, D), BF16),
                        pltpu.VMEM((nj, D, tf), BF16), pltpu.VMEM((nj, D, tf), BF16), pltpu.VMEM((nj, tf, D), BF16)],
        compiler_params=_cparams(("arbitrary", "arbitrary")),
        name="ffn_ln",
    )(x, wg, wu, wd, g, bb)


def _pool_kernel(x_ref, st_ref, w_ref, sc_ref, g_ref, bb_ref, o_ref, nst_ref, ext_ref, *, tl, start, alpha):
    c = pl.program_id(1)

    x = x_ref[...]
    nb, _, D = x.shape
    gw = D // len(POOL_WINDOWS)
    n_ext = POOL_PAD + tl

    @pl.when(c == 0)
    def _():
        ext_ref[:, 0:POOL_PAD - POOL_HIST, :] = jnp.zeros((nb, POOL_PAD - POOL_HIST, D), F32)
        ext_ref[:, POOL_PAD - POOL_HIST:POOL_PAD, :] = st_ref[...]

    ext_ref[:, POOL_PAD:n_ext, :] = x
    pos = start + c * tl + lax.broadcasted_iota(jnp.int32, (1, tl, 1), 1)
    outs = []
    for gi, wdw in enumerate(POOL_WINDOWS):
        lo = gi * gw
        acc = ext_ref[:, 1:n_ext, lo:lo + gw] + ext_ref[:, 0:n_ext - 1, lo:lo + gw]
        h = 2
        while h < wdw:
            rows = acc.shape[1]
            acc = acc[:, h:rows, :] + acc[:, 0:rows - h, :]
            h *= 2
        first = POOL_PAD + 1 - wdw
        ssum = acc[:, first:first + tl, :]
        cnt = jnp.minimum(wdw, pos + 1).astype(F32)
        pooled = ssum / cnt - x[..., lo:lo + gw]
        y = jnp.dot(pooled.reshape(nb * tl, gw).astype(BF16), w_ref[gi], preferred_element_type=F32)
        outs.append(y.reshape(nb, tl, gw))
    y = jnp.concatenate(outs, axis=-1) * sc_ref[...]
    o_ref[...] = _ln(alpha * x + y, g_ref[...], bb_ref[...], LN_EPS)
    tail = ext_ref[:, tl + POOL_PAD - POOL_HIST:tl + POOL_PAD, :]
    ext_ref[:, POOL_PAD - POOL_HIST:POOL_PAD, :] = tail
    nst_ref[...] = tail


def _pool_mix(x, state, w, scale, g, bb, nb, tl, start, alpha):
    n_seq, seq_len, D = x.shape
    vec = pl.BlockSpec((1, D), lambda n, c: (0, 0))
    return pl.pallas_call(
        functools.partial(_pool_kernel, tl=tl, start=start, alpha=alpha),
        grid=(n_seq // nb, seq_len // tl),
        in_specs=[pl.BlockSpec((nb, tl, D), lambda n, c: (n, c, 0)),
                  pl.BlockSpec((nb, POOL_HIST, D), lambda n, c: (n, 0, 0)),
                  pl.BlockSpec(w.shape, lambda n, c: (0, 0, 0)),
                  vec, vec, vec],
        out_specs=[pl.BlockSpec((nb, tl, D), lambda n, c: (n, c, 0)),
                   pl.BlockSpec((nb, POOL_HIST, D), lambda n, c: (n, 0, 0))],
        out_shape=[jax.ShapeDtypeStruct((n_seq, seq_len, D), F32),
                   jax.ShapeDtypeStruct((n_seq, POOL_HIST, D), F32)],
        scratch_shapes=[pltpu.VMEM((nb, POOL_PAD + tl, D), F32)],
        compiler_params=_cparams(("parallel", "arbitrary")),
        name="pool_mix_ln",
    )(x, state, w, scale, g, bb)


def _trunk(x3, conv_prev, shift_prev, wkv_prev, pool_prev, start, prm, tiles):
    n_seq, seq_len, D = x3.shape
    T = n_seq * seq_len
    x = x3.reshape(T, D)
    alpha = prm["alpha"]
    tm = tiles["tm"]
    glu, proj_b = _inproj(x, prm["w_in_a"], prm["w_in_b"], tm)
    a_out, new_conv = _conv_branch(glu.reshape(n_seq, seq_len, -1), conv_prev, prm["conv_w"], prm["conv_b"],
                                   prm["conv_ln_g"], prm["conv_ln_b"], tiles["seq_nb"], tiles["conv_tl"])
    b_out, new_shift, new_wkv = _rwkv_branch(proj_b, shift_prev, wkv_prev, prm["rwkv"], n_seq, seq_len,
                                             tiles["rwkv_nsb"], tiles["rwkv_tb"])
    x = _outproj(a_out.reshape(T, -1), b_out, x, prm["w_out_a"], prm["w_out_b"], prm["ln_mix_g"][0],
                 prm["ln_mix_b"][0], alpha, tm)
    x = _ffn(x, prm["ffn_gate"], prm["ffn_up"], prm["ffn_down"], 0, prm["ln_ffn_g"][0], prm["ln_ffn_b"][0],
             alpha, tm, tiles["tf"])
    x, new_pool = _pool_mix(x.reshape(n_seq, seq_len, D), pool_prev, prm["pool_w"], prm["pool_scale"],
                            prm["ln_mix_g"][1], prm["ln_mix_b"][1], tiles["seq_nb"], tiles["pool_tl"], start, alpha)
    x = _ffn(x.reshape(T, D), prm["ffn_gate"], prm["ffn_up"], prm["ffn_down"], 1, prm["ln_ffn_g"][1],
             prm["ln_ffn_b"][1], alpha, tm, tiles["tf"])
    return (x.reshape(n_seq, seq_len, D), new_conv[None], new_shift.reshape(1, n_seq, -1), new_wkv[None],
            new_pool[None])


def kernel(x_prompt, x_sample, state_conv, state_shift, state_wkv, state_pool, w_in, conv_w, conv_b, conv_ln_g, conv_ln_b, rwkv_mu, rwkv_w0, rwkv_w2, rwkv_a0, rwkv_a2, rwkv_g2, rwkv_kk, rwkv_ka, rwkv_rk, rwkv_lnx_g, rwkv_lnx_b, w_out, pool_w, pool_scale, ln_mix_g, ln_mix_b, ffn_gate, ffn_up, ffn_down, ln_ffn_g, ln_ffn_b):
    depth = ln_mix_g.shape[0]
    assert depth == 2 and w_in.shape[0] == 1 and pool_w.shape[0] == 1
    B, L, D = x_prompt.shape
    Bs, Ls, _ = x_sample.shape
    a_w = conv_w.shape[2]
    b_w = rwkv_w0.shape[1]
    n_heads = b_w // HEAD
    alpha = float((2 * depth) ** 0.25)

    row = lambda a: a.reshape(1, -1)
    prm = dict(
        alpha=alpha,
        w_in_a=w_in[0, :, :2 * a_w].astype(BF16), w_in_b=w_in[0, :, 2 * a_w:].astype(BF16),
        conv_w=conv_w[0], conv_b=row(conv_b[0]), conv_ln_g=row(conv_ln_g[0]), conv_ln_b=row(conv_ln_b[0]),
        rwkv=dict(mu=row(rwkv_mu[0]), w0=row(rwkv_w0[0]),
                  w2a2=jnp.concatenate([rwkv_w2[0], rwkv_a2[0]], axis=0), a0=row(rwkv_a0[0]), g2=rwkv_g2[0],
                  kk=row(rwkv_kk[0]), ka=row(rwkv_ka[0]), rk=row(rwkv_rk[0]),
                  lng=row(rwkv_lnx_g[0]), lnb=row(rwkv_lnx_b[0])),
        w_out_a=w_out[0, :a_w].astype(BF16), w_out_b=w_out[0, a_w:].astype(BF16),
        pool_w=pool_w[0].astype(BF16), pool_scale=row(pool_scale[0]),
        ln_mix_g=[row(ln_mix_g[i]) for i in range(depth)], ln_mix_b=[row(ln_mix_b[i]) for i in range(depth)],
        ffn_gate=ffn_gate, ffn_up=ffn_up, ffn_down=ffn_down,
        ln_ffn_g=[row(ln_ffn_g[i]) for i in range(depth)], ln_ffn_b=[row(ln_ffn_b[i]) for i in range(depth)],
    )
    dt = x_prompt.dtype
    z_conv = jnp.zeros((B, CONV_HIST, a_w), dt)
    z_shift = jnp.zeros((B, state_shift.shape[-1]), dt)
    z_wkv = jnp.zeros((B, n_heads, HEAD, HEAD), state_wkv.dtype)
    z_pool = jnp.zeros((B, POOL_HIST, D), dt)
    tiles_p = dict(tm=1024, tf=256, seq_nb=1, conv_tl=512, pool_tl=1024, rwkv_nsb=1, rwkv_tb=8 * CHUNK)
    tiles_s = dict(tm=1024, tf=256, seq_nb=64, conv_tl=Ls, pool_tl=Ls, rwkv_nsb=4 * CHUNK // Ls, rwkv_tb=Ls)
    yp = _trunk(x_prompt, z_conv, z_shift, z_wkv, z_pool, 0, prm, tiles_p)
    ys = _trunk(x_sample, state_conv[0], state_shift[0], state_wkv[0], state_pool[0], PAST_LEN, prm, tiles_s)
    return (yp[0], ys[0], yp[1], yp[2], yp[3], yp[4], ys[1], ys[2], ys[3], ys[4])
```

```python
import functools

import jax
import jax.numpy as jnp
from jax import lax
from jax.experimental import pallas as pl
from jax.experimental.pallas import tpu as pltpu

F32 = jnp.float32
BF16 = jnp.bfloat16

HEAD = 64
PAIR = 2 * HEAD
CHUNK = 64
SOLVE_BLOCK = 16
SUBLANES = 8
CONV_TAPS = 31
CONV_HIST = CONV_TAPS - 1
CONV_PAD = 32
POOL_WINDOWS = (2, 4, 8, 16)
POOL_HIST = max(POOL_WINDOWS) - 1
POOL_PAD = 16
PAST_LEN = 16384
LN_EPS = 1e-5
GN_EPS = 64e-5
DECAY_SCALE = 0.6065306597126334
KK_NORM_FLOOR = 1e-12
VMEM_LIMIT = 56 * 1024 * 1024


def _ln(z, g, b, eps):
    mu = jnp.mean(z, axis=-1, keepdims=True)
    zc = z - mu
    var = jnp.mean(zc * zc, axis=-1, keepdims=True)
    return zc * lax.rsqrt(var + eps) * g + b


def _sigmoid(x):
    return 1.0 / (1.0 + jnp.exp(-x))


def _cparams(sem):
    return pltpu.CompilerParams(dimension_semantics=sem, vmem_limit_bytes=VMEM_LIMIT)


def _log2(n):
    assert n & (n - 1) == 0
    return n.bit_length() - 1


def _inproj_kernel(x_ref, wa_ref, wb_ref, oa_ref, ob_ref):
    xb = x_ref[...].astype(BF16)
    pa = jnp.dot(xb, wa_ref[...], preferred_element_type=F32)
    width = pa.shape[1] // 2
    oa_ref[...] = pa[:, :width] * _sigmoid(pa[:, width:])
    ob_ref[...] = jnp.dot(xb, wb_ref[...], preferred_element_type=F32)


def _inproj(x, wa, wb, tm):
    T, D = x.shape
    na, nb = wa.shape[1] // 2, wb.shape[1]
    return pl.pallas_call(
        _inproj_kernel,
        grid=(T // tm,),
        in_specs=[pl.BlockSpec((tm, D), lambda i: (i, 0)),
                  pl.BlockSpec(wa.shape, lambda i: (0, 0)),
                  pl.BlockSpec((D, nb), lambda i: (0, 0))],
        out_specs=[pl.BlockSpec((tm, na), lambda i: (i, 0)),
                   pl.BlockSpec((tm, nb), lambda i: (i, 0))],
        out_shape=[jax.ShapeDtypeStruct((T, na), F32), jax.ShapeDtypeStruct((T, nb), F32)],
        compiler_params=_cparams(("parallel",)),
        name="in_proj",
    )(x, wa, wb)


def _conv_kernel(u_ref, st_ref, w_ref, b_ref, g_ref, bb_ref, o_ref, nst_ref, ext_ref, *, tl, width):
    c = pl.program_id(1)

    base = CONV_PAD - CONV_HIST

    @pl.when(c == 0)
    def _():
        ext_ref[:, base:CONV_PAD, :] = st_ref[...]
        ext_ref[:, CONV_PAD + tl:CONV_PAD + tl + SUBLANES, :] = jnp.zeros((u_ref.shape[0], SUBLANES, width), F32)

    ext_ref[:, CONV_PAD:CONV_PAD + tl, :] = u_ref[...]
    acc = None
    for rho in range(SUBLANES):
        part = None
        for m in range((base + CONV_TAPS) // SUBLANES + 1):
            j = SUBLANES * m + rho - base
            if 0 <= j < CONV_TAPS:
                term = ext_ref[:, SUBLANES * m:SUBLANES * m + tl + SUBLANES, :] * w_ref[j:j + 1, :]
                part = term if part is None else part + term
        shifted = part[:, rho:rho + tl, :]
        acc = shifted if acc is None else acc + shifted
    acc = acc + b_ref[...]
    y = _ln(acc, g_ref[...], bb_ref[...], LN_EPS)
    o_ref[...] = (y * _sigmoid(y)).astype(o_ref.dtype)
    tail = ext_ref[:, tl + CONV_PAD - CONV_HIST:tl + CONV_PAD, :]
    ext_ref[:, CONV_PAD - CONV_HIST:CONV_PAD, :] = tail
    nst_ref[...] = tail


def _conv_branch(glu, state, w, b, g, bb, nb, tl):
    n_seq, seq_len, width = glu.shape
    vec = pl.BlockSpec((1, width), lambda n, c: (0, 0))
    return pl.pallas_call(
        functools.partial(_conv_kernel, tl=tl, width=width),
        grid=(n_seq // nb, seq_len // tl),
        in_specs=[pl.BlockSpec((nb, tl, width), lambda n, c: (n, c, 0)),
                  pl.BlockSpec((nb, CONV_HIST, width), lambda n, c: (n, 0, 0)),
                  pl.BlockSpec((CONV_TAPS, width), lambda n, c: (0, 0)),
                  vec, vec, vec],
        out_specs=[pl.BlockSpec((nb, tl, width), lambda n, c: (n, c, 0)),
                   pl.BlockSpec((nb, CONV_HIST, width), lambda n, c: (n, 0, 0))],
        out_shape=[jax.ShapeDtypeStruct((n_seq, seq_len, width), BF16 if tl % (2 * SUBLANES) == 0 else F32),
                   jax.ShapeDtypeStruct((n_seq, CONV_HIST, width), F32)],
        scratch_shapes=[pltpu.VMEM((nb, CONV_PAD + tl + SUBLANES, width), F32)],
        compiler_params=_cparams(("parallel", "arbitrary")),
        name="conv_branch",
    )(glu, state, w, b, g, bb)


def _dotb(a, b):
    return jnp.dot(a, b, preferred_element_type=F32)


def _dotb_nt(a, b):
    return lax.dot_general(a, b, (((1,), (1,)), ((), ())), preferred_element_type=F32)


def _b(x):
    return x.astype(BF16)


def _rwkv_kernel(p_ref, sh_ref, wkv_ref, mu_ref, w0_ref, w2a2_ref, a0_ref, g2_ref, kk_ref, ka_ref, rk_ref,
                 lng_ref, lnb_ref, o_ref, nsh_ref, nwkv_ref, s_scr, prev_scr, *, nsb, tb, bw, lw):
    c = pl.program_id(1)
    n_pairs = bw // PAIR
    C = CHUNK
    R = nsb * tb
    nq = R // C
    ts = min(tb, C)
    ns = C // ts
    assert HEAD == C and nq * C == R and (nsb == 1 or tb == ts)
    ts_bits = _log2(ts)
    sb = min(SOLVE_BLOCK, ts)
    sb_bits = _log2(sb)

    @pl.when(c == 0)
    def _():
        prev_scr[...] = sh_ref[...]
        s_scr[...] = jnp.zeros_like(s_scr)
        for j in range(nsb):
            for h in range(2 * n_pairs):
                lo = (h % 2) * HEAD
                s_scr[j, h // 2, lo:lo + HEAD, lo:lo + HEAD] = wkv_ref[j, h]

    p = p_ref[...]
    pw = p.shape[1]
    row = lax.broadcasted_iota(jnp.int32, (R, 1), 0)
    shifted = pltpu.roll(p, 1, 0)
    if nsb == 1:
        xm = p + mu_ref[...] * (shifted - p)
        head = p[0:SUBLANES]
        prev_h = jnp.where(row[0:SUBLANES] == 0, prev_scr[0], shifted[0:SUBLANES])
        xm = jnp.concatenate([head + mu_ref[...] * (prev_h - head), xm[SUBLANES:]], axis=0)
    else:
        first = jnp.concatenate([jnp.broadcast_to(prev_scr[j], (tb, pw)) for j in range(nsb)], axis=0)
        prev = jnp.where((row & (tb - 1)) == 0, first, shifted)
        xm = p + mu_ref[...] * (prev - p)
    for j in range(nsb):
        last = p[(j + 1) * tb - 1:(j + 1) * tb, :]
        prev_scr[j] = last
        nsh_ref[j] = last
    r = xm[:, 0:bw]
    k = xm[:, bw:2 * bw]
    v = xm[:, 2 * bw:3 * bw]
    dwa = xm[:, 3 * bw:3 * bw + 2 * lw]
    dg = xm[:, 3 * bw + 2 * lw:]
    lane_l = lax.broadcasted_iota(jnp.int32, (1, 2 * lw), 1)
    is_w = lane_l < lw
    w2a2 = _b(w2a2_ref[...])
    wpre = w0_ref[...] + _dotb(_b(jnp.where(is_w, jnp.tanh(dwa), 0.0)), w2a2)
    apre = a0_ref[...] + _dotb(_b(jnp.where(is_w, 0.0, dwa)), w2a2)
    logw = -DECAY_SCALE * _sigmoid(wpre)
    a_lr = _sigmoid(apre)
    gate = _dotb(_b(_sigmoid(dg)), _b(g2_ref[...]))
    kkv = k * kk_ref[...]
    k2 = k * (1.0 + (a_lr - 1.0) * ka_ref[...])
    rk2 = r * k2 * rk_ref[...]

    lane = lax.broadcasted_iota(jnp.int32, (1, PAIR), 1)
    m_l = lane < HEAD
    rowp = lax.broadcasted_iota(jnp.int32, (PAIR, PAIR), 0)
    colp = lax.broadcasted_iota(jnp.int32, (PAIR, PAIR), 1)
    bd = (rowp >= HEAD) == (colp >= HEAD)
    t_i = lax.broadcasted_iota(jnp.int32, (C, PAIR), 0)
    s_i = lax.broadcasted_iota(jnp.int32, (C, PAIR), 1) & (HEAD - 1)
    same = (s_i >> ts_bits) == (t_i >> ts_bits)
    strict = same & (s_i < t_i)
    incl = same & (s_i <= t_i)
    eye = (s_i == t_i).astype(F32)
    dblk = (s_i >> sb_bits) == (t_i >> sb_bits)

    def vs(y):
        return jnp.concatenate([jnp.where(m_l, y, 0.0), jnp.where(m_l, 0.0, y)], axis=0)

    def pm(a, y):
        return _dotb(_b(a), vs(_b(y)))

    def pmm(lhs, y):
        out = _dotb(_b(jnp.concatenate(lhs, axis=0)), vs(_b(y)))
        return [out[k * C:(k + 1) * C] for k in range(len(lhs))]

    def bdsum(xs):
        out = []
        for x in xs:
            s0 = jnp.sum(jnp.where(m_l, x, 0.0), axis=1, keepdims=True)
            s1 = jnp.sum(jnp.where(m_l, 0.0, x), axis=1, keepdims=True)
            out.append(jnp.where(m_l, s0, s1))
        return out

    P = range(n_pairs)
    units = [(q, i) for q in range(nq) for i in P]
    NU = range(len(units))

    def blk(x, n):
        q, i = units[n]
        return x[q * C:(q + 1) * C, i * PAIR:(i + 1) * PAIR]

    def neumann(m, n_terms):
        t = [eye + m[n] for n in NU]
        levels = _log2(n_terms)
        if levels == 1:
            return t
        p = [pm(m[n], m[n]) for n in NU]
        for _ in range(levels - 2):
            both = [pmm([p[n], t[n]], p[n]) for n in NU]
            t = [t[n] + both[n][1] for n in NU]
            p = [both[n][0] for n in NU]
        return [t[n] + pm(t[n], p[n]) for n in NU]

    t_c = lax.broadcasted_iota(jnp.int32, (C, C), 0)
    s_c = lax.broadcasted_iota(jnp.int32, (C, C), 1)
    tri_b = (((s_c >> ts_bits) == (t_c >> ts_bits)) & (s_c <= t_c)).astype(BF16)
    tri2 = jnp.concatenate([tri_b, tri_b], axis=1)
    lw_hi = _b(logw)
    lw_lo = _b(logw - lw_hi.astype(F32))
    cs = jnp.concatenate(
        [_dotb(tri2, jnp.concatenate([lw_hi[q * C:(q + 1) * C], lw_lo[q * C:(q + 1) * C]], axis=0))
         for q in range(nq)], axis=0)
    g_in = jnp.exp(cs)
    g_ex = jnp.exp(cs - logw)
    g_inv = jnp.exp(-cs)
    runs = g_in.reshape(R // ts, ts, bw)
    g_end = jnp.broadcast_to(runs[:, ts - 1:ts, :], runs.shape).reshape(R, bw)

    sums = bdsum([blk(kkv, n) * blk(kkv, n) for n in NU] + [blk(rk2, n) for n in NU])
    kk_n = [blk(kkv, n) * lax.rsqrt(jnp.maximum(sums[n], KK_NORM_FLOOR * KK_NORM_FLOOR)) for n in NU]
    v_p = [blk(v, n) for n in NU]
    at = [-kk_n[n] * blk(g_ex, n) for n in NU]
    rt = [blk(r, n) * blk(g_in, n) for n in NU]
    bt = [kk_n[n] * blk(a_lr, n) * blk(g_inv, n) for n in NU]
    kt = [blk(k2, n) * blk(g_inv, n) for n in NU]
    sm = [_dotb_nt(_b(jnp.concatenate([at[n], rt[n]], axis=0)),
                   jnp.concatenate([vs(_b(bt[n])), vs(_b(kt[n]))], axis=0)) for n in NU]
    lab = [jnp.where(strict, sm[n][0:C, 0:PAIR], 0.0) for n in NU]
    lak = [jnp.where(strict, sm[n][0:C, PAIR:2 * PAIR], 0.0) for n in NU]
    mrb = [jnp.where(incl, sm[n][C:2 * C, 0:PAIR], 0.0) for n in NU]
    mrk = [jnp.where(incl, sm[n][C:2 * C, PAIR:2 * PAIR], 0.0) for n in NU]
    d1 = [jnp.where(dblk, lab[n], 0.0) for n in NU]
    tinv = neumann(d1, sb)
    if sb < ts:
        fp = [pm(lab[n] - d1[n], tinv[n]) for n in NU]
        levels = _log2(ts // sb)
        for lev in range(levels):
            if lev < levels - 1:
                both = [pmm([fp[n], tinv[n]], fp[n]) for n in NU]
                tinv = [tinv[n] + both[n][1] for n in NU]
                fp = [both[n][0] for n in NU]
            else:
                tinv = [tinv[n] + pm(tinv[n], fp[n]) for n in NU]
    tinv = [_b(tinv[n]) for n in NU]
    xv = [pmm([lak[n], mrk[n]], v_p[n]) for n in NU]
    x0 = [xv[n][0] for n in NU]
    s_cur = [[s_scr[sq, i] for i in P] for sq in range(nsb)]
    col_seq = (lax.broadcasted_iota(jnp.int32, (1, PAIR), 1) & (HEAD - 1)) >> ts_bits

    def finish(q, o_q):
        un = [q * n_pairs + i for i in P]
        om = bdsum(o_q)
        oc = [o_q[i] - om[i] * (1.0 / HEAD) for i in P]
        ov = bdsum([oc[i] * oc[i] for i in P])
        for i in P:
            n = un[i]
            ln = slice(i * PAIR, (i + 1) * PAIR)
            bonus = sums[len(units) + n] * v_p[n]
            y = oc[i] * lax.rsqrt(ov[i] * (1.0 / HEAD) + GN_EPS) * lng_ref[:, ln] + lnb_ref[:, ln] + bonus
            o_ref[q * C:(q + 1) * C, ln] = (y * blk(gate, n)).astype(o_ref.dtype)

    o_prev = None
    for q in range(nq):
        un = [q * n_pairs + i for i in P]
        seq = [q * ns + j if nsb > 1 else 0 for j in range(ns)]
        a_s, r_s = [], []
        for i in P:
            parts = []
            for j in range(ns):
                rows = slice(j * ts, (j + 1) * ts)
                lhs = _b(jnp.concatenate([at[un[i]][rows], rt[un[i]][rows]], axis=0))
                parts.append(_dotb_nt(lhs, _b(s_cur[seq[j]][i])))
            a_s.append(jnp.concatenate([x[0:ts] for x in parts], axis=0))
            r_s.append(jnp.concatenate([x[ts:2 * ts] for x in parts], axis=0))
        if o_prev is not None:
            finish(q - 1, o_prev)
        u = [_dotb(tinv[un[i]], vs(_b(a_s[i] + x0[un[i]]))) for i in P]
        for i in P:
            n = un[i]
            ge = blk(g_end, n)
            uv = _b(jnp.concatenate([u[i], v_p[n]], axis=0).T)
            bk = _b(jnp.concatenate([bt[n] * ge, kt[n] * ge], axis=0))
            for j in range(ns):
                uv_j = uv if ns == 1 else jnp.where(col_seq == j, uv, 0.0)
                upd = _dotb(uv_j, bk)
                s_cur[seq[j]][i] = s_cur[seq[j]][i] * ge[j * ts:j * ts + 1, :] + jnp.where(bd, upd, 0.0)
        o_prev = [r_s[i] + pm(mrb[un[i]], u[i]) + xv[un[i]][1] for i in P]
    for sq in range(nsb):
        for i in P:
            s_scr[sq, i] = s_cur[sq][i]
    finish(nq - 1, o_prev)

    @pl.when(c == pl.num_programs(1) - 1)
    def _():
        for j in range(nsb):
            for h in range(2 * n_pairs):
                lo = (h % 2) * HEAD
                nwkv_ref[j, h] = s_scr[j, h // 2, lo:lo + HEAD, lo:lo + HEAD]


def _rwkv_branch(proj_b, shift, wkv, prm, n_seq, seq_len, nsb, tb):
    T, pw = proj_b.shape
    n_heads = wkv.shape[1]
    bw = n_heads * HEAD
    lw = prm["w2a2"].shape[0] // 2
    nc = seq_len // tb
    assert nsb == 1 or nc == 1
    rows = nsb * tb

    def full(a):
        return pl.BlockSpec(a.shape, lambda n, c: (0,) * a.ndim)

    names = ("mu", "w0", "w2a2", "a0", "g2", "kk", "ka", "rk", "lng", "lnb")
    return pl.pallas_call(
        functools.partial(_rwkv_kernel, nsb=nsb, tb=tb, bw=bw, lw=lw),
        grid=(n_seq // nsb, nc),
        in_specs=[pl.BlockSpec((rows, pw), lambda n, c: (n * nc + c, 0)),
                  pl.BlockSpec((nsb, 1, pw), lambda n, c: (n, 0, 0)),
                  pl.BlockSpec((nsb, n_heads, HEAD, HEAD), lambda n, c: (n, 0, 0, 0))]
                 + [full(prm[k]) for k in names],
        out_specs=[pl.BlockSpec((rows, bw), lambda n, c: (n * nc + c, 0)),
                   pl.BlockSpec((nsb, 1, pw), lambda n, c: (n, 0, 0)),
                   pl.BlockSpec((nsb, n_heads, HEAD, HEAD), lambda n, c: (n, 0, 0, 0))],
        out_shape=[jax.ShapeDtypeStruct((T, bw), BF16),
                   jax.ShapeDtypeStruct((n_seq, 1, pw), F32),
                   jax.ShapeDtypeStruct((n_seq, n_heads, HEAD, HEAD), F32)],
        scratch_shapes=[pltpu.VMEM((nsb, n_heads // 2, PAIR, PAIR), F32), pltpu.VMEM((nsb, 1, pw), F32)],
        compiler_params=_cparams(("parallel", "arbitrary")),
        name="rwkv_branch",
    )(proj_b, shift.reshape(n_seq, 1, pw), wkv, *[prm[k] for k in names])


def _outproj_kernel(a_ref, b_ref, x_ref, wa_ref, wb_ref, g_ref, bb_ref, o_ref, *, alpha, sub):
    for r0 in range(0, o_ref.shape[0], sub):
        rows = slice(r0, r0 + sub)
        mix = jnp.dot(a_ref[rows, :].astype(BF16), wa_ref[...], preferred_element_type=F32)
        mix = mix + jnp.dot(b_ref[rows, :].astype(BF16), wb_ref[...], preferred_element_type=F32)
        o_ref[rows, :] = _ln(alpha * x_ref[rows, :] + mix, g_ref[...], bb_ref[...], LN_EPS)


def _outproj(a, b, x, wa, wb, g, bb, alpha, tm):
    T, D = x.shape
    wdt = a.shape[1]
    vec = pl.BlockSpec((1, D), lambda i: (0, 0))
    return pl.pallas_call(
        functools.partial(_outproj_kernel, alpha=alpha, sub=tm // 4),
        grid=(T // tm,),
        in_specs=[pl.BlockSpec((tm, wdt), lambda i: (i, 0)),
                  pl.BlockSpec((tm, wdt), lambda i: (i, 0)),
                  pl.BlockSpec((tm, D), lambda i: (i, 0)),
                  pl.BlockSpec((wdt, D), lambda i: (0, 0)),
                  pl.BlockSpec((wdt, D), lambda i: (0, 0)),
                  vec, vec],
        out_specs=pl.BlockSpec((tm, D), lambda i: (i, 0)),
        out_shape=jax.ShapeDtypeStruct((T, D), F32),
        compiler_params=_cparams(("parallel",)),
        name="out_proj_ln",
    )(a, b, x, wa, wb, g, bb)


def _ffn_kernel(x_ref, wg_ref, wu_ref, wd_ref, g_ref, bb_ref, o_ref, acc_ref, xb_ref, wg_s, wu_s, wd_s, *, alpha):
    i = pl.program_id(0)
    j = pl.program_id(1)

    @pl.when(i == 0)
    def _():
        wg_s[j] = wg_ref[...].astype(BF16)
        wu_s[j] = wu_ref[...].astype(BF16)
        wd_s[j] = wd_ref[...].astype(BF16)

    @pl.when(j == 0)
    def _():
        xb_ref[...] = x_ref[...].astype(BF16)
        acc_ref[...] = jnp.zeros_like(acc_ref)

    xb = xb_ref[...]
    hg = jnp.dot(xb, wg_s[j], preferred_element_type=F32)
    hu = jnp.dot(xb, wu_s[j], preferred_element_type=F32)
    h = (hg * _sigmoid(hg) * hu).astype(BF16)
    acc_ref[...] += jnp.dot(h, wd_s[j], preferred_element_type=F32)

    @pl.when(j == pl.num_programs(1) - 1)
    def _():
        o_ref[...] = _ln(alpha * x_ref[...] + acc_ref[...], g_ref[...], bb_ref[...], LN_EPS)


def _ffn(x, wg, wu, wd, layer, g, bb, alpha, tm, tf):
    T, D = x.shape
    Fd = wg.shape[2]
    nj = Fd // tf
    vec = pl.BlockSpec((1, D), lambda i, j: (0, 0))

    def wj(i, j):
        return jnp.where(i == 0, j, nj - 1)

    return pl.pallas_call(
        functools.partial(_ffn_kernel, alpha=alpha),
        grid=(T // tm, nj),
        in_specs=[pl.BlockSpec((tm, D), lambda i, j: (i, 0)),
                  pl.BlockSpec((None, D, tf), lambda i, j: (layer, 0, wj(i, j))),
                  pl.BlockSpec((None, D, tf), lambda i, j: (layer, 0, wj(i, j))),
                  pl.BlockSpec((None, tf, D), lambda i, j: (layer, wj(i, j), 0)),
                  vec, vec],
        out_specs=pl.BlockSpec((tm, D), lambda i, j: (i, 0)),
        out_shape=jax.ShapeDtypeStruct((T, D), F32),
        scratch_shapes=[pltpu.VMEM((tm, D), F32), pltpu.VMEM((tm, D), BF16),
                        pltpu.VMEM((nj, D, tf), BF16), pltpu.VMEM((nj, D, tf), BF16), pltpu.VMEM((nj, tf, D), BF16)],
        compiler_params=_cparams(("arbitrary", "arbitrary")),
        name="ffn_ln",
    )(x, wg, wu, wd, g, bb)


def _pool_kernel(x_ref, st_ref, w_ref, sc_ref, g_ref, bb_ref, o_ref, nst_ref, ext_ref, *, tl, start, alpha):
    c = pl.program_id(1)

    x = x_ref[...]
    nb, _, D = x.shape
    gw = D // len(POOL_WINDOWS)
    n_ext = POOL_PAD + tl

    @pl.when(c == 0)
    def _():
        ext_ref[:, 0:POOL_PAD - POOL_HIST, :] = jnp.zeros((nb, POOL_PAD - POOL_HIST, D), F32)
        ext_ref[:, POOL_PAD - POOL_HIST:POOL_PAD, :] = st_ref[...]

    ext_ref[:, POOL_PAD:n_ext, :] = x
    pos = start + c * tl + lax.broadcasted_iota(jnp.int32, (1, tl, 1), 1)
    outs = []
    for gi, wdw in enumerate(POOL_WINDOWS):
        lo = gi * gw
        acc = ext_ref[:, 1:n_ext, lo:lo + gw] + ext_ref[:, 0:n_ext - 1, lo:lo + gw]
        h = 2
        while h < wdw:
            rows = acc.shape[1]
            acc = acc[:, h:rows, :] + acc[:, 0:rows - h, :]
            h *= 2
        first = POOL_PAD + 1 - wdw
        ssum = acc[:, first:first + tl, :]
        cnt = jnp.minimum(wdw, pos + 1).astype(F32)
        pooled = ssum / cnt - x[..., lo:lo + gw]
        y = jnp.dot(pooled.reshape(nb * tl, gw).astype(BF16), w_ref[gi], preferred_element_type=F32)
        outs.append(y.reshape(nb, tl, gw))
    y = jnp.concatenate(outs, axis=-1) * sc_ref[...]
    o_ref[...] = _ln(alpha * x + y, g_ref[...], bb_ref[...], LN_EPS)
    tail = ext_ref[:, tl + POOL_PAD - POOL_HIST:tl + POOL_PAD, :]
    ext_ref[:, POOL_PAD - POOL_HIST:POOL_PAD, :] = tail
    nst_ref[...] = tail


def _pool_mix(x, state, w, scale, g, bb, nb, tl, start, alpha):
    n_seq, seq_len, D = x.shape
    vec = pl.BlockSpec((1, D), lambda n, c: (0, 0))
    return pl.pallas_call(
        functools.partial(_pool_kernel, tl=tl, start=start, alpha=alpha),
        grid=(n_seq // nb, seq_len // tl),
        in_specs=[pl.BlockSpec((nb, tl, D), lambda n, c: (n, c, 0)),
                  pl.BlockSpec((nb, POOL_HIST, D), lambda n, c: (n, 0, 0)),
                  pl.BlockSpec(w.shape, lambda n, c: (0, 0, 0)),
                  vec, vec, vec],
        out_specs=[pl.BlockSpec((nb, tl, D), lambda n, c: (n, c, 0)),
                   pl.BlockSpec((nb, POOL_HIST, D), lambda n, c: (n, 0, 0))],
        out_shape=[jax.ShapeDtypeStruct((n_seq, seq_len, D), F32),
                   jax.ShapeDtypeStruct((n_seq, POOL_HIST, D), F32)],
        scratch_shapes=[pltpu.VMEM((nb, POOL_PAD + tl, D), F32)],
        compiler_params=_cparams(("parallel", "arbitrary")),
        name="pool_mix_ln",
    )(x, state, w, scale, g, bb)


def _trunk(x3, conv_prev, shift_prev, wkv_prev, pool_prev, start, prm, tiles):
    n_seq, seq_len, D = x3.shape
    T = n_seq * seq_len
    x = x3.reshape(T, D)
    alpha = prm["alpha"]
    tm = tiles["tm"]
    glu, proj_b = _inproj(x, prm["w_in_a"], prm["w_in_b"], tm)
    a_out, new_conv = _conv_branch(glu.reshape(n_seq, seq_len, -1), conv_prev, prm["conv_w"], prm["conv_b"],
                                   prm["conv_ln_g"], prm["conv_ln_b"], tiles["seq_nb"], tiles["conv_tl"])
    b_out, new_shift, new_wkv = _rwkv_branch(proj_b, shift_prev, wkv_prev, prm["rwkv"], n_seq, seq_len,
                                             tiles["rwkv_nsb"], tiles["rwkv_tb"])
    x = _outproj(a_out.reshape(T, -1), b_out, x, prm["w_out_a"], prm["w_out_b"], prm["ln_mix_g"][0],
                 prm["ln_mix_b"][0], alpha, tm)
    x = _ffn(x, prm["ffn_gate"], prm["ffn_up"], prm["ffn_down"], 0, prm["ln_ffn_g"][0], prm["ln_ffn_b"][0],
             alpha, tm, tiles["tf"])
    x, new_pool = _pool_mix(x.reshape(n_seq, seq_len, D), pool_prev, prm["pool_w"], prm["pool_scale"],
                            prm["ln_mix_g"][1], prm["ln_mix_b"][1], tiles["seq_nb"], tiles["pool_tl"], start, alpha)
    x = _ffn(x.reshape(T, D), prm["ffn_gate"], prm["ffn_up"], prm["ffn_down"], 1, prm["ln_ffn_g"][1],
             prm["ln_ffn_b"][1], alpha, tm, tiles["tf"])
    return (x.reshape(n_seq, seq_len, D), new_conv[None], new_shift.reshape(1, n_seq, -1), new_wkv[None],
            new_pool[None])


def kernel(x_prompt, x_sample, state_conv, state_shift, state_wkv, state_pool, w_in, conv_w, conv_b, conv_ln_g, conv_ln_b, rwkv_mu, rwkv_w0, rwkv_w2, rwkv_a0, rwkv_a2, rwkv_g2, rwkv_kk, rwkv_ka, rwkv_rk, rwkv_lnx_g, rwkv_lnx_b, w_out, pool_w, pool_scale, ln_mix_g, ln_mix_b, ffn_gate, ffn_up, ffn_down, ln_ffn_g, ln_ffn_b):
    depth = ln_mix_g.shape[0]
    assert depth == 2 and w_in.shape[0] == 1 and pool_w.shape[0] == 1
    B, L, D = x_prompt.shape
    Bs, Ls, _ = x_sample.shape
    a_w = conv_w.shape[2]
    b_w = rwkv_w0.shape[1]
    n_heads = b_w // HEAD
    alpha = float((2 * depth) ** 0.25)

    row = lambda a: a.reshape(1, -1)
    prm = dict(
        alpha=alpha,
        w_in_a=w_in[0, :, :2 * a_w].astype(BF16), w_in_b=w_in[0, :, 2 * a_w:].astype(BF16),
        conv_w=conv_w[0], conv_b=row(conv_b[0]), conv_ln_g=row(conv_ln_g[0]), conv_ln_b=row(conv_ln_b[0]),
        rwkv=dict(mu=row(rwkv_mu[0]), w0=row(rwkv_w0[0]),
                  w2a2=jnp.concatenate([rwkv_w2[0], rwkv_a2[0]], axis=0), a0=row(rwkv_a0[0]), g2=rwkv_g2[0],
                  kk=row(rwkv_kk[0]), ka=row(rwkv_ka[0]), rk=row(rwkv_rk[0]),
                  lng=row(rwkv_lnx_g[0]), lnb=row(rwkv_lnx_b[0])),
        w_out_a=w_out[0, :a_w].astype(BF16), w_out_b=w_out[0, a_w:].astype(BF16),
        pool_w=pool_w[0].astype(BF16), pool_scale=row(pool_scale[0]),
        ln_mix_g=[row(ln_mix_g[i]) for i in range(depth)], ln_mix_b=[row(ln_mix_b[i]) for i in range(depth)],
        ffn_gate=ffn_gate, ffn_up=ffn_up, ffn_down=ffn_down,
        ln_ffn_g=[row(ln_ffn_g[i]) for i in range(depth)], ln_ffn_b=[row(ln_ffn_b[i]) for i in range(depth)],
    )
    dt = x_prompt.dtype
    z_conv = jnp.zeros((B, CONV_HIST, a_w), dt)
    z_shift = jnp.zeros((B, state_shift.shape[-1]), dt)
    z_wkv = jnp.zeros((B, n_heads, HEAD, HEAD), state_wkv.dtype)
    z_pool = jnp.zeros((B, POOL_HIST, D), dt)
    tiles_p = dict(tm=1024, tf=256, seq_nb=1, conv_tl=512, pool_tl=1024, rwkv_nsb=1, rwkv_tb=8 * CHUNK)
    tiles_s = dict(tm=1024, tf=256, seq_nb=64, conv_tl=Ls, pool_tl=Ls, rwkv_nsb=4 * CHUNK // Ls, rwkv_tb=Ls)
    yp = _trunk(x_prompt, z_conv, z_shift, z_wkv, z_pool, 0, prm, tiles_p)
    ys = _trunk(x_sample, state_conv[0], state_shift[0], state_wkv[0], state_pool[0], PAST_LEN, prm, tiles_s)
    return (yp[0], ys[0], yp[1], yp[2], yp[3], yp[4], ys[1], ys[2], ys[3], ys[4])
```

```python
import functools

import jax
import jax.numpy as jnp
from jax import lax
from jax.experimental import pallas as pl
from jax.experimental.pallas import tpu as pltpu

F32 = jnp.float32
BF16 = jnp.bfloat16

HEAD = 64
PAIR = 2 * HEAD
CHUNK = 64
SOLVE_BLOCK = 16
SUBLANES = 8
CONV_TAPS = 31
CONV_HIST = CONV_TAPS - 1
CONV_PAD = 32
POOL_WINDOWS = (2, 4, 8, 16)
POOL_HIST = max(POOL_WINDOWS) - 1
POOL_PAD = 16
PAST_LEN = 16384
LN_EPS = 1e-5
GN_EPS = 64e-5
DECAY_SCALE = 0.6065306597126334
KK_NORM_FLOOR = 1e-12
VMEM_LIMIT = 56 * 1024 * 1024


def _ln(z, g, b, eps):
    mu = jnp.mean(z, axis=-1, keepdims=True)
    zc = z - mu
    var = jnp.mean(zc * zc, axis=-1, keepdims=True)
    return zc * lax.rsqrt(var + eps) * g + b


def _sigmoid(x):
    return 1.0 / (1.0 + jnp.exp(-x))


def _cparams(sem):
    return pltpu.CompilerParams(dimension_semantics=sem, vmem_limit_bytes=VMEM_LIMIT)


def _log2(n):
    assert n & (n - 1) == 0
    return n.bit_length() - 1


def _inproj_kernel(x_ref, wa_ref, wb_ref, oa_ref, ob_ref):
    xb = x_ref[...].astype(BF16)
    pa = jnp.dot(xb, wa_ref[...], preferred_element_type=F32)
    width = pa.shape[1] // 2
    oa_ref[...] = pa[:, :width] * _sigmoid(pa[:, width:])
    ob_ref[...] = jnp.dot(xb, wb_ref[...], preferred_element_type=F32)


def _inproj(x, wa, wb, tm):
    T, D = x.shape
    na, nb = wa.shape[1] // 2, wb.shape[1]
    return pl.pallas_call(
        _inproj_kernel,
        grid=(T // tm,),
        in_specs=[pl.BlockSpec((tm, D), lambda i: (i, 0)),
                  pl.BlockSpec(wa.shape, lambda i: (0, 0)),
                  pl.BlockSpec((D, nb), lambda i: (0, 0))],
        out_specs=[pl.BlockSpec((tm, na), lambda i: (i, 0)),
                   pl.BlockSpec((tm, nb), lambda i: (i, 0))],
        out_shape=[jax.ShapeDtypeStruct((T, na), F32), jax.ShapeDtypeStruct((T, nb), F32)],
        compiler_params=_cparams(("parallel",)),
        name="in_proj",
    )(x, wa, wb)


def _conv_kernel(u_ref, st_ref, w_ref, b_ref, g_ref, bb_ref, o_ref, nst_ref, ext_ref, *, tl, width):
    c = pl.program_id(1)

    base = CONV_PAD - CONV_HIST

    @pl.when(c == 0)
    def _():
        ext_ref[:, base:CONV_PAD, :] = st_ref[...]
        ext_ref[:, CONV_PAD + tl:CONV_PAD + tl + SUBLANES, :] = jnp.zeros((u_ref.shape[0], SUBLANES, width), F32)

    ext_ref[:, CONV_PAD:CONV_PAD + tl, :] = u_ref[...]
    acc = None
    for rho in range(SUBLANES):
        part = None
        for m in range((base + CONV_TAPS) // SUBLANES + 1):
            j = SUBLANES * m + rho - base
            if 0 <= j < CONV_TAPS:
                term = ext_ref[:, SUBLANES * m:SUBLANES * m + tl + SUBLANES, :] * w_ref[j:j + 1, :]
                part = term if part is None else part + term
        shifted = part[:, rho:rho + tl, :]
        acc = shifted if acc is None else acc + shifted
    acc = acc + b_ref[...]
    y = _ln(acc, g_ref[...], bb_ref[...], LN_EPS)
    o_ref[...] = (y * _sigmoid(y)).astype(o_ref.dtype)
    tail = ext_ref[:, tl + CONV_PAD - CONV_HIST:tl + CONV_PAD, :]
    ext_ref[:, CONV_PAD - CONV_HIST:CONV_PAD, :] = tail
    nst_ref[...] = tail


def _conv_branch(glu, state, w, b, g, bb, nb, tl):
    n_seq, seq_len, width = glu.shape
    vec = pl.BlockSpec((1, width), lambda n, c: (0, 0))
    return pl.pallas_call(
        functools.partial(_conv_kernel, tl=tl, width=width),
        grid=(n_seq // nb, seq_len // tl),
        in_specs=[pl.BlockSpec((nb, tl, width), lambda n, c: (n, c, 0)),
                  pl.BlockSpec((nb, CONV_HIST, width), lambda n, c: (n, 0, 0)),
                  pl.BlockSpec((CONV_TAPS, width), lambda n, c: (0, 0)),
                  vec, vec, vec],
        out_specs=[pl.BlockSpec((nb, tl, width), lambda n, c: (n, c, 0)),
                   pl.BlockSpec((nb, CONV_HIST, width), lambda n, c: (n, 0, 0))],
        out_shape=[jax.ShapeDtypeStruct((n_seq, seq_len, width), BF16 if tl % (2 * SUBLANES) == 0 else F32),
                   jax.ShapeDtypeStruct((n_seq, CONV_HIST, width), F32)],
        scratch_shapes=[pltpu.VMEM((nb, CONV_PAD + tl + SUBLANES, width), F32)],
        compiler_params=_cparams(("parallel", "arbitrary")),
        name="conv_branch",
    )(glu, state, w, b, g, bb)


def _dotb(a, b):
    return jnp.dot(a, b, preferred_element_type=F32)


def _dotb_nt(a, b):
    return lax.dot_general(a, b, (((1,), (1,)), ((), ())), preferred_element_type=F32)


def _b(x):
    return x.astype(BF16)


def _rwkv_kernel(p_ref, sh_ref, wkv_ref, mu_ref, w0_ref, w2a2_ref, a0_ref, g2_ref, kk_ref, ka_ref, rk_ref,
                 lng_ref, lnb_ref, o_ref, nsh_ref, nwkv_ref, s_scr, prev_scr, *, nsb, tb, bw, lw):
    c = pl.program_id(1)
    n_pairs = bw // PAIR
    C = CHUNK
    R = nsb * tb
    nq = R // C
    ts = min(tb, C)
    ns = C // ts
    assert HEAD == C and nq * C == R and (nsb == 1 or tb == ts)
    ts_bits = _log2(ts)
    sb = min(SOLVE_BLOCK, ts)
    sb_bits = _log2(sb)

    @pl.when(c == 0)
    def _():
        prev_scr[...] = sh_ref[...]
        s_scr[...] = jnp.zeros_like(s_scr)
        for j in range(nsb):
            for h in range(2 * n_pairs):
                lo = (h % 2) * HEAD
                s_scr[j, h // 2, lo:lo + HEAD, lo:lo + HEAD] = wkv_ref[j, h]

    p = p_ref[...]
    pw = p.shape[1]
    row = lax.broadcasted_iota(jnp.int32, (R, 1), 0)
    shifted = pltpu.roll(p, 1, 0)
    if nsb == 1:
        xm = p + mu_ref[...] * (shifted - p)
        head = p[0:SUBLANES]
        prev_h = jnp.where(row[0:SUBLANES] == 0, prev_scr[0], shifted[0:SUBLANES])
        xm = jnp.concatenate([head + mu_ref[...] * (prev_h - head), xm[SUBLANES:]], axis=0)
    else:
        first = jnp.concatenate([jnp.broadcast_to(prev_scr[j], (tb, pw)) for j in range(nsb)], axis=0)
        prev = jnp.where((row & (tb - 1)) == 0, first, shifted)
        xm = p + mu_ref[...] * (prev - p)
    for j in range(nsb):
        last = p[(j + 1) * tb - 1:(j + 1) * tb, :]
        prev_scr[j] = last
        nsh_ref[j] = last
    r = xm[:, 0:bw]
    k = xm[:, bw:2 * bw]
    v = xm[:, 2 * bw:3 * bw]
    dwa = xm[:, 3 * bw:3 * bw + 2 * lw]
    dg = xm[:, 3 * bw + 2 * lw:]
    lane_l = lax.broadcasted_iota(jnp.int32, (1, 2 * lw), 1)
    is_w = lane_l < lw
    w2a2 = _b(w2a2_ref[...])
    wpre = w0_ref[...] + _dotb(_b(jnp.where(is_w, jnp.tanh(dwa), 0.0)), w2a2)
    apre = a0_ref[...] + _dotb(_b(jnp.where(is_w, 0.0, dwa)), w2a2)
    logw = -DECAY_SCALE * _sigmoid(wpre)
    a_lr = _sigmoid(apre)
    gate = _dotb(_b(_sigmoid(dg)), _b(g2_ref[...]))
    kkv = k * kk_ref[...]
    k2 = k * (1.0 + (a_lr - 1.0) * ka_ref[...])
    rk2 = r * k2 * rk_ref[...]

    lane = lax.broadcasted_iota(jnp.int32, (1, PAIR), 1)
    m_l = lane < HEAD
    rowp = lax.broadcasted_iota(jnp.int32, (PAIR, PAIR), 0)
    colp = lax.broadcasted_iota(jnp.int32, (PAIR, PAIR), 1)
    bd = (rowp >= HEAD) == (colp >= HEAD)
    t_i = lax.broadcasted_iota(jnp.int32, (C, PAIR), 0)
    s_i = lax.broadcasted_iota(jnp.int32, (C, PAIR), 1) & (HEAD - 1)
    same = (s_i >> ts_bits) == (t_i >> ts_bits)
    strict = same & (s_i < t_i)
    incl = same & (s_i <= t_i)
    eye = (s_i == t_i).astype(F32)
    dblk = (s_i >> sb_bits) == (t_i >> sb_bits)

    def vs(y):
        return jnp.concatenate([jnp.where(m_l, y, 0.0), jnp.where(m_l, 0.0, y)], axis=0)

    def pm(a, y):
        return _dotb(_b(a), vs(_b(y)))

    def pmm(lhs, y):
        out = _dotb(_b(jnp.concatenate(lhs, axis=0)), vs(_b(y)))
        return [out[k * C:(k + 1) * C] for k in range(len(lhs))]

    def bdsum(xs):
        out = []
        for x in xs:
            s0 = jnp.sum(jnp.where(m_l, x, 0.0), axis=1, keepdims=True)
            s1 = jnp.sum(jnp.where(m_l, 0.0, x), axis=1, keepdims=True)
            out.append(jnp.where(m_l, s0, s1))
        return out

    P = range(n_pairs)
    units = [(q, i) for q in range(nq) for i in P]
    NU = range(len(units))

    def blk(x, n):
        q, i = units[n]
        return x[q * C:(q + 1) * C, i * PAIR:(i + 1) * PAIR]

    def neumann(m, n_terms):
        t = [eye + m[n] for n in NU]
        levels = _log2(n_terms)
        if levels == 1:
            return t
        p = [pm(m[n], m[n]) for n in NU]
        for _ in range(levels - 2):
            both = [pmm([p[n], t[n]], p[n]) for n in NU]
            t = [t[n] + both[n][1] for n in NU]
            p = [both[n][0] for n in NU]
        return [t[n] + pm(t[n], p[n]) for n in NU]

    t_c = lax.broadcasted_iota(jnp.int32, (C, C), 0)
    s_c = lax.broadcasted_iota(jnp.int32, (C, C), 1)
    tri_b = (((s_c >> ts_bits) == (t_c >> ts_bits)) & (s_c <= t_c)).astype(BF16)
    tri2 = jnp.concatenate([tri_b, tri_b], axis=1)
    lw_hi = _b(logw)
    lw_lo = _b(logw - lw_hi.astype(F32))
    cs = jnp.concatenate(
        [_dotb(tri2, jnp.concatenate([lw_hi[q * C:(q + 1) * C], lw_lo[q * C:(q + 1) * C]], axis=0))
         for q in range(nq)], axis=0)
    g_in = jnp.exp(cs)
    g_ex = jnp.exp(cs - logw)
    g_inv = jnp.exp(-cs)
    runs = g_in.reshape(R // ts, ts, bw)
    g_end = jnp.broadcast_to(runs[:, ts - 1:ts, :], runs.shape).reshape(R, bw)

    sums = bdsum([blk(kkv, n) * blk(kkv, n) for n in NU] + [blk(rk2, n) for n in NU])
    kk_n = [blk(kkv, n) * lax.rsqrt(jnp.maximum(sums[n], KK_NORM_FLOOR * KK_NORM_FLOOR)) for n in NU]
    v_p = [blk(v, n) for n in NU]
    at = [-kk_n[n] * blk(g_ex, n) for n in NU]
    rt = [blk(r, n) * blk(g_in, n) for n in NU]
    bt = [kk_n[n] * blk(a_lr, n) * blk(g_inv, n) for n in NU]
    kt = [blk(k2, n) * blk(g_inv, n) for n in NU]
    sm = [_dotb_nt(_b(jnp.concatenate([at[n], rt[n]], axis=0)),
                   jnp.concatenate([vs(_b(bt[n])), vs(_b(kt[n]))], axis=0)) for n in NU]
    lab = [jnp.where(strict, sm[n][0:C, 0:PAIR], 0.0) for n in NU]
    lak = [jnp.where(strict, sm[n][0:C, PAIR:2 * PAIR], 0.0) for n in NU]
    mrb = [jnp.where(incl, sm[n][C:2 * C, 0:PAIR], 0.0) for n in NU]
    mrk = [jnp.where(incl, sm[n][C:2 * C, PAIR:2 * PAIR], 0.0) for n in NU]
    d1 = [jnp.where(dblk, lab[n], 0.0) for n in NU]
    tinv = neumann(d1, sb)
    if sb < ts:
        fp = [pm(lab[n] - d1[n], tinv[n]) for n in NU]
        levels = _log2(ts // sb)
        for lev in range(levels):
            if lev < levels - 1:
                both = [pmm([fp[n], tinv[n]], fp[n]) for n in NU]
                tinv = [tinv[n] + both[n][1] for n in NU]
                fp = [both[n][0] for n in NU]
            else:
                tinv = [tinv[n] + pm(tinv[n], fp[n]) for n in NU]
    tinv = [_b(tinv[n]) for n in NU]
    xv = [pmm([lak[n], mrk[n]], v_p[n]) for n in NU]
    x0 = [xv[n][0] for n in NU]
    s_cur = [[s_scr[sq, i] for i in P] for sq in range(nsb)]
    col_seq = (lax.broadcasted_iota(jnp.int32, (1, PAIR), 1) & (HEAD - 1)) >> ts_bits

    def finish(q, o_q):
        un = [q * n_pairs + i for i in P]
        om = bdsum(o_q)
        oc = [o_q[i] - om[i] * (1.0 / HEAD) for i in P]
        ov = bdsum([oc[i] * oc[i] for i in P])
        for i in P:
            n = un[i]
            ln = slice(i * PAIR, (i + 1) * PAIR)
            bonus = sums[len(units) + n] * v_p[n]
            y = oc[i] * lax.rsqrt(ov[i] * (1.0 / HEAD) + GN_EPS) * lng_ref[:, ln] + lnb_ref[:, ln] + bonus
            o_ref[q * C:(q + 1) * C, ln] = (y * blk(gate, n)).astype(o_ref.dtype)

    o_prev = None
    for q in range(nq):
        un = [q * n_pairs + i for i in P]
        seq = [q * ns + j if nsb > 1 else 0 for j in range(ns)]
        a_s, r_s = [], []
        for i in P:
            parts = []
            for j in range(ns):
                rows = slice(j * ts, (j + 1) * ts)
                lhs = _b(jnp.concatenate([at[un[i]][rows], rt[un[i]][rows]], axis=0))
                parts.append(_dotb_nt(lhs, _b(s_cur[seq[j]][i])))
            a_s.append(jnp.concatenate([x[0:ts] for x in parts], axis=0))
            r_s.append(jnp.concatenate([x[ts:2 * ts] for x in parts], axis=0))
        if o_prev is not None:
            finish(q - 1, o_prev)
        u = [_dotb(tinv[un[i]], vs(_b(a_s[i] + x0[un[i]]))) for i in P]
        for i in P:
            n = un[i]
            ge = blk(g_end, n)
            uv = _b(jnp.concatenate([u[i], v_p[n]], axis=0).T)
            bk = _b(jnp.concatenate([bt[n] * ge, kt[n] * ge], axis=0))
            for j in range(ns):
                uv_j = uv if ns == 1 else jnp.where(col_seq == j, uv, 0.0)
                upd = _dotb(uv_j, bk)
                s_cur[seq[j]][i] = s_cur[seq[j]][i] * ge[j * ts:j * ts + 1, :] + jnp.where(bd, upd, 0.0)
        o_prev = [r_s[i] + pm(mrb[un[i]], u[i]) + xv[un[i]][1] for i in P]
    for sq in range(nsb):
        for i in P:
            s_scr[sq, i] = s_cur[sq][i]
    finish(nq - 1, o_prev)

    @pl.when(c == pl.num_programs(1) - 1)
    def _():
        for j in range(nsb):
            for h in range(2 * n_pairs):
                lo = (h % 2) * HEAD
                nwkv_ref[j, h] = s_scr[j, h // 2, lo:lo + HEAD, lo:lo + HEAD]


def _rwkv_branch(proj_b, shift, wkv, prm, n_seq, seq_len, nsb, tb):
    T, pw = proj_b.shape
    n_heads = wkv.shape[1]
    bw = n_heads * HEAD
    lw = prm["w2a2"].shape[0] // 2
    nc = seq_len // tb
    assert nsb == 1 or nc == 1
    rows = nsb * tb

    def full(a):
        return pl.BlockSpec(a.shape, lambda n, c: (0,) * a.ndim)

    names = ("mu", "w0", "w2a2", "a0", "g2", "kk", "ka", "rk", "lng", "lnb")
    return pl.pallas_call(
        functools.partial(_rwkv_kernel, nsb=nsb, tb=tb, bw=bw, lw=lw),
        grid=(n_seq // nsb, nc),
        in_specs=[pl.BlockSpec((rows, pw), lambda n, c: (n * nc + c, 0)),
                  pl.BlockSpec((nsb, 1, pw), lambda n, c: (n, 0, 0)),
                  pl.BlockSpec((nsb, n_heads, HEAD, HEAD), lambda n, c: (n, 0, 0, 0))]
                 + [full(prm[k]) for k in names],
        out_specs=[pl.BlockSpec((rows, bw), lambda n, c: (n * nc + c, 0)),
                   pl.BlockSpec((nsb, 1, pw), lambda n, c: (n, 0, 0)),
                   pl.BlockSpec((nsb, n_heads, HEAD, HEAD), lambda n, c: (n, 0, 0, 0))],
        out_shape=[jax.ShapeDtypeStruct((T, bw), BF16),
                   jax.ShapeDtypeStruct((n_seq, 1, pw), F32),
                   jax.ShapeDtypeStruct((n_seq, n_heads, HEAD, HEAD), F32)],
        scratch_shapes=[pltpu.VMEM((nsb, n_heads // 2, PAIR, PAIR), F32), pltpu.VMEM((nsb, 1, pw), F32)],
        compiler_params=_cparams(("parallel", "arbitrary")),
        name="rwkv_branch",
    )(proj_b, shift.reshape(n_seq, 1, pw), wkv, *[prm[k] for k in names])


def _outproj_kernel(a_ref, b_ref, x_ref, wa_ref, wb_ref, g_ref, bb_ref, o_ref, *, alpha, sub):
    for r0 in range(0, o_ref.shape[0], sub):
        rows = slice(r0, r0 + sub)
        mix = jnp.dot(a_ref[rows, :].astype(BF16), wa_ref[...], preferred_element_type=F32)
        mix = mix + jnp.dot(b_ref[rows, :].astype(BF16), wb_ref[...], preferred_element_type=F32)
        o_ref[rows, :] = _ln(alpha * x_ref[rows, :] + mix, g_ref[...], bb_ref[...], LN_EPS)


def _outproj(a, b, x, wa, wb, g, bb, alpha, tm):
    T, D = x.shape
    wdt = a.shape[1]
    vec = pl.BlockSpec((1, D), lambda i: (0, 0))
    return pl.pallas_call(
        functools.partial(_outproj_kernel, alpha=alpha, sub=tm // 4),
        grid=(T // tm,),
        in_specs=[pl.BlockSpec((tm, wdt), lambda i: (i, 0)),
                  pl.BlockSpec((tm, wdt), lambda i: (i, 0)),
                  pl.BlockSpec((tm, D), lambda i: (i, 0)),
                  pl.BlockSpec((wdt, D), lambda i: (0, 0)),
                  pl.BlockSpec((wdt, D), lambda i: (0, 0)),
                  vec, vec],
        out_specs=pl.BlockSpec((tm, D), lambda i: (i, 0)),
        out_shape=jax.ShapeDtypeStruct((T, D), F32),
        compiler_params=_cparams(("parallel",)),
        name="out_proj_ln",
    )(a, b, x, wa, wb, g, bb)


def _ffn_kernel(x_ref, wg_ref, wu_ref, wd_ref, g_ref, bb_ref, o_ref, acc_ref, xb_ref, wg_s, wu_s, wd_s,
                *, alpha, ln_groups):
    i = pl.program_id(0)
    j = pl.program_id(1)

    @pl.when(i == 0)
    def _():
        wg_s[j] = wg_ref[...].astype(BF16)
        wu_s[j] = wu_ref[...].astype(BF16)
        wd_s[j] = wd_ref[...].astype(BF16)

    @pl.when(j == 0)
    def _():
        xb_ref[...] = x_ref[...].astype(BF16)
        acc_ref[...] = jnp.zeros_like(acc_ref)

    def partial_sum(rows):
        xb = xb_ref[rows, :]
        hg = jnp.dot(xb, wg_s[j], preferred_element_type=F32)
        hu = jnp.dot(xb, wu_s[j], preferred_element_type=F32)
        h = (hg * _sigmoid(hg) * hu).astype(BF16)
        return acc_ref[rows, :] + jnp.dot(h, wd_s[j], preferred_element_type=F32)

    last = pl.num_programs(1) - 1

    @pl.when(j < last)
    def _():
        acc_ref[...] = partial_sum(slice(None))

    @pl.when(j == last)
    def _():
        tm = o_ref.shape[0]
        for r0 in range(0, tm, tm // ln_groups):
            rows = slice(r0, r0 + tm // ln_groups)
            o_ref[rows, :] = _ln(alpha * x_ref[rows, :] + partial_sum(rows), g_ref[...], bb_ref[...], LN_EPS)


def _ffn(x, wg, wu, wd, layer, g, bb, alpha, tm, tf):
    T, D = x.shape
    Fd = wg.shape[2]
    nj = Fd // tf
    vec = pl.BlockSpec((1, D), lambda i, j: (0, 0))

    def wj(i, j):
        return jnp.where(i == 0, j, nj - 1)

    return pl.pallas_call(
        functools.partial(_ffn_kernel, alpha=alpha, ln_groups=2),
        grid=(T // tm, nj),
        in_specs=[pl.BlockSpec((tm, D), lambda i, j: (i, 0)),
                  pl.BlockSpec((None, D, tf), lambda i, j: (layer, 0, wj(i, j))),
                  pl.BlockSpec((None, D, tf), lambda i, j: (layer, 0, wj(i, j))),
                  pl.BlockSpec((None, tf, D), lambda i, j: (layer, wj(i, j), 0)),
                  vec, vec],
        out_specs=pl.BlockSpec((tm, D), lambda i, j: (i, 0)),
        out_shape=jax.ShapeDtypeStruct((T, D), F32),
        scratch_shapes=[pltpu.VMEM((tm, D), F32), pltpu.VMEM((tm, D), BF16),
                        pltpu.VMEM((nj, D, tf), BF16), pltpu.VMEM((nj, D, tf), BF16), pltpu.VMEM((nj, tf, D), BF16)],
        compiler_params=_cparams(("arbitrary", "arbitrary")),
        name="ffn_ln",
    )(x, wg, wu, wd, g, bb)


def _pool_kernel(x_ref, st_ref, w_ref, sc_ref, g_ref, bb_ref, o_ref, nst_ref, ext_ref, *, tl, start, alpha):
    c = pl.program_id(1)

    x = x_ref[...]
    nb, _, D = x.shape
    gw = D // len(POOL_WINDOWS)
    n_ext = POOL_PAD + tl

    @pl.when(c == 0)
    def _():
        ext_ref[:, 0:POOL_PAD - POOL_HIST, :] = jnp.zeros((nb, POOL_PAD - POOL_HIST, D), F32)
        ext_ref[:, POOL_PAD - POOL_HIST:POOL_PAD, :] = st_ref[...]

    ext_ref[:, POOL_PAD:n_ext, :] = x
    pos = start + c * tl + lax.broadcasted_iota(jnp.int32, (1, tl, 1), 1)
    outs = []
    for gi, wdw in enumerate(POOL_WINDOWS):
        lo = gi * gw
        acc = ext_ref[:, 1:n_ext, lo:lo + gw] + ext_ref[:, 0:n_ext - 1, lo:lo + gw]
        h = 2
        while h < wdw:
            rows = acc.shape[1]
            acc = acc[:, h:rows, :] + acc[:, 0:rows - h, :]
            h *= 2
        first = POOL_PAD + 1 - wdw
        ssum = acc[:, first:first + tl, :]
        cnt = jnp.minimum(wdw, pos + 1).astype(F32)
        pooled = ssum / cnt - x[..., lo:lo + gw]
        y = jnp.dot(pooled.reshape(nb * tl, gw).astype(BF16), w_ref[gi], preferred_element_type=F32)
        outs.append(y.reshape(nb, tl, gw))
    y = jnp.concatenate(outs, axis=-1) * sc_ref[...]
    o_ref[...] = _ln(alpha * x + y, g_ref[...], bb_ref[...], LN_EPS)
    tail = ext_ref[:, tl + POOL_PAD - POOL_HIST:tl + POOL_PAD, :]
    ext_ref[:, POOL_PAD - POOL_HIST:POOL_PAD, :] = tail
    nst_ref[...] = tail


def _pool_mix(x, state, w, scale, g, bb, nb, tl, start, alpha):
    n_seq, seq_len, D = x.shape
    vec = pl.BlockSpec((1, D), lambda n, c: (0, 0))
    return pl.pallas_call(
        functools.partial(_pool_kernel, tl=tl, start=start, alpha=alpha),
        grid=(n_seq // nb, seq_len // tl),
        in_specs=[pl.BlockSpec((nb, tl, D), lambda n, c: (n, c, 0)),
                  pl.BlockSpec((nb, POOL_HIST, D), lambda n, c: (n, 0, 0)),
                  pl.BlockSpec(w.shape, lambda n, c: (0, 0, 0)),
                  vec, vec, vec],
        out_specs=[pl.BlockSpec((nb, tl, D), lambda n, c: (n, c, 0)),
                   pl.BlockSpec((nb, POOL_HIST, D), lambda n, c: (n, 0, 0))],
        out_shape=[jax.ShapeDtypeStruct((n_seq, seq_len, D), F32),
                   jax.ShapeDtypeStruct((n_seq, POOL_HIST, D), F32)],
        scratch_shapes=[pltpu.VMEM((nb, POOL_PAD + tl, D), F32)],
        compiler_params=_cparams(("parallel", "arbitrary")),
        name="pool_mix_ln",
    )(x, state, w, scale, g, bb)


def _trunk(x3, conv_prev, shift_prev, wkv_prev, pool_prev, start, prm, tiles):
    n_seq, seq_len, D = x3.shape
    T = n_seq * seq_len
    x = x3.reshape(T, D)
    alpha = prm["alpha"]
    tm = tiles["tm"]
    glu, proj_b = _inproj(x, prm["w_in_a"], prm["w_in_b"], tm)
    a_out, new_conv = _conv_branch(glu.reshape(n_seq, seq_len, -1), conv_prev, prm["conv_w"], prm["conv_b"],
                                   prm["conv_ln_g"], prm["conv_ln_b"], tiles["seq_nb"], tiles["conv_tl"])
    b_out, new_shift, new_wkv = _rwkv_branch(proj_b, shift_prev, wkv_prev, prm["rwkv"], n_seq, seq_len,
                                             tiles["rwkv_nsb"], tiles["rwkv_tb"])
    x = _outproj(a_out.reshape(T, -1), b_out, x, prm["w_out_a"], prm["w_out_b"], prm["ln_mix_g"][0],
                 prm["ln_mix_b"][0], alpha, tm)
    x = _ffn(x, prm["ffn_gate"], prm["ffn_up"], prm["ffn_down"], 0, prm["ln_ffn_g"][0], prm["ln_ffn_b"][0],
             alpha, tm, tiles["tf"])
    x, new_pool = _pool_mix(x.reshape(n_seq, seq_len, D), pool_prev, prm["pool_w"], prm["pool_scale"],
                            prm["ln_mix_g"][1], prm["ln_mix_b"][1], tiles["seq_nb"], tiles["pool_tl"], start, alpha)
    x = _ffn(x.reshape(T, D), prm["ffn_gate"], prm["ffn_up"], prm["ffn_down"], 1, prm["ln_ffn_g"][1],
             prm["ln_ffn_b"][1], alpha, tm, tiles["tf"])
    return (x.reshape(n_seq, seq_len, D), new_conv[None], new_shift.reshape(1, n_seq, -1), new_wkv[None],
            new_pool[None])


def kernel(x_prompt, x_sample, state_conv, state_shift, state_wkv, state_pool, w_in, conv_w, conv_b, conv_ln_g, conv_ln_b, rwkv_mu, rwkv_w0, rwkv_w2, rwkv_a0, rwkv_a2, rwkv_g2, rwkv_kk, rwkv_ka, rwkv_rk, rwkv_lnx_g, rwkv_lnx_b, w_out, pool_w, pool_scale, ln_mix_g, ln_mix_b, ffn_gate, ffn_up, ffn_down, ln_ffn_g, ln_ffn_b):
    depth = ln_mix_g.shape[0]
    assert depth == 2 and w_in.shape[0] == 1 and pool_w.shape[0] == 1
    B, L, D = x_prompt.shape
    Bs, Ls, _ = x_sample.shape
    a_w = conv_w.shape[2]
    b_w = rwkv_w0.shape[1]
    n_heads = b_w // HEAD
    alpha = float((2 * depth) ** 0.25)

    row = lambda a: a.reshape(1, -1)
    prm = dict(
        alpha=alpha,
        w_in_a=w_in[0, :, :2 * a_w].astype(BF16), w_in_b=w_in[0, :, 2 * a_w:].astype(BF16),
        conv_w=conv_w[0], conv_b=row(conv_b[0]), conv_ln_g=row(conv_ln_g[0]), conv_ln_b=row(conv_ln_b[0]),
        rwkv=dict(mu=row(rwkv_mu[0]), w0=row(rwkv_w0[0]),
                  w2a2=jnp.concatenate([rwkv_w2[0], rwkv_a2[0]], axis=0), a0=row(rwkv_a0[0]), g2=rwkv_g2[0],
                  kk=row(rwkv_kk[0]), ka=row(rwkv_ka[0]), rk=row(rwkv_rk[0]),
                  lng=row(rwkv_lnx_g[0]), lnb=row(rwkv_lnx_b[0])),
        w_out_a=w_out[0, :a_w].astype(BF16), w_out_b=w_out[0, a_w:].astype(BF16),
        pool_w=pool_w[0].astype(BF16), pool_scale=row(pool_scale[0]),
        ln_mix_g=[row(ln_mix_g[i]) for i in range(depth)], ln_mix_b=[row(ln_mix_b[i]) for i in range(depth)],
        ffn_gate=ffn_gate, ffn_up=ffn_up, ffn_down=ffn_down,
        ln_ffn_g=[row(ln_ffn_g[i]) for i in range(depth)], ln_ffn_b=[row(ln_ffn_b[i]) for i in range(depth)],
    )
    dt = x_prompt.dtype
    z_conv = jnp.zeros((B, CONV_HIST, a_w), dt)
    z_shift = jnp.zeros((B, state_shift.shape[-1]), dt)
    z_wkv = jnp.zeros((B, n_heads, HEAD, HEAD), state_wkv.dtype)
    z_pool = jnp.zeros((B, POOL_HIST, D), dt)
    tiles_p = dict(tm=1024, tf=256, seq_nb=1, conv_tl=512, pool_tl=1024, rwkv_nsb=1, rwkv_tb=8 * CHUNK)
    tiles_s = dict(tm=1024, tf=256, seq_nb=64, conv_tl=Ls, pool_tl=Ls, rwkv_nsb=4 * CHUNK // Ls, rwkv_tb=Ls)
    yp = _trunk(x_prompt, z_conv, z_shift, z_wkv, z_pool, 0, prm, tiles_p)
    ys = _trunk(x_sample, state_conv[0], state_shift[0], state_wkv[0], state_pool[0], PAST_LEN, prm, tiles_s)
    return (yp[0], ys[0], yp[1], yp[2], yp[3], yp[4], ys[1], ys[2], ys[3], ys[4])
```

```python
import functools

import jax
import jax.numpy as jnp
from jax import lax
from jax.experimental import pallas as pl
from jax.experimental.pallas import tpu as pltpu

F32 = jnp.float32
BF16 = jnp.bfloat16

HEAD = 64
PAIR = 2 * HEAD
CHUNK = 64
SOLVE_BLOCK = 16
SUBLANES = 8
CONV_TAPS = 31
CONV_HIST = CONV_TAPS - 1
CONV_PAD = 32
POOL_WINDOWS = (2, 4, 8, 16)
POOL_HIST = max(POOL_WINDOWS) - 1
POOL_PAD = 16
PAST_LEN = 16384
LN_EPS = 1e-5
GN_EPS = 64e-5
DECAY_SCALE = 0.6065306597126334
KK_NORM_FLOOR = 1e-12
VMEM_LIMIT = 56 * 1024 * 1024


def _ln(z, g, b, eps):
    mu = jnp.mean(z, axis=-1, keepdims=True)
    zc = z - mu
    var = jnp.mean(zc * zc, axis=-1, keepdims=True)
    return zc * lax.rsqrt(var + eps) * g + b


def _sigmoid(x):
    return 1.0 / (1.0 + jnp.exp(-x))


def _cparams(sem):
    return pltpu.CompilerParams(dimension_semantics=sem, vmem_limit_bytes=VMEM_LIMIT)


def _log2(n):
    assert n & (n - 1) == 0
    return n.bit_length() - 1


def _inproj_kernel(x_ref, wa_ref, wb_ref, oa_ref, ob_ref):
    xb = x_ref[...].astype(BF16)
    pa = jnp.dot(xb, wa_ref[...], preferred_element_type=F32)
    width = pa.shape[1] // 2
    oa_ref[...] = pa[:, :width] * _sigmoid(pa[:, width:])
    ob_ref[...] = jnp.dot(xb, wb_ref[...], preferred_element_type=F32)


def _inproj(x, wa, wb, tm):
    T, D = x.shape
    na, nb = wa.shape[1] // 2, wb.shape[1]
    return pl.pallas_call(
        _inproj_kernel,
        grid=(T // tm,),
        in_specs=[pl.BlockSpec((tm, D), lambda i: (i, 0)),
                  pl.BlockSpec(wa.shape, lambda i: (0, 0)),
                  pl.BlockSpec((D, nb), lambda i: (0, 0))],
        out_specs=[pl.BlockSpec((tm, na), lambda i: (i, 0)),
                   pl.BlockSpec((tm, nb), lambda i: (i, 0))],
        out_shape=[jax.ShapeDtypeStruct((T, na), F32), jax.ShapeDtypeStruct((T, nb), F32)],
        compiler_params=_cparams(("parallel",)),
        name="in_proj",
    )(x, wa, wb)


def _conv_kernel(u_ref, st_ref, w_ref, b_ref, g_ref, bb_ref, o_ref, nst_ref, ext_ref, *, tl, width):
    c = pl.program_id(1)

    base = CONV_PAD - CONV_HIST

    @pl.when(c == 0)
    def _():
        ext_ref[:, base:CONV_PAD, :] = st_ref[...]
        ext_ref[:, CONV_PAD + tl:CONV_PAD + tl + SUBLANES, :] = jnp.zeros((u_ref.shape[0], SUBLANES, width), F32)

    ext_ref[:, CONV_PAD:CONV_PAD + tl, :] = u_ref[...]
    acc = None
    for rho in range(SUBLANES):
        part = None
        for m in range((base + CONV_TAPS) // SUBLANES + 1):
            j = SUBLANES * m + rho - base
            if 0 <= j < CONV_TAPS:
                term = ext_ref[:, SUBLANES * m:SUBLANES * m + tl + SUBLANES, :] * w_ref[j:j + 1, :]
                part = term if part is None else part + term
        shifted = part[:, rho:rho + tl, :]
        acc = shifted if acc is None else acc + shifted
    acc = acc + b_ref[...]
    y = _ln(acc, g_ref[...], bb_ref[...], LN_EPS)
    o_ref[...] = (y * _sigmoid(y)).astype(o_ref.dtype)
    tail = ext_ref[:, tl + CONV_PAD - CONV_HIST:tl + CONV_PAD, :]
    ext_ref[:, CONV_PAD - CONV_HIST:CONV_PAD, :] = tail
    nst_ref[...] = tail


def _conv_branch(glu, state, w, b, g, bb, nb, tl):
    n_seq, seq_len, width = glu.shape
    vec = pl.BlockSpec((1, width), lambda n, c: (0, 0))
    return pl.pallas_call(
        functools.partial(_conv_kernel, tl=tl, width=width),
        grid=(n_seq // nb, seq_len // tl),
        in_specs=[pl.BlockSpec((nb, tl, width), lambda n, c: (n, c, 0)),
                  pl.BlockSpec((nb, CONV_HIST, width), lambda n, c: (n, 0, 0)),
                  pl.BlockSpec((CONV_TAPS, width), lambda n, c: (0, 0)),
                  vec, vec, vec],
        out_specs=[pl.BlockSpec((nb, tl, width), lambda n, c: (n, c, 0)),
                   pl.BlockSpec((nb, CONV_HIST, width), lambda n, c: (n, 0, 0))],
        out_shape=[jax.ShapeDtypeStruct((n_seq, seq_len, width), BF16 if tl % (2 * SUBLANES) == 0 else F32),
                   jax.ShapeDtypeStruct((n_seq, CONV_HIST, width), F32)],
        scratch_shapes=[pltpu.VMEM((nb, CONV_PAD + tl + SUBLANES, width), F32)],
        compiler_params=_cparams(("parallel", "arbitrary")),
        name="conv_branch",
    )(glu, state, w, b, g, bb)


def _dotb(a, b):
    return jnp.dot(a, b, preferred_element_type=F32)


def _dotb_nt(a, b):
    return lax.dot_general(a, b, (((1,), (1,)), ((), ())), preferred_element_type=F32)


def _b(x):
    return x.astype(BF16)


def _rwkv_kernel(p_ref, sh_ref, wkv_ref, mu_ref, w0_ref, w2a2_ref, a0_ref, g2_ref, kk_ref, ka_ref, rk_ref,
                 lng_ref, lnb_ref, o_ref, nsh_ref, nwkv_ref, s_scr, prev_scr, *, nsb, tb, bw, lw):
    c = pl.program_id(1)
    n_pairs = bw // PAIR
    C = CHUNK
    R = nsb * tb
    nq = R // C
    ts = min(tb, C)
    ns = C // ts
    assert HEAD == C and nq * C == R and (nsb == 1 or tb == ts)
    ts_bits = _log2(ts)
    sb = min(SOLVE_BLOCK, ts)
    sb_bits = _log2(sb)

    @pl.when(c == 0)
    def _():
        prev_scr[...] = sh_ref[...]
        s_scr[...] = jnp.zeros_like(s_scr)
        for j in range(nsb):
            for h in range(2 * n_pairs):
                lo = (h % 2) * HEAD
                s_scr[j, h // 2, lo:lo + HEAD, lo:lo + HEAD] = wkv_ref[j, h]

    p = p_ref[...]
    pw = p.shape[1]
    row = lax.broadcasted_iota(jnp.int32, (R, 1), 0)
    shifted = pltpu.roll(p, 1, 0)
    if nsb == 1:
        xm = p + mu_ref[...] * (shifted - p)
        head = p[0:SUBLANES]
        prev_h = jnp.where(row[0:SUBLANES] == 0, prev_scr[0], shifted[0:SUBLANES])
        xm = jnp.concatenate([head + mu_ref[...] * (prev_h - head), xm[SUBLANES:]], axis=0)
    else:
        first = jnp.concatenate([jnp.broadcast_to(prev_scr[j], (tb, pw)) for j in range(nsb)], axis=0)
        prev = jnp.where((row & (tb - 1)) == 0, first, shifted)
        xm = p + mu_ref[...] * (prev - p)
    for j in range(nsb):
        last = p[(j + 1) * tb - 1:(j + 1) * tb, :]
        prev_scr[j] = last
        nsh_ref[j] = last
    r = xm[:, 0:bw]
    k = xm[:, bw:2 * bw]
    v = xm[:, 2 * bw:3 * bw]
    dwa = xm[:, 3 * bw:3 * bw + 2 * lw]
    dg = xm[:, 3 * bw + 2 * lw:]
    lane_l = lax.broadcasted_iota(jnp.int32, (1, 2 * lw), 1)
    is_w = lane_l < lw
    w2a2 = _b(w2a2_ref[...])
    wpre = w0_ref[...] + _dotb(_b(jnp.where(is_w, jnp.tanh(dwa), 0.0)), w2a2)
    apre = a0_ref[...] + _dotb(_b(jnp.where(is_w, 0.0, dwa)), w2a2)
    logw = -DECAY_SCALE * _sigmoid(wpre)
    a_lr = _sigmoid(apre)
    gate = _dotb(_b(_sigmoid(dg)), _b(g2_ref[...]))
    kkv = k * kk_ref[...]
    k2 = k * (1.0 + (a_lr - 1.0) * ka_ref[...])
    rk2 = r * k2 * rk_ref[...]

    lane = lax.broadcasted_iota(jnp.int32, (1, PAIR), 1)
    m_l = lane < HEAD
    rowp = lax.broadcasted_iota(jnp.int32, (PAIR, PAIR), 0)
    colp = lax.broadcasted_iota(jnp.int32, (PAIR, PAIR), 1)
    bd = (rowp >= HEAD) == (colp >= HEAD)
    t_i = lax.broadcasted_iota(jnp.int32, (C, PAIR), 0)
    s_i = lax.broadcasted_iota(jnp.int32, (C, PAIR), 1) & (HEAD - 1)
    same = (s_i >> ts_bits) == (t_i >> ts_bits)
    strict = same & (s_i < t_i)
    incl = same & (s_i <= t_i)
    eye = (s_i == t_i).astype(F32)
    dblk = (s_i >> sb_bits) == (t_i >> sb_bits)

    def vs(y):
        return jnp.concatenate([jnp.where(m_l, y, 0.0), jnp.where(m_l, 0.0, y)], axis=0)

    def pm(a, y):
        return _dotb(_b(a), vs(_b(y)))

    def pmm(lhs, y):
        out = _dotb(_b(jnp.concatenate(lhs, axis=0)), vs(_b(y)))
        return [out[k * C:(k + 1) * C] for k in range(len(lhs))]

    def bdsum(xs):
        out = []
        for x in xs:
            s0 = jnp.sum(jnp.where(m_l, x, 0.0), axis=1, keepdims=True)
            s1 = jnp.sum(jnp.where(m_l, 0.0, x), axis=1, keepdims=True)
            out.append(jnp.where(m_l, s0, s1))
        return out

    P = range(n_pairs)
    units = [(q, i) for q in range(nq) for i in P]
    NU = range(len(units))

    def blk(x, n):
        q, i = units[n]
        return x[q * C:(q + 1) * C, i * PAIR:(i + 1) * PAIR]

    def neumann(m, n_terms):
        t = [eye + m[n] for n in NU]
        levels = _log2(n_terms)
        if levels == 1:
            return t
        p = [pm(m[n], m[n]) for n in NU]
        for _ in range(levels - 2):
            both = [pmm([p[n], t[n]], p[n]) for n in NU]
            t = [t[n] + both[n][1] for n in NU]
            p = [both[n][0] for n in NU]
        return [t[n] + pm(t[n], p[n]) for n in NU]

    t_c = lax.broadcasted_iota(jnp.int32, (C, C), 0)
    s_c = lax.broadcasted_iota(jnp.int32, (C, C), 1)
    tri_b = (((s_c >> ts_bits) == (t_c >> ts_bits)) & (s_c <= t_c)).astype(BF16)
    tri2 = jnp.concatenate([tri_b, tri_b], axis=1)
    lw_hi = _b(logw)
    lw_lo = _b(logw - lw_hi.astype(F32))
    cs = jnp.concatenate(
        [_dotb(tri2, jnp.concatenate([lw_hi[q * C:(q + 1) * C], lw_lo[q * C:(q + 1) * C]], axis=0))
         for q in range(nq)], axis=0)
    g_in = jnp.exp(cs)
    g_ex = jnp.exp(cs - logw)
    g_inv = jnp.exp(-cs)
    runs = g_in.reshape(R // ts, ts, bw)
    g_end = jnp.broadcast_to(runs[:, ts - 1:ts, :], runs.shape).reshape(R, bw)

    sums = bdsum([blk(kkv, n) * blk(kkv, n) for n in NU] + [blk(rk2, n) for n in NU])
    kk_n = [blk(kkv, n) * lax.rsqrt(jnp.maximum(sums[n], KK_NORM_FLOOR * KK_NORM_FLOOR)) for n in NU]
    v_p = [blk(v, n) for n in NU]
    at = [-kk_n[n] * blk(g_ex, n) for n in NU]
    rt = [blk(r, n) * blk(g_in, n) for n in NU]
    bt = [kk_n[n] * blk(a_lr, n) * blk(g_inv, n) for n in NU]
    kt = [blk(k2, n) * blk(g_inv, n) for n in NU]
    sm = [_dotb_nt(_b(jnp.concatenate([at[n], rt[n]], axis=0)),
                   jnp.concatenate([vs(_b(bt[n])), vs(_b(kt[n]))], axis=0)) for n in NU]
    lab = [jnp.where(strict, sm[n][0:C, 0:PAIR], 0.0) for n in NU]
    lak = [jnp.where(strict, sm[n][0:C, PAIR:2 * PAIR], 0.0) for n in NU]
    mrb = [jnp.where(incl, sm[n][C:2 * C, 0:PAIR], 0.0) for n in NU]
    mrk = [jnp.where(incl, sm[n][C:2 * C, PAIR:2 * PAIR], 0.0) for n in NU]
    d1 = [jnp.where(dblk, lab[n], 0.0) for n in NU]
    tinv = neumann(d1, sb)
    if sb < ts:
        fp = [pm(lab[n] - d1[n], tinv[n]) for n in NU]
        levels = _log2(ts // sb)
        for lev in range(levels):
            if lev < levels - 1:
                both = [pmm([fp[n], tinv[n]], fp[n]) for n in NU]
                tinv = [tinv[n] + both[n][1] for n in NU]
                fp = [both[n][0] for n in NU]
            else:
                tinv = [tinv[n] + pm(tinv[n], fp[n]) for n in NU]
    tinv = [_b(tinv[n]) for n in NU]
    xv = [pmm([lak[n], mrk[n]], v_p[n]) for n in NU]
    x0 = [xv[n][0] for n in NU]
    s_cur = [[s_scr[sq, i] for i in P] for sq in range(nsb)]
    col_seq = (lax.broadcasted_iota(jnp.int32, (1, PAIR), 1) & (HEAD - 1)) >> ts_bits

    def finish(q, o_q):
        un = [q * n_pairs + i for i in P]
        om = bdsum(o_q)
        oc = [o_q[i] - om[i] * (1.0 / HEAD) for i in P]
        ov = bdsum([oc[i] * oc[i] for i in P])
        for i in P:
            n = un[i]
            ln = slice(i * PAIR, (i + 1) * PAIR)
            bonus = sums[len(units) + n] * v_p[n]
            y = oc[i] * lax.rsqrt(ov[i] * (1.0 / HEAD) + GN_EPS) * lng_ref[:, ln] + lnb_ref[:, ln] + bonus
            o_ref[q * C:(q + 1) * C, ln] = (y * blk(gate, n)).astype(o_ref.dtype)

    o_prev = None
    for q in range(nq):
        un = [q * n_pairs + i for i in P]
        seq = [q * ns + j if nsb > 1 else 0 for j in range(ns)]
        a_s, r_s = [], []
        for i in P:
            parts = []
            for j in range(ns):
                rows = slice(j * ts, (j + 1) * ts)
                lhs = _b(jnp.concatenate([at[un[i]][rows], rt[un[i]][rows]], axis=0))
                parts.append(_dotb_nt(lhs, _b(s_cur[seq[j]][i])))
            a_s.append(jnp.concatenate([x[0:ts] for x in parts], axis=0))
            r_s.append(jnp.concatenate([x[ts:2 * ts] for x in parts], axis=0))
        if o_prev is not None:
            finish(q - 1, o_prev)
        u = [_dotb(tinv[un[i]], vs(_b(a_s[i] + x0[un[i]]))) for i in P]
        for i in P:
            n = un[i]
            ge = blk(g_end, n)
            uv = _b(jnp.concatenate([u[i], v_p[n]], axis=0).T)
            bk = _b(jnp.concatenate([bt[n] * ge, kt[n] * ge], axis=0))
            for j in range(ns):
                uv_j = uv if ns == 1 else jnp.where(col_seq == j, uv, 0.0)
                upd = _dotb(uv_j, bk)
                s_cur[seq[j]][i] = s_cur[seq[j]][i] * ge[j * ts:j * ts + 1, :] + jnp.where(bd, upd, 0.0)
        o_prev = [r_s[i] + pm(mrb[un[i]], u[i]) + xv[un[i]][1] for i in P]
    for sq in range(nsb):
        for i in P:
            s_scr[sq, i] = s_cur[sq][i]
    finish(nq - 1, o_prev)

    @pl.when(c == pl.num_programs(1) - 1)
    def _():
        for j in range(nsb):
            for h in range(2 * n_pairs):
                lo = (h % 2) * HEAD
                nwkv_ref[j, h] = s_scr[j, h // 2, lo:lo + HEAD, lo:lo + HEAD]


def _rwkv_branch(proj_b, shift, wkv, prm, n_seq, seq_len, nsb, tb):
    T, pw = proj_b.shape
    n_heads = wkv.shape[1]
    bw = n_heads * HEAD
    lw = prm["w2a2"].shape[0] // 2
    nc = seq_len // tb
    assert nsb == 1 or nc == 1
    rows = nsb * tb

    def full(a):
        return pl.BlockSpec(a.shape, lambda n, c: (0,) * a.ndim)

    names = ("mu", "w0", "w2a2", "a0", "g2", "kk", "ka", "rk", "lng", "lnb")
    return pl.pallas_call(
        functools.partial(_rwkv_kernel, nsb=nsb, tb=tb, bw=bw, lw=lw),
        grid=(n_seq // nsb, nc),
        in_specs=[pl.BlockSpec((rows, pw), lambda n, c: (n * nc + c, 0)),
                  pl.BlockSpec((nsb, 1, pw), lambda n, c: (n, 0, 0)),
                  pl.BlockSpec((nsb, n_heads, HEAD, HEAD), lambda n, c: (n, 0, 0, 0))]
                 + [full(prm[k]) for k in names],
        out_specs=[pl.BlockSpec((rows, bw), lambda n, c: (n * nc + c, 0)),
                   pl.BlockSpec((nsb, 1, pw), lambda n, c: (n, 0, 0)),
                   pl.BlockSpec((nsb, n_heads, HEAD, HEAD), lambda n, c: (n, 0, 0, 0))],
        out_shape=[jax.ShapeDtypeStruct((T, bw), BF16),
                   jax.ShapeDtypeStruct((n_seq, 1, pw), F32),
                   jax.ShapeDtypeStruct((n_seq, n_heads, HEAD, HEAD), F32)],
        scratch_shapes=[pltpu.VMEM((nsb, n_heads // 2, PAIR, PAIR), F32), pltpu.VMEM((nsb, 1, pw), F32)],
        compiler_params=_cparams(("parallel", "arbitrary")),
        name="rwkv_branch",
    )(proj_b, shift.reshape(n_seq, 1, pw), wkv, *[prm[k] for k in names])


def _outproj_kernel(a_ref, b_ref, x_ref, wa_ref, wb_ref, g_ref, bb_ref, o_ref, *, alpha, sub):
    for r0 in range(0, o_ref.shape[0], sub):
        rows = slice(r0, r0 + sub)
        mix = jnp.dot(a_ref[rows, :].astype(BF16), wa_ref[...], preferred_element_type=F32)
        mix = mix + jnp.dot(b_ref[rows, :].astype(BF16), wb_ref[...], preferred_element_type=F32)
        o_ref[rows, :] = _ln(alpha * x_ref[rows, :] + mix, g_ref[...], bb_ref[...], LN_EPS)


def _outproj(a, b, x, wa, wb, g, bb, alpha, tm):
    T, D = x.shape
    wdt = a.shape[1]
    vec = pl.BlockSpec((1, D), lambda i: (0, 0))
    return pl.pallas_call(
        functools.partial(_outproj_kernel, alpha=alpha, sub=tm // 4),
        grid=(T // tm,),
        in_specs=[pl.BlockSpec((tm, wdt), lambda i: (i, 0)),
                  pl.BlockSpec((tm, wdt), lambda i: (i, 0)),
                  pl.BlockSpec((tm, D), lambda i: (i, 0)),
                  pl.BlockSpec((wdt, D), lambda i: (0, 0)),
                  pl.BlockSpec((wdt, D), lambda i: (0, 0)),
                  vec, vec],
        out_specs=pl.BlockSpec((tm, D), lambda i: (i, 0)),
        out_shape=jax.ShapeDtypeStruct((T, D), F32),
        compiler_params=_cparams(("parallel",)),
        name="out_proj_ln",
    )(a, b, x, wa, wb, g, bb)


def _ffn_kernel(x_ref, wg_ref, wu_ref, wd_ref, g_ref, bb_ref, o_ref, acc_ref, xb_ref, wg_s, wu_s, wd_s,
                *, alpha, ln_groups):
    i = pl.program_id(0)
    j = pl.program_id(1)

    @pl.when(i == 0)
    def _():
        wg_s[j] = wg_ref[...].astype(BF16)
        wu_s[j] = wu_ref[...].astype(BF16)
        wd_s[j] = wd_ref[...].astype(BF16)

    def term(xb):
        hg = jnp.dot(xb, wg_s[j], preferred_element_type=F32)
        hu = jnp.dot(xb, wu_s[j], preferred_element_type=F32)
        h = (hg * _sigmoid(hg) * hu).astype(BF16)
        return jnp.dot(h, wd_s[j], preferred_element_type=F32)

    def partial_sum(rows):
        return acc_ref[rows, :] + term(xb_ref[rows, :])

    last = pl.num_programs(1) - 1
    tm = o_ref.shape[0]

    @pl.when(j == 0)
    def _():
        for r0 in range(0, tm, tm // ln_groups):
            rows = slice(r0, r0 + tm // ln_groups)
            xb = x_ref[rows, :].astype(BF16)
            xb_ref[rows, :] = xb
            acc_ref[rows, :] = term(xb)

    @pl.when((j > 0) & (j < last))
    def _():
        acc_ref[...] = partial_sum(slice(None))

    @pl.when(j == last)
    def _():
        for r0 in range(0, tm, tm // ln_groups):
            rows = slice(r0, r0 + tm // ln_groups)
            o_ref[rows, :] = _ln(alpha * x_ref[rows, :] + partial_sum(rows), g_ref[...], bb_ref[...], LN_EPS)


def _ffn(x, wg, wu, wd, layer, g, bb, alpha, tm, tf):
    T, D = x.shape
    Fd = wg.shape[2]
    nj = Fd // tf
    assert nj >= 2
    vec = pl.BlockSpec((1, D), lambda i, j: (0, 0))

    def wj(i, j):
        return jnp.where(i == 0, j, nj - 1)

    return pl.pallas_call(
        functools.partial(_ffn_kernel, alpha=alpha, ln_groups=2),
        grid=(T // tm, nj),
        in_specs=[pl.BlockSpec((tm, D), lambda i, j: (i, 0)),
                  pl.BlockSpec((None, D, tf), lambda i, j: (layer, 0, wj(i, j))),
                  pl.BlockSpec((None, D, tf), lambda i, j: (layer, 0, wj(i, j))),
                  pl.BlockSpec((None, tf, D), lambda i, j: (layer, wj(i, j), 0)),
                  vec, vec],
        out_specs=pl.BlockSpec((tm, D), lambda i, j: (i, 0)),
        out_shape=jax.ShapeDtypeStruct((T, D), F32),
        scratch_shapes=[pltpu.VMEM((tm, D), F32), pltpu.VMEM((tm, D), BF16),
                        pltpu.VMEM((nj, D, tf), BF16), pltpu.VMEM((nj, D, tf), BF16), pltpu.VMEM((nj, tf, D), BF16)],
        compiler_params=_cparams(("arbitrary", "arbitrary")),
        name="ffn_ln",
    )(x, wg, wu, wd, g, bb)


def _pool_kernel(x_ref, st_ref, w_ref, sc_ref, g_ref, bb_ref, o_ref, nst_ref, ext_ref, *, tl, start, alpha):
    c = pl.program_id(1)

    x = x_ref[...]
    nb, _, D = x.shape
    gw = D // len(POOL_WINDOWS)
    n_ext = POOL_PAD + tl

    @pl.when(c == 0)
    def _():
        ext_ref[:, 0:POOL_PAD - POOL_HIST, :] = jnp.zeros((nb, POOL_PAD - POOL_HIST, D), F32)
        ext_ref[:, POOL_PAD - POOL_HIST:POOL_PAD, :] = st_ref[...]

    ext_ref[:, POOL_PAD:n_ext, :] = x
    pos = start + c * tl + lax.broadcasted_iota(jnp.int32, (1, tl, 1), 1)
    outs = []
    for gi, wdw in enumerate(POOL_WINDOWS):
        lo = gi * gw
        acc = ext_ref[:, 1:n_ext, lo:lo + gw] + ext_ref[:, 0:n_ext - 1, lo:lo + gw]
        h = 2
        while h < wdw:
            rows = acc.shape[1]
            acc = acc[:, h:rows, :] + acc[:, 0:rows - h, :]
            h *= 2
        first = POOL_PAD + 1 - wdw
        ssum = acc[:, first:first + tl, :]
        cnt = jnp.minimum(wdw, pos + 1).astype(F32)
        pooled = ssum / cnt - x[..., lo:lo + gw]
        y = jnp.dot(pooled.reshape(nb * tl, gw).astype(BF16), w_ref[gi], preferred_element_type=F32)
        outs.append(y.reshape(nb, tl, gw))
    y = jnp.concatenate(outs, axis=-1) * sc_ref[...]
    o_ref[...] = _ln(alpha * x + y, g_ref[...], bb_ref[...], LN_EPS)
    tail = ext_ref[:, tl + POOL_PAD - POOL_HIST:tl + POOL_PAD, :]
    ext_ref[:, POOL_PAD - POOL_HIST:POOL_PAD, :] = tail
    nst_ref[...] = tail


def _pool_mix(x, state, w, scale, g, bb, nb, tl, start, alpha):
    n_seq, seq_len, D = x.shape
    vec = pl.BlockSpec((1, D), lambda n, c: (0, 0))
    return pl.pallas_call(
        functools.partial(_pool_kernel, tl=tl, start=start, alpha=alpha),
        grid=(n_seq // nb, seq_len // tl),
        in_specs=[pl.BlockSpec((nb, tl, D), lambda n, c: (n, c, 0)),
                  pl.BlockSpec((nb, POOL_HIST, D), lambda n, c: (n, 0, 0)),
                  pl.BlockSpec(w.shape, lambda n, c: (0, 0, 0)),
                  vec, vec, vec],
        out_specs=[pl.BlockSpec((nb, tl, D), lambda n, c: (n, c, 0)),
                   pl.BlockSpec((nb, POOL_HIST, D), lambda n, c: (n, 0, 0))],
        out_shape=[jax.ShapeDtypeStruct((n_seq, seq_len, D), F32),
                   jax.ShapeDtypeStruct((n_seq, POOL_HIST, D), F32)],
        scratch_shapes=[pltpu.VMEM((nb, POOL_PAD + tl, D), F32)],
        compiler_params=_cparams(("parallel", "arbitrary")),
        name="pool_mix_ln",
    )(x, state, w, scale, g, bb)


def _trunk(x3, conv_prev, shift_prev, wkv_prev, pool_prev, start, prm, tiles):
    n_seq, seq_len, D = x3.shape
    T = n_seq * seq_len
    x = x3.reshape(T, D)
    alpha = prm["alpha"]
    tm = tiles["tm"]
    glu, proj_b = _inproj(x, prm["w_in_a"], prm["w_in_b"], tm)
    a_out, new_conv = _conv_branch(glu.reshape(n_seq, seq_len, -1), conv_prev, prm["conv_w"], prm["conv_b"],
                                   prm["conv_ln_g"], prm["conv_ln_b"], tiles["seq_nb"], tiles["conv_tl"])
    b_out, new_shift, new_wkv = _rwkv_branch(proj_b, shift_prev, wkv_prev, prm["rwkv"], n_seq, seq_len,
                                             tiles["rwkv_nsb"], tiles["rwkv_tb"])
    x = _outproj(a_out.reshape(T, -1), b_out, x, prm["w_out_a"], prm["w_out_b"], prm["ln_mix_g"][0],
                 prm["ln_mix_b"][0], alpha, tm)
    x = _ffn(x, prm["ffn_gate"], prm["ffn_up"], prm["ffn_down"], 0, prm["ln_ffn_g"][0], prm["ln_ffn_b"][0],
             alpha, tm, tiles["tf"])
    x, new_pool = _pool_mix(x.reshape(n_seq, seq_len, D), pool_prev, prm["pool_w"], prm["pool_scale"],
                            prm["ln_mix_g"][1], prm["ln_mix_b"][1], tiles["seq_nb"], tiles["pool_tl"], start, alpha)
    x = _ffn(x.reshape(T, D), prm["ffn_gate"], prm["ffn_up"], prm["ffn_down"], 1, prm["ln_ffn_g"][1],
             prm["ln_ffn_b"][1], alpha, tm, tiles["tf"])
    return (x.reshape(n_seq, seq_len, D), new_conv[None], new_shift.reshape(1, n_seq, -1), new_wkv[None],
            new_pool[None])


def kernel(x_prompt, x_sample, state_conv, state_shift, state_wkv, state_pool, w_in, conv_w, conv_b, conv_ln_g, conv_ln_b, rwkv_mu, rwkv_w0, rwkv_w2, rwkv_a0, rwkv_a2, rwkv_g2, rwkv_kk, rwkv_ka, rwkv_rk, rwkv_lnx_g, rwkv_lnx_b, w_out, pool_w, pool_scale, ln_mix_g, ln_mix_b, ffn_gate, ffn_up, ffn_down, ln_ffn_g, ln_ffn_b):
    depth = ln_mix_g.shape[0]
    assert depth == 2 and w_in.shape[0] == 1 and pool_w.shape[0] == 1
    B, L, D = x_prompt.shape
    Bs, Ls, _ = x_sample.shape
    a_w = conv_w.shape[2]
    b_w = rwkv_w0.shape[1]
    n_heads = b_w // HEAD
    alpha = float((2 * depth) ** 0.25)

    row = lambda a: a.reshape(1, -1)
    prm = dict(
        alpha=alpha,
        w_in_a=w_in[0, :, :2 * a_w].astype(BF16), w_in_b=w_in[0, :, 2 * a_w:].astype(BF16),
        conv_w=conv_w[0], conv_b=row(conv_b[0]), conv_ln_g=row(conv_ln_g[0]), conv_ln_b=row(conv_ln_b[0]),
        rwkv=dict(mu=row(rwkv_mu[0]), w0=row(rwkv_w0[0]),
                  w2a2=jnp.concatenate([rwkv_w2[0], rwkv_a2[0]], axis=0), a0=row(rwkv_a0[0]), g2=rwkv_g2[0],
                  kk=row(rwkv_kk[0]), ka=row(rwkv_ka[0]), rk=row(rwkv_rk[0]),
                  lng=row(rwkv_lnx_g[0]), lnb=row(rwkv_lnx_b[0])),
        w_out_a=w_out[0, :a_w].astype(BF16), w_out_b=w_out[0, a_w:].astype(BF16),
        pool_w=pool_w[0].astype(BF16), pool_scale=row(pool_scale[0]),
        ln_mix_g=[row(ln_mix_g[i]) for i in range(depth)], ln_mix_b=[row(ln_mix_b[i]) for i in range(depth)],
        ffn_gate=ffn_gate, ffn_up=ffn_up, ffn_down=ffn_down,
        ln_ffn_g=[row(ln_ffn_g[i]) for i in range(depth)], ln_ffn_b=[row(ln_ffn_b[i]) for i in range(depth)],
    )
    dt = x_prompt.dtype
    z_conv = jnp.zeros((B, CONV_HIST, a_w), dt)
    z_shift = jnp.zeros((B, state_shift.shape[-1]), dt)
    z_wkv = jnp.zeros((B, n_heads, HEAD, HEAD), state_wkv.dtype)
    z_pool = jnp.zeros((B, POOL_HIST, D), dt)
    tiles_p = dict(tm=1024, tf=256, seq_nb=1, conv_tl=512, pool_tl=1024, rwkv_nsb=1, rwkv_tb=8 * CHUNK)
    tiles_s = dict(tm=1024, tf=256, seq_nb=64, conv_tl=Ls, pool_tl=Ls, rwkv_nsb=4 * CHUNK // Ls, rwkv_tb=Ls)
    yp = _trunk(x_prompt, z_conv, z_shift, z_wkv, z_pool, 0, prm, tiles_p)
    ys = _trunk(x_sample, state_conv[0], state_shift[0], state_wkv[0], state_pool[0], PAST_LEN, prm, tiles_s)
    return (yp[0], ys[0], yp[1], yp[2], yp[3], yp[4], ys[1], ys[2], ys[3], ys[4])
```

```python
import functools

import jax
import jax.numpy as jnp
from jax import lax
from jax.experimental import pallas as pl
from jax.experimental.pallas import tpu as pltpu

F32 = jnp.float32
BF16 = jnp.bfloat16

HEAD = 64
PAIR = 2 * HEAD
CHUNK = 64
SOLVE_BLOCK = 16
SUBLANES = 8
CONV_TAPS = 31
CONV_HIST = CONV_TAPS - 1
CONV_PAD = 32
POOL_WINDOWS = (2, 4, 8, 16)
POOL_HIST = max(POOL_WINDOWS) - 1
POOL_PAD = 16
PAST_LEN = 16384
LN_EPS = 1e-5
GN_EPS = 64e-5
DECAY_SCALE = 0.6065306597126334
KK_NORM_FLOOR = 1e-12
VMEM_LIMIT = 56 * 1024 * 1024


def _ln(z, g, b, eps):
    mu = jnp.mean(z, axis=-1, keepdims=True)
    zc = z - mu
    var = jnp.mean(zc * zc, axis=-1, keepdims=True)
    return zc * lax.rsqrt(var + eps) * g + b


def _sigmoid(x):
    return 1.0 / (1.0 + jnp.exp(-x))


def _cparams(sem):
    return pltpu.CompilerParams(dimension_semantics=sem, vmem_limit_bytes=VMEM_LIMIT)


def _log2(n):
    assert n & (n - 1) == 0
    return n.bit_length() - 1


def _inproj_kernel(x_ref, wa_ref, wb_ref, oa_ref, ob_ref):
    xb = x_ref[...].astype(BF16)
    pa = jnp.dot(xb, wa_ref[...], preferred_element_type=F32)
    width = pa.shape[1] // 2
    oa_ref[...] = pa[:, :width] * _sigmoid(pa[:, width:])
    ob_ref[...] = jnp.dot(xb, wb_ref[...], preferred_element_type=F32)


def _inproj(x, wa, wb, tm):
    T, D = x.shape
    na, nb = wa.shape[1] // 2, wb.shape[1]
    return pl.pallas_call(
        _inproj_kernel,
        grid=(T // tm,),
        in_specs=[pl.BlockSpec((tm, D), lambda i: (i, 0)),
                  pl.BlockSpec(wa.shape, lambda i: (0, 0)),
                  pl.BlockSpec((D, nb), lambda i: (0, 0))],
        out_specs=[pl.BlockSpec((tm, na), lambda i: (i, 0)),
                   pl.BlockSpec((tm, nb), lambda i: (i, 0))],
        out_shape=[jax.ShapeDtypeStruct((T, na), F32), jax.ShapeDtypeStruct((T, nb), F32)],
        compiler_params=_cparams(("parallel",)),
        name="in_proj",
    )(x, wa, wb)


def _conv_kernel(u_ref, st_ref, w_ref, b_ref, g_ref, bb_ref, o_ref, nst_ref, ext_ref, *, tl, width):
    c = pl.program_id(1)

    base = CONV_PAD - CONV_HIST

    @pl.when(c == 0)
    def _():
        ext_ref[:, base:CONV_PAD, :] = st_ref[...]
        ext_ref[:, CONV_PAD + tl:CONV_PAD + tl + SUBLANES, :] = jnp.zeros((u_ref.shape[0], SUBLANES, width), F32)

    ext_ref[:, CONV_PAD:CONV_PAD + tl, :] = u_ref[...]
    acc = None
    for rho in range(SUBLANES):
        part = None
        for m in range((base + CONV_TAPS) // SUBLANES + 1):
            j = SUBLANES * m + rho - base
            if 0 <= j < CONV_TAPS:
                term = ext_ref[:, SUBLANES * m:SUBLANES * m + tl + SUBLANES, :] * w_ref[j:j + 1, :]
                part = term if part is None else part + term
        shifted = part[:, rho:rho + tl, :]
        acc = shifted if acc is None else acc + shifted
    acc = acc + b_ref[...]
    y = _ln(acc, g_ref[...], bb_ref[...], LN_EPS)
    o_ref[...] = (y * _sigmoid(y)).astype(o_ref.dtype)
    tail = ext_ref[:, tl + CONV_PAD - CONV_HIST:tl + CONV_PAD, :]
    ext_ref[:, CONV_PAD - CONV_HIST:CONV_PAD, :] = tail
    nst_ref[...] = tail


def _conv_branch(glu, state, w, b, g, bb, nb, tl):
    n_seq, seq_len, width = glu.shape
    vec = pl.BlockSpec((1, width), lambda n, c: (0, 0))
    return pl.pallas_call(
        functools.partial(_conv_kernel, tl=tl, width=width),
        grid=(n_seq // nb, seq_len // tl),
        in_specs=[pl.BlockSpec((nb, tl, width), lambda n, c: (n, c, 0)),
                  pl.BlockSpec((nb, CONV_HIST, width), lambda n, c: (n, 0, 0)),
                  pl.BlockSpec((CONV_TAPS, width), lambda n, c: (0, 0)),
                  vec, vec, vec],
        out_specs=[pl.BlockSpec((nb, tl, width), lambda n, c: (n, c, 0)),
                   pl.BlockSpec((nb, CONV_HIST, width), lambda n, c: (n, 0, 0))],
        out_shape=[jax.ShapeDtypeStruct((n_seq, seq_len, width), BF16 if tl % (2 * SUBLANES) == 0 else F32),
                   jax.ShapeDtypeStruct((n_seq, CONV_HIST, width), F32)],
        scratch_shapes=[pltpu.VMEM((nb, CONV_PAD + tl + SUBLANES, width), F32)],
        compiler_params=_cparams(("parallel", "arbitrary")),
        name="conv_branch",
    )(glu, state, w, b, g, bb)


def _dotb(a, b):
    return jnp.dot(a, b, preferred_element_type=F32)


def _dotb_nt(a, b):
    return lax.dot_general(a, b, (((1,), (1,)), ((), ())), preferred_element_type=F32)


def _b(x):
    return x.astype(BF16)


def _rwkv_kernel(p_ref, sh_ref, wkv_ref, mu_ref, w0_ref, w2a2_ref, a0_ref, g2_ref, kk_ref, ka_ref, rk_ref,
                 lng_ref, lnb_ref, o_ref, nsh_ref, nwkv_ref, s_scr, prev_scr, *, nsb, tb, bw, lw):
    c = pl.program_id(1)
    n_pairs = bw // PAIR
    C = CHUNK
    R = nsb * tb
    nq = R // C
    ts = min(tb, C)
    ns = C // ts
    assert HEAD == C and nq * C == R and (nsb == 1 or tb == ts)
    ts_bits = _log2(ts)
    sb = min(SOLVE_BLOCK, ts)
    sb_bits = _log2(sb)

    @pl.when(c == 0)
    def _():
        prev_scr[...] = sh_ref[...]
        s_scr[...] = jnp.zeros_like(s_scr)
        for j in range(nsb):
            for h in range(2 * n_pairs):
                lo = (h % 2) * HEAD
                s_scr[j, h // 2, lo:lo + HEAD, lo:lo + HEAD] = wkv_ref[j, h]

    p = p_ref[...]
    pw = p.shape[1]
    row = lax.broadcasted_iota(jnp.int32, (R, 1), 0)
    shifted = pltpu.roll(p, 1, 0)
    if nsb == 1:
        xm = p + mu_ref[...] * (shifted - p)
        head = p[0:SUBLANES]
        prev_h = jnp.where(row[0:SUBLANES] == 0, prev_scr[0], shifted[0:SUBLANES])
        xm = jnp.concatenate([head + mu_ref[...] * (prev_h - head), xm[SUBLANES:]], axis=0)
    else:
        first = jnp.concatenate([jnp.broadcast_to(prev_scr[j], (tb, pw)) for j in range(nsb)], axis=0)
        prev = jnp.where((row & (tb - 1)) == 0, first, shifted)
        xm = p + mu_ref[...] * (prev - p)
    for j in range(nsb):
        last = p[(j + 1) * tb - 1:(j + 1) * tb, :]
        prev_scr[j] = last
        nsh_ref[j] = last
    r = xm[:, 0:bw]
    k = xm[:, bw:2 * bw]
    v = xm[:, 2 * bw:3 * bw]
    dwa = xm[:, 3 * bw:3 * bw + 2 * lw]
    dg = xm[:, 3 * bw + 2 * lw:]
    lane_l = lax.broadcasted_iota(jnp.int32, (1, 2 * lw), 1)
    is_w = lane_l < lw
    w2a2 = _b(w2a2_ref[...])
    wpre = w0_ref[...] + _dotb(_b(jnp.where(is_w, jnp.tanh(dwa), 0.0)), w2a2)
    apre = a0_ref[...] + _dotb(_b(jnp.where(is_w, 0.0, dwa)), w2a2)
    logw = -DECAY_SCALE * _sigmoid(wpre)
    a_lr = _sigmoid(apre)
    gate = _dotb(_b(_sigmoid(dg)), _b(g2_ref[...]))
    kkv = k * kk_ref[...]
    k2 = k * (1.0 + (a_lr - 1.0) * ka_ref[...])
    rk2 = r * k2 * rk_ref[...]

    lane = lax.broadcasted_iota(jnp.int32, (1, PAIR), 1)
    m_l = lane < HEAD
    rowp = lax.broadcasted_iota(jnp.int32, (PAIR, PAIR), 0)
    colp = lax.broadcasted_iota(jnp.int32, (PAIR, PAIR), 1)
    bd = (rowp >= HEAD) == (colp >= HEAD)
    t_i = lax.broadcasted_iota(jnp.int32, (C, PAIR), 0)
    s_i = lax.broadcasted_iota(jnp.int32, (C, PAIR), 1) & (HEAD - 1)
    same = (s_i >> ts_bits) == (t_i >> ts_bits)
    strict = same & (s_i < t_i)
    incl = same & (s_i <= t_i)
    eye = (s_i == t_i).astype(F32)
    dblk = (s_i >> sb_bits) == (t_i >> sb_bits)

    def vs(y):
        return jnp.concatenate([jnp.where(m_l, y, 0.0), jnp.where(m_l, 0.0, y)], axis=0)

    def pm(a, y):
        return _dotb(_b(a), vs(_b(y)))

    def pmm(lhs, y):
        out = _dotb(_b(jnp.concatenate(lhs, axis=0)), vs(_b(y)))
        return [out[k * C:(k + 1) * C] for k in range(len(lhs))]

    def bdsum(xs):
        out = []
        for x in xs:
            s0 = jnp.sum(jnp.where(m_l, x, 0.0), axis=1, keepdims=True)
            s1 = jnp.sum(jnp.where(m_l, 0.0, x), axis=1, keepdims=True)
            out.append(jnp.where(m_l, s0, s1))
        return out

    P = range(n_pairs)
    units = [(q, i) for q in range(nq) for i in P]
    NU = range(len(units))

    def blk(x, n):
        q, i = units[n]
        return x[q * C:(q + 1) * C, i * PAIR:(i + 1) * PAIR]

    def neumann(m, n_terms):
        t = [eye + m[n] for n in NU]
        levels = _log2(n_terms)
        if levels == 1:
            return t
        p = [pm(m[n], m[n]) for n in NU]
        for _ in range(levels - 2):
            both = [pmm([p[n], t[n]], p[n]) for n in NU]
            t = [t[n] + both[n][1] for n in NU]
            p = [both[n][0] for n in NU]
        return [t[n] + pm(t[n], p[n]) for n in NU]

    t_c = lax.broadcasted_iota(jnp.int32, (C, C), 0)
    s_c = lax.broadcasted_iota(jnp.int32, (C, C), 1)
    tri_b = (((s_c >> ts_bits) == (t_c >> ts_bits)) & (s_c <= t_c)).astype(BF16)
    tri2 = jnp.concatenate([tri_b, tri_b], axis=1)
    lw_hi = _b(logw)
    lw_lo = _b(logw - lw_hi.astype(F32))
    cs = jnp.concatenate(
        [_dotb(tri2, jnp.concatenate([lw_hi[q * C:(q + 1) * C], lw_lo[q * C:(q + 1) * C]], axis=0))
         for q in range(nq)], axis=0)
    g_in = jnp.exp(cs)
    g_ex = jnp.exp(cs - logw)
    g_inv = jnp.exp(-cs)
    runs = g_in.reshape(R // ts, ts, bw)
    g_end = jnp.broadcast_to(runs[:, ts - 1:ts, :], runs.shape).reshape(R, bw)

    sums = bdsum([blk(kkv, n) * blk(kkv, n) for n in NU] + [blk(rk2, n) for n in NU])
    kk_n = [blk(kkv, n) * lax.rsqrt(jnp.maximum(sums[n], KK_NORM_FLOOR * KK_NORM_FLOOR)) for n in NU]
    v_p = [blk(v, n) for n in NU]
    at = [-kk_n[n] * blk(g_ex, n) for n in NU]
    rt = [blk(r, n) * blk(g_in, n) for n in NU]
    bt = [kk_n[n] * blk(a_lr, n) * blk(g_inv, n) for n in NU]
    kt = [blk(k2, n) * blk(g_inv, n) for n in NU]
    sm = [_dotb_nt(_b(jnp.concatenate([at[n], rt[n]], axis=0)),
                   jnp.concatenate([vs(_b(bt[n])), vs(_b(kt[n]))], axis=0)) for n in NU]
    lab = [jnp.where(strict, sm[n][0:C, 0:PAIR], 0.0) for n in NU]
    lak = [jnp.where(strict, sm[n][0:C, PAIR:2 * PAIR], 0.0) for n in NU]
    mrb = [jnp.where(incl, sm[n][C:2 * C, 0:PAIR], 0.0) for n in NU]
    mrk = [jnp.where(incl, sm[n][C:2 * C, PAIR:2 * PAIR], 0.0) for n in NU]
    d1 = [jnp.where(dblk, lab[n], 0.0) for n in NU]
    tinv = neumann(d1, sb)
    if sb < ts:
        fp = [pm(lab[n] - d1[n], tinv[n]) for n in NU]
        levels = _log2(ts // sb)
        for lev in range(levels):
            if lev < levels - 1:
                both = [pmm([fp[n], tinv[n]], fp[n]) for n in NU]
                tinv = [tinv[n] + both[n][1] for n in NU]
                fp = [both[n][0] for n in NU]
            else:
                tinv = [tinv[n] + pm(tinv[n], fp[n]) for n in NU]
    tinv = [_b(tinv[n]) for n in NU]
    xv = [pmm([lak[n], mrk[n]], v_p[n]) for n in NU]
    x0 = [xv[n][0] for n in NU]
    s_cur = [[s_scr[sq, i] for i in P] for sq in range(nsb)]
    col_seq = (lax.broadcasted_iota(jnp.int32, (1, PAIR), 1) & (HEAD - 1)) >> ts_bits

    def finish(q, o_q):
        un = [q * n_pairs + i for i in P]
        om = bdsum(o_q)
        oc = [o_q[i] - om[i] * (1.0 / HEAD) for i in P]
        ov = bdsum([oc[i] * oc[i] for i in P])
        for i in P:
            n = un[i]
            ln = slice(i * PAIR, (i + 1) * PAIR)
            bonus = sums[len(units) + n] * v_p[n]
            y = oc[i] * lax.rsqrt(ov[i] * (1.0 / HEAD) + GN_EPS) * lng_ref[:, ln] + lnb_ref[:, ln] + bonus
            o_ref[q * C:(q + 1) * C, ln] = (y * blk(gate, n)).astype(o_ref.dtype)

    o_prev = None
    for q in range(nq):
        un = [q * n_pairs + i for i in P]
        seq = [q * ns + j if nsb > 1 else 0 for j in range(ns)]
        a_s, r_s = [], []
        for i in P:
            parts = []
            for j in range(ns):
                rows = slice(j * ts, (j + 1) * ts)
                lhs = _b(jnp.concatenate([at[un[i]][rows], rt[un[i]][rows]], axis=0))
                parts.append(_dotb_nt(lhs, _b(s_cur[seq[j]][i])))
            a_s.append(jnp.concatenate([x[0:ts] for x in parts], axis=0))
            r_s.append(jnp.concatenate([x[ts:2 * ts] for x in parts], axis=0))
        if o_prev is not None:
            finish(q - 1, o_prev)
        u = [_dotb(tinv[un[i]], vs(_b(a_s[i] + x0[un[i]]))) for i in P]
        for i in P:
            n = un[i]
            ge = blk(g_end, n)
            uv = _b(jnp.concatenate([u[i], v_p[n]], axis=0).T)
            bk = _b(jnp.concatenate([bt[n] * ge, kt[n] * ge], axis=0))
            for j in range(ns):
                uv_j = uv if ns == 1 else jnp.where(col_seq == j, uv, 0.0)
                upd = _dotb(uv_j, bk)
                s_cur[seq[j]][i] = s_cur[seq[j]][i] * ge[j * ts:j * ts + 1, :] + jnp.where(bd, upd, 0.0)
        o_prev = [r_s[i] + pm(mrb[un[i]], u[i]) + xv[un[i]][1] for i in P]
    for sq in range(nsb):
        for i in P:
            s_scr[sq, i] = s_cur[sq][i]
    finish(nq - 1, o_prev)

    @pl.when(c == pl.num_programs(1) - 1)
    def _():
        for j in range(nsb):
            for h in range(2 * n_pairs):
                lo = (h % 2) * HEAD
                nwkv_ref[j, h] = s_scr[j, h // 2, lo:lo + HEAD, lo:lo + HEAD]


def _rwkv_branch(proj_b, shift, wkv, prm, n_seq, seq_len, nsb, tb):
    T, pw = proj_b.shape
    n_heads = wkv.shape[1]
    bw = n_heads * HEAD
    lw = prm["w2a2"].shape[0] // 2
    nc = seq_len // tb
    assert nsb == 1 or nc == 1
    rows = nsb * tb

    def full(a):
        return pl.BlockSpec(a.shape, lambda n, c: (0,) * a.ndim)

    names = ("mu", "w0", "w2a2", "a0", "g2", "kk", "ka", "rk", "lng", "lnb")
    return pl.pallas_call(
        functools.partial(_rwkv_kernel, nsb=nsb, tb=tb, bw=bw, lw=lw),
        grid=(n_seq // nsb, nc),
        in_specs=[pl.BlockSpec((rows, pw), lambda n, c: (n * nc + c, 0)),
                  pl.BlockSpec((nsb, 1, pw), lambda n, c: (n, 0, 0)),
                  pl.BlockSpec((nsb, n_heads, HEAD, HEAD), lambda n, c: (n, 0, 0, 0))]
                 + [full(prm[k]) for k in names],
        out_specs=[pl.BlockSpec((rows, bw), lambda n, c: (n * nc + c, 0)),
                   pl.BlockSpec((nsb, 1, pw), lambda n, c: (n, 0, 0)),
                   pl.BlockSpec((nsb, n_heads, HEAD, HEAD), lambda n, c: (n, 0, 0, 0))],
        out_shape=[jax.ShapeDtypeStruct((T, bw), BF16),
                   jax.ShapeDtypeStruct((n_seq, 1, pw), F32),
                   jax.ShapeDtypeStruct((n_seq, n_heads, HEAD, HEAD), F32)],
        scratch_shapes=[pltpu.VMEM((nsb, n_heads // 2, PAIR, PAIR), F32), pltpu.VMEM((nsb, 1, pw), F32)],
        compiler_params=_cparams(("parallel", "arbitrary")),
        name="rwkv_branch",
    )(proj_b, shift.reshape(n_seq, 1, pw), wkv, *[prm[k] for k in names])


def _outproj_kernel(a_ref, b_ref, x_ref, wa_ref, wb_ref, g_ref, bb_ref, o_ref, *, alpha, sub):
    for r0 in range(0, o_ref.shape[0], sub):
        rows = slice(r0, r0 + sub)
        mix = jnp.dot(a_ref[rows, :].astype(BF16), wa_ref[...], preferred_element_type=F32)
        mix = mix + jnp.dot(b_ref[rows, :].astype(BF16), wb_ref[...], preferred_element_type=F32)
        o_ref[rows, :] = _ln(alpha * x_ref[rows, :] + mix, g_ref[...], bb_ref[...], LN_EPS)


def _outproj(a, b, x, wa, wb, g, bb, alpha, tm):
    T, D = x.shape
    wdt = a.shape[1]
    vec = pl.BlockSpec((1, D), lambda i: (0, 0))
    return pl.pallas_call(
        functools.partial(_outproj_kernel, alpha=alpha, sub=tm // 4),
        grid=(T // tm,),
        in_specs=[pl.BlockSpec((tm, wdt), lambda i: (i, 0)),
                  pl.BlockSpec((tm, wdt), lambda i: (i, 0)),
                  pl.BlockSpec((tm, D), lambda i: (i, 0)),
                  pl.BlockSpec((wdt, D), lambda i: (0, 0)),
                  pl.BlockSpec((wdt, D), lambda i: (0, 0)),
                  vec, vec],
        out_specs=pl.BlockSpec((tm, D), lambda i: (i, 0)),
        out_shape=jax.ShapeDtypeStruct((T, D), F32),
        compiler_params=_cparams(("parallel",)),
        name="out_proj_ln",
    )(a, b, x, wa, wb, g, bb)


def _ffn_kernel(x_ref, wg_ref, wu_ref, wd_ref, g_ref, bb_ref, o_ref, acc_ref, xb_ref, wg_s, wu_s, wd_s,
                *, alpha, ln_groups, cache):
    i = pl.program_id(0)
    j = pl.program_id(1)

    if cache:
        @pl.when(i == 0)
        def _():
            wg_s[j] = wg_ref[...].astype(BF16)
            wu_s[j] = wu_ref[...].astype(BF16)
            wd_s[j] = wd_ref[...].astype(BF16)

    def weights():
        if cache:
            return wg_s[j], wu_s[j], wd_s[j]
        return wg_ref[...].astype(BF16), wu_ref[...].astype(BF16), wd_ref[...].astype(BF16)

    def term(xb, w):
        hg = jnp.dot(xb, w[0], preferred_element_type=F32)
        hu = jnp.dot(xb, w[1], preferred_element_type=F32)
        h = (hg * _sigmoid(hg) * hu).astype(BF16)
        return jnp.dot(h, w[2], preferred_element_type=F32)

    last = pl.num_programs(1) - 1
    tm = o_ref.shape[0]

    @pl.when(j == 0)
    def _():
        w = weights()
        for r0 in range(0, tm, tm // ln_groups):
            rows = slice(r0, r0 + tm // ln_groups)
            xb = x_ref[rows, :].astype(BF16)
            xb_ref[rows, :] = xb
            acc_ref[rows, :] = term(xb, w)

    @pl.when((j > 0) & (j < last))
    def _():
        acc_ref[...] = acc_ref[...] + term(xb_ref[...], weights())

    @pl.when(j == last)
    def _():
        w = weights()
        for r0 in range(0, tm, tm // ln_groups):
            rows = slice(r0, r0 + tm // ln_groups)
            total = alpha * x_ref[rows, :] + (acc_ref[rows, :] + term(xb_ref[rows, :], w))
            o_ref[rows, :] = _ln(total, g_ref[...], bb_ref[...], LN_EPS)


def _ffn(x, wg, wu, wd, layer, g, bb, alpha, tm, tf):
    T, D = x.shape
    Fd = wg.shape[2]
    nj = Fd // tf
    assert nj >= 2
    vec = pl.BlockSpec((1, D), lambda i, j: (0, 0))
    cache = T // tm > 1
    nj_s = nj if cache else 1

    def wj(i, j):
        return jnp.where(i == 0, j, nj - 1)

    return pl.pallas_call(
        functools.partial(_ffn_kernel, alpha=alpha, ln_groups=2, cache=cache),
        grid=(T // tm, nj),
        in_specs=[pl.BlockSpec((tm, D), lambda i, j: (i, 0)),
                  pl.BlockSpec((None, D, tf), lambda i, j: (layer, 0, wj(i, j))),
                  pl.BlockSpec((None, D, tf), lambda i, j: (layer, 0, wj(i, j))),
                  pl.BlockSpec((None, tf, D), lambda i, j: (layer, wj(i, j), 0)),
                  vec, vec],
        out_specs=pl.BlockSpec((tm, D), lambda i, j: (i, 0)),
        out_shape=jax.ShapeDtypeStruct((T, D), F32),
        scratch_shapes=[pltpu.VMEM((tm, D), F32), pltpu.VMEM((tm, D), BF16),
                        pltpu.VMEM((nj_s, D, tf), BF16), pltpu.VMEM((nj_s, D, tf), BF16),
                        pltpu.VMEM((nj_s, tf, D), BF16)],
        compiler_params=_cparams(("arbitrary", "arbitrary")),
        name="ffn_ln",
    )(x, wg, wu, wd, g, bb)


def _pool_kernel(x_ref, st_ref, w_ref, sc_ref, g_ref, bb_ref, o_ref, nst_ref, ext_ref, *, tl, start, alpha):
    c = pl.program_id(1)

    x = x_ref[...]
    nb, _, D = x.shape
    gw = D // len(POOL_WINDOWS)
    n_ext = POOL_PAD + tl

    @pl.when(c == 0)
    def _():
        ext_ref[:, 0:POOL_PAD - POOL_HIST, :] = jnp.zeros((nb, POOL_PAD - POOL_HIST, D), F32)
        ext_ref[:, POOL_PAD - POOL_HIST:POOL_PAD, :] = st_ref[...]

    ext_ref[:, POOL_PAD:n_ext, :] = x
    pos = start + c * tl + lax.broadcasted_iota(jnp.int32, (1, tl, 1), 1)
    outs = []
    for gi, wdw in enumerate(POOL_WINDOWS):
        lo = gi * gw
        acc = ext_ref[:, 1:n_ext, lo:lo + gw] + ext_ref[:, 0:n_ext - 1, lo:lo + gw]
        h = 2
        while h < wdw:
            rows = acc.shape[1]
            acc = acc[:, h:rows, :] + acc[:, 0:rows - h, :]
            h *= 2
        first = POOL_PAD + 1 - wdw
        ssum = acc[:, first:first + tl, :]
        cnt = jnp.minimum(wdw, pos + 1).astype(F32)
        pooled = ssum / cnt - x[..., lo:lo + gw]
        y = jnp.dot(pooled.reshape(nb * tl, gw).astype(BF16), w_ref[gi], preferred_element_type=F32)
        outs.append(y.reshape(nb, tl, gw))
    y = jnp.concatenate(outs, axis=-1) * sc_ref[...]
    o_ref[...] = _ln(alpha * x + y, g_ref[...], bb_ref[...], LN_EPS)
    tail = ext_ref[:, tl + POOL_PAD - POOL_HIST:tl + POOL_PAD, :]
    ext_ref[:, POOL_PAD - POOL_HIST:POOL_PAD, :] = tail
    nst_ref[...] = tail


def _pool_mix(x, state, w, scale, g, bb, nb, tl, start, alpha):
    n_seq, seq_len, D = x.shape
    vec = pl.BlockSpec((1, D), lambda n, c: (0, 0))
    return pl.pallas_call(
        functools.partial(_pool_kernel, tl=tl, start=start, alpha=alpha),
        grid=(n_seq // nb, seq_len // tl),
        in_specs=[pl.BlockSpec((nb, tl, D), lambda n, c: (n, c, 0)),
                  pl.BlockSpec((nb, POOL_HIST, D), lambda n, c: (n, 0, 0)),
                  pl.BlockSpec(w.shape, lambda n, c: (0, 0, 0)),
                  vec, vec, vec],
        out_specs=[pl.BlockSpec((nb, tl, D), lambda n, c: (n, c, 0)),
                   pl.BlockSpec((nb, POOL_HIST, D), lambda n, c: (n, 0, 0))],
        out_shape=[jax.ShapeDtypeStruct((n_seq, seq_len, D), F32),
                   jax.ShapeDtypeStruct((n_seq, POOL_HIST, D), F32)],
        scratch_shapes=[pltpu.VMEM((nb, POOL_PAD + tl, D), F32)],
        compiler_params=_cparams(("parallel", "arbitrary")),
        name="pool_mix_ln",
    )(x, state, w, scale, g, bb)


def _trunk(x3, conv_prev, shift_prev, wkv_prev, pool_prev, start, prm, tiles):
    n_seq, seq_len, D = x3.shape
    T = n_seq * seq_len
    x = x3.reshape(T, D)
    alpha = prm["alpha"]
    tm = tiles["tm"]
    glu, proj_b = _inproj(x, prm["w_in_a"], prm["w_in_b"], tm)
    a_out, new_conv = _conv_branch(glu.reshape(n_seq, seq_len, -1), conv_prev, prm["conv_w"], prm["conv_b"],
                                   prm["conv_ln_g"], prm["conv_ln_b"], tiles["seq_nb"], tiles["conv_tl"])
    b_out, new_shift, new_wkv = _rwkv_branch(proj_b, shift_prev, wkv_prev, prm["rwkv"], n_seq, seq_len,
                                             tiles["rwkv_nsb"], tiles["rwkv_tb"])
    x = _outproj(a_out.reshape(T, -1), b_out, x, prm["w_out_a"], prm["w_out_b"], prm["ln_mix_g"][0],
                 prm["ln_mix_b"][0], alpha, tm)
    x = _ffn(x, prm["ffn_gate"], prm["ffn_up"], prm["ffn_down"], 0, prm["ln_ffn_g"][0], prm["ln_ffn_b"][0],
             alpha, tm, tiles["tf"])
    x, new_pool = _pool_mix(x.reshape(n_seq, seq_len, D), pool_prev, prm["pool_w"], prm["pool_scale"],
                            prm["ln_mix_g"][1], prm["ln_mix_b"][1], tiles["seq_nb"], tiles["pool_tl"], start, alpha)
    x = _ffn(x.reshape(T, D), prm["ffn_gate"], prm["ffn_up"], prm["ffn_down"], 1, prm["ln_ffn_g"][1],
             prm["ln_ffn_b"][1], alpha, tm, tiles["tf"])
    return (x.reshape(n_seq, seq_len, D), new_conv[None], new_shift.reshape(1, n_seq, -1), new_wkv[None],
            new_pool[None])


def kernel(x_prompt, x_sample, state_conv, state_shift, state_wkv, state_pool, w_in, conv_w, conv_b, conv_ln_g, conv_ln_b, rwkv_mu, rwkv_w0, rwkv_w2, rwkv_a0, rwkv_a2, rwkv_g2, rwkv_kk, rwkv_ka, rwkv_rk, rwkv_lnx_g, rwkv_lnx_b, w_out, pool_w, pool_scale, ln_mix_g, ln_mix_b, ffn_gate, ffn_up, ffn_down, ln_ffn_g, ln_ffn_b):
    depth = ln_mix_g.shape[0]
    assert depth == 2 and w_in.shape[0] == 1 and pool_w.shape[0] == 1
    B, L, D = x_prompt.shape
    Bs, Ls, _ = x_sample.shape
    a_w = conv_w.shape[2]
    b_w = rwkv_w0.shape[1]
    n_heads = b_w // HEAD
    alpha = float((2 * depth) ** 0.25)

    row = lambda a: a.reshape(1, -1)
    prm = dict(
        alpha=alpha,
        w_in_a=w_in[0, :, :2 * a_w].astype(BF16), w_in_b=w_in[0, :, 2 * a_w:].astype(BF16),
        conv_w=conv_w[0], conv_b=row(conv_b[0]), conv_ln_g=row(conv_ln_g[0]), conv_ln_b=row(conv_ln_b[0]),
        rwkv=dict(mu=row(rwkv_mu[0]), w0=row(rwkv_w0[0]),
                  w2a2=jnp.concatenate([rwkv_w2[0], rwkv_a2[0]], axis=0), a0=row(rwkv_a0[0]), g2=rwkv_g2[0],
                  kk=row(rwkv_kk[0]), ka=row(rwkv_ka[0]), rk=row(rwkv_rk[0]),
                  lng=row(rwkv_lnx_g[0]), lnb=row(rwkv_lnx_b[0])),
        w_out_a=w_out[0, :a_w].astype(BF16), w_out_b=w_out[0, a_w:].astype(BF16),
        pool_w=pool_w[0].astype(BF16), pool_scale=row(pool_scale[0]),
        ln_mix_g=[row(ln_mix_g[i]) for i in range(depth)], ln_mix_b=[row(ln_mix_b[i]) for i in range(depth)],
        ffn_gate=ffn_gate, ffn_up=ffn_up, ffn_down=ffn_down,
        ln_ffn_g=[row(ln_ffn_g[i]) for i in range(depth)], ln_ffn_b=[row(ln_ffn_b[i]) for i in range(depth)],
    )
    dt = x_prompt.dtype
    z_conv = jnp.zeros((B, CONV_HIST, a_w), dt)
    z_shift = jnp.zeros((B, state_shift.shape[-1]), dt)
    z_wkv = jnp.zeros((B, n_heads, HEAD, HEAD), state_wkv.dtype)
    z_pool = jnp.zeros((B, POOL_HIST, D), dt)
    tiles_p = dict(tm=1024, tf=256, seq_nb=1, conv_tl=512, pool_tl=1024, rwkv_nsb=1, rwkv_tb=8 * CHUNK)
    tiles_s = dict(tm=1024, tf=256, seq_nb=64, conv_tl=Ls, pool_tl=Ls, rwkv_nsb=4 * CHUNK // Ls, rwkv_tb=Ls)
    yp = _trunk(x_prompt, z_conv, z_shift, z_wkv, z_pool, 0, prm, tiles_p)
    ys = _trunk(x_sample, state_conv[0], state_shift[0], state_wkv[0], state_pool[0], PAST_LEN, prm, tiles_s)
    return (yp[0], ys[0], yp[1], yp[2], yp[3], yp[4], ys[1], ys[2], ys[3], ys[4])
```

```python
import functools

import jax
import jax.numpy as jnp
from jax import lax
from jax.experimental import pallas as pl
from jax.experimental.pallas import tpu as pltpu

F32 = jnp.float32
BF16 = jnp.bfloat16

HEAD = 64
PAIR = 2 * HEAD
CHUNK = 64
SOLVE_BLOCK = 16
SUBLANES = 8
CONV_TAPS = 31
CONV_HIST = CONV_TAPS - 1
CONV_PAD = 32
POOL_WINDOWS = (2, 4, 8, 16)
POOL_HIST = max(POOL_WINDOWS) - 1
POOL_PAD = 16
PAST_LEN = 16384
LN_EPS = 1e-5
GN_EPS = 64e-5
DECAY_SCALE = 0.6065306597126334
KK_NORM_FLOOR = 1e-12
VMEM_LIMIT = 56 * 1024 * 1024


def _ln(z, g, b, eps):
    mu = jnp.mean(z, axis=-1, keepdims=True)
    zc = z - mu
    var = jnp.mean(zc * zc, axis=-1, keepdims=True)
    return zc * lax.rsqrt(var + eps) * g + b


def _sigmoid(x):
    return 1.0 / (1.0 + jnp.exp(-x))


def _cparams(sem):
    return pltpu.CompilerParams(dimension_semantics=sem, vmem_limit_bytes=VMEM_LIMIT)


def _log2(n):
    assert n & (n - 1) == 0
    return n.bit_length() - 1


def _inproj_kernel(x_ref, wa_ref, wb_ref, oa_ref, ob_ref):
    xb = x_ref[...].astype(BF16)
    pa = jnp.dot(xb, wa_ref[...], preferred_element_type=F32)
    width = pa.shape[1] // 2
    oa_ref[...] = pa[:, :width] * _sigmoid(pa[:, width:])
    ob_ref[...] = jnp.dot(xb, wb_ref[...], preferred_element_type=F32)


def _inproj(x, wa, wb, tm):
    T, D = x.shape
    na, nb = wa.shape[1] // 2, wb.shape[1]
    return pl.pallas_call(
        _inproj_kernel,
        grid=(T // tm,),
        in_specs=[pl.BlockSpec((tm, D), lambda i: (i, 0)),
                  pl.BlockSpec(wa.shape, lambda i: (0, 0)),
                  pl.BlockSpec((D, nb), lambda i: (0, 0))],
        out_specs=[pl.BlockSpec((tm, na), lambda i: (i, 0)),
                   pl.BlockSpec((tm, nb), lambda i: (i, 0))],
        out_shape=[jax.ShapeDtypeStruct((T, na), F32), jax.ShapeDtypeStruct((T, nb), F32)],
        compiler_params=_cparams(("parallel",)),
        name="in_proj",
    )(x, wa, wb)


def _conv_kernel(u_ref, st_ref, w_ref, b_ref, g_ref, bb_ref, o_ref, nst_ref, ext_ref, *, tl, width):
    c = pl.program_id(1)

    base = CONV_PAD - CONV_HIST

    @pl.when(c == 0)
    def _():
        ext_ref[:, base:CONV_PAD, :] = st_ref[...]
        ext_ref[:, CONV_PAD + tl:CONV_PAD + tl + SUBLANES, :] = jnp.zeros((u_ref.shape[0], SUBLANES, width), F32)

    ext_ref[:, CONV_PAD:CONV_PAD + tl, :] = u_ref[...]
    acc = None
    for rho in range(SUBLANES):
        part = None
        for m in range((base + CONV_TAPS) // SUBLANES + 1):
            j = SUBLANES * m + rho - base
            if 0 <= j < CONV_TAPS:
                term = ext_ref[:, SUBLANES * m:SUBLANES * m + tl + SUBLANES, :] * w_ref[j:j + 1, :]
                part = term if part is None else part + term
        shifted = part[:, rho:rho + tl, :]
        acc = shifted if acc is None else acc + shifted
    acc = acc + b_ref[...]
    y = _ln(acc, g_ref[...], bb_ref[...], LN_EPS)
    o_ref[...] = (y * _sigmoid(y)).astype(o_ref.dtype)
    tail = ext_ref[:, tl + CONV_PAD - CONV_HIST:tl + CONV_PAD, :]
    ext_ref[:, CONV_PAD - CONV_HIST:CONV_PAD, :] = tail
    nst_ref[...] = tail


def _conv_branch(glu, state, w, b, g, bb, nb, tl):
    n_seq, seq_len, width = glu.shape
    vec = pl.BlockSpec((1, width), lambda n, c: (0, 0))
    return pl.pallas_call(
        functools.partial(_conv_kernel, tl=tl, width=width),
        grid=(n_seq // nb, seq_len // tl),
        in_specs=[pl.BlockSpec((nb, tl, width), lambda n, c: (n, c, 0)),
                  pl.BlockSpec((nb, CONV_HIST, width), lambda n, c: (n, 0, 0)),
                  pl.BlockSpec((CONV_TAPS, width), lambda n, c: (0, 0)),
                  vec, vec, vec],
        out_specs=[pl.BlockSpec((nb, tl, width), lambda n, c: (n, c, 0)),
                   pl.BlockSpec((nb, CONV_HIST, width), lambda n, c: (n, 0, 0))],
        out_shape=[jax.ShapeDtypeStruct((n_seq, seq_len, width), BF16 if tl % (2 * SUBLANES) == 0 else F32),
                   jax.ShapeDtypeStruct((n_seq, CONV_HIST, width), F32)],
        scratch_shapes=[pltpu.VMEM((nb, CONV_PAD + tl + SUBLANES, width), F32)],
        compiler_params=_cparams(("parallel", "arbitrary")),
        name="conv_branch",
    )(glu, state, w, b, g, bb)


def _dotb(a, b):
    return jnp.dot(a, b, preferred_element_type=F32)


def _dotb_nt(a, b):
    return lax.dot_general(a, b, (((1,), (1,)), ((), ())), preferred_element_type=F32)


def _b(x):
    return x.astype(BF16)


def _rwkv_kernel(p_ref, sh_ref, wkv_ref, mu_ref, w0_ref, w2a2_ref, a0_ref, g2_ref, kk_ref, ka_ref, rk_ref,
                 lng_ref, lnb_ref, o_ref, nsh_ref, nwkv_ref, s_scr, prev_scr, *, nsb, tb, bw, lw):
    c = pl.program_id(1)
    n_pairs = bw // PAIR
    C = CHUNK
    R = nsb * tb
    nq = R // C
    ts = min(tb, C)
    ns = C // ts
    assert HEAD == C and nq * C == R and (nsb == 1 or tb == ts)
    ts_bits = _log2(ts)
    sb = min(SOLVE_BLOCK, ts)
    sb_bits = _log2(sb)

    @pl.when(c == 0)
    def _():
        prev_scr[...] = sh_ref[...]
        s_scr[...] = jnp.zeros_like(s_scr)
        for j in range(nsb):
            for h in range(2 * n_pairs):
                lo = (h % 2) * HEAD
                s_scr[j, h // 2, lo:lo + HEAD, lo:lo + HEAD] = wkv_ref[j, h]

    p = p_ref[...]
    pw = p.shape[1]
    row = lax.broadcasted_iota(jnp.int32, (R, 1), 0)
    shifted = pltpu.roll(p, 1, 0)
    if nsb == 1:
        xm = p + mu_ref[...] * (shifted - p)
        head = p[0:SUBLANES]
        prev_h = jnp.where(row[0:SUBLANES] == 0, prev_scr[0], shifted[0:SUBLANES])
        xm = jnp.concatenate([head + mu_ref[...] * (prev_h - head), xm[SUBLANES:]], axis=0)
    else:
        first = jnp.concatenate([jnp.broadcast_to(prev_scr[j], (tb, pw)) for j in range(nsb)], axis=0)
        prev = jnp.where((row & (tb - 1)) == 0, first, shifted)
        xm = p + mu_ref[...] * (prev - p)
    for j in range(nsb):
        last = p[(j + 1) * tb - 1:(j + 1) * tb, :]
        prev_scr[j] = last
        nsh_ref[j] = last
    r = xm[:, 0:bw]
    k = xm[:, bw:2 * bw]
    v = xm[:, 2 * bw:3 * bw]
    dwa = xm[:, 3 * bw:3 * bw + 2 * lw]
    dg = xm[:, 3 * bw + 2 * lw:]
    lane_l = lax.broadcasted_iota(jnp.int32, (1, 2 * lw), 1)
    is_w = lane_l < lw
    w2a2 = _b(w2a2_ref[...])
    wpre = w0_ref[...] + _dotb(_b(jnp.where(is_w, jnp.tanh(dwa), 0.0)), w2a2)
    apre = a0_ref[...] + _dotb(_b(jnp.where(is_w, 0.0, dwa)), w2a2)
    logw = -DECAY_SCALE * _sigmoid(wpre)
    a_lr = _sigmoid(apre)
    gate = _dotb(_b(_sigmoid(dg)), _b(g2_ref[...]))
    kkv = k * kk_ref[...]
    k2 = k * (1.0 + (a_lr - 1.0) * ka_ref[...])
    rk2 = r * k2 * rk_ref[...]

    lane = lax.broadcasted_iota(jnp.int32, (1, PAIR), 1)
    m_l = lane < HEAD
    rowp = lax.broadcasted_iota(jnp.int32, (PAIR, PAIR), 0)
    colp = lax.broadcasted_iota(jnp.int32, (PAIR, PAIR), 1)
    bd = (rowp >= HEAD) == (colp >= HEAD)
    t_i = lax.broadcasted_iota(jnp.int32, (C, PAIR), 0)
    s_i = lax.broadcasted_iota(jnp.int32, (C, PAIR), 1) & (HEAD - 1)
    same = (s_i >> ts_bits) == (t_i >> ts_bits)
    strict = same & (s_i < t_i)
    incl = same & (s_i <= t_i)
    eye = (s_i == t_i).astype(F32)
    dblk = (s_i >> sb_bits) == (t_i >> sb_bits)

    def vs(y):
        return jnp.concatenate([jnp.where(m_l, y, 0.0), jnp.where(m_l, 0.0, y)], axis=0)

    def pm(a, y):
        return _dotb(_b(a), vs(_b(y)))

    def pmm(lhs, y):
        out = _dotb(_b(jnp.concatenate(lhs, axis=0)), vs(_b(y)))
        return [out[k * C:(k + 1) * C] for k in range(len(lhs))]

    def bdsum(xs):
        out = []
        for x in xs:
            s0 = jnp.sum(jnp.where(m_l, x, 0.0), axis=1, keepdims=True)
            s1 = jnp.sum(jnp.where(m_l, 0.0, x), axis=1, keepdims=True)
            out.append(jnp.where(m_l, s0, s1))
        return out

    P = range(n_pairs)
    units = [(q, i) for q in range(nq) for i in P]
    NU = range(len(units))

    def blk(x, n):
        q, i = units[n]
        return x[q * C:(q + 1) * C, i * PAIR:(i + 1) * PAIR]

    def neumann(m, n_terms):
        t = [eye + m[n] for n in NU]
        levels = _log2(n_terms)
        if levels == 1:
            return t
        p = [pm(m[n], m[n]) for n in NU]
        for _ in range(levels - 2):
            both = [pmm([p[n], t[n]], p[n]) for n in NU]
            t = [t[n] + both[n][1] for n in NU]
            p = [both[n][0] for n in NU]
        return [t[n] + pm(t[n], p[n]) for n in NU]

    t_c = lax.broadcasted_iota(jnp.int32, (C, C), 0)
    s_c = lax.broadcasted_iota(jnp.int32, (C, C), 1)
    tri_b = (((s_c >> ts_bits) == (t_c >> ts_bits)) & (s_c <= t_c)).astype(BF16)
    tri2 = jnp.concatenate([tri_b, tri_b], axis=1)
    lw_hi = _b(logw)
    lw_lo = _b(logw - lw_hi.astype(F32))
    cs = jnp.concatenate(
        [_dotb(tri2, jnp.concatenate([lw_hi[q * C:(q + 1) * C], lw_lo[q * C:(q + 1) * C]], axis=0))
         for q in range(nq)], axis=0)
    g_in = jnp.exp(cs)
    g_ex = jnp.exp(cs - logw)
    g_inv = jnp.exp(-cs)
    runs = g_in.reshape(R // ts, ts, bw)
    g_end = jnp.broadcast_to(runs[:, ts - 1:ts, :], runs.shape).reshape(R, bw)

    sums = bdsum([blk(kkv, n) * blk(kkv, n) for n in NU] + [blk(rk2, n) for n in NU])
    kk_n = [blk(kkv, n) * lax.rsqrt(jnp.maximum(sums[n], KK_NORM_FLOOR * KK_NORM_FLOOR)) for n in NU]
    v_p = [blk(v, n) for n in NU]
    at = [-kk_n[n] * blk(g_ex, n) for n in NU]
    rt = [blk(r, n) * blk(g_in, n) for n in NU]
    bt = [kk_n[n] * blk(a_lr, n) * blk(g_inv, n) for n in NU]
    kt = [blk(k2, n) * blk(g_inv, n) for n in NU]
    sm = [_dotb_nt(_b(jnp.concatenate([at[n], rt[n]], axis=0)),
                   jnp.concatenate([vs(_b(bt[n])), vs(_b(kt[n]))], axis=0)) for n in NU]
    lab = [jnp.where(strict, sm[n][0:C, 0:PAIR], 0.0) for n in NU]
    lak = [jnp.where(strict, sm[n][0:C, PAIR:2 * PAIR], 0.0) for n in NU]
    mrb = [jnp.where(incl, sm[n][C:2 * C, 0:PAIR], 0.0) for n in NU]
    mrk = [jnp.where(incl, sm[n][C:2 * C, PAIR:2 * PAIR], 0.0) for n in NU]
    xv = [pmm([lak[n], mrk[n]], v_p[n]) for n in NU]
    x0 = [xv[n][0] for n in NU]
    d1 = [jnp.where(dblk, lab[n], 0.0) for n in NU]
    tinv = neumann(d1, sb)
    if sb < ts:
        fp = [pm(lab[n] - d1[n], tinv[n]) for n in NU]
        levels = _log2(ts // sb)
        for lev in range(levels):
            if lev < levels - 1:
                both = [pmm([fp[n], tinv[n]], fp[n]) for n in NU]
                tinv = [tinv[n] + both[n][1] for n in NU]
                fp = [both[n][0] for n in NU]
            else:
                tinv = [tinv[n] + pm(tinv[n], fp[n]) for n in NU]
    tinv = [_b(tinv[n]) for n in NU]
    s_cur = [[s_scr[sq, i] for i in P] for sq in range(nsb)]
    col_seq = (lax.broadcasted_iota(jnp.int32, (1, PAIR), 1) & (HEAD - 1)) >> ts_bits

    def finish(q, o_q):
        un = [q * n_pairs + i for i in P]
        om = bdsum(o_q)
        oc = [o_q[i] - om[i] * (1.0 / HEAD) for i in P]
        ov = bdsum([oc[i] * oc[i] for i in P])
        for i in P:
            n = un[i]
            ln = slice(i * PAIR, (i + 1) * PAIR)
            bonus = sums[len(units) + n] * v_p[n]
            y = oc[i] * lax.rsqrt(ov[i] * (1.0 / HEAD) + GN_EPS) * lng_ref[:, ln] + lnb_ref[:, ln] + bonus
            o_ref[q * C:(q + 1) * C, ln] = (y * blk(gate, n)).astype(o_ref.dtype)

    o_prev = None
    for q in range(nq):
        un = [q * n_pairs + i for i in P]
        seq = [q * ns + j if nsb > 1 else 0 for j in range(ns)]
        a_s, r_s = [], []
        for i in P:
            parts = []
            for j in range(ns):
                rows = slice(j * ts, (j + 1) * ts)
                lhs = _b(jnp.concatenate([at[un[i]][rows], rt[un[i]][rows]], axis=0))
                parts.append(_dotb_nt(lhs, _b(s_cur[seq[j]][i])))
            a_s.append(jnp.concatenate([x[0:ts] for x in parts], axis=0))
            r_s.append(jnp.concatenate([x[ts:2 * ts] for x in parts], axis=0))
        if o_prev is not None:
            finish(q - 1, o_prev)
        u = [_dotb(tinv[un[i]], vs(_b(a_s[i] + x0[un[i]]))) for i in P]
        for i in P:
            n = un[i]
            ge = blk(g_end, n)
            uv = _b(jnp.concatenate([u[i], v_p[n]], axis=0).T)
            bk = _b(jnp.concatenate([bt[n] * ge, kt[n] * ge], axis=0))
            for j in range(ns):
                uv_j = uv if ns == 1 else jnp.where(col_seq == j, uv, 0.0)
                upd = _dotb(uv_j, bk)
                s_cur[seq[j]][i] = s_cur[seq[j]][i] * ge[j * ts:j * ts + 1, :] + jnp.where(bd, upd, 0.0)
        o_prev = [r_s[i] + pm(mrb[un[i]], u[i]) + xv[un[i]][1] for i in P]
    for sq in range(nsb):
        for i in P:
            s_scr[sq, i] = s_cur[sq][i]
    finish(nq - 1, o_prev)

    @pl.when(c == pl.num_programs(1) - 1)
    def _():
        for j in range(nsb):
            for h in range(2 * n_pairs):
                lo = (h % 2) * HEAD
                nwkv_ref[j, h] = s_scr[j, h // 2, lo:lo + HEAD, lo:lo + HEAD]


def _rwkv_branch(proj_b, shift, wkv, prm, n_seq, seq_len, nsb, tb):
    T, pw = proj_b.shape
    n_heads = wkv.shape[1]
    bw = n_heads * HEAD
    lw = prm["w2a2"].shape[0] // 2
    nc = seq_len // tb
    assert nsb == 1 or nc == 1
    rows = nsb * tb

    def full(a):
        return pl.BlockSpec(a.shape, lambda n, c: (0,) * a.ndim)

    names = ("mu", "w0", "w2a2", "a0", "g2", "kk", "ka", "rk", "lng", "lnb")
    return pl.pallas_call(
        functools.partial(_rwkv_kernel, nsb=nsb, tb=tb, bw=bw, lw=lw),
        grid=(n_seq // nsb, nc),
        in_specs=[pl.BlockSpec((rows, pw), lambda n, c: (n * nc + c, 0)),
                  pl.BlockSpec((nsb, 1, pw), lambda n, c: (n, 0, 0)),
                  pl.BlockSpec((nsb, n_heads, HEAD, HEAD), lambda n, c: (n, 0, 0, 0))]
                 + [full(prm[k]) for k in names],
        out_specs=[pl.BlockSpec((rows, bw), lambda n, c: (n * nc + c, 0)),
                   pl.BlockSpec((nsb, 1, pw), lambda n, c: (n, 0, 0)),
                   pl.BlockSpec((nsb, n_heads, HEAD, HEAD), lambda n, c: (n, 0, 0, 0))],
        out_shape=[jax.ShapeDtypeStruct((T, bw), BF16),
                   jax.ShapeDtypeStruct((n_seq, 1, pw), F32),
                   jax.ShapeDtypeStruct((n_seq, n_heads, HEAD, HEAD), F32)],
        scratch_shapes=[pltpu.VMEM((nsb, n_heads // 2, PAIR, PAIR), F32), pltpu.VMEM((nsb, 1, pw), F32)],
        compiler_params=_cparams(("parallel", "arbitrary")),
        name="rwkv_branch",
    )(proj_b, shift.reshape(n_seq, 1, pw), wkv, *[prm[k] for k in names])


def _outproj_kernel(a_ref, b_ref, x_ref, wa_ref, wb_ref, g_ref, bb_ref, o_ref, *, alpha, sub):
    groups = [slice(r0, r0 + sub) for r0 in range(0, o_ref.shape[0], sub)]
    mixes = [jnp.dot(a_ref[rows, :].astype(BF16), wa_ref[...], preferred_element_type=F32) for rows in groups]
    mixes = [m + jnp.dot(b_ref[rows, :].astype(BF16), wb_ref[...], preferred_element_type=F32)
             for m, rows in zip(mixes, groups)]
    for m, rows in zip(mixes, groups):
        o_ref[rows, :] = _ln(alpha * x_ref[rows, :] + m, g_ref[...], bb_ref[...], LN_EPS)


def _outproj(a, b, x, wa, wb, g, bb, alpha, tm):
    T, D = x.shape
    wdt = a.shape[1]
    vec = pl.BlockSpec((1, D), lambda i: (0, 0))
    return pl.pallas_call(
        functools.partial(_outproj_kernel, alpha=alpha, sub=tm // 4),
        grid=(T // tm,),
        in_specs=[pl.BlockSpec((tm, wdt), lambda i: (i, 0)),
                  pl.BlockSpec((tm, wdt), lambda i: (i, 0)),
                  pl.BlockSpec((tm, D), lambda i: (i, 0)),
                  pl.BlockSpec((wdt, D), lambda i: (0, 0)),
                  pl.BlockSpec((wdt, D), lambda i: (0, 0)),
                  vec, vec],
        out_specs=pl.BlockSpec((tm, D), lambda i: (i, 0)),
        out_shape=jax.ShapeDtypeStruct((T, D), F32),
        compiler_params=_cparams(("parallel",)),
        name="out_proj_ln",
    )(a, b, x, wa, wb, g, bb)


def _ffn_kernel(x_ref, wg_ref, wu_ref, wd_ref, g_ref, bb_ref, o_ref, acc_ref, xb_ref, wg_s, wu_s, wd_s,
                *, alpha, ln_groups, cache):
    i = pl.program_id(0)
    j = pl.program_id(1)

    if cache:
        @pl.when(i == 0)
        def _():
            wg_s[j] = wg_ref[...].astype(BF16)
            wu_s[j] = wu_ref[...].astype(BF16)
            wd_s[j] = wd_ref[...].astype(BF16)

    def weights():
        if cache:
            return wg_s[j], wu_s[j], wd_s[j]
        return wg_ref[...].astype(BF16), wu_ref[...].astype(BF16), wd_ref[...].astype(BF16)

    def term(xb, w):
        hg = jnp.dot(xb, w[0], preferred_element_type=F32)
        hu = jnp.dot(xb, w[1], preferred_element_type=F32)
        h = (hg * _sigmoid(hg) * hu).astype(BF16)
        return jnp.dot(h, w[2], preferred_element_type=F32)

    last = pl.num_programs(1) - 1
    tm = o_ref.shape[0]

    @pl.when(j == 0)
    def _():
        w = weights()
        for r0 in range(0, tm, tm // ln_groups):
            rows = slice(r0, r0 + tm // ln_groups)
            xb = x_ref[rows, :].astype(BF16)
            xb_ref[rows, :] = xb
            acc_ref[rows, :] = term(xb, w)

    @pl.when((j > 0) & (j < last))
    def _():
        acc_ref[...] = acc_ref[...] + term(xb_ref[...], weights())

    @pl.when(j == last)
    def _():
        w = weights()
        groups = [slice(r0, r0 + tm // ln_groups) for r0 in range(0, tm, tm // ln_groups)]
        totals = [alpha * x_ref[rows, :] + (acc_ref[rows, :] + term(xb_ref[rows, :], w)) for rows in groups]
        for rows, total in zip(groups, totals):
            o_ref[rows, :] = _ln(total, g_ref[...], bb_ref[...], LN_EPS)


def _ffn(x, wg, wu, wd, layer, g, bb, alpha, tm, tf):
    T, D = x.shape
    Fd = wg.shape[2]
    nj = Fd // tf
    assert nj >= 2
    vec = pl.BlockSpec((1, D), lambda i, j: (0, 0))
    cache = T // tm > 1
    nj_s = nj if cache else 1

    def wj(i, j):
        return jnp.where(i == 0, j, nj - 1)

    return pl.pallas_call(
        functools.partial(_ffn_kernel, alpha=alpha, ln_groups=2, cache=cache),
        grid=(T // tm, nj),
        in_specs=[pl.BlockSpec((tm, D), lambda i, j: (i, 0)),
                  pl.BlockSpec((None, D, tf), lambda i, j: (layer, 0, wj(i, j))),
                  pl.BlockSpec((None, D, tf), lambda i, j: (layer, 0, wj(i, j))),
                  pl.BlockSpec((None, tf, D), lambda i, j: (layer, wj(i, j), 0)),
                  vec, vec],
        out_specs=pl.BlockSpec((tm, D), lambda i, j: (i, 0)),
        out_shape=jax.ShapeDtypeStruct((T, D), F32),
        scratch_shapes=[pltpu.VMEM((tm, D), F32), pltpu.VMEM((tm, D), BF16),
                        pltpu.VMEM((nj_s, D, tf), BF16), pltpu.VMEM((nj_s, D, tf), BF16),
                        pltpu.VMEM((nj_s, tf, D), BF16)],
        compiler_params=_cparams(("arbitrary", "arbitrary")),
        name="ffn_ln",
    )(x, wg, wu, wd, g, bb)


def _pool_kernel(x_ref, st_ref, w_ref, sc_ref, g_ref, bb_ref, o_ref, nst_ref, ext_ref, *, tl, start, alpha):
    c = pl.program_id(1)

    x = x_ref[...]
    nb, _, D = x.shape
    gw = D // len(POOL_WINDOWS)
    n_ext = POOL_PAD + tl

    @pl.when(c == 0)
    def _():
        ext_ref[:, 0:POOL_PAD - POOL_HIST, :] = jnp.zeros((nb, POOL_PAD - POOL_HIST, D), F32)
        ext_ref[:, POOL_PAD - POOL_HIST:POOL_PAD, :] = st_ref[...]

    ext_ref[:, POOL_PAD:n_ext, :] = x
    pos = start + c * tl + lax.broadcasted_iota(jnp.int32, (1, tl, 1), 1)
    outs = []
    for gi, wdw in enumerate(POOL_WINDOWS):
        lo = gi * gw
        acc = ext_ref[:, 1:n_ext, lo:lo + gw] + ext_ref[:, 0:n_ext - 1, lo:lo + gw]
        h = 2
        while h < wdw:
            rows = acc.shape[1]
            acc = acc[:, h:rows, :] + acc[:, 0:rows - h, :]
            h *= 2
        first = POOL_PAD + 1 - wdw
        ssum = acc[:, first:first + tl, :]
        cnt = jnp.minimum(wdw, pos + 1).astype(F32)
        pooled = ssum / cnt - x[..., lo:lo + gw]
        y = jnp.dot(pooled.reshape(nb * tl, gw).astype(BF16), w_ref[gi], preferred_element_type=F32)
        outs.append(y.reshape(nb, tl, gw))
    y = jnp.concatenate(outs, axis=-1) * sc_ref[...]
    o_ref[...] = _ln(alpha * x + y, g_ref[...], bb_ref[...], LN_EPS)
    tail = ext_ref[:, tl + POOL_PAD - POOL_HIST:tl + POOL_PAD, :]
    ext_ref[:, POOL_PAD - POOL_HIST:POOL_PAD, :] = tail
    nst_ref[...] = tail


def _pool_mix(x, state, w, scale, g, bb, nb, tl, start, alpha):
    n_seq, seq_len, D = x.shape
    vec = pl.BlockSpec((1, D), lambda n, c: (0, 0))
    return pl.pallas_call(
        functools.partial(_pool_kernel, tl=tl, start=start, alpha=alpha),
        grid=(n_seq // nb, seq_len // tl),
        in_specs=[pl.BlockSpec((nb, tl, D), lambda n, c: (n, c, 0)),
                  pl.BlockSpec((nb, POOL_HIST, D), lambda n, c: (n, 0, 0)),
                  pl.BlockSpec(w.shape, lambda n, c: (0, 0, 0)),
                  vec, vec, vec],
        out_specs=[pl.BlockSpec((nb, tl, D), lambda n, c: (n, c, 0)),
                   pl.BlockSpec((nb, POOL_HIST, D), lambda n, c: (n, 0, 0))],
        out_shape=[jax.ShapeDtypeStruct((n_seq, seq_len, D), F32),
                   jax.ShapeDtypeStruct((n_seq, POOL_HIST, D), F32)],
        scratch_shapes=[pltpu.VMEM((nb, POOL_PAD + tl, D), F32)],
        compiler_params=_cparams(("parallel", "arbitrary")),
        name="pool_mix_ln",
    )(x, state, w, scale, g, bb)


def _trunk(x3, conv_prev, shift_prev, wkv_prev, pool_prev, start, prm, tiles):
    n_seq, seq_len, D = x3.shape
    T = n_seq * seq_len
    x = x3.reshape(T, D)
    alpha = prm["alpha"]
    tm = tiles["tm"]
    glu, proj_b = _inproj(x, prm["w_in_a"], prm["w_in_b"], tm)
    a_out, new_conv = _conv_branch(glu.reshape(n_seq, seq_len, -1), conv_prev, prm["conv_w"], prm["conv_b"],
                                   prm["conv_ln_g"], prm["conv_ln_b"], tiles["seq_nb"], tiles["conv_tl"])
    b_out, new_shift, new_wkv = _rwkv_branch(proj_b, shift_prev, wkv_prev, prm["rwkv"], n_seq, seq_len,
                                             tiles["rwkv_nsb"], tiles["rwkv_tb"])
    x = _outproj(a_out.reshape(T, -1), b_out, x, prm["w_out_a"], prm["w_out_b"], prm["ln_mix_g"][0],
                 prm["ln_mix_b"][0], alpha, tm)
    x = _ffn(x, prm["ffn_gate"], prm["ffn_up"], prm["ffn_down"], 0, prm["ln_ffn_g"][0], prm["ln_ffn_b"][0],
             alpha, tm, tiles["tf"])
    x, new_pool = _pool_mix(x.reshape(n_seq, seq_len, D), pool_prev, prm["pool_w"], prm["pool_scale"],
                            prm["ln_mix_g"][1], prm["ln_mix_b"][1], tiles["seq_nb"], tiles["pool_tl"], start, alpha)
    x = _ffn(x.reshape(T, D), prm["ffn_gate"], prm["ffn_up"], prm["ffn_down"], 1, prm["ln_ffn_g"][1],
             prm["ln_ffn_b"][1], alpha, tm, tiles["tf"])
    return (x.reshape(n_seq, seq_len, D), new_conv[None], new_shift.reshape(1, n_seq, -1), new_wkv[None],
            new_pool[None])


def kernel(x_prompt, x_sample, state_conv, state_shift, state_wkv, state_pool, w_in, conv_w, conv_b, conv_ln_g, conv_ln_b, rwkv_mu, rwkv_w0, rwkv_w2, rwkv_a0, rwkv_a2, rwkv_g2, rwkv_kk, rwkv_ka, rwkv_rk, rwkv_lnx_g, rwkv_lnx_b, w_out, pool_w, pool_scale, ln_mix_g, ln_mix_b, ffn_gate, ffn_up, ffn_down, ln_ffn_g, ln_ffn_b):
    depth = ln_mix_g.shape[0]
    assert depth == 2 and w_in.shape[0] == 1 and pool_w.shape[0] == 1
    B, L, D = x_prompt.shape
    Bs, Ls, _ = x_sample.shape
    a_w = conv_w.shape[2]
    b_w = rwkv_w0.shape[1]
    n_heads = b_w // HEAD
    alpha = float((2 * depth) ** 0.25)

    row = lambda a: a.reshape(1, -1)
    prm = dict(
        alpha=alpha,
        w_in_a=w_in[0, :, :2 * a_w].astype(BF16), w_in_b=w_in[0, :, 2 * a_w:].astype(BF16),
        conv_w=conv_w[0], conv_b=row(conv_b[0]), conv_ln_g=row(conv_ln_g[0]), conv_ln_b=row(conv_ln_b[0]),
        rwkv=dict(mu=row(rwkv_mu[0]), w0=row(rwkv_w0[0]),
                  w2a2=jnp.concatenate([rwkv_w2[0], rwkv_a2[0]], axis=0), a0=row(rwkv_a0[0]), g2=rwkv_g2[0],
                  kk=row(rwkv_kk[0]), ka=row(rwkv_ka[0]), rk=row(rwkv_rk[0]),
                  lng=row(rwkv_lnx_g[0]), lnb=row(rwkv_lnx_b[0])),
        w_out_a=w_out[0, :a_w].astype(BF16), w_out_b=w_out[0, a_w:].astype(BF16),
        pool_w=pool_w[0].astype(BF16), pool_scale=row(pool_scale[0]),
        ln_mix_g=[row(ln_mix_g[i]) for i in range(depth)], ln_mix_b=[row(ln_mix_b[i]) for i in range(depth)],
        ffn_gate=ffn_gate, ffn_up=ffn_up, ffn_down=ffn_down,
        ln_ffn_g=[row(ln_ffn_g[i]) for i in range(depth)], ln_ffn_b=[row(ln_ffn_b[i]) for i in range(depth)],
    )
    dt = x_prompt.dtype
    z_conv = jnp.zeros((B, CONV_HIST, a_w), dt)
    z_shift = jnp.zeros((B, state_shift.shape[-1]), dt)
    z_wkv = jnp.zeros((B, n_heads, HEAD, HEAD), state_wkv.dtype)
    z_pool = jnp.zeros((B, POOL_HIST, D), dt)
    tiles_p = dict(tm=1024, tf=256, seq_nb=1, conv_tl=512, pool_tl=1024, rwkv_nsb=1, rwkv_tb=8 * CHUNK)
    tiles_s = dict(tm=1024, tf=256, seq_nb=64, conv_tl=Ls, pool_tl=Ls, rwkv_nsb=4 * CHUNK // Ls, rwkv_tb=Ls)
    yp = _trunk(x_prompt, z_conv, z_shift, z_wkv, z_pool, 0, prm, tiles_p)
    ys = _trunk(x_sample, state_conv[0], state_shift[0], state_wkv[0], state_pool[0], PAST_LEN, prm, tiles_s)
    return (yp[0], ys[0], yp[1], yp[2], yp[3], yp[4], ys[1], ys[2], ys[3], ys[4])
```
